```python
import math
import jax
import jax.numpy as jnp
from jax import lax
import numpy as np

D_MODEL = 2048
BATCH = 2
SEQ = 16384
DEPTH = 2

D_ATTN = D_MODEL // 2
D_SSM = D_MODEL // 2

NSA_HEADS = 16
NSA_KV_HEADS = 2
NSA_GROUP = NSA_HEADS // NSA_KV_HEADS
HEAD_DIM = D_ATTN // NSA_HEADS
CMP_BLOCK = 32
CMP_STRIDE = 16
CMP_HIDDEN = 4 * HEAD_DIM
SLC_BLOCK = 64
SLC_TOPN = 16
WINDOW = 512
Q_BLOCK = 128
FORCE_SCORE = 1e4
NEG_INF = -1e30

REL_BUCKETS = 32
REL_MAX_DIST = 2048

SSM_HEAD_DIM = 64
SSM_HEADS = D_SSM // SSM_HEAD_DIM
SSM_GROUPS = 2
SSM_STATE = 128
CONV_WIDTH = 4
CONV_CH = D_SSM + 2 * SSM_GROUPS * SSM_STATE
SSM_CHUNK = 256

PEER_HEADS = 8
PEER_NKEYS = 128
PEER_EXPERTS = PEER_NKEYS * PEER_NKEYS
PEER_KEY_DIM = 256
PEER_TOPK = 16
PEER_TOKEN_BLOCK = 128

N_Q = NSA_HEADS * HEAD_DIM
N_KV = 6 * NSA_KV_HEADS * HEAD_DIM
N_GATE = 3 * NSA_HEADS
N_Z = D_SSM
N_XBC = CONV_CH
N_DT = SSM_HEADS
N_IN = N_Q + N_KV + N_GATE + N_Z + N_XBC + N_DT

RMS_EPS = 1e-6

kernel_name = 'hybrid_nsa_ssd_peer_block'


def rms_norm(x, w):
    xf = x.astype(jnp.float32)
    y = xf * lax.rsqrt(jnp.mean(xf * xf, axis=-1, keepdims=True) + RMS_EPS)
    return (y * w.astype(jnp.float32)).astype(x.dtype)


def masked_softmax(s, mask):
    p = jax.nn.softmax(jnp.where(mask, s.astype(jnp.float32), NEG_INF), axis=-1)
    return jnp.where(mask, p, 0.0)


def rel_bucket(diff):
    dist = jnp.maximum(diff, 0)
    max_exact = REL_BUCKETS // 2
    log_ratio = jnp.log(jnp.maximum(dist, max_exact).astype(jnp.float32) / max_exact) / math.log(REL_MAX_DIST / max_exact)
    large = max_exact + (log_ratio * (REL_BUCKETS - max_exact)).astype(jnp.int32)
    return jnp.where(dist < max_exact, dist, jnp.minimum(large, REL_BUCKETS - 1))


def compress_blocks(k, pe, w1, w2):
    bsz, s = k.shape[0], k.shape[1]
    n_cmp = (s - CMP_BLOCK) // CMP_STRIDE + 1
    idx = np.arange(n_cmp)[:, None] * CMP_STRIDE + np.arange(CMP_BLOCK)[None, :]
    blocks = k[:, idx] + pe[None, None, :, None, :]
    flat = blocks.transpose(0, 1, 3, 2, 4).reshape(bsz, n_cmp, NSA_KV_HEADS, CMP_BLOCK * HEAD_DIM)
    return jax.nn.gelu(flat @ w1) @ w2


def nsa_attention(q, kv, gate_logits, cmp_pe, cmp_w1, cmp_w2, rel_bias):
    bsz, s, _ = q.shape
    f32 = jnp.float32
    q = q.reshape(bsz, s, NSA_KV_HEADS, NSA_GROUP, HEAD_DIM)
    kv = kv.reshape(bsz, s, 6, NSA_KV_HEADS, HEAD_DIM)
    k_cmp_raw, v_cmp_raw, k_slc, v_slc, k_win, v_win = [kv[:, :, i] for i in range(6)]
    gates = jax.nn.sigmoid(gate_logits.astype(f32)).reshape(bsz, s, NSA_KV_HEADS, NSA_GROUP, 3).astype(q.dtype)

    n_cmp = (s - CMP_BLOCK) // CMP_STRIDE + 1
    k_cmp = compress_blocks(k_cmp_raw, cmp_pe[0], cmp_w1[0], cmp_w2[0])
    v_cmp = compress_blocks(v_cmp_raw, cmp_pe[1], cmp_w1[1], cmp_w2[1])
    cmp_end = jnp.arange(n_cmp) * CMP_STRIDE + CMP_BLOCK - 1

    n_slc = s // SLC_BLOCK
    n_sel = min(SLC_TOPN, n_slc)
    k_blocks = k_slc.reshape(bsz, n_slc, SLC_BLOCK, NSA_KV_HEADS, HEAD_DIM).transpose(0, 3, 1, 2, 4)
    v_blocks = v_slc.reshape(bsz, n_slc, SLC_BLOCK, NSA_KV_HEADS, HEAD_DIM).transpose(0, 3, 1, 2, 4)
    j = np.arange(n_slc)
    lo = np.clip((j * SLC_BLOCK - CMP_BLOCK) // CMP_STRIDE + 1, 0, n_cmp)
    hi = np.clip(-((-(j * SLC_BLOCK + SLC_BLOCK)) // CMP_STRIDE), 0, n_cmp)
    blk = jnp.arange(n_slc)

    pad = ((0, 0), (WINDOW, 0), (0, 0), (0, 0))
    k_win_p = jnp.pad(k_win, pad)
    v_win_p = jnp.pad(v_win, pad)

    rel_kg = rel_bias.reshape(REL_BUCKETS, NSA_KV_HEADS, NSA_GROUP)
    b_idx = jnp.arange(bsz)[:, None, None, None]
    h_idx = jnp.arange(NSA_KV_HEADS)[None, :, None, None]
    kv_idx = jnp.arange(NSA_KV_HEADS)[None, :, None, None, None, None]
    g_idx = jnp.arange(NSA_GROUP)[None, None, :, None, None, None]
    scale = HEAD_DIM ** -0.5

    def head_bias(diff):
        return rel_kg[rel_bucket(diff)].transpose(2, 3, 0, 1)

    def query_block(i):
        qs = i * Q_BLOCK
        t = qs + jnp.arange(Q_BLOCK)
        qb = lax.dynamic_slice_in_dim(q, qs, Q_BLOCK, axis=1)
        gb = lax.dynamic_slice_in_dim(gates, qs, Q_BLOCK, axis=1)

        s_c = jnp.einsum('bqkgd,bnkd->bkgqn', qb, k_cmp).astype(f32) * scale + head_bias(t[:, None] - cmp_end[None, :])
        p_c = masked_softmax(s_c, cmp_end[None, :] <= t[:, None])
        o_c = jnp.einsum('bkgqn,bnkd->bqkgd', p_c.astype(qb.dtype), v_cmp)

        cs = jnp.pad(jnp.cumsum(p_c.sum(axis=2), axis=-1), ((0, 0), (0, 0), (0, 0), (1, 0)))
        imp = cs[..., hi] - cs[..., lo]
        cur = (t // SLC_BLOCK)[:, None]
        forced = (blk == 0) | (blk == cur) | (blk == cur - 1)
        imp = jnp.where(forced, FORCE_SCORE, jnp.where(blk <= cur, imp, -FORCE_SCORE))
        _, sel = lax.top_k(imp, n_sel)

        k_sel = k_blocks[b_idx, h_idx, sel]
        v_sel = v_blocks[b_idx, h_idx, sel]
        pos_s = sel[..., None] * SLC_BLOCK + jnp.arange(SLC_BLOCK)
        diff_s = t[:, None, None] - pos_s
        bias_s = rel_kg[rel_bucket(diff_s)[:, :, None], kv_idx, g_idx]
        s_s = jnp.einsum('bqkgd,bkqnld->bkgqnl', qb, k_sel).astype(f32) * scale + bias_s
        s_s = s_s.reshape(bsz, NSA_KV_HEADS, NSA_GROUP, Q_BLOCK, n_sel * SLC_BLOCK)
        mask_s = (diff_s >= 0).reshape(bsz, NSA_KV_HEADS, Q_BLOCK, n_sel * SLC_BLOCK)[:, :, None]
        p_s = masked_softmax(s_s, mask_s)
        o_s = jnp.einsum('bkgqm,bkqmd->bqkgd', p_s.astype(qb.dtype),
                         v_sel.reshape(bsz, NSA_KV_HEADS, Q_BLOCK, n_sel * SLC_BLOCK, HEAD_DIM))

        kw = lax.dynamic_slice_in_dim(k_win_p, qs, WINDOW + Q_BLOCK, axis=1)
        vw = lax.dynamic_slice_in_dim(v_win_p, qs, WINDOW + Q_BLOCK, axis=1)
        pos_w = qs - WINDOW + jnp.arange(WINDOW + Q_BLOCK)
        diff_w = t[:, None] - pos_w[None, :]
        mask_w = (diff_w >= 0) & (diff_w < WINDOW) & (pos_w[None, :] >= 0)
        s_w = jnp.einsum('bqkgd,bmkd->bkgqm', qb, kw).astype(f32) * scale + head_bias(diff_w)
        p_w = masked_softmax(s_w, mask_w)
        o_w = jnp.einsum('bkgqm,bmkd->bqkgd', p_w.astype(qb.dtype), vw)

        return gb[..., 0:1] * o_c + gb[..., 1:2] * o_s + gb[..., 2:3] * o_w

    out = lax.map(query_block, jnp.arange(s // Q_BLOCK))
    return out.transpose(1, 0, 2, 3, 4, 5).reshape(bsz, s, NSA_HEADS * HEAD_DIM)


def causal_depthwise_conv(u, w, b):
    ch = u.shape[-1]
    out = lax.conv_general_dilated(u, w[:, None, :].astype(u.dtype), window_strides=(1,),
                                   padding=[(CONV_WIDTH - 1, 0)],
                                   dimension_numbers=('NWC', 'WIO', 'NWC'),
                                   feature_group_count=ch)
    return out + b


def ssd_chunked_scan(x, dt, a, bmat, cmat):
    f32 = jnp.float32
    bsz, s, nh, p = x.shape
    ng, ns = bmat.shape[2], bmat.shape[3]
    nj = nh // ng
    pad = (-s) % SSM_CHUNK
    sp = s + pad
    nc = sp // SSM_CHUNK
    ln = SSM_CHUNK

    def padc(u):
        return jnp.pad(u, [(0, 0), (0, pad)] + [(0, 0)] * (u.ndim - 2))

    def chunks(u):
        return jnp.moveaxis(u.reshape((bsz, nc, ln) + u.shape[2:]), 1, 0)

    xc = chunks(padc(x.astype(f32)).reshape(bsz, sp, ng, nj, p))
    dtc = chunks(padc(dt).reshape(bsz, sp, ng, nj))
    bc = chunks(padc(bmat.astype(f32)))
    cc = chunks(padc(cmat.astype(f32)))
    a_gj = a.reshape(ng, nj)
    causal = jnp.tril(jnp.ones((ln, ln), bool))[None, :, :, None, None]

    def step(state, inp):
        xk, dtk, bk, ck = inp
        acs = jnp.cumsum(dtk * a_gj, axis=1)
        seg = acs[:, :, None] - acs[:, None, :]
        decay = jnp.exp(jnp.where(causal, seg, -jnp.inf))
        cb = jnp.einsum('blgn,bsgn->blsg', ck, bk)
        m = cb[..., None] * decay * dtk[:, None]
        y = jnp.einsum('blsgj,bsgjp->blgjp', m, xk)
        y = y + jnp.einsum('blgn,bgjpn->blgjp', ck, state) * jnp.exp(acs)[..., None]
        last = acs[:, -1]
        w = jnp.exp(last[:, None] - acs) * dtk
        state = state * jnp.exp(last)[..., None, None] + jnp.einsum('bsgj,bsgn,bsgjp->bgjpn', w, bk, xk)
        return state, y

    state0 = jnp.zeros((bsz, ng, nj, p, ns), f32)
    _, ys = lax.scan(step, state0, (xc, dtc, bc, cc))
    ys = jnp.moveaxis(ys, 0, 1).reshape(bsz, sp, nh, p)
    return ys[:, :s]


def mamba2_ssd(z, xbc, dt_raw, conv_w, conv_b, dt_bias, a_log, d_skip, norm_w):
    bsz, s, _ = z.shape
    f32 = jnp.float32
    xbc = jax.nn.silu(causal_depthwise_conv(xbc, conv_w, conv_b))
    xs, bm, cm = jnp.split(xbc, [D_SSM, D_SSM + SSM_GROUPS * SSM_STATE], axis=-1)
    xs = xs.reshape(bsz, s, SSM_HEADS, SSM_HEAD_DIM)
    bm = bm.reshape(bsz, s, SSM_GROUPS, SSM_STATE)
    cm = cm.reshape(bsz, s, SSM_GROUPS, SSM_STATE)
    dt = jax.nn.softplus(dt_raw.astype(f32) + dt_bias.astype(f32))
    a = -jnp.exp(a_log.astype(f32))
    y = ssd_chunked_scan(xs, dt, a, bm, cm)
    y = y + d_skip.astype(f32)[:, None] * xs.astype(f32)
    gw = D_SSM // SSM_GROUPS
    y = y.reshape(bsz, s, SSM_GROUPS, gw) * jax.nn.silu(z.astype(f32)).reshape(bsz, s, SSM_GROUPS, gw)
    y = y * lax.rsqrt(jnp.mean(y * y, axis=-1, keepdims=True) + RMS_EPS)
    return (y.reshape(bsz, s, D_SSM) * norm_w.astype(f32)).astype(z.dtype)


def hybrid_mixer(h, w_in, cmp_pe, cmp_w1, cmp_w2, rel_bias, attn_out_norm,
                 conv_w, conv_b, dt_bias, a_log, d_skip, ssm_norm, w_out):
    proj = h @ w_in
    cuts = [int(v) for v in np.cumsum([N_Q, N_KV, N_GATE, N_Z, N_XBC])]
    q, kv, gate_logits, z, xbc, dt_raw = jnp.split(proj, cuts, axis=-1)
    attn = rms_norm(nsa_attention(q, kv, gate_logits, cmp_pe, cmp_w1, cmp_w2, rel_bias), attn_out_norm)
    ssm = mamba2_ssd(z, xbc, dt_raw, conv_w, conv_b, dt_bias, a_log, d_skip, ssm_norm)
    return jnp.concatenate([attn, ssm], axis=-1) @ w_out


def peer_ffn(h, wq, subkeys, u_tab, v_tab):
    bsz, s, d = h.shape
    f32 = jnp.float32
    q = (h @ wq).reshape(bsz, s, PEER_HEADS, 2, PEER_KEY_DIM // 2)
    s1 = jnp.einsum('bshd,kd->bshk', q[:, :, :, 0], subkeys[0]).astype(f32)
    s2 = jnp.einsum('bshd,kd->bshk', q[:, :, :, 1], subkeys[1]).astype(f32)
    v1, i1 = lax.top_k(s1, PEER_TOPK)
    v2, i2 = lax.top_k(s2, PEER_TOPK)
    n_cand = PEER_TOPK * PEER_TOPK
    cand = (v1[..., :, None] + v2[..., None, :]).reshape(bsz, s, PEER_HEADS, n_cand)
    cidx = (i1[..., :, None] * PEER_NKEYS + i2[..., None, :]).reshape(bsz, s, PEER_HEADS, n_cand)
    best, pos = lax.top_k(cand, PEER_TOPK)
    eidx = jnp.take_along_axis(cidx, pos, axis=-1)
    gate = jax.nn.softmax(best, axis=-1).astype(h.dtype)
    n_blk = bsz * s // PEER_TOKEN_BLOCK
    hb = h.reshape(n_blk, PEER_TOKEN_BLOCK, d)
    eb = eidx.reshape(n_blk, PEER_TOKEN_BLOCK, PEER_HEADS * PEER_TOPK)
    gb = gate.reshape(n_blk, PEER_TOKEN_BLOCK, PEER_HEADS * PEER_TOPK)

    def expert_block(args):
        hx, ex, gx = args
        act = jax.nn.gelu(jnp.einsum('td,tkd->tk', hx, u_tab[ex]))
        return jnp.einsum('tk,tkd->td', act * gx, v_tab[ex])

    return lax.map(expert_block, (hb, eb, gb)).reshape(bsz, s, d)


def setup_inputs(seed: int = 0) -> dict:
    key = jax.random.key(seed)
    ks = jax.random.split(key, 24)
    f32 = jnp.float32

    def nrm(k, shape, sd):
        return jax.random.normal(k, shape, f32) * sd

    dt0 = jnp.exp(jax.random.uniform(ks[14], (DEPTH, SSM_HEADS), f32) * (math.log(0.1) - math.log(0.001)) + math.log(0.001))
    return {
        'x': nrm(ks[0], (BATCH, SEQ, D_MODEL), 1.0),
        'c': nrm(ks[1], (BATCH, D_MODEL), 1.0),
        'ada_w': nrm(ks[2], (DEPTH, D_MODEL, 6 * D_MODEL), 0.5 * D_MODEL ** -0.5),
        'ada_b': nrm(ks[3], (DEPTH, 6 * D_MODEL), 0.01),
        'norm_mix': 1.0 + nrm(ks[4], (DEPTH, D_MODEL), 0.05),
        'norm_ffn': 1.0 + nrm(ks[5], (DEPTH, D_MODEL), 0.05),
        'w_in': nrm(ks[6], (DEPTH, D_MODEL, N_IN), D_MODEL ** -0.5),
        'cmp_pe': nrm(ks[7], (DEPTH, 2, CMP_BLOCK, HEAD_DIM), 0.1),
        'cmp_w1': nrm(ks[8], (DEPTH, 2, CMP_BLOCK * HEAD_DIM, CMP_HIDDEN), (CMP_BLOCK * HEAD_DIM) ** -0.5),
        'cmp_w2': nrm(ks[9], (DEPTH, 2, CMP_HIDDEN, HEAD_DIM), CMP_HIDDEN ** -0.5),
        'rel_bias': nrm(ks[10], (REL_BUCKETS, NSA_HEADS), 0.5),
        'attn_out_norm': 1.0 + nrm(ks[11], (DEPTH, D_ATTN), 0.05),
        'conv_w': nrm(ks[12], (DEPTH, CONV_WIDTH, CONV_CH), CONV_WIDTH ** -0.5),
        'conv_b': nrm(ks[13], (DEPTH, CONV_CH), 0.01),
        'dt_bias': dt0 + jnp.log(-jnp.expm1(-dt0)),
        'a_log': jnp.log(jax.random.uniform(ks[15], (DEPTH, SSM_HEADS), f32, 1.0, 16.0)),
        'd_skip': 1.0 + nrm(ks[16], (DEPTH, SSM_HEADS), 0.1),
        'ssm_norm': 1.0 + nrm(ks[17], (DEPTH, D_SSM), 0.05),
        'w_out': nrm(ks[18], (DEPTH, D_MODEL, D_MODEL), D_MODEL ** -0.5),
        'peer_wq': nrm(ks[19], (DEPTH, D_MODEL, PEER_HEADS * PEER_KEY_DIM), D_MODEL ** -0.5),
        'peer_subkeys': nrm(ks[20], (DEPTH, 2, PEER_NKEYS, PEER_KEY_DIM // 2), (PEER_KEY_DIM // 2) ** -0.5),
        'peer_u': nrm(ks[21], (DEPTH, PEER_EXPERTS, D_MODEL), D_MODEL ** -0.5),
        'peer_v': nrm(ks[22], (DEPTH, PEER_EXPERTS, D_MODEL), PEER_HEADS ** -0.5),
        'norm_final': 1.0 + nrm(ks[23], (D_MODEL,), 0.05),
    }


def reference(x, c, ada_w, ada_b, norm_mix, norm_ffn, w_in, cmp_pe, cmp_w1, cmp_w2, rel_bias,
              attn_out_norm, conv_w, conv_b, dt_bias, a_log, d_skip, ssm_norm, w_out,
              peer_wq, peer_subkeys, peer_u, peer_v, norm_final):
    cond = jax.nn.silu(c)
    for l in range(DEPTH):
        mod = (cond @ ada_w[l] + ada_b[l])[:, None, :]
        sh1, sc1, g1, sh2, sc2, g2 = jnp.split(mod, 6, axis=-1)
        h = rms_norm(x, norm_mix[l]) * (1.0 + sc1) + sh1
        x = x + g1 * hybrid_mixer(h, w_in[l], cmp_pe[l], cmp_w1[l], cmp_w2[l], rel_bias, attn_out_norm[l],
                                  conv_w[l], conv_b[l], dt_bias[l], a_log[l], d_skip[l], ssm_norm[l], w_out[l])
        h = rms_norm(x, norm_ffn[l]) * (1.0 + sc2) + sh2
        x = x + g2 * peer_ffn(h, peer_wq[l], peer_subkeys[l], peer_u[l], peer_v[l])
    return rms_norm(x, norm_final)
```

```python
import functools
import math

import numpy as np
import jax
import jax.numpy as jnp
from jax import lax
from jax.experimental import pallas as pl
from jax.experimental.pallas import tpu as pltpu

F32 = jnp.float32
BF16 = jnp.bfloat16
HI = lax.Precision.HIGHEST

N_HEADS = 16
N_KV = 2
N_GRP = N_HEADS // N_KV
HEAD_DIM = 64
CMP_BLOCK = 32
CMP_STRIDE = 16
CMP_HIDDEN = 4 * HEAD_DIM
SLC_BLOCK = 64
SLC_TOPN = 16
WINDOW = 512
Q_BLOCK = 128
FORCE_SCORE = 1e4
NEG_INF = -1e30
REL_BUCKETS = 32
REL_MAX_DIST = 2048
SSM_HEAD_DIM = 64
SSM_GROUPS = 2
SSM_STATE = 128
CONV_WIDTH = 4
SSM_CHUNK = 256
PEER_HEADS = 8
PEER_NKEYS = 128
PEER_TOPK = 16
RMS_EPS = 1e-6

LANES = 128
SUBLANES = 8
VMEM_LIMIT = 56 * 1024 * 1024

SLC_KEY_TILE = 512
SLC_TILE_BLOCKS = SLC_KEY_TILE // SLC_BLOCK
SLC_GROUP_BLOCKS = LANES
SLC_NEAR_CHUNKS = 14
SLC_TAB_MASKED = 14
SLC_TAB_CONST = 15
MASK_BIG = 2.0 ** 100

CMP_NEAR = 128
CMP_PAD = CMP_NEAR - Q_BLOCK // CMP_STRIDE


def _cparams(sem, vmem=VMEM_LIMIT):
    return pltpu.CompilerParams(dimension_semantics=sem, vmem_limit_bytes=vmem)


def _nt(a, b, precision=None):
    return lax.dot_general(a, b, (((1,), (1,)), ((), ())), precision=precision,
                           preferred_element_type=F32)


def _silu(x):
    return x * jax.nn.sigmoid(x)


def _rel_bucket_np(d):
    d = np.maximum(np.asarray(d, np.int64), 0)
    max_exact = REL_BUCKETS // 2
    ratio = np.log(np.maximum(d, max_exact).astype(np.float64) / max_exact) / math.log(REL_MAX_DIST / max_exact)
    scaled = ratio * (REL_BUCKETS - max_exact)
    large = max_exact + np.floor(scaled).astype(np.int64)
    return np.where(d < max_exact, d, np.minimum(large, REL_BUCKETS - 1)).astype(np.int32)


def _ada_kernel(c_ref, w_ref, b_ref, o_ref):
    cond = _silu(c_ref[...])
    o_ref[0] = jnp.dot(cond, w_ref[0], precision=HI, preferred_element_type=F32) + b_ref[0]


def _ada_mod(c, ada_w, ada_b):
    depth, d, n = ada_w.shape
    bsz = c.shape[0]
    rows = SUBLANES
    c_pad = jnp.zeros((rows, d), F32).at[:bsz].set(c)
    tn = 1024
    out = pl.pallas_call(
        _ada_kernel,
        grid=(depth, n // tn),
        in_specs=[pl.BlockSpec((rows, d), lambda l, j: (0, 0)),
                  pl.BlockSpec((1, d, tn), lambda l, j: (l, 0, j)),
                  pl.BlockSpec((1, 1, tn), lambda l, j: (l, 0, j))],
        out_specs=pl.BlockSpec((1, rows, tn), lambda l, j: (l, 0, j)),
        out_shape=jax.ShapeDtypeStruct((depth, rows, n), F32),
        compiler_params=_cparams(("parallel", "parallel")),
        name="ada_mod",
    )(c_pad, ada_w, ada_b.reshape(depth, 1, n))
    return out[:, :bsz]


def _norm_proj_kernel(x_ref, nw_ref, sc_ref, sh_ref, w_ref, o_ref, h_ref, hs_ref):
    @pl.when(pl.program_id(2) == 0)
    def _():
        x = x_ref[0]
        y = x * lax.rsqrt(jnp.mean(x * x, axis=-1, keepdims=True) + RMS_EPS)
        h = (y * nw_ref[...]) * (1.0 + sc_ref[0]) + sh_ref[0]
        hs_ref[...] = h.astype(BF16)
        h_ref[0] = h.astype(BF16)

    o_ref[0] = jnp.dot(hs_ref[...], w_ref[...], preferred_element_type=F32)


def _norm_proj(x, nw, sc, sh, w_bf16, tm=512, tn=512):
    bsz, s, d = x.shape
    n = w_bf16.shape[1]
    return pl.pallas_call(
        _norm_proj_kernel,
        grid=(bsz, s // tm, n // tn),
        in_specs=[pl.BlockSpec((1, tm, d), lambda b, i, j: (b, i, 0)),
                  pl.BlockSpec((1, d), lambda b, i, j: (0, 0)),
                  pl.BlockSpec((1, 1, d), lambda b, i, j: (b, 0, 0)),
                  pl.BlockSpec((1, 1, d), lambda b, i, j: (b, 0, 0)),
                  pl.BlockSpec((d, tn), lambda b, i, j: (0, j))],
        out_specs=[pl.BlockSpec((1, tm, tn), lambda b, i, j: (b, i, j)),
                   pl.BlockSpec((1, tm, d), lambda b, i, j: (b, i, 0))],
        out_shape=[jax.ShapeDtypeStruct((bsz, s, n), F32),
                   jax.ShapeDtypeStruct((bsz, s, d), BF16)],
        scratch_shapes=[pltpu.VMEM((tm, d), BF16)],
        compiler_params=_cparams(("parallel", "parallel", "arbitrary")),
        name="norm_proj",
    )(x, nw.reshape(1, d), sc, sh, w_bf16)


def _bias_kernel(rel_ref, bk_ref, o_ref):
    h = pl.program_id(0)
    bk = bk_ref[...]
    acc = jnp.full(bk.shape, NEG_INF, F32)
    for b in range(REL_BUCKETS):
        acc = jnp.where(bk == b, rel_ref[b, h], acc)
    o_ref[0] = acc


def _bias_tables(rel_bias):
    r = np.arange(Q_BLOCK)[:, None]
    dw = r - np.arange(WINDOW + Q_BLOCK)[None, :] + WINDOW
    win = np.where((dw >= 0) & (dw < WINDOW), _rel_bucket_np(dw), -1)
    off = CMP_STRIDE * CMP_PAD - (CMP_BLOCK - 1)
    dc = r + off - CMP_STRIDE * np.arange(CMP_NEAR)[None, :]
    cmp_near = np.where(dc >= 0, _rel_bucket_np(dc), -1)
    chunks = []
    for m in range(SLC_NEAR_CHUNKS):
        ds_ = Q_BLOCK * m + r - np.arange(Q_BLOCK)[None, :]
        chunks.append(np.where(ds_ >= 0, _rel_bucket_np(ds_), -1))
    assert _rel_bucket_np(Q_BLOCK * SLC_NEAR_CHUNKS - (Q_BLOCK - 1)) == REL_BUCKETS - 1
    assert _rel_bucket_np(off + CMP_STRIDE) == REL_BUCKETS - 1
    chunks.append(np.full((Q_BLOCK, Q_BLOCK), -1))
    chunks.append(np.full((Q_BLOCK, Q_BLOCK), REL_BUCKETS - 1))
    bk = np.concatenate([win, cmp_near] + chunks, axis=1).astype(np.int32)
    cols = bk.shape[1]
    out = pl.pallas_call(
        _bias_kernel,
        grid=(N_HEADS,),
        in_specs=[pl.BlockSpec(memory_space=pltpu.SMEM),
                  pl.BlockSpec((Q_BLOCK, cols), lambda h: (0, 0))],
        out_specs=pl.BlockSpec((1, Q_BLOCK, cols), lambda h: (h, 0, 0)),
        out_shape=jax.ShapeDtypeStruct((N_HEADS, Q_BLOCK, cols), F32),
        compiler_params=_cparams(("arbitrary",)),
        name="bias_tables",
    )(rel_bias, jnp.asarray(bk))
    nw = WINDOW + Q_BLOCK
    tab_win = out[:, :, :nw].reshape(N_KV, N_GRP, Q_BLOCK, nw)
    tab_cmp = out[:, :, nw:nw + CMP_NEAR].reshape(N_KV, N_GRP, Q_BLOCK, CMP_NEAR)
    nch = SLC_NEAR_CHUNKS + 2
    tab_slc = out[:, :, nw + CMP_NEAR:].reshape(N_KV, N_GRP, Q_BLOCK, nch, Q_BLOCK)
    tab_slc = tab_slc.transpose(0, 3, 1, 2, 4)
    far = rel_bias[REL_BUCKETS - 1].reshape(N_KV, N_GRP, 1)
    far_col = jnp.broadcast_to(far, (N_KV, N_GRP, Q_BLOCK)).reshape(N_KV, N_GRP * Q_BLOCK, 1)
    return tab_win, tab_cmp, tab_slc, far_col


def _compress_kernel(a_ref, pe_ref, w1_ref, w2_ref, o_ref):
    a = a_ref[0, 0, 0]
    half = a.shape[1]
    lo = jnp.dot((a + pe_ref[0, 0:1, :]).astype(BF16), w1_ref[0, :half, :].astype(BF16),
                 preferred_element_type=F32)
    hi = jnp.dot((a + pe_ref[0, 1:2, :]).astype(BF16), w1_ref[0, half:, :].astype(BF16),
                 preferred_element_type=F32)
    nc = a.shape[0]
    hid = jax.nn.gelu(lo + pltpu.roll(hi, nc - 1, axis=0))
    out = jnp.dot(hid.astype(BF16), w2_ref[0].astype(BF16), preferred_element_type=F32)
    o_ref[0, 0, 0] = jnp.zeros(o_ref.shape[3:], F32)
    o_ref[0, 0, 0, CMP_PAD:CMP_PAD + nc, :] = out


def _compress(kv_cmp, cmp_pe, cmp_w1, cmp_w2):
    bsz, _, _, nc, half = kv_cmp.shape
    pe = cmp_pe.reshape(2, 2, half)
    rows = CMP_PAD + nc + SUBLANES
    return pl.pallas_call(
        _compress_kernel,
        grid=(bsz, 2, N_KV),
        in_specs=[pl.BlockSpec((1, 1, 1, nc, half), lambda b, w, k: (b, w, k, 0, 0)),
                  pl.BlockSpec((1, 2, half), lambda b, w, k: (w, 0, 0)),
                  pl.BlockSpec((1, 2 * half, CMP_HIDDEN), lambda b, w, k: (w, 0, 0)),
                  pl.BlockSpec((1, CMP_HIDDEN, HEAD_DIM), lambda b, w, k: (w, 0, 0))],
        out_specs=pl.BlockSpec((1, 1, 1, rows, HEAD_DIM), lambda b, w, k: (b, w, k, 0, 0)),
        out_shape=jax.ShapeDtypeStruct((bsz, 2, N_KV, rows, HEAD_DIM), F32),
        compiler_params=_cparams(("parallel", "parallel", "parallel")),
        name="nsa_compress",
    )(kv_cmp, pe, cmp_w1, cmp_w2)


def _topk_mark(vals, k, axis):
    n = vals.shape[axis]
    iota = lax.broadcasted_iota(jnp.int32, vals.shape, axis).astype(F32)
    rank = jnp.full(vals.shape, float(k), F32)
    work = vals
    picked = []
    for r in range(k):
        m = jnp.max(work, axis=axis, keepdims=True)
        ix = jnp.min(jnp.where(work == m, iota, float(n)), axis=axis, keepdims=True)
        hit = iota == ix
        rank = jnp.where(hit, float(r), rank)
        work = jnp.where(hit, -jnp.inf, work)
        picked.append(m)
    return rank, picked


def _cmp_attn_kernel(q_ref, k_ref, v_ref, tab_ref, far_ref, m_ref, o_ref, sel_ref):
    i = pl.program_id(2)
    rows = N_GRP * Q_BLOCK
    nc = k_ref.shape[3] - CMP_PAD - SUBLANES
    nbp = sel_ref.shape[3]
    q = q_ref[0, 0].reshape(rows, HEAD_DIM)
    start = pl.multiple_of(i * (Q_BLOCK // CMP_STRIDE), SUBLANES)

    k_far = k_ref[0, 0, 0, CMP_PAD:CMP_PAD + nc, :].astype(BF16)
    v_far = v_ref[0, 0, 0, CMP_PAD:CMP_PAD + nc, :].astype(BF16)
    k_near = k_ref[0, 0, 0, pl.ds(start, CMP_NEAR), :].astype(BF16)
    v_near = v_ref[0, 0, 0, pl.ds(start, CMP_NEAR), :].astype(BF16)

    n_idx = lax.broadcasted_iota(jnp.int32, (1, nc), 1)
    mask_far = n_idx < (i * (Q_BLOCK // CMP_STRIDE) - CMP_PAD)
    s_far = jnp.where(mask_far, _nt(q, k_far) + far_ref[0], NEG_INF)
    tab = tab_ref[0].reshape(rows, CMP_NEAR)
    c_idx = lax.broadcasted_iota(jnp.int32, (1, CMP_NEAR), 1)
    mask_near = (c_idx >= (CMP_PAD - i * (Q_BLOCK // CMP_STRIDE))) & (tab > 0.5 * NEG_INF)
    s_near = jnp.where(mask_near, _nt(q, k_near) + tab, NEG_INF)

    m = jnp.maximum(jnp.max(s_far, axis=-1, keepdims=True), jnp.max(s_near, axis=-1, keepdims=True))
    e_far = jnp.exp(s_far - m)
    e_near = jnp.exp(s_near - m)
    l = jnp.sum(e_far, axis=-1, keepdims=True) + jnp.sum(e_near, axis=-1, keepdims=True)
    p_far = jnp.where(mask_far, e_far / l, 0.0)
    p_near = jnp.where(mask_near, e_near / l, 0.0)
    o = (jnp.dot(p_far.astype(BF16), v_far, preferred_element_type=F32)
         + jnp.dot(p_near.astype(BF16), v_near, preferred_element_type=F32))
    o_ref[0, 0] = o.reshape(N_GRP, Q_BLOCK, HEAD_DIM)

    ps_far = jnp.sum(p_far.reshape(N_GRP, Q_BLOCK, nc), axis=0)
    ps_near = jnp.sum(p_near.reshape(N_GRP, Q_BLOCK, CMP_NEAR), axis=0)
    imp = (jnp.dot(ps_far, m_ref[CMP_PAD:CMP_PAD + nc, :], precision=HI, preferred_element_type=F32)
           + jnp.dot(ps_near, m_ref[pl.ds(start, CMP_NEAR), :], precision=HI, preferred_element_type=F32))
    t = i * Q_BLOCK + lax.broadcasted_iota(jnp.int32, (Q_BLOCK, 1), 0)
    cur = t // SLC_BLOCK
    blk = lax.broadcasted_iota(jnp.int32, (1, nbp), 1)
    forced = (blk == 0) | (blk == cur) | (blk == cur - 1)
    imp = jnp.where(forced, FORCE_SCORE, jnp.where(blk <= cur, imp, -FORCE_SCORE))
    n_blocks = (nc * CMP_STRIDE) // SLC_BLOCK
    imp = jnp.where(blk < n_blocks, imp, -jnp.inf)
    rank, _ = _topk_mark(imp, min(SLC_TOPN, n_blocks), axis=1)
    sel_ref[0, 0] = jnp.where(rank < float(SLC_TOPN), 1.0, 0.0).astype(BF16)


def _overlap_matrix(nc, nbp):
    n_cmp = nc - 1
    n_slc = nc * CMP_STRIDE // SLC_BLOCK
    j = np.arange(n_slc)
    lo = np.clip((j * SLC_BLOCK - CMP_BLOCK) // CMP_STRIDE + 1, 0, n_cmp)
    hi = np.clip(-((-(j * SLC_BLOCK + SLC_BLOCK)) // CMP_STRIDE), 0, n_cmp)
    m = np.zeros((CMP_PAD + nc + SUBLANES, nbp), np.float32)
    n = np.arange(nc)[:, None]
    m[CMP_PAD:CMP_PAD + nc, :n_slc] = (n >= lo[None, :]) & (n < hi[None, :])
    return m


def _cmp_attn(q64, kv_c, tab_cmp, far_col, nbp):
    bsz, _, _, s, _ = q64.shape
    rows_c = kv_c.shape[3]
    nc = rows_c - CMP_PAD - SUBLANES
    nq = s // Q_BLOCK
    m_pad = jnp.asarray(_overlap_matrix(nc, nbp))
    return pl.pallas_call(
        _cmp_attn_kernel,
        grid=(bsz, N_KV, nq),
        in_specs=[pl.BlockSpec((1, 1, N_GRP, Q_BLOCK, HEAD_DIM), lambda b, k, i: (b, k, 0, i, 0)),
                  pl.BlockSpec((1, 1, 1, rows_c, HEAD_DIM), lambda b, k, i: (b, 0, k, 0, 0)),
                  pl.BlockSpec((1, 1, 1, rows_c, HEAD_DIM), lambda b, k, i: (b, 1, k, 0, 0)),
                  pl.BlockSpec((1, N_GRP, Q_BLOCK, CMP_NEAR), lambda b, k, i: (k, 0, 0, 0)),
                  pl.BlockSpec((1, N_GRP * Q_BLOCK, 1), lambda b, k, i: (k, 0, 0)),
                  pl.BlockSpec((rows_c, nbp), lambda b, k, i: (0, 0))],
        out_specs=[pl.BlockSpec((1, 1, N_GRP, Q_BLOCK, HEAD_DIM), lambda b, k, i: (b, k, 0, i, 0)),
                   pl.BlockSpec((1, 1, Q_BLOCK, nbp), lambda b, k, i: (b, k, i, 0))],
        out_shape=[jax.ShapeDtypeStruct((bsz, N_KV, N_GRP, s, HEAD_DIM), F32),
                   jax.ShapeDtypeStruct((bsz, N_KV, s, nbp), BF16)],
        compiler_params=_cparams(("parallel", "parallel", "arbitrary")),
        name="nsa_cmp_attn",
    )(q64, kv_c, kv_c, tab_cmp, far_col, m_pad)


def _slc_attn_kernel(q_ref, ka_ref, v_ref, sel_ref, tab_ref, o_ref, qa_ref):
    i = pl.program_id(2)
    rows = N_GRP * Q_BLOCK
    ngroups = qa_ref.shape[0]
    q = q_ref[0, 0].reshape(rows, LANES)
    selneg = ((sel_ref[0, 0].astype(F32) - 1.0) * MASK_BIG).astype(BF16)
    for g in range(ngroups):
        part = selneg[:, g * SLC_GROUP_BLOCKS:(g + 1) * SLC_GROUP_BLOCKS]
        part = jnp.broadcast_to(part[None], (N_GRP, Q_BLOCK, SLC_GROUP_BLOCKS)).reshape(rows, SLC_GROUP_BLOCKS)
        qa_ref[g] = jnp.concatenate([q, part], axis=1)

    tiles_per_group = SLC_GROUP_BLOCKS // SLC_TILE_BLOCKS
    sub = SLC_KEY_TILE // Q_BLOCK

    def body(kt, carry):
        m, l, acc = carry
        ks = pl.multiple_of(kt * SLC_KEY_TILE, SLC_KEY_TILE)
        qa = qa_ref[kt // tiles_per_group]
        s = _nt(qa, ka_ref[0, 0, pl.ds(ks, SLC_KEY_TILE), :])
        chunks = []
        for a in range(sub):
            mm = i - sub * kt - a
            idx = jnp.where(mm < 0, SLC_TAB_MASKED, jnp.where(mm >= SLC_NEAR_CHUNKS, SLC_TAB_CONST, mm))
            chunks.append(tab_ref[0, idx])
        s = s + jnp.concatenate(chunks, axis=-1).reshape(rows, SLC_KEY_TILE)
        m_new = jnp.maximum(m, jnp.max(s, axis=-1, keepdims=True))
        alpha = jnp.exp(m - m_new)
        p = jnp.exp(s - m_new)
        l = alpha * l + jnp.sum(p, axis=-1, keepdims=True)
        acc = alpha * acc + jnp.dot(p.astype(BF16), v_ref[0, 0, pl.ds(ks, SLC_KEY_TILE), :],
                                    preferred_element_type=F32)
        return m_new, l, acc

    init = (jnp.full((rows, 1), -jnp.inf, F32), jnp.zeros((rows, 1), F32), jnp.zeros((rows, HEAD_DIM), F32))
    _, l, acc = lax.fori_loop(0, i // sub + 1, body, init)
    o_ref[0, 0] = (acc / l).reshape(N_GRP, Q_BLOCK, HEAD_DIM)


def _slc_attn(q128, k_aug, v_slc, sel, tab_slc):
    bsz, _, _, s, _ = q128.shape
    nq = s // Q_BLOCK
    nbp = sel.shape[3]
    ngroups = nbp // SLC_GROUP_BLOCKS
    nch = tab_slc.shape[1]
    once = pl.Buffered(1)
    return pl.pallas_call(
        _slc_attn_kernel,
        grid=(bsz, N_KV, nq),
        in_specs=[pl.BlockSpec((1, 1, N_GRP, Q_BLOCK, LANES), lambda b, k, i: (b, k, 0, i, 0)),
                  pl.BlockSpec((1, 1, s, 2 * LANES), lambda b, k, i: (b, k, 0, 0), pipeline_mode=once),
                  pl.BlockSpec((1, 1, s, HEAD_DIM), lambda b, k, i: (b, k, 0, 0), pipeline_mode=once),
                  pl.BlockSpec((1, 1, Q_BLOCK, nbp), lambda b, k, i: (b, k, i, 0)),
                  pl.BlockSpec((1, nch, N_GRP, Q_BLOCK, Q_BLOCK), lambda b, k, i: (k, 0, 0, 0, 0),
                               pipeline_mode=once)],
        out_specs=pl.BlockSpec((1, 1, N_GRP, Q_BLOCK, HEAD_DIM), lambda b, k, i: (b, k, 0, i, 0)),
        out_shape=jax.ShapeDtypeStruct((bsz, N_KV, N_GRP, s, HEAD_DIM), F32),
        scratch_shapes=[pltpu.VMEM((ngroups, N_GRP * Q_BLOCK, 2 * LANES), BF16)],
        compiler_params=_cparams(("parallel", "parallel", "arbitrary")),
        name="nsa_slc_attn",
    )(q128, k_aug, v_slc, sel, tab_slc)


def _win_attn_kernel(q_ref, k_ref, v_ref, tab_ref, o_ref):
    i = pl.program_id(2)
    rows = N_GRP * Q_BLOCK
    nw = WINDOW + Q_BLOCK
    q = q_ref[0, 0].reshape(rows, HEAD_DIM)
    qs = pl.multiple_of(i * Q_BLOCK, Q_BLOCK)
    k = k_ref[0, 0, pl.ds(qs, nw), :]
    v = v_ref[0, 0, pl.ds(qs, nw), :]
    s = _nt(q, k) + tab_ref[0].reshape(rows, nw)
    col = lax.broadcasted_iota(jnp.int32, (1, nw), 1)
    s = jnp.where(col >= WINDOW - i * Q_BLOCK, s, NEG_INF)
    m = jnp.max(s, axis=-1, keepdims=True)
    e = jnp.exp(s - m)
    p = e / jnp.sum(e, axis=-1, keepdims=True)
    o = jnp.dot(p.astype(BF16), v, preferred_element_type=F32)
    o_ref[0, 0] = o.reshape(N_GRP, Q_BLOCK, HEAD_DIM)


def _win_attn(q64, k_win, v_win, tab_win):
    bsz, _, _, s, _ = q64.shape
    nq = s // Q_BLOCK
    sp = k_win.shape[2]
    nw = WINDOW + Q_BLOCK
    return pl.pallas_call(
        _win_attn_kernel,
        grid=(bsz, N_KV, nq),
        in_specs=[pl.BlockSpec((1, 1, N_GRP, Q_BLOCK, HEAD_DIM), lambda b, k, i: (b, k, 0, i, 0)),
                  pl.BlockSpec((1, 1, sp, HEAD_DIM), lambda b, k, i: (b, k, 0, 0)),
                  pl.BlockSpec((1, 1, sp, HEAD_DIM), lambda b, k, i: (b, k, 0, 0)),
                  pl.BlockSpec((1, N_GRP, Q_BLOCK, nw), lambda b, k, i: (k, 0, 0, 0))],
        out_specs=pl.BlockSpec((1, 1, N_GRP, Q_BLOCK, HEAD_DIM), lambda b, k, i: (b, k, 0, i, 0)),
        out_shape=jax.ShapeDtypeStruct((bsz, N_KV, N_GRP, s, HEAD_DIM), F32),
        compiler_params=_cparams(("parallel", "parallel", "arbitrary")),
        name="nsa_win_attn",
    )(q64, k_win, v_win, tab_win)


def _ssd_kernel(xbc_ref, z_ref, dt_ref, cw_ref, cb_ref, dtb_ref, alog_ref, dskip_ref, nw_ref, o_ref,
                ext_ref, state_ref):
    ln = SSM_CHUNK
    d_ssm = z_ref.shape[2]
    nh = dt_ref.shape[2]
    gw = d_ssm // SSM_GROUPS
    hpg = nh // SSM_GROUPS
    gn = SSM_GROUPS * SSM_STATE

    @pl.when(pl.program_id(1) == 0)
    def _():
        ext_ref[0:SUBLANES, :] = jnp.zeros((SUBLANES, ext_ref.shape[1]), F32)
        state_ref[...] = jnp.zeros(state_ref.shape, F32)

    ext_ref[SUBLANES:SUBLANES + ln, :] = xbc_ref[0]
    conv = cw_ref[0:1, :] * ext_ref[SUBLANES - CONV_WIDTH + 1:SUBLANES - CONV_WIDTH + 1 + ln, :]
    for k in range(1, CONV_WIDTH):
        lo = SUBLANES - CONV_WIDTH + 1 + k
        conv = conv + cw_ref[k:k + 1, :] * ext_ref[lo:lo + ln, :]
    conv = conv + cb_ref[...]
    ext_ref[0:SUBLANES, :] = xbc_ref[0, ln - SUBLANES:ln, :]
    xc = _silu(conv)
    xs = xc[:, :d_ssm]
    bm = xc[:, d_ssm:d_ssm + gn]
    cm = xc[:, d_ssm + gn:d_ssm + 2 * gn]

    xdt = dt_ref[0] + dtb_ref[...]
    dt = jnp.maximum(xdt, 0.0) + jnp.log1p(jnp.exp(-jnp.abs(xdt)))
    a = -jnp.exp(alog_ref[...])
    da = dt * a

    row = lax.broadcasted_iota(jnp.int32, (ln, ln), 0)
    colm = lax.broadcasted_iota(jnp.int32, (ln, ln), 1)
    causal = row >= colm
    acs = jnp.dot(causal.astype(F32), da, precision=HI, preferred_element_type=F32)
    eye = (lax.broadcasted_iota(jnp.int32, (2 * nh, 2 * nh), 0)
           == lax.broadcasted_iota(jnp.int32, (2 * nh, 2 * nh), 1)).astype(F32)
    rows_t = _nt(eye, jnp.concatenate([acs, dt], axis=1), precision=HI)
    expand = (lax.broadcasted_iota(jnp.int32, (nh, d_ssm), 0)
              == lax.broadcasted_iota(jnp.int32, (nh, d_ssm), 1) // SSM_HEAD_DIM).astype(F32)
    last = acs[ln - 1:ln, :]
    exp_acs_x = jnp.dot(jnp.exp(acs), expand, precision=HI, preferred_element_type=F32)
    w_x = jnp.dot(jnp.exp(last - acs) * dt, expand, precision=HI, preferred_element_type=F32)
    exp_last_x = exp_acs_x[ln - 1:ln, :]

    ys = []
    for g in range(SSM_GROUPS):
        cg = cm[:, g * SSM_STATE:(g + 1) * SSM_STATE].astype(BF16)
        bg32 = bm[:, g * SSM_STATE:(g + 1) * SSM_STATE]
        bg = bg32.astype(BF16)
        xg = xs[:, g * gw:(g + 1) * gw]
        cb = _nt(cg, bg)
        st = state_ref[g]
        y_state = jnp.dot(cg, st.astype(BF16), preferred_element_type=F32) * exp_acs_x[:, g * gw:(g + 1) * gw]
        y_heads = []
        for j in range(hpg):
            h = g * hpg + j
            seg = acs[:, h:h + 1] - rows_t[h:h + 1, :]
            decay = jnp.exp(jnp.where(causal, seg, -jnp.inf))
            mmat = cb * decay * rows_t[nh + h:nh + h + 1, :]
            xh = xg[:, j * SSM_HEAD_DIM:(j + 1) * SSM_HEAD_DIM].astype(BF16)
            y_heads.append(jnp.dot(mmat.astype(BF16), xh, preferred_element_type=F32))
        ys.append(jnp.concatenate(y_heads, axis=1) + y_state)
        xw = (xg * w_x[:, g * gw:(g + 1) * gw]).astype(BF16)
        state_ref[g] = st * exp_last_x[:, g * gw:(g + 1) * gw] + jnp.dot(
            bg32.T.astype(BF16), xw, preferred_element_type=F32)

    y = jnp.concatenate(ys, axis=1) + dskip_ref[...] * xs
    y = y * _silu(z_ref[0])
    outs = []
    for g in range(SSM_GROUPS):
        yg = y[:, g * gw:(g + 1) * gw]
        outs.append(yg * lax.rsqrt(jnp.mean(yg * yg, axis=-1, keepdims=True) + RMS_EPS))
    o_ref[0] = jnp.concatenate(outs, axis=1) * nw_ref[...]


def _ssd(xbc, z, dt_raw, conv_w, conv_b, dt_bias, a_log, d_skip, norm_w):
    bsz, s, ch = xbc.shape
    d_ssm = z.shape[2]
    nh = dt_raw.shape[2]
    gw = d_ssm // SSM_GROUPS
    nchunks = s // SSM_CHUNK
    full = lambda shape: pl.BlockSpec(shape, lambda b, c: (0,) * len(shape))
    return pl.pallas_call(
        _ssd_kernel,
        grid=(bsz, nchunks),
        in_specs=[pl.BlockSpec((1, SSM_CHUNK, ch), lambda b, c: (b, c, 0)),
                  pl.BlockSpec((1, SSM_CHUNK, d_ssm), lambda b, c: (b, c, 0)),
                  pl.BlockSpec((1, SSM_CHUNK, nh), lambda b, c: (b, c, 0)),
                  full((CONV_WIDTH, ch)), full((1, ch)), full((1, nh)), full((1, nh)),
                  full((1, d_ssm)), full((1, d_ssm))],
        out_specs=pl.BlockSpec((1, SSM_CHUNK, d_ssm), lambda b, c: (b, c, 0)),
        out_shape=jax.ShapeDtypeStruct((bsz, s, d_ssm), F32),
        scratch_shapes=[pltpu.VMEM((SUBLANES + SSM_CHUNK, ch), F32),
                        pltpu.VMEM((SSM_GROUPS, SSM_STATE, gw), F32)],
        compiler_params=_cparams(("parallel", "arbitrary")),
        name="ssd_scan",
    )(xbc, z, dt_raw, conv_w, conv_b.reshape(1, ch), dt_bias.reshape(1, nh), a_log.reshape(1, nh),
      jnp.repeat(d_skip, SSM_HEAD_DIM).reshape(1, d_ssm), norm_w.reshape(1, d_ssm))


def _mix_out_kernel(oc_ref, os_ref, ow_ref, gl_ref, ex_ref, an_ref, ssm_ref, w_ref, x_ref, g1_ref, o_ref):
    d_attn = oc_ref.shape[2]
    sig = jax.nn.sigmoid(gl_ref[0])
    gc = jnp.dot(sig, ex_ref[0], precision=HI, preferred_element_type=F32)
    gs = jnp.dot(sig, ex_ref[1], precision=HI, preferred_element_type=F32)
    gw = jnp.dot(sig, ex_ref[2], precision=HI, preferred_element_type=F32)
    attn = gc * oc_ref[0] + gs * os_ref[0] + gw * ow_ref[0]
    attn = attn * lax.rsqrt(jnp.mean(attn * attn, axis=-1, keepdims=True) + RMS_EPS) * an_ref[...]
    mix = (jnp.dot(attn.astype(BF16), w_ref[:d_attn, :], preferred_element_type=F32)
           + jnp.dot(ssm_ref[0].astype(BF16), w_ref[d_attn:, :], preferred_element_type=F32))
    o_ref[0] = x_ref[0] + g1_ref[0] * mix


def _gate_expand():
    ex = np.zeros((3, LANES, N_HEADS * HEAD_DIM), np.float32)
    for r in range(3):
        for h in range(N_HEADS):
            ex[r, h * 3 + r, h * HEAD_DIM:(h + 1) * HEAD_DIM] = 1.0
    return ex


def _mix_out(oc, os_, ow, gl, attn_norm, ssm, w_out_bf16, x, g1, tm=256):
    bsz, s, d = x.shape
    d_attn = oc.shape[2]
    d_ssm = ssm.shape[2]
    tok = lambda w: pl.BlockSpec((1, tm, w), lambda b, i: (b, i, 0))
    return pl.pallas_call(
        _mix_out_kernel,
        grid=(bsz, s // tm),
        in_specs=[tok(d_attn), tok(d_attn), tok(d_attn), tok(LANES),
                  pl.BlockSpec((3, LANES, d_attn), lambda b, i: (0, 0, 0)),
                  pl.BlockSpec((1, d_attn), lambda b, i: (0, 0)),
                  tok(d_ssm),
                  pl.BlockSpec((d_attn + d_ssm, d), lambda b, i: (0, 0)),
                  tok(d),
                  pl.BlockSpec((1, 1, d), lambda b, i: (b, 0, 0))],
        out_specs=tok(d),
        out_shape=jax.ShapeDtypeStruct((bsz, s, d), F32),
        compiler_params=_cparams(("parallel", "parallel")),
        name="mix_out",
    )(oc, os_, ow, gl, jnp.asarray(_gate_expand()), attn_norm.reshape(1, d_attn), ssm, w_out_bf16, x, g1)


def _peer_route_kernel(q_ref, sk_ref, rank2_ref, e2_ref, n1_ref, c1_ref):
    q = q_ref[...]
    half = q.shape[1] // 2
    s1 = _nt(sk_ref[0], q[:, :half], precision=HI)
    s2 = _nt(sk_ref[1], q[:, half:], precision=HI)
    k = PEER_TOPK
    rank1, v1 = _topk_mark(s1, k, axis=0)
    rank2, v2 = _topk_mark(s2, k, axis=0)
    v2_all = jnp.concatenate(v2, axis=0)
    cand = jnp.concatenate([v1[a] + v2_all for a in range(k)], axis=0)
    rank_c, best = _topk_mark(cand, k, axis=0)
    chosen = jnp.where(rank_c < float(k), 1.0, 0.0).reshape(k, k, cand.shape[1])
    count = jnp.sum(chosen, axis=1)
    z = best[0] * 0.0
    for r in range(k):
        z = z + jnp.exp(best[r] - best[0])
    n1 = jnp.zeros(s1.shape, F32)
    for a in range(k):
        n1 = jnp.where(rank1 == float(a), count[a:a + 1, :], n1)
    rank2_ref[0] = rank2
    e2_ref[0] = jnp.exp(s2 - v2[0])
    n1_ref[0] = n1
    c1_ref[0] = jnp.exp(s1 - v1[0]) / z


def _peer_route(q, subkeys, tm=256):
    t, width = q.shape
    kd = width // PEER_HEADS
    nk = subkeys.shape[1]
    spec = pl.BlockSpec((1, nk, tm), lambda i, h: (h, 0, i))
    shape = jax.ShapeDtypeStruct((PEER_HEADS, nk, t), F32)
    return pl.pallas_call(
        _peer_route_kernel,
        grid=(t // tm, PEER_HEADS),
        in_specs=[pl.BlockSpec((tm, kd), lambda i, h: (i, h)),
                  pl.BlockSpec((2, nk, kd // 2), lambda i, h: (0, 0, 0))],
        out_specs=[spec, spec, spec, spec],
        out_shape=[shape, shape, shape, shape],
        compiler_params=_cparams(("parallel", "parallel")),
        name="peer_route",
    )(q, subkeys)


def _peer_dense_kernel(ht_ref, u_ref, vt_ref, rank2_ref, e2_ref, n1_ref, c1_ref, o_ref):
    j = pl.program_id(1)
    te = u_ref.shape[0]
    nk = rank2_ref.shape[1]

    @pl.when(j == 0)
    def _():
        o_ref[...] = jnp.zeros(o_ref.shape, F32)

    act = jax.nn.gelu(jnp.dot(u_ref[...], ht_ref[...], preferred_element_type=F32))
    parts = []
    for c in range(te // nk):
        i1 = j * (te // nk) + c
        w = jnp.zeros((nk, act.shape[1]), F32)
        for h in range(PEER_HEADS):
            n_row = n1_ref[h, pl.ds(i1, 1), :]
            c_row = c1_ref[h, pl.ds(i1, 1), :]
            w = w + jnp.where(rank2_ref[h] < n_row, e2_ref[h], 0.0) * c_row
        parts.append((act[c * nk:(c + 1) * nk] * w).astype(BF16))
    o_ref[...] += jnp.dot(vt_ref[...], jnp.concatenate(parts, axis=0), preferred_element_type=F32)


def _peer_dense(h_t, u_bf16, v_t_bf16, rank2, e2, n1, c1, tm=512, te=512):
    d, t = h_t.shape
    n_exp = u_bf16.shape[0]
    nk = rank2.shape[1]
    route = pl.BlockSpec((PEER_HEADS, nk, tm), lambda i, j: (0, 0, i))
    return pl.pallas_call(
        _peer_dense_kernel,
        grid=(t // tm, n_exp // te),
        in_specs=[pl.BlockSpec((d, tm), lambda i, j: (0, i)),
                  pl.BlockSpec((te, d), lambda i, j: (j, 0)),
                  pl.BlockSpec((d, te), lambda i, j: (0, j)),
                  route, route, route, route],
        out_specs=pl.BlockSpec((d, tm), lambda i, j: (0, i)),
        out_shape=jax.ShapeDtypeStruct((d, t), F32),
        compiler_params=_cparams(("parallel", "arbitrary")),
        name="peer_dense",
    )(h_t, u_bf16, v_t_bf16, rank2, e2, n1, c1)


def _residual_kernel(x_ref, y_ref, g_ref, o_ref):
    o_ref[0] = x_ref[0] + g_ref[0] * y_ref[0]


def _residual(x, y, g, tm=512):
    bsz, s, d = x.shape
    tok = pl.BlockSpec((1, tm, d), lambda b, i: (b, i, 0))
    return pl.pallas_call(
        _residual_kernel,
        grid=(bsz, s // tm),
        in_specs=[tok, tok, pl.BlockSpec((1, 1, d), lambda b, i: (b, 0, 0))],
        out_specs=tok,
        out_shape=jax.ShapeDtypeStruct((bsz, s, d), F32),
        compiler_params=_cparams(("parallel", "parallel")),
        name="residual",
    )(x, y, g)


def _final_norm_kernel(x_ref, w_ref, o_ref):
    x = x_ref[0]
    o_ref[0] = x * lax.rsqrt(jnp.mean(x * x, axis=-1, keepdims=True) + RMS_EPS) * w_ref[...]


def _final_norm(x, w, tm=512):
    bsz, s, d = x.shape
    tok = pl.BlockSpec((1, tm, d), lambda b, i: (b, i, 0))
    return pl.pallas_call(
        _final_norm_kernel,
        grid=(bsz, s // tm),
        in_specs=[tok, pl.BlockSpec((1, d), lambda b, i: (0, 0))],
        out_specs=tok,
        out_shape=jax.ShapeDtypeStruct((bsz, s, d), F32),
        compiler_params=_cparams(("parallel", "parallel")),
        name="final_norm",
    )(x, w.reshape(1, d))


def _pad_cols(w, width):
    return jnp.pad(w, ((0, 0), (0, width - w.shape[1])))


def _nsa(q, kv, tables, cmp_pe, cmp_w1, cmp_w2):
    tab_win, tab_cmp, tab_slc, far_col = tables
    bsz, s, _ = q.shape
    nc = s // CMP_STRIDE
    n_slc = s // SLC_BLOCK
    nbp = -(-n_slc // SLC_GROUP_BLOCKS) * SLC_GROUP_BLOCKS
    scale = HEAD_DIM ** -0.5
    qh = (q * scale).astype(BF16).reshape(bsz, s, N_KV, N_GRP, HEAD_DIM).transpose(0, 2, 3, 1, 4)
    q128 = jnp.pad(qh, ((0, 0),) * 4 + ((0, LANES - HEAD_DIM),))
    kv6 = kv.reshape(bsz, s, 6, N_KV, HEAD_DIM)

    kv_cmp = kv6[:, :, 0:2].reshape(bsz, nc, CMP_STRIDE, 2, N_KV, HEAD_DIM)
    kv_cmp = kv_cmp.transpose(0, 3, 4, 1, 2, 5).reshape(bsz, 2, N_KV, nc, CMP_STRIDE * HEAD_DIM)
    kv_c = _compress(kv_cmp, cmp_pe, cmp_w1, cmp_w2)
    o_c, sel = _cmp_attn(qh, kv_c, tab_cmp, far_col, nbp)

    k_slc = kv6[:, :, 2].transpose(0, 2, 1, 3).astype(BF16)
    v_slc = kv6[:, :, 3].transpose(0, 2, 1, 3).astype(BF16)
    blk = np.arange(s) // SLC_BLOCK
    onehot = (blk[:, None] % SLC_GROUP_BLOCKS == np.arange(SLC_GROUP_BLOCKS)[None, :]).astype(np.float32)
    k_aug = jnp.concatenate([
        k_slc, jnp.zeros((bsz, N_KV, s, LANES - HEAD_DIM), BF16),
        jnp.broadcast_to(jnp.asarray(onehot, BF16), (bsz, N_KV, s, SLC_GROUP_BLOCKS))], axis=-1)
    o_s = _slc_attn(q128, k_aug, v_slc, sel, tab_slc)

    front = ((0, 0), (0, 0), (WINDOW, 0), (0, 0))
    k_win = jnp.pad(kv6[:, :, 4].transpose(0, 2, 1, 3).astype(BF16), front)
    v_win = jnp.pad(kv6[:, :, 5].transpose(0, 2, 1, 3).astype(BF16), front)
    o_w = _win_attn(qh, k_win, v_win, tab_win)

    back = lambda o: o.transpose(0, 3, 1, 2, 4).reshape(bsz, s, N_HEADS * HEAD_DIM)
    return back(o_c), back(o_s), back(o_w)


def kernel(x, c, ada_w, ada_b, norm_mix, norm_ffn, w_in, cmp_pe, cmp_w1, cmp_w2, rel_bias, attn_out_norm,
           conv_w, conv_b, dt_bias, a_log, d_skip, ssm_norm, w_out, peer_wq, peer_subkeys, peer_u, peer_v,
           norm_final):
    bsz, s, d = x.shape
    depth = ada_w.shape[0]
    d_attn = N_HEADS * HEAD_DIM
    n_kv = 6 * N_KV * HEAD_DIM
    n_gate = 3 * N_HEADS
    d_ssm = ssm_norm.shape[1]
    ch = conv_w.shape[2]
    nh = dt_bias.shape[1]

    mod = _ada_mod(c, ada_w, ada_b)
    tables = _bias_tables(rel_bias)

    for l in range(depth):
        sh1, sc1, g1, sh2, sc2, g2 = [mod[l, :, i * d:(i + 1) * d].reshape(bsz, 1, d) for i in range(6)]
        cuts = np.cumsum([0, d_attn, n_kv, n_gate, d_ssm, ch, nh])
        seg = [w_in[l][:, cuts[i]:cuts[i + 1]] for i in range(6)]
        seg[2] = _pad_cols(seg[2], LANES)
        seg[5] = _pad_cols(seg[5], LANES)
        w_cat = jnp.concatenate(seg, axis=1).astype(BF16)
        pc = np.cumsum([0] + [w.shape[1] for w in seg])
        proj, _ = _norm_proj(x, norm_mix[l], sc1, sh1, w_cat)
        q = proj[:, :, pc[0]:pc[1]]
        kv = proj[:, :, pc[1]:pc[2]]
        gl = proj[:, :, pc[2]:pc[3]]
        z = proj[:, :, pc[3]:pc[4]]
        xbc = proj[:, :, pc[4]:pc[5]]
        dt_raw = proj[:, :, pc[5]:pc[5] + nh]

        o_c, o_s, o_w = _nsa(q, kv, tables, cmp_pe[l], cmp_w1[l], cmp_w2[l])
        ssm = _ssd(xbc, z, dt_raw, conv_w[l], conv_b[l], dt_bias[l], a_log[l], d_skip[l], ssm_norm[l])
        x = _mix_out(o_c, o_s, o_w, gl, attn_out_norm[l], ssm, w_out[l].astype(BF16), x, g1)

        pq, h2 = _norm_proj(x, norm_ffn[l], sc2, sh2, peer_wq[l].astype(BF16))
        rank2, e2, n1, c1 = _peer_route(pq.reshape(bsz * s, -1), peer_subkeys[l])
        h_t = h2.reshape(bsz * s, d).T
        ffn_t = _peer_dense(h_t, peer_u[l].astype(BF16), peer_v[l].T.astype(BF16), rank2, e2, n1, c1)
        x = _residual(x, ffn_t.T.reshape(bsz, s, d), g2)

    return _final_norm(x, norm_final)
```

```python
import functools
import math

import numpy as np
import jax
import jax.numpy as jnp
from jax import lax
from jax.experimental import pallas as pl
from jax.experimental.pallas import tpu as pltpu

F32 = jnp.float32
BF16 = jnp.bfloat16
HI = lax.Precision.HIGHEST

N_HEADS = 16
N_KV = 2
N_GRP = N_HEADS // N_KV
HEAD_DIM = 64
CMP_BLOCK = 32
CMP_STRIDE = 16
CMP_HIDDEN = 4 * HEAD_DIM
SLC_BLOCK = 64
SLC_TOPN = 16
WINDOW = 512
Q_BLOCK = 128
FORCE_SCORE = 1e4
NEG_INF = -1e30
REL_BUCKETS = 32
REL_MAX_DIST = 2048
SSM_HEAD_DIM = 64
SSM_GROUPS = 2
SSM_STATE = 128
CONV_WIDTH = 4
SSM_CHUNK = 256
PEER_HEADS = 8
PEER_NKEYS = 128
PEER_TOPK = 16
RMS_EPS = 1e-6

LANES = 128
SUBLANES = 8
VMEM_LIMIT = 56 * 1024 * 1024

SLC_KEY_TILE = 512
SLC_TILE_BLOCKS = SLC_KEY_TILE // SLC_BLOCK
SLC_GROUP_BLOCKS = LANES
SLC_NEAR_CHUNKS = 14
SLC_TAB_MASKED = 14
SLC_TAB_CONST = 15
MASK_BIG = 2.0 ** 100
SLC_LANE_STRIP = 256
SLC_V_ROWS = HEAD_DIM + 16
SLC_BIAS_PIECES = 3

CMP_NEAR = 128
CMP_PAD = CMP_NEAR - Q_BLOCK // CMP_STRIDE


def _cparams(sem, vmem=VMEM_LIMIT):
    return pltpu.CompilerParams(dimension_semantics=sem, vmem_limit_bytes=vmem)


def _nt(a, b, precision=None):
    return lax.dot_general(a, b, (((1,), (1,)), ((), ())), precision=precision,
                           preferred_element_type=F32)


def _silu(x):
    return x * jax.nn.sigmoid(x)


def _rel_bucket_np(d):
    d = np.maximum(np.asarray(d, np.int64), 0)
    max_exact = REL_BUCKETS // 2
    ratio = np.log(np.maximum(d, max_exact).astype(np.float64) / max_exact) / math.log(REL_MAX_DIST / max_exact)
    scaled = ratio * (REL_BUCKETS - max_exact)
    large = max_exact + np.floor(scaled).astype(np.int64)
    return np.where(d < max_exact, d, np.minimum(large, REL_BUCKETS - 1)).astype(np.int32)


def _ada_kernel(c_ref, w_ref, b_ref, o_ref):
    cond = _silu(c_ref[...])
    o_ref[0] = jnp.dot(cond, w_ref[0], precision=HI, preferred_element_type=F32) + b_ref[0]


def _ada_mod(c, ada_w, ada_b):
    depth, d, n = ada_w.shape
    bsz = c.shape[0]
    rows = SUBLANES
    c_pad = jnp.zeros((rows, d), F32).at[:bsz].set(c)
    tn = 1024
    out = pl.pallas_call(
        _ada_kernel,
        grid=(depth, n // tn),
        in_specs=[pl.BlockSpec((rows, d), lambda l, j: (0, 0)),
                  pl.BlockSpec((1, d, tn), lambda l, j: (l, 0, j)),
                  pl.BlockSpec((1, 1, tn), lambda l, j: (l, 0, j))],
        out_specs=pl.BlockSpec((1, rows, tn), lambda l, j: (l, 0, j)),
        out_shape=jax.ShapeDtypeStruct((depth, rows, n), F32),
        compiler_params=_cparams(("parallel", "parallel")),
        name="ada_mod",
    )(c_pad, ada_w, ada_b.reshape(depth, 1, n))
    return out[:, :bsz]


def _norm_proj_kernel(x_ref, nw_ref, sc_ref, sh_ref, w_ref, o_ref, h_ref, hs_ref):
    @pl.when(pl.program_id(2) == 0)
    def _():
        x = x_ref[0]
        y = x * lax.rsqrt(jnp.mean(x * x, axis=-1, keepdims=True) + RMS_EPS)
        h = (y * nw_ref[...]) * (1.0 + sc_ref[0]) + sh_ref[0]
        hs_ref[...] = h.astype(BF16)
        h_ref[0] = h.astype(BF16)

    o_ref[0] = jnp.dot(hs_ref[...], w_ref[...], preferred_element_type=F32)


def _norm_proj(x, nw, sc, sh, w_bf16, tm=512, tn=512):
    bsz, s, d = x.shape
    n = w_bf16.shape[1]
    return pl.pallas_call(
        _norm_proj_kernel,
        grid=(bsz, s // tm, n // tn),
        in_specs=[pl.BlockSpec((1, tm, d), lambda b, i, j: (b, i, 0)),
                  pl.BlockSpec((1, d), lambda b, i, j: (0, 0)),
                  pl.BlockSpec((1, 1, d), lambda b, i, j: (b, 0, 0)),
                  pl.BlockSpec((1, 1, d), lambda b, i, j: (b, 0, 0)),
                  pl.BlockSpec((d, tn), lambda b, i, j: (0, j))],
        out_specs=[pl.BlockSpec((1, tm, tn), lambda b, i, j: (b, i, j)),
                   pl.BlockSpec((1, tm, d), lambda b, i, j: (b, i, 0))],
        out_shape=[jax.ShapeDtypeStruct((bsz, s, n), F32),
                   jax.ShapeDtypeStruct((bsz, s, d), BF16)],
        scratch_shapes=[pltpu.VMEM((tm, d), BF16)],
        compiler_params=_cparams(("parallel", "parallel", "arbitrary")),
        name="norm_proj",
    )(x, nw.reshape(1, d), sc, sh, w_bf16)


def _bias_kernel(rel_ref, bk_ref, o_ref):
    h = pl.program_id(0)
    bk = bk_ref[...]
    acc = jnp.full(bk.shape, NEG_INF, F32)
    for b in range(REL_BUCKETS):
        acc = jnp.where(bk == b, rel_ref[b, h], acc)
    o_ref[0] = acc


def _bias_tables(rel_bias):
    r = np.arange(Q_BLOCK)[:, None]
    dw = r - np.arange(WINDOW + Q_BLOCK)[None, :] + WINDOW
    win = np.where((dw >= 0) & (dw < WINDOW), _rel_bucket_np(dw), -1)
    off = CMP_STRIDE * CMP_PAD - (CMP_BLOCK - 1)
    dc = r + off - CMP_STRIDE * np.arange(CMP_NEAR)[None, :]
    cmp_near = np.where(dc >= 0, _rel_bucket_np(dc), -1)
    chunks = []
    for m in range(SLC_NEAR_CHUNKS):
        ds_ = Q_BLOCK * m + r - np.arange(Q_BLOCK)[None, :]
        chunks.append(np.where(ds_ >= 0, _rel_bucket_np(ds_), -1))
    assert _rel_bucket_np(Q_BLOCK * SLC_NEAR_CHUNKS - (Q_BLOCK - 1)) == REL_BUCKETS - 1
    assert _rel_bucket_np(off + CMP_STRIDE) == REL_BUCKETS - 1
    chunks.append(np.full((Q_BLOCK, Q_BLOCK), -1))
    chunks.append(np.full((Q_BLOCK, Q_BLOCK), REL_BUCKETS - 1))
    bk = np.concatenate([win, cmp_near] + chunks, axis=1).astype(np.int32)
    cols = bk.shape[1]
    out = pl.pallas_call(
        _bias_kernel,
        grid=(N_HEADS,),
        in_specs=[pl.BlockSpec(memory_space=pltpu.SMEM),
                  pl.BlockSpec((Q_BLOCK, cols), lambda h: (0, 0))],
        out_specs=pl.BlockSpec((1, Q_BLOCK, cols), lambda h: (h, 0, 0)),
        out_shape=jax.ShapeDtypeStruct((N_HEADS, Q_BLOCK, cols), F32),
        compiler_params=_cparams(("arbitrary",)),
        name="bias_tables",
    )(rel_bias, jnp.asarray(bk))
    nw = WINDOW + Q_BLOCK
    tab_win = out[:, :, :nw].reshape(N_KV, N_GRP, Q_BLOCK, nw)
    tab_cmp = out[:, :, nw:nw + CMP_NEAR].reshape(N_KV, N_GRP, Q_BLOCK, CMP_NEAR)
    nch = SLC_NEAR_CHUNKS + 2
    tab_slc = out[:, :, nw + CMP_NEAR:].reshape(N_KV, N_GRP, Q_BLOCK, nch, Q_BLOCK)
    tab_slc = tab_slc.transpose(0, 3, 4, 1, 2).reshape(N_KV, nch, Q_BLOCK, N_GRP * Q_BLOCK)
    far = rel_bias[REL_BUCKETS - 1].reshape(N_KV, N_GRP, 1)
    far_col = jnp.broadcast_to(far, (N_KV, N_GRP, Q_BLOCK)).reshape(N_KV, N_GRP * Q_BLOCK, 1)
    far_row = far_col.reshape(N_KV, 1, N_GRP * Q_BLOCK)
    pieces, rest = [], far_row
    for _ in range(SLC_BIAS_PIECES):
        piece = rest.astype(BF16)
        pieces.append(piece)
        rest = rest - piece.astype(F32)
    zeros = lambda w: jnp.zeros((N_KV, w, N_GRP * Q_BLOCK), BF16)
    far_t = jnp.concatenate([zeros(HEAD_DIM)] + pieces + [zeros(LANES - HEAD_DIM - SLC_BIAS_PIECES)], axis=1)
    return tab_win, tab_cmp, tab_slc, far_col, far_t


def _compress_kernel(a_ref, pe_ref, w1_ref, w2_ref, o_ref):
    a = a_ref[0, 0, 0]
    half = a.shape[1]
    lo = jnp.dot((a + pe_ref[0, 0:1, :]).astype(BF16), w1_ref[0, :half, :].astype(BF16),
                 preferred_element_type=F32)
    hi = jnp.dot((a + pe_ref[0, 1:2, :]).astype(BF16), w1_ref[0, half:, :].astype(BF16),
                 preferred_element_type=F32)
    nc = a.shape[0]
    hid = jax.nn.gelu(lo + pltpu.roll(hi, nc - 1, axis=0))
    out = jnp.dot(hid.astype(BF16), w2_ref[0].astype(BF16), preferred_element_type=F32)
    o_ref[0, 0, 0] = jnp.zeros(o_ref.shape[3:], F32)
    o_ref[0, 0, 0, CMP_PAD:CMP_PAD + nc, :] = out


def _compress(kv_cmp, cmp_pe, cmp_w1, cmp_w2):
    bsz, _, _, nc, half = kv_cmp.shape
    pe = cmp_pe.reshape(2, 2, half)
    rows = CMP_PAD + nc + SUBLANES
    return pl.pallas_call(
        _compress_kernel,
        grid=(bsz, 2, N_KV),
        in_specs=[pl.BlockSpec((1, 1, 1, nc, half), lambda b, w, k: (b, w, k, 0, 0)),
                  pl.BlockSpec((1, 2, half), lambda b, w, k: (w, 0, 0)),
                  pl.BlockSpec((1, 2 * half, CMP_HIDDEN), lambda b, w, k: (w, 0, 0)),
                  pl.BlockSpec((1, CMP_HIDDEN, HEAD_DIM), lambda b, w, k: (w, 0, 0))],
        out_specs=pl.BlockSpec((1, 1, 1, rows, HEAD_DIM), lambda b, w, k: (b, w, k, 0, 0)),
        out_shape=jax.ShapeDtypeStruct((bsz, 2, N_KV, rows, HEAD_DIM), F32),
        compiler_params=_cparams(("parallel", "parallel", "parallel")),
        name="nsa_compress",
    )(kv_cmp, pe, cmp_w1, cmp_w2)


def _topk_mark(vals, k, axis):
    n = vals.shape[axis]
    iota = lax.broadcasted_iota(jnp.int32, vals.shape, axis).astype(F32)
    rank = jnp.full(vals.shape, float(k), F32)
    work = vals
    picked = []
    for r in range(k):
        m = jnp.max(work, axis=axis, keepdims=True)
        ix = jnp.min(jnp.where(work == m, iota, float(n)), axis=axis, keepdims=True)
        hit = iota == ix
        rank = jnp.where(hit, float(r), rank)
        work = jnp.where(hit, -jnp.inf, work)
        picked.append(m)
    return rank, picked


def _cmp_attn_kernel(q_ref, k_ref, v_ref, tab_ref, far_ref, m_ref, o_ref, sel_ref):
    i = pl.program_id(2)
    rows = N_GRP * Q_BLOCK
    nc = k_ref.shape[3] - CMP_PAD - SUBLANES
    nbp = sel_ref.shape[3]
    q = q_ref[0, 0].reshape(rows, HEAD_DIM)
    start = pl.multiple_of(i * (Q_BLOCK // CMP_STRIDE), SUBLANES)

    k_far = k_ref[0, 0, 0, CMP_PAD:CMP_PAD + nc, :].astype(BF16)
    v_far = v_ref[0, 0, 0, CMP_PAD:CMP_PAD + nc, :].astype(BF16)
    k_near = k_ref[0, 0, 0, pl.ds(start, CMP_NEAR), :].astype(BF16)
    v_near = v_ref[0, 0, 0, pl.ds(start, CMP_NEAR), :].astype(BF16)

    n_idx = lax.broadcasted_iota(jnp.int32, (1, nc), 1)
    mask_far = n_idx < (i * (Q_BLOCK // CMP_STRIDE) - CMP_PAD)
    s_far = jnp.where(mask_far, _nt(q, k_far) + far_ref[0], NEG_INF)
    tab = tab_ref[0].reshape(rows, CMP_NEAR)
    c_idx = lax.broadcasted_iota(jnp.int32, (1, CMP_NEAR), 1)
    mask_near = (c_idx >= (CMP_PAD - i * (Q_BLOCK // CMP_STRIDE))) & (tab > 0.5 * NEG_INF)
    s_near = jnp.where(mask_near, _nt(q, k_near) + tab, NEG_INF)

    m = jnp.maximum(jnp.max(s_far, axis=-1, keepdims=True), jnp.max(s_near, axis=-1, keepdims=True))
    e_far = jnp.exp(s_far - m)
    e_near = jnp.exp(s_near - m)
    l = jnp.sum(e_far, axis=-1, keepdims=True) + jnp.sum(e_near, axis=-1, keepdims=True)
    p_far = jnp.where(mask_far, e_far / l, 0.0)
    p_near = jnp.where(mask_near, e_near / l, 0.0)
    o = (jnp.dot(p_far.astype(BF16), v_far, preferred_element_type=F32)
         + jnp.dot(p_near.astype(BF16), v_near, preferred_element_type=F32))
    o_ref[0, 0] = o.reshape(N_GRP, Q_BLOCK, HEAD_DIM)

    ps_far = jnp.sum(p_far.reshape(N_GRP, Q_BLOCK, nc), axis=0)
    ps_near = jnp.sum(p_near.reshape(N_GRP, Q_BLOCK, CMP_NEAR), axis=0)
    imp = (jnp.dot(ps_far, m_ref[CMP_PAD:CMP_PAD + nc, :], precision=HI, preferred_element_type=F32)
           + jnp.dot(ps_near, m_ref[pl.ds(start, CMP_NEAR), :], precision=HI, preferred_element_type=F32))
    t = i * Q_BLOCK + lax.broadcasted_iota(jnp.int32, (Q_BLOCK, 1), 0)
    cur = t // SLC_BLOCK
    blk = lax.broadcasted_iota(jnp.int32, (1, nbp), 1)
    forced = (blk == 0) | (blk == cur) | (blk == cur - 1)
    imp = jnp.where(forced, FORCE_SCORE, jnp.where(blk <= cur, imp, -FORCE_SCORE))
    n_blocks = (nc * CMP_STRIDE) // SLC_BLOCK
    imp = jnp.where(blk < n_blocks, imp, -jnp.inf)
    rank, _ = _topk_mark(imp, min(SLC_TOPN, n_blocks), axis=1)
    sel_ref[0, 0, 0] = jnp.where(rank < float(SLC_TOPN), 1.0, 0.0).T.astype(BF16)


def _overlap_matrix(nc, nbp):
    n_cmp = nc - 1
    n_slc = nc * CMP_STRIDE // SLC_BLOCK
    j = np.arange(n_slc)
    lo = np.clip((j * SLC_BLOCK - CMP_BLOCK) // CMP_STRIDE + 1, 0, n_cmp)
    hi = np.clip(-((-(j * SLC_BLOCK + SLC_BLOCK)) // CMP_STRIDE), 0, n_cmp)
    m = np.zeros((CMP_PAD + nc + SUBLANES, nbp), np.float32)
    n = np.arange(nc)[:, None]
    m[CMP_PAD:CMP_PAD + nc, :n_slc] = (n >= lo[None, :]) & (n < hi[None, :])
    return m


def _cmp_attn(q64, kv_c, tab_cmp, far_col, nbp):
    bsz, _, _, s, _ = q64.shape
    rows_c = kv_c.shape[3]
    nc = rows_c - CMP_PAD - SUBLANES
    nq = s // Q_BLOCK
    m_pad = jnp.asarray(_overlap_matrix(nc, nbp))
    return pl.pallas_call(
        _cmp_attn_kernel,
        grid=(bsz, N_KV, nq),
        in_specs=[pl.BlockSpec((1, 1, N_GRP, Q_BLOCK, HEAD_DIM), lambda b, k, i: (b, k, 0, i, 0)),
                  pl.BlockSpec((1, 1, 1, rows_c, HEAD_DIM), lambda b, k, i: (b, 0, k, 0, 0)),
                  pl.BlockSpec((1, 1, 1, rows_c, HEAD_DIM), lambda b, k, i: (b, 1, k, 0, 0)),
                  pl.BlockSpec((1, N_GRP, Q_BLOCK, CMP_NEAR), lambda b, k, i: (k, 0, 0, 0)),
                  pl.BlockSpec((1, N_GRP * Q_BLOCK, 1), lambda b, k, i: (k, 0, 0)),
                  pl.BlockSpec((rows_c, nbp), lambda b, k, i: (0, 0))],
        out_specs=[pl.BlockSpec((1, 1, N_GRP, Q_BLOCK, HEAD_DIM), lambda b, k, i: (b, k, 0, i, 0)),
                   pl.BlockSpec((1, 1, 1, nbp, Q_BLOCK), lambda b, k, i: (b, k, i, 0, 0))],
        out_shape=[jax.ShapeDtypeStruct((bsz, N_KV, N_GRP, s, HEAD_DIM), F32),
                   jax.ShapeDtypeStruct((bsz, N_KV, nq, nbp, Q_BLOCK), BF16)],
        compiler_params=_cparams(("parallel", "parallel", "arbitrary")),
        name="nsa_cmp_attn",
    )(q64, kv_c, kv_c, tab_cmp, far_col, m_pad)


def _slc_attn_kernel(qt_ref, far_ref, ka_ref, vt_ref, selt_ref, tab_ref, o_ref,
                     qa_ref, s0_ref, s1_ref, x0_ref, x1_ref, m_ref, acc_ref):
    i = pl.program_id(2)
    cols = N_GRP * Q_BLOCK
    ngroups = qa_ref.shape[0] // 2
    qt = qt_ref[0, 0, 0]
    qt_far = qt + far_ref[0]
    selneg = ((selt_ref[0, 0, 0].astype(F32) - 1.0) * MASK_BIG).astype(BF16)
    for g in range(ngroups):
        part = selneg[g * SLC_GROUP_BLOCKS:(g + 1) * SLC_GROUP_BLOCKS, :]
        part = jnp.concatenate([part] * N_GRP, axis=1)
        qa_ref[2 * g] = jnp.concatenate([qt_far, part], axis=0)
        qa_ref[2 * g + 1] = jnp.concatenate([qt, part], axis=0)

    tiles_per_group = SLC_GROUP_BLOCKS // SLC_TILE_BLOCKS
    sub = SLC_KEY_TILE // Q_BLOCK
    last_tile = vt_ref.shape[2] - 1
    n_pairs = (i // sub + 2) // 2
    n_far = jnp.maximum(0, (i - (SLC_NEAR_CHUNKS - 1)) // sub)
    far_pairs = jnp.maximum(0, (n_far - 1) // 2)
    first_table = jnp.where(n_far > 0, 2 * far_pairs + 1, 0)

    def produce(kt, ls, s_ref, mx_ref, with_table):
        near = (kt >= first_table).astype(jnp.int32)
        kc = jnp.minimum(kt, last_tile)
        ks = pl.multiple_of(kc * SLC_KEY_TILE, SLC_KEY_TILE)
        s = jnp.dot(ka_ref[0, 0, pl.ds(ks, SLC_KEY_TILE), :], qa_ref[2 * (kc // tiles_per_group) + near, :, ls],
                    preferred_element_type=F32)
        if with_table:
            chunks = []
            for a in range(sub):
                mm = i - sub * kt - a
                idx = jnp.where(mm < 0, SLC_TAB_MASKED, jnp.where(mm >= SLC_NEAR_CHUNKS, SLC_TAB_CONST, mm))
                chunks.append(tab_ref[0, idx, :, ls])
            s = s + jnp.concatenate(chunks, axis=0)
        s_ref[:, ls] = s
        mx_ref[:, ls] = jnp.max(s, axis=0, keepdims=True)

    def consume(kt, ls, s_ref, mx_ref):
        m_old = m_ref[:, ls]
        m_new = jnp.maximum(m_old, mx_ref[:, ls])
        m_ref[:, ls] = m_new
        p = jnp.exp(s_ref[:, ls] - m_new).astype(BF16)
        acc_ref[:, ls] = jnp.exp(m_old - m_new) * acc_ref[:, ls] + jnp.dot(
            vt_ref[0, 0, jnp.minimum(kt, last_tile)], p, preferred_element_type=F32)

    strips = [slice(c * SLC_LANE_STRIP, (c + 1) * SLC_LANE_STRIP) for c in range(cols // SLC_LANE_STRIP)]

    def pair_step(j, with_table):
        for ls in strips:
            produce(2 * j + 1, ls, s1_ref, x1_ref, with_table)
            consume(2 * j, ls, s0_ref, x0_ref)
        for ls in strips:
            produce(2 * j + 2, ls, s0_ref, x0_ref, with_table)
            consume(2 * j + 1, ls, s1_ref, x1_ref)

    m_ref[...] = jnp.full(m_ref.shape, -jnp.inf, F32)
    acc_ref[...] = jnp.zeros(acc_ref.shape, F32)

    @pl.when(n_far > 0)
    def _():
        for ls in strips:
            produce(0, ls, s0_ref, x0_ref, False)

    @pl.when(n_far == 0)
    def _():
        for ls in strips:
            produce(0, ls, s0_ref, x0_ref, True)

    def far_body(j, carry):
        pair_step(j, False)
        return carry

    def near_body(j, carry):
        pair_step(j, True)
        return carry

    lax.fori_loop(0, far_pairs, far_body, 0)
    lax.fori_loop(far_pairs, n_pairs, near_body, 0)
    o_ref[0, 0, 0] = acc_ref[:HEAD_DIM, :] / acc_ref[HEAD_DIM:HEAD_DIM + 1, :]


def _slc_attn(qt, far_t, k_aug, vt_aug, sel_t, tab_t):
    bsz, _, nq, _, cols = qt.shape
    s = k_aug.shape[2]
    nbp = sel_t.shape[3]
    ngroups = nbp // SLC_GROUP_BLOCKS
    nch = tab_t.shape[1]
    once = pl.Buffered(1)
    return pl.pallas_call(
        _slc_attn_kernel,
        grid=(bsz, N_KV, nq),
        in_specs=[pl.BlockSpec((1, 1, 1, LANES, cols), lambda b, k, i: (b, k, i, 0, 0)),
                  pl.BlockSpec((1, LANES, cols), lambda b, k, i: (k, 0, 0)),
                  pl.BlockSpec((1, 1, s, 2 * LANES), lambda b, k, i: (b, k, 0, 0), pipeline_mode=once),
                  pl.BlockSpec((1, 1, s // SLC_KEY_TILE, SLC_V_ROWS, SLC_KEY_TILE),
                               lambda b, k, i: (b, k, 0, 0, 0), pipeline_mode=once),
                  pl.BlockSpec((1, 1, 1, nbp, Q_BLOCK), lambda b, k, i: (b, k, i, 0, 0)),
                  pl.BlockSpec((1, nch, Q_BLOCK, cols), lambda b, k, i: (k, 0, 0, 0), pipeline_mode=once)],
        out_specs=pl.BlockSpec((1, 1, 1, HEAD_DIM, cols), lambda b, k, i: (b, k, i, 0, 0)),
        out_shape=jax.ShapeDtypeStruct((bsz, N_KV, nq, HEAD_DIM, cols), F32),
        scratch_shapes=[pltpu.VMEM((2 * ngroups, 2 * LANES, cols), BF16),
                        pltpu.VMEM((SLC_KEY_TILE, cols), F32),
                        pltpu.VMEM((SLC_KEY_TILE, cols), F32),
                        pltpu.VMEM((1, cols), F32),
                        pltpu.VMEM((1, cols), F32),
                        pltpu.VMEM((1, cols), F32),
                        pltpu.VMEM((SLC_V_ROWS, cols), F32)],
        compiler_params=_cparams(("parallel", "parallel", "arbitrary")),
        name="nsa_slc_attn",
    )(qt, far_t, k_aug, vt_aug, sel_t, tab_t)


def _win_attn_kernel(q_ref, k_ref, v_ref, tab_ref, o_ref):
    i = pl.program_id(2)
    rows = N_GRP * Q_BLOCK
    nw = WINDOW + Q_BLOCK
    q = q_ref[0, 0].reshape(rows, HEAD_DIM)
    qs = pl.multiple_of(i * Q_BLOCK, Q_BLOCK)
    k = k_ref[0, 0, pl.ds(qs, nw), :]
    v = v_ref[0, 0, pl.ds(qs, nw), :]
    s = _nt(q, k) + tab_ref[0].reshape(rows, nw)
    col = lax.broadcasted_iota(jnp.int32, (1, nw), 1)
    s = jnp.where(col >= WINDOW - i * Q_BLOCK, s, NEG_INF)
    m = jnp.max(s, axis=-1, keepdims=True)
    e = jnp.exp(s - m)
    p = e / jnp.sum(e, axis=-1, keepdims=True)
    o = jnp.dot(p.astype(BF16), v, preferred_element_type=F32)
    o_ref[0, 0] = o.reshape(N_GRP, Q_BLOCK, HEAD_DIM)


def _win_attn(q64, k_win, v_win, tab_win):
    bsz, _, _, s, _ = q64.shape
    nq = s // Q_BLOCK
    sp = k_win.shape[2]
    nw = WINDOW + Q_BLOCK
    return pl.pallas_call(
        _win_attn_kernel,
        grid=(bsz, N_KV, nq),
        in_specs=[pl.BlockSpec((1, 1, N_GRP, Q_BLOCK, HEAD_DIM), lambda b, k, i: (b, k, 0, i, 0)),
                  pl.BlockSpec((1, 1, sp, HEAD_DIM), lambda b, k, i: (b, k, 0, 0)),
                  pl.BlockSpec((1, 1, sp, HEAD_DIM), lambda b, k, i: (b, k, 0, 0)),
                  pl.BlockSpec((1, N_GRP, Q_BLOCK, nw), lambda b, k, i: (k, 0, 0, 0))],
        out_specs=pl.BlockSpec((1, 1, N_GRP, Q_BLOCK, HEAD_DIM), lambda b, k, i: (b, k, 0, i, 0)),
        out_shape=jax.ShapeDtypeStruct((bsz, N_KV, N_GRP, s, HEAD_DIM), F32),
        compiler_params=_cparams(("parallel", "parallel", "arbitrary")),
        name="nsa_win_attn",
    )(q64, k_win, v_win, tab_win)


def _ssd_kernel(xbc_ref, z_ref, dt_ref, cw_ref, cb_ref, dtb_ref, alog_ref, dskip_ref, nw_ref, o_ref,
                ext_ref, state_ref):
    ln = SSM_CHUNK
    d_ssm = z_ref.shape[2]
    nh = dt_ref.shape[2]
    gw = d_ssm // SSM_GROUPS
    hpg = nh // SSM_GROUPS
    gn = SSM_GROUPS * SSM_STATE

    @pl.when(pl.program_id(1) == 0)
    def _():
        ext_ref[0:SUBLANES, :] = jnp.zeros((SUBLANES, ext_ref.shape[1]), F32)
        state_ref[...] = jnp.zeros(state_ref.shape, F32)

    ext_ref[SUBLANES:SUBLANES + ln, :] = xbc_ref[0]
    conv = cw_ref[0:1, :] * ext_ref[SUBLANES - CONV_WIDTH + 1:SUBLANES - CONV_WIDTH + 1 + ln, :]
    for k in range(1, CONV_WIDTH):
        lo = SUBLANES - CONV_WIDTH + 1 + k
        conv = conv + cw_ref[k:k + 1, :] * ext_ref[lo:lo + ln, :]
    conv = conv + cb_ref[...]
    ext_ref[0:SUBLANES, :] = xbc_ref[0, ln - SUBLANES:ln, :]
    xc = _silu(conv)
    xs = xc[:, :d_ssm]
    bm = xc[:, d_ssm:d_ssm + gn]
    cm = xc[:, d_ssm + gn:d_ssm + 2 * gn]

    xdt = dt_ref[0] + dtb_ref[...]
    dt = jnp.maximum(xdt, 0.0) + jnp.log1p(jnp.exp(-jnp.abs(xdt)))
    a = -jnp.exp(alog_ref[...])
    da = dt * a

    row = lax.broadcasted_iota(jnp.int32, (ln, ln), 0)
    colm = lax.broadcasted_iota(jnp.int32, (ln, ln), 1)
    causal = row >= colm
    acs = jnp.dot(causal.astype(F32), da, precision=HI, preferred_element_type=F32)
    eye = (lax.broadcasted_iota(jnp.int32, (2 * nh, 2 * nh), 0)
           == lax.broadcasted_iota(jnp.int32, (2 * nh, 2 * nh), 1)).astype(F32)
    rows_t = _nt(eye, jnp.concatenate([acs, dt], axis=1), precision=HI)
    expand = (lax.broadcasted_iota(jnp.int32, (nh, d_ssm), 0)
              == lax.broadcasted_iota(jnp.int32, (nh, d_ssm), 1) // SSM_HEAD_DIM).astype(F32)
    last = acs[ln - 1:ln, :]
    exp_acs_x = jnp.dot(jnp.exp(acs), expand, precision=HI, preferred_element_type=F32)
    w_x = jnp.dot(jnp.exp(last - acs) * dt, expand, precision=HI, preferred_element_type=F32)
    exp_last_x = exp_acs_x[ln - 1:ln, :]

    ys = []
    for g in range(SSM_GROUPS):
        cg = cm[:, g * SSM_STATE:(g + 1) * SSM_STATE].astype(BF16)
        bg32 = bm[:, g * SSM_STATE:(g + 1) * SSM_STATE]
        bg = bg32.astype(BF16)
        xg = xs[:, g * gw:(g + 1) * gw]
        cb = _nt(cg, bg)
        st = state_ref[g]
        y_state = jnp.dot(cg, st.astype(BF16), preferred_element_type=F32) * exp_acs_x[:, g * gw:(g + 1) * gw]
        y_heads = []
        for j in range(hpg):
            h = g * hpg + j
            seg = acs[:, h:h + 1] - rows_t[h:h + 1, :]
            decay = jnp.exp(jnp.where(causal, seg, -jnp.inf))
            mmat = cb * decay * rows_t[nh + h:nh + h + 1, :]
            xh = xg[:, j * SSM_HEAD_DIM:(j + 1) * SSM_HEAD_DIM].astype(BF16)
            y_heads.append(jnp.dot(mmat.astype(BF16), xh, preferred_element_type=F32))
        ys.append(jnp.concatenate(y_heads, axis=1) + y_state)
        xw = (xg * w_x[:, g * gw:(g + 1) * gw]).astype(BF16)
        state_ref[g] = st * exp_last_x[:, g * gw:(g + 1) * gw] + jnp.dot(
            bg32.T.astype(BF16), xw, preferred_element_type=F32)

    y = jnp.concatenate(ys, axis=1) + dskip_ref[...] * xs
    y = y * _silu(z_ref[0])
    outs = []
    for g in range(SSM_GROUPS):
        yg = y[:, g * gw:(g + 1) * gw]
        outs.append(yg * lax.rsqrt(jnp.mean(yg * yg, axis=-1, keepdims=True) + RMS_EPS))
    o_ref[0] = jnp.concatenate(outs, axis=1) * nw_ref[...]


def _ssd(xbc, z, dt_raw, conv_w, conv_b, dt_bias, a_log, d_skip, norm_w):
    bsz, s, ch = xbc.shape
    d_ssm = z.shape[2]
    nh = dt_raw.shape[2]
    gw = d_ssm // SSM_GROUPS
    nchunks = s // SSM_CHUNK
    full = lambda shape: pl.BlockSpec(shape, lambda b, c: (0,) * len(shape))
    return pl.pallas_call(
        _ssd_kernel,
        grid=(bsz, nchunks),
        in_specs=[pl.BlockSpec((1, SSM_CHUNK, ch), lambda b, c: (b, c, 0)),
                  pl.BlockSpec((1, SSM_CHUNK, d_ssm), lambda b, c: (b, c, 0)),
                  pl.BlockSpec((1, SSM_CHUNK, nh), lambda b, c: (b, c, 0)),
                  full((CONV_WIDTH, ch)), full((1, ch)), full((1, nh)), full((1, nh)),
                  full((1, d_ssm)), full((1, d_ssm))],
        out_specs=pl.BlockSpec((1, SSM_CHUNK, d_ssm), lambda b, c: (b, c, 0)),
        out_shape=jax.ShapeDtypeStruct((bsz, s, d_ssm), F32),
        scratch_shapes=[pltpu.VMEM((SUBLANES + SSM_CHUNK, ch), F32),
                        pltpu.VMEM((SSM_GROUPS, SSM_STATE, gw), F32)],
        compiler_params=_cparams(("parallel", "arbitrary")),
        name="ssd_scan",
    )(xbc, z, dt_raw, conv_w, conv_b.reshape(1, ch), dt_bias.reshape(1, nh), a_log.reshape(1, nh),
      jnp.repeat(d_skip, SSM_HEAD_DIM).reshape(1, d_ssm), norm_w.reshape(1, d_ssm))


def _mix_out_kernel(oc_ref, os_ref, ow_ref, gl_ref, ex_ref, an_ref, ssm_ref, w_ref, x_ref, g1_ref, o_ref):
    d_attn = oc_ref.shape[2]
    sig = jax.nn.sigmoid(gl_ref[0])
    gc = jnp.dot(sig, ex_ref[0], precision=HI, preferred_element_type=F32)
    gs = jnp.dot(sig, ex_ref[1], precision=HI, preferred_element_type=F32)
    gw = jnp.dot(sig, ex_ref[2], precision=HI, preferred_element_type=F32)
    attn = gc * oc_ref[0] + gs * os_ref[0] + gw * ow_ref[0]
    attn = attn * lax.rsqrt(jnp.mean(attn * attn, axis=-1, keepdims=True) + RMS_EPS) * an_ref[...]
    mix = (jnp.dot(attn.astype(BF16), w_ref[:d_attn, :], preferred_element_type=F32)
           + jnp.dot(ssm_ref[0].astype(BF16), w_ref[d_attn:, :], preferred_element_type=F32))
    o_ref[0] = x_ref[0] + g1_ref[0] * mix


def _gate_expand():
    ex = np.zeros((3, LANES, N_HEADS * HEAD_DIM), np.float32)
    for r in range(3):
        for h in range(N_HEADS):
            ex[r, h * 3 + r, h * HEAD_DIM:(h + 1) * HEAD_DIM] = 1.0
    return ex


def _mix_out(oc, os_, ow, gl, attn_norm, ssm, w_out_bf16, x, g1, tm=256):
    bsz, s, d = x.shape
    d_attn = oc.shape[2]
    d_ssm = ssm.shape[2]
    tok = lambda w: pl.BlockSpec((1, tm, w), lambda b, i: (b, i, 0))
    return pl.pallas_call(
        _mix_out_kernel,
        grid=(bsz, s // tm),
        in_specs=[tok(d_attn), tok(d_attn), tok(d_attn), tok(LANES),
                  pl.BlockSpec((3, LANES, d_attn), lambda b, i: (0, 0, 0)),
                  pl.BlockSpec((1, d_attn), lambda b, i: (0, 0)),
                  tok(d_ssm),
                  pl.BlockSpec((d_attn + d_ssm, d), lambda b, i: (0, 0)),
                  tok(d),
                  pl.BlockSpec((1, 1, d), lambda b, i: (b, 0, 0))],
        out_specs=tok(d),
        out_shape=jax.ShapeDtypeStruct((bsz, s, d), F32),
        compiler_params=_cparams(("parallel", "parallel")),
        name="mix_out",
    )(oc, os_, ow, gl, jnp.asarray(_gate_expand()), attn_norm.reshape(1, d_attn), ssm, w_out_bf16, x, g1)


def _peer_route_kernel(q_ref, sk_ref, rank2_ref, e2_ref, n1_ref, c1_ref):
    q = q_ref[...]
    half = q.shape[1] // 2
    s1 = _nt(sk_ref[0], q[:, :half], precision=HI)
    s2 = _nt(sk_ref[1], q[:, half:], precision=HI)
    k = PEER_TOPK
    rank1, v1 = _topk_mark(s1, k, axis=0)
    rank2, v2 = _topk_mark(s2, k, axis=0)
    v2_all = jnp.concatenate(v2, axis=0)
    cand = jnp.concatenate([v1[a] + v2_all for a in range(k)], axis=0)
    rank_c, best = _topk_mark(cand, k, axis=0)
    chosen = jnp.where(rank_c < float(k), 1.0, 0.0).reshape(k, k, cand.shape[1])
    count = jnp.sum(chosen, axis=1)
    z = best[0] * 0.0
    for r in range(k):
        z = z + jnp.exp(best[r] - best[0])
    n1 = jnp.zeros(s1.shape, F32)
    for a in range(k):
        n1 = jnp.where(rank1 == float(a), count[a:a + 1, :], n1)
    rank2_ref[0] = rank2
    e2_ref[0] = jnp.exp(s2 - v2[0])
    n1_ref[0] = n1
    c1_ref[0] = jnp.exp(s1 - v1[0]) / z


def _peer_route(q, subkeys, tm=256):
    t, width = q.shape
    kd = width // PEER_HEADS
    nk = subkeys.shape[1]
    spec = pl.BlockSpec((1, nk, tm), lambda i, h: (h, 0, i))
    shape = jax.ShapeDtypeStruct((PEER_HEADS, nk, t), F32)
    return pl.pallas_call(
        _peer_route_kernel,
        grid=(t // tm, PEER_HEADS),
        in_specs=[pl.BlockSpec((tm, kd), lambda i, h: (i, h)),
                  pl.BlockSpec((2, nk, kd // 2), lambda i, h: (0, 0, 0))],
        out_specs=[spec, spec, spec, spec],
        out_shape=[shape, shape, shape, shape],
        compiler_params=_cparams(("parallel", "parallel")),
        name="peer_route",
    )(q, subkeys)


def _peer_dense_kernel(ht_ref, u_ref, vt_ref, rank2_ref, e2_ref, n1_ref, c1_ref, o_ref):
    j = pl.program_id(1)
    te = u_ref.shape[0]
    nk = rank2_ref.shape[1]

    @pl.when(j == 0)
    def _():
        o_ref[...] = jnp.zeros(o_ref.shape, F32)

    act = jax.nn.gelu(jnp.dot(u_ref[...], ht_ref[...], preferred_element_type=F32))
    parts = []
    for c in range(te // nk):
        i1 = j * (te // nk) + c
        w = jnp.zeros((nk, act.shape[1]), F32)
        for h in range(PEER_HEADS):
            n_row = n1_ref[h, pl.ds(i1, 1), :]
            c_row = c1_ref[h, pl.ds(i1, 1), :]
            w = w + jnp.where(rank2_ref[h] < n_row, e2_ref[h], 0.0) * c_row
        parts.append((act[c * nk:(c + 1) * nk] * w).astype(BF16))
    o_ref[...] += jnp.dot(vt_ref[...], jnp.concatenate(parts, axis=0), preferred_element_type=F32)


def _peer_dense(h_t, u_bf16, v_t_bf16, rank2, e2, n1, c1, tm=512, te=512):
    d, t = h_t.shape
    n_exp = u_bf16.shape[0]
    nk = rank2.shape[1]
    route = pl.BlockSpec((PEER_HEADS, nk, tm), lambda i, j: (0, 0, i))
    return pl.pallas_call(
        _peer_dense_kernel,
        grid=(t // tm, n_exp // te),
        in_specs=[pl.BlockSpec((d, tm), lambda i, j: (0, i)),
                  pl.BlockSpec((te, d), lambda i, j: (j, 0)),
                  pl.BlockSpec((d, te), lambda i, j: (0, j)),
                  route, route, route, route],
        out_specs=pl.BlockSpec((d, tm), lambda i, j: (0, i)),
        out_shape=jax.ShapeDtypeStruct((d, t), F32),
        compiler_params=_cparams(("parallel", "arbitrary")),
        name="peer_dense",
    )(h_t, u_bf16, v_t_bf16, rank2, e2, n1, c1)


def _residual_kernel(x_ref, y_ref, g_ref, o_ref):
    o_ref[0] = x_ref[0] + g_ref[0] * y_ref[0]


def _residual(x, y, g, tm=512):
    bsz, s, d = x.shape
    tok = pl.BlockSpec((1, tm, d), lambda b, i: (b, i, 0))
    return pl.pallas_call(
        _residual_kernel,
        grid=(bsz, s // tm),
        in_specs=[tok, tok, pl.BlockSpec((1, 1, d), lambda b, i: (b, 0, 0))],
        out_specs=tok,
        out_shape=jax.ShapeDtypeStruct((bsz, s, d), F32),
        compiler_params=_cparams(("parallel", "parallel")),
        name="residual",
    )(x, y, g)


def _final_norm_kernel(x_ref, w_ref, o_ref):
    x = x_ref[0]
    o_ref[0] = x * lax.rsqrt(jnp.mean(x * x, axis=-1, keepdims=True) + RMS_EPS) * w_ref[...]


def _final_norm(x, w, tm=512):
    bsz, s, d = x.shape
    tok = pl.BlockSpec((1, tm, d), lambda b, i: (b, i, 0))
    return pl.pallas_call(
        _final_norm_kernel,
        grid=(bsz, s // tm),
        in_specs=[tok, pl.BlockSpec((1, d), lambda b, i: (0, 0))],
        out_specs=tok,
        out_shape=jax.ShapeDtypeStruct((bsz, s, d), F32),
        compiler_params=_cparams(("parallel", "parallel")),
        name="final_norm",
    )(x, w.reshape(1, d))


def _pad_cols(w, width):
    return jnp.pad(w, ((0, 0), (0, width - w.shape[1])))


def _nsa(q, kv, tables, cmp_pe, cmp_w1, cmp_w2):
    tab_win, tab_cmp, tab_slc, far_col, far_t = tables
    bsz, s, _ = q.shape
    nc = s // CMP_STRIDE
    nq = s // Q_BLOCK
    n_slc = s // SLC_BLOCK
    nbp = -(-n_slc // SLC_GROUP_BLOCKS) * SLC_GROUP_BLOCKS
    scale = HEAD_DIM ** -0.5
    q6 = (q * scale).astype(BF16).reshape(bsz, nq, Q_BLOCK, N_KV, N_GRP, HEAD_DIM)
    qh = q6.transpose(0, 3, 4, 1, 2, 5).reshape(bsz, N_KV, N_GRP, s, HEAD_DIM)
    qt = q6.transpose(0, 3, 1, 5, 4, 2).reshape(bsz, N_KV, nq, HEAD_DIM, N_GRP * Q_BLOCK)
    qt = jnp.pad(qt, ((0, 0),) * 3 + ((0, LANES - HEAD_DIM), (0, 0)))
    kv6 = kv.reshape(bsz, s, 6, N_KV, HEAD_DIM)

    kv_cmp = kv6[:, :, 0:2].reshape(bsz, nc, CMP_STRIDE, 2, N_KV, HEAD_DIM)
    kv_cmp = kv_cmp.transpose(0, 3, 4, 1, 2, 5).reshape(bsz, 2, N_KV, nc, CMP_STRIDE * HEAD_DIM)
    kv_c = _compress(kv_cmp, cmp_pe, cmp_w1, cmp_w2)
    o_c, sel_t = _cmp_attn(qh, kv_c, tab_cmp, far_col, nbp)

    k_slc = kv6[:, :, 2].transpose(0, 2, 1, 3).astype(BF16)
    nkt = s // SLC_KEY_TILE
    vt = kv6[:, :, 3].astype(BF16).reshape(bsz, nkt, SLC_KEY_TILE, N_KV, HEAD_DIM).transpose(0, 3, 1, 4, 2)
    vt_aug = jnp.concatenate([
        vt, jnp.ones((bsz, N_KV, nkt, 1, SLC_KEY_TILE), BF16),
        jnp.zeros((bsz, N_KV, nkt, SLC_V_ROWS - HEAD_DIM - 1, SLC_KEY_TILE), BF16)], axis=3)
    blk = np.arange(s) // SLC_BLOCK
    onehot = (blk[:, None] % SLC_GROUP_BLOCKS == np.arange(SLC_GROUP_BLOCKS)[None, :]).astype(np.float32)
    k_aug = jnp.concatenate([
        k_slc, jnp.ones((bsz, N_KV, s, SLC_BIAS_PIECES), BF16),
        jnp.zeros((bsz, N_KV, s, LANES - HEAD_DIM - SLC_BIAS_PIECES), BF16),
        jnp.broadcast_to(jnp.asarray(onehot, BF16), (bsz, N_KV, s, SLC_GROUP_BLOCKS))], axis=-1)
    o_st = _slc_attn(qt, far_t, k_aug, vt_aug, sel_t, tab_slc)
    o_s = o_st.reshape(bsz, N_KV, nq, HEAD_DIM, N_GRP, Q_BLOCK).transpose(0, 2, 5, 1, 4, 3)
    o_s = o_s.reshape(bsz, s, N_HEADS * HEAD_DIM)

    front = ((0, 0), (0, 0), (WINDOW, 0), (0, 0))
    k_win = jnp.pad(kv6[:, :, 4].transpose(0, 2, 1, 3).astype(BF16), front)
    v_win = jnp.pad(kv6[:, :, 5].transpose(0, 2, 1, 3).astype(BF16), front)
    o_w = _win_attn(qh, k_win, v_win, tab_win)

    back = lambda o: o.transpose(0, 3, 1, 2, 4).reshape(bsz, s, N_HEADS * HEAD_DIM)
    return back(o_c), o_s, back(o_w)


def kernel(x, c, ada_w, ada_b, norm_mix, norm_ffn, w_in, cmp_pe, cmp_w1, cmp_w2, rel_bias, attn_out_norm,
           conv_w, conv_b, dt_bias, a_log, d_skip, ssm_norm, w_out, peer_wq, peer_subkeys, peer_u, peer_v,
           norm_final):
    bsz, s, d = x.shape
    depth = ada_w.shape[0]
    d_attn = N_HEADS * HEAD_DIM
    n_kv = 6 * N_KV * HEAD_DIM
    n_gate = 3 * N_HEADS
    d_ssm = ssm_norm.shape[1]
    ch = conv_w.shape[2]
    nh = dt_bias.shape[1]

    mod = _ada_mod(c, ada_w, ada_b)
    tables = _bias_tables(rel_bias)

    for l in range(depth):
        sh1, sc1, g1, sh2, sc2, g2 = [mod[l, :, i * d:(i + 1) * d].reshape(bsz, 1, d) for i in range(6)]
        cuts = np.cumsum([0, d_attn, n_kv, n_gate, d_ssm, ch, nh])
        seg = [w_in[l][:, cuts[i]:cuts[i + 1]] for i in range(6)]
        seg[2] = _pad_cols(seg[2], LANES)
        seg[5] = _pad_cols(seg[5], LANES)
        w_cat = jnp.concatenate(seg, axis=1).astype(BF16)
        pc = np.cumsum([0] + [w.shape[1] for w in seg])
        proj, _ = _norm_proj(x, norm_mix[l], sc1, sh1, w_cat)
        q = proj[:, :, pc[0]:pc[1]]
        kv = proj[:, :, pc[1]:pc[2]]
        gl = proj[:, :, pc[2]:pc[3]]
        z = proj[:, :, pc[3]:pc[4]]
        xbc = proj[:, :, pc[4]:pc[5]]
        dt_raw = proj[:, :, pc[5]:pc[5] + nh]

        o_c, o_s, o_w = _nsa(q, kv, tables, cmp_pe[l], cmp_w1[l], cmp_w2[l])
        ssm = _ssd(xbc, z, dt_raw, conv_w[l], conv_b[l], dt_bias[l], a_log[l], d_skip[l], ssm_norm[l])
        x = _mix_out(o_c, o_s, o_w, gl, attn_out_norm[l], ssm, w_out[l].astype(BF16), x, g1)

        pq, h2 = _norm_proj(x, norm_ffn[l], sc2, sh2, peer_wq[l].astype(BF16))
        rank2, e2, n1, c1 = _peer_route(pq.reshape(bsz * s, -1), peer_subkeys[l])
        h_t = h2.reshape(bsz * s, d).T
        ffn_t = _peer_dense(h_t, peer_u[l].astype(BF16), peer_v[l].T.astype(BF16), rank2, e2, n1, c1)
        x = _residual(x, ffn_t.T.reshape(bsz, s, d), g2)

    return _final_norm(x, norm_final)
```

```python
import functools
import math

import numpy as np
import jax
import jax.numpy as jnp
from jax import lax
from jax.experimental import pallas as pl
from jax.experimental.pallas import tpu as pltpu

F32 = jnp.float32
BF16 = jnp.bfloat16
HI = lax.Precision.HIGHEST

N_HEADS = 16
N_KV = 2
N_GRP = N_HEADS // N_KV
HEAD_DIM = 64
CMP_BLOCK = 32
CMP_STRIDE = 16
CMP_HIDDEN = 4 * HEAD_DIM
SLC_BLOCK = 64
SLC_TOPN = 16
WINDOW = 512
Q_BLOCK = 128
FORCE_SCORE = 1e4
NEG_INF = -1e30
REL_BUCKETS = 32
REL_MAX_DIST = 2048
SSM_HEAD_DIM = 64
SSM_GROUPS = 2
SSM_STATE = 128
CONV_WIDTH = 4
SSM_CHUNK = 256
PEER_HEADS = 8
PEER_NKEYS = 128
PEER_TOPK = 16
RMS_EPS = 1e-6

LANES = 128
SUBLANES = 8
VMEM_LIMIT = 56 * 1024 * 1024

SLC_KEY_TILE = 512
SLC_TILE_BLOCKS = SLC_KEY_TILE // SLC_BLOCK
SLC_GROUP_BLOCKS = LANES
SLC_NEAR_CHUNKS = 14
SLC_TAB_MASKED = 14
SLC_TAB_CONST = 15
MASK_BIG = 2.0 ** 100
SLC_LANE_STRIP = 256
PEER_LANE_STRIP = 256
PEER_EXPERT_PIECE = 256
PEER_EXPERT_UNIT = 512
SLC_V_ROWS = HEAD_DIM + 16
SLC_BIAS_PIECES = 3

CMP_NEAR = 128
CMP_PAD = CMP_NEAR - Q_BLOCK // CMP_STRIDE


def _cparams(sem, vmem=VMEM_LIMIT):
    return pltpu.CompilerParams(dimension_semantics=sem, vmem_limit_bytes=vmem)


def _nt(a, b, precision=None):
    return lax.dot_general(a, b, (((1,), (1,)), ((), ())), precision=precision,
                           preferred_element_type=F32)


def _silu(x):
    return x * jax.nn.sigmoid(x)


def _rel_bucket_np(d):
    d = np.maximum(np.asarray(d, np.int64), 0)
    max_exact = REL_BUCKETS // 2
    ratio = np.log(np.maximum(d, max_exact).astype(np.float64) / max_exact) / math.log(REL_MAX_DIST / max_exact)
    scaled = ratio * (REL_BUCKETS - max_exact)
    large = max_exact + np.floor(scaled).astype(np.int64)
    return np.where(d < max_exact, d, np.minimum(large, REL_BUCKETS - 1)).astype(np.int32)


def _ada_kernel(c_ref, w_ref, b_ref, o_ref):
    cond = _silu(c_ref[...])
    o_ref[0] = jnp.dot(cond, w_ref[0], precision=HI, preferred_element_type=F32) + b_ref[0]


def _ada_mod(c, ada_w, ada_b):
    depth, d, n = ada_w.shape
    bsz = c.shape[0]
    rows = SUBLANES
    c_pad = jnp.zeros((rows, d), F32).at[:bsz].set(c)
    tn = 1024
    out = pl.pallas_call(
        _ada_kernel,
        grid=(depth, n // tn),
        in_specs=[pl.BlockSpec((rows, d), lambda l, j: (0, 0)),
                  pl.BlockSpec((1, d, tn), lambda l, j: (l, 0, j)),
                  pl.BlockSpec((1, 1, tn), lambda l, j: (l, 0, j))],
        out_specs=pl.BlockSpec((1, rows, tn), lambda l, j: (l, 0, j)),
        out_shape=jax.ShapeDtypeStruct((depth, rows, n), F32),
        compiler_params=_cparams(("parallel", "parallel")),
        name="ada_mod",
    )(c_pad, ada_w, ada_b.reshape(depth, 1, n))
    return out[:, :bsz]


def _norm_proj_kernel(x_ref, nw_ref, sc_ref, sh_ref, w_ref, o_ref, h_ref, hs_ref):
    @pl.when(pl.program_id(2) == 0)
    def _():
        x = x_ref[0]
        y = x * lax.rsqrt(jnp.mean(x * x, axis=-1, keepdims=True) + RMS_EPS)
        h = (y * nw_ref[...]) * (1.0 + sc_ref[0]) + sh_ref[0]
        hs_ref[...] = h.astype(BF16)
        h_ref[0] = h.astype(BF16)

    o_ref[0] = jnp.dot(hs_ref[...], w_ref[...], preferred_element_type=F32)


def _norm_proj(x, nw, sc, sh, w_bf16, tm=512, tn=512):
    bsz, s, d = x.shape
    n = w_bf16.shape[1]
    return pl.pallas_call(
        _norm_proj_kernel,
        grid=(bsz, s // tm, n // tn),
        in_specs=[pl.BlockSpec((1, tm, d), lambda b, i, j: (b, i, 0)),
                  pl.BlockSpec((1, d), lambda b, i, j: (0, 0)),
                  pl.BlockSpec((1, 1, d), lambda b, i, j: (b, 0, 0)),
                  pl.BlockSpec((1, 1, d), lambda b, i, j: (b, 0, 0)),
                  pl.BlockSpec((d, tn), lambda b, i, j: (0, j))],
        out_specs=[pl.BlockSpec((1, tm, tn), lambda b, i, j: (b, i, j)),
                   pl.BlockSpec((1, tm, d), lambda b, i, j: (b, i, 0))],
        out_shape=[jax.ShapeDtypeStruct((bsz, s, n), F32),
                   jax.ShapeDtypeStruct((bsz, s, d), BF16)],
        scratch_shapes=[pltpu.VMEM((tm, d), BF16)],
        compiler_params=_cparams(("parallel", "parallel", "arbitrary")),
        name="norm_proj",
    )(x, nw.reshape(1, d), sc, sh, w_bf16)


def _bias_kernel(rel_ref, bk_ref, o_ref):
    h = pl.program_id(0)
    bk = bk_ref[...]
    acc = jnp.full(bk.shape, NEG_INF, F32)
    for b in range(REL_BUCKETS):
        acc = jnp.where(bk == b, rel_ref[b, h], acc)
    o_ref[0] = acc


def _bias_tables(rel_bias):
    r = np.arange(Q_BLOCK)[:, None]
    dw = r - np.arange(WINDOW + Q_BLOCK)[None, :] + WINDOW
    win = np.where((dw >= 0) & (dw < WINDOW), _rel_bucket_np(dw), -1)
    off = CMP_STRIDE * CMP_PAD - (CMP_BLOCK - 1)
    dc = r + off - CMP_STRIDE * np.arange(CMP_NEAR)[None, :]
    cmp_near = np.where(dc >= 0, _rel_bucket_np(dc), -1)
    chunks = []
    for m in range(SLC_NEAR_CHUNKS):
        ds_ = Q_BLOCK * m + r - np.arange(Q_BLOCK)[None, :]
        chunks.append(np.where(ds_ >= 0, _rel_bucket_np(ds_), -1))
    assert _rel_bucket_np(Q_BLOCK * SLC_NEAR_CHUNKS - (Q_BLOCK - 1)) == REL_BUCKETS - 1
    assert _rel_bucket_np(off + CMP_STRIDE) == REL_BUCKETS - 1
    chunks.append(np.full((Q_BLOCK, Q_BLOCK), -1))
    chunks.append(np.full((Q_BLOCK, Q_BLOCK), REL_BUCKETS - 1))
    bk = np.concatenate([win, cmp_near] + chunks, axis=1).astype(np.int32)
    cols = bk.shape[1]
    out = pl.pallas_call(
        _bias_kernel,
        grid=(N_HEADS,),
        in_specs=[pl.BlockSpec(memory_space=pltpu.SMEM),
                  pl.BlockSpec((Q_BLOCK, cols), lambda h: (0, 0))],
        out_specs=pl.BlockSpec((1, Q_BLOCK, cols), lambda h: (h, 0, 0)),
        out_shape=jax.ShapeDtypeStruct((N_HEADS, Q_BLOCK, cols), F32),
        compiler_params=_cparams(("arbitrary",)),
        name="bias_tables",
    )(rel_bias, jnp.asarray(bk))
    nw = WINDOW + Q_BLOCK
    tab_win = out[:, :, :nw].reshape(N_KV, N_GRP, Q_BLOCK, nw)
    tab_cmp = out[:, :, nw:nw + CMP_NEAR].reshape(N_KV, N_GRP, Q_BLOCK, CMP_NEAR)
    nch = SLC_NEAR_CHUNKS + 2
    tab_slc = out[:, :, nw + CMP_NEAR:].reshape(N_KV, N_GRP, Q_BLOCK, nch, Q_BLOCK)
    tab_slc = tab_slc.transpose(0, 3, 4, 1, 2).reshape(N_KV, nch, Q_BLOCK, N_GRP * Q_BLOCK)
    far = rel_bias[REL_BUCKETS - 1].reshape(N_KV, N_GRP, 1)
    far_col = jnp.broadcast_to(far, (N_KV, N_GRP, Q_BLOCK)).reshape(N_KV, N_GRP * Q_BLOCK, 1)
    far_row = far_col.reshape(N_KV, 1, N_GRP * Q_BLOCK)
    pieces, rest = [], far_row
    for _ in range(SLC_BIAS_PIECES):
        piece = rest.astype(BF16)
        pieces.append(piece)
        rest = rest - piece.astype(F32)
    zeros = lambda w: jnp.zeros((N_KV, w, N_GRP * Q_BLOCK), BF16)
    far_t = jnp.concatenate([zeros(HEAD_DIM)] + pieces + [zeros(LANES - HEAD_DIM - SLC_BIAS_PIECES)], axis=1)
    return tab_win, tab_cmp, tab_slc, far_col, far_t


def _compress_kernel(a_ref, pe_ref, w1_ref, w2_ref, o_ref):
    a = a_ref[0, 0, 0]
    half = a.shape[1]
    lo = jnp.dot((a + pe_ref[0, 0:1, :]).astype(BF16), w1_ref[0, :half, :].astype(BF16),
                 preferred_element_type=F32)
    hi = jnp.dot((a + pe_ref[0, 1:2, :]).astype(BF16), w1_ref[0, half:, :].astype(BF16),
                 preferred_element_type=F32)
    nc = a.shape[0]
    hid = jax.nn.gelu(lo + pltpu.roll(hi, nc - 1, axis=0))
    out = jnp.dot(hid.astype(BF16), w2_ref[0].astype(BF16), preferred_element_type=F32)
    o_ref[0, 0, 0] = jnp.zeros(o_ref.shape[3:], F32)
    o_ref[0, 0, 0, CMP_PAD:CMP_PAD + nc, :] = out


def _compress(kv_cmp, cmp_pe, cmp_w1, cmp_w2):
    bsz, _, _, nc, half = kv_cmp.shape
    pe = cmp_pe.reshape(2, 2, half)
    rows = CMP_PAD + nc + SUBLANES
    return pl.pallas_call(
        _compress_kernel,
        grid=(bsz, 2, N_KV),
        in_specs=[pl.BlockSpec((1, 1, 1, nc, half), lambda b, w, k: (b, w, k, 0, 0)),
                  pl.BlockSpec((1, 2, half), lambda b, w, k: (w, 0, 0)),
                  pl.BlockSpec((1, 2 * half, CMP_HIDDEN), lambda b, w, k: (w, 0, 0)),
                  pl.BlockSpec((1, CMP_HIDDEN, HEAD_DIM), lambda b, w, k: (w, 0, 0))],
        out_specs=pl.BlockSpec((1, 1, 1, rows, HEAD_DIM), lambda b, w, k: (b, w, k, 0, 0)),
        out_shape=jax.ShapeDtypeStruct((bsz, 2, N_KV, rows, HEAD_DIM), F32),
        compiler_params=_cparams(("parallel", "parallel", "parallel")),
        name="nsa_compress",
    )(kv_cmp, pe, cmp_w1, cmp_w2)


def _topk_mark(vals, k, axis, order=None, order_bound=None):
    n = vals.shape[axis] if order is None else order_bound
    iota = lax.broadcasted_iota(jnp.int32, vals.shape, axis).astype(F32) if order is None else order
    rank = jnp.full(vals.shape, float(k), F32)
    work = vals
    picked = []
    for r in range(k):
        m = jnp.max(work, axis=axis, keepdims=True)
        ix = jnp.min(jnp.where(work == m, iota, float(n)), axis=axis, keepdims=True)
        hit = iota == ix
        rank = jnp.where(hit, float(r), rank)
        work = jnp.where(hit, -jnp.inf, work)
        picked.append(m)
    return rank, picked


def _cmp_attn_kernel(q_ref, k_ref, v_ref, tab_ref, far_ref, m_ref, o_ref, sel_ref):
    i = pl.program_id(2)
    rows = N_GRP * Q_BLOCK
    nc = k_ref.shape[3] - CMP_PAD - SUBLANES
    nbp = sel_ref.shape[3]
    q = q_ref[0, 0].reshape(rows, HEAD_DIM)
    start = pl.multiple_of(i * (Q_BLOCK // CMP_STRIDE), SUBLANES)

    k_far = k_ref[0, 0, 0, CMP_PAD:CMP_PAD + nc, :].astype(BF16)
    v_far = v_ref[0, 0, 0, CMP_PAD:CMP_PAD + nc, :].astype(BF16)
    k_near = k_ref[0, 0, 0, pl.ds(start, CMP_NEAR), :].astype(BF16)
    v_near = v_ref[0, 0, 0, pl.ds(start, CMP_NEAR), :].astype(BF16)

    n_idx = lax.broadcasted_iota(jnp.int32, (1, nc), 1)
    mask_far = n_idx < (i * (Q_BLOCK // CMP_STRIDE) - CMP_PAD)
    s_far = jnp.where(mask_far, _nt(q, k_far) + far_ref[0], NEG_INF)
    tab = tab_ref[0].reshape(rows, CMP_NEAR)
    c_idx = lax.broadcasted_iota(jnp.int32, (1, CMP_NEAR), 1)
    mask_near = (c_idx >= (CMP_PAD - i * (Q_BLOCK // CMP_STRIDE))) & (tab > 0.5 * NEG_INF)
    s_near = jnp.where(mask_near, _nt(q, k_near) + tab, NEG_INF)

    m = jnp.maximum(jnp.max(s_far, axis=-1, keepdims=True), jnp.max(s_near, axis=-1, keepdims=True))
    e_far = jnp.exp(s_far - m)
    e_near = jnp.exp(s_near - m)
    l = jnp.sum(e_far, axis=-1, keepdims=True) + jnp.sum(e_near, axis=-1, keepdims=True)
    p_far = jnp.where(mask_far, e_far / l, 0.0)
    p_near = jnp.where(mask_near, e_near / l, 0.0)
    o = (jnp.dot(p_far.astype(BF16), v_far, preferred_element_type=F32)
         + jnp.dot(p_near.astype(BF16), v_near, preferred_element_type=F32))
    o_ref[0, 0] = o.reshape(N_GRP, Q_BLOCK, HEAD_DIM)

    ps_far = jnp.sum(p_far.reshape(N_GRP, Q_BLOCK, nc), axis=0)
    ps_near = jnp.sum(p_near.reshape(N_GRP, Q_BLOCK, CMP_NEAR), axis=0)
    imp = (jnp.dot(ps_far, m_ref[CMP_PAD:CMP_PAD + nc, :], precision=HI, preferred_element_type=F32)
           + jnp.dot(ps_near, m_ref[pl.ds(start, CMP_NEAR), :], precision=HI, preferred_element_type=F32))
    t = i * Q_BLOCK + lax.broadcasted_iota(jnp.int32, (Q_BLOCK, 1), 0)
    cur = t // SLC_BLOCK
    blk = lax.broadcasted_iota(jnp.int32, (1, nbp), 1)
    forced = (blk == 0) | (blk == cur) | (blk == cur - 1)
    imp = jnp.where(forced, FORCE_SCORE, jnp.where(blk <= cur, imp, -FORCE_SCORE))
    n_blocks = (nc * CMP_STRIDE) // SLC_BLOCK
    imp = jnp.where(blk < n_blocks, imp, -jnp.inf)
    rank, _ = _topk_mark(imp, min(SLC_TOPN, n_blocks), axis=1)
    sel_ref[0, 0, 0] = jnp.where(rank < float(SLC_TOPN), 1.0, 0.0).T.astype(BF16)


def _overlap_matrix(nc, nbp):
    n_cmp = nc - 1
    n_slc = nc * CMP_STRIDE // SLC_BLOCK
    j = np.arange(n_slc)
    lo = np.clip((j * SLC_BLOCK - CMP_BLOCK) // CMP_STRIDE + 1, 0, n_cmp)
    hi = np.clip(-((-(j * SLC_BLOCK + SLC_BLOCK)) // CMP_STRIDE), 0, n_cmp)
    m = np.zeros((CMP_PAD + nc + SUBLANES, nbp), np.float32)
    n = np.arange(nc)[:, None]
    m[CMP_PAD:CMP_PAD + nc, :n_slc] = (n >= lo[None, :]) & (n < hi[None, :])
    return m


def _cmp_attn(q64, kv_c, tab_cmp, far_col, nbp):
    bsz, _, _, s, _ = q64.shape
    rows_c = kv_c.shape[3]
    nc = rows_c - CMP_PAD - SUBLANES
    nq = s // Q_BLOCK
    m_pad = jnp.asarray(_overlap_matrix(nc, nbp))
    return pl.pallas_call(
        _cmp_attn_kernel,
        grid=(bsz, N_KV, nq),
        in_specs=[pl.BlockSpec((1, 1, N_GRP, Q_BLOCK, HEAD_DIM), lambda b, k, i: (b, k, 0, i, 0)),
                  pl.BlockSpec((1, 1, 1, rows_c, HEAD_DIM), lambda b, k, i: (b, 0, k, 0, 0)),
                  pl.BlockSpec((1, 1, 1, rows_c, HEAD_DIM), lambda b, k, i: (b, 1, k, 0, 0)),
                  pl.BlockSpec((1, N_GRP, Q_BLOCK, CMP_NEAR), lambda b, k, i: (k, 0, 0, 0)),
                  pl.BlockSpec((1, N_GRP * Q_BLOCK, 1), lambda b, k, i: (k, 0, 0)),
                  pl.BlockSpec((rows_c, nbp), lambda b, k, i: (0, 0))],
        out_specs=[pl.BlockSpec((1, 1, N_GRP, Q_BLOCK, HEAD_DIM), lambda b, k, i: (b, k, 0, i, 0)),
                   pl.BlockSpec((1, 1, 1, nbp, Q_BLOCK), lambda b, k, i: (b, k, i, 0, 0))],
        out_shape=[jax.ShapeDtypeStruct((bsz, N_KV, N_GRP, s, HEAD_DIM), F32),
                   jax.ShapeDtypeStruct((bsz, N_KV, nq, nbp, Q_BLOCK), BF16)],
        compiler_params=_cparams(("parallel", "parallel", "arbitrary")),
        name="nsa_cmp_attn",
    )(q64, kv_c, kv_c, tab_cmp, far_col, m_pad)


def _slc_attn_kernel(qt_ref, far_ref, ka_ref, vt_ref, selt_ref, tab_ref, o_ref,
                     qa_ref, s0_ref, s1_ref, x0_ref, x1_ref, m_ref, acc_ref):
    i = pl.program_id(2)
    cols = N_GRP * Q_BLOCK
    ngroups = qa_ref.shape[0] // 2
    qt = qt_ref[0, 0, 0]
    qt_far = qt + far_ref[0]
    selneg = ((selt_ref[0, 0, 0].astype(F32) - 1.0) * MASK_BIG).astype(BF16)
    for g in range(ngroups):
        part = selneg[g * SLC_GROUP_BLOCKS:(g + 1) * SLC_GROUP_BLOCKS, :]
        part = jnp.concatenate([part] * N_GRP, axis=1)
        qa_ref[2 * g] = jnp.concatenate([qt_far, part], axis=0)
        qa_ref[2 * g + 1] = jnp.concatenate([qt, part], axis=0)

    tiles_per_group = SLC_GROUP_BLOCKS // SLC_TILE_BLOCKS
    sub = SLC_KEY_TILE // Q_BLOCK
    last_tile = vt_ref.shape[2] - 1
    n_pairs = (i // sub + 2) // 2
    n_far = jnp.maximum(0, (i - (SLC_NEAR_CHUNKS - 1)) // sub)
    far_pairs = jnp.maximum(0, (n_far - 1) // 2)
    first_table = jnp.where(n_far > 0, 2 * far_pairs + 1, 0)

    def produce(kt, ls, s_ref, mx_ref, with_table):
        near = (kt >= first_table).astype(jnp.int32)
        kc = jnp.minimum(kt, last_tile)
        ks = pl.multiple_of(kc * SLC_KEY_TILE, SLC_KEY_TILE)
        s = jnp.dot(ka_ref[0, 0, pl.ds(ks, SLC_KEY_TILE), :], qa_ref[2 * (kc // tiles_per_group) + near, :, ls],
                    preferred_element_type=F32)
        if with_table:
            chunks = []
            for a in range(sub):
                mm = i - sub * kt - a
                idx = jnp.where(mm < 0, SLC_TAB_MASKED, jnp.where(mm >= SLC_NEAR_CHUNKS, SLC_TAB_CONST, mm))
                chunks.append(tab_ref[0, idx, :, ls])
            s = s + jnp.concatenate(chunks, axis=0)
        s_ref[:, ls] = s
        mx_ref[:, ls] = jnp.max(s, axis=0, keepdims=True)

    def consume(kt, ls, s_ref, mx_ref):
        m_old = m_ref[:, ls]
        m_new = jnp.maximum(m_old, mx_ref[:, ls])
        m_ref[:, ls] = m_new
        p = jnp.exp(s_ref[:, ls] - m_new).astype(BF16)
        acc_ref[:, ls] = jnp.exp(m_old - m_new) * acc_ref[:, ls] + jnp.dot(
            vt_ref[0, 0, jnp.minimum(kt, last_tile)], p, preferred_element_type=F32)

    strips = [slice(c * SLC_LANE_STRIP, (c + 1) * SLC_LANE_STRIP) for c in range(cols // SLC_LANE_STRIP)]

    def pair_step(j, with_table):
        for ls in strips:
            produce(2 * j + 1, ls, s1_ref, x1_ref, with_table)
            consume(2 * j, ls, s0_ref, x0_ref)
        for ls in strips:
            produce(2 * j + 2, ls, s0_ref, x0_ref, with_table)
            consume(2 * j + 1, ls, s1_ref, x1_ref)

    m_ref[...] = jnp.full(m_ref.shape, -jnp.inf, F32)
    acc_ref[...] = jnp.zeros(acc_ref.shape, F32)

    @pl.when(n_far > 0)
    def _():
        for ls in strips:
            produce(0, ls, s0_ref, x0_ref, False)

    @pl.when(n_far == 0)
    def _():
        for ls in strips:
            produce(0, ls, s0_ref, x0_ref, True)

    def far_body(j, carry):
        pair_step(j, False)
        return carry

    def near_body(j, carry):
        pair_step(j, True)
        return carry

    lax.fori_loop(0, far_pairs, far_body, 0)
    lax.fori_loop(far_pairs, n_pairs, near_body, 0)
    o_ref[0, 0, 0] = acc_ref[:HEAD_DIM, :] / acc_ref[HEAD_DIM:HEAD_DIM + 1, :]


def _slc_attn(qt, far_t, k_aug, vt_aug, sel_t, tab_t):
    bsz, _, nq, _, cols = qt.shape
    s = k_aug.shape[2]
    nbp = sel_t.shape[3]
    ngroups = nbp // SLC_GROUP_BLOCKS
    nch = tab_t.shape[1]
    once = pl.Buffered(1)
    return pl.pallas_call(
        _slc_attn_kernel,
        grid=(bsz, N_KV, nq),
        in_specs=[pl.BlockSpec((1, 1, 1, LANES, cols), lambda b, k, i: (b, k, i, 0, 0)),
                  pl.BlockSpec((1, LANES, cols), lambda b, k, i: (k, 0, 0)),
                  pl.BlockSpec((1, 1, s, 2 * LANES), lambda b, k, i: (b, k, 0, 0), pipeline_mode=once),
                  pl.BlockSpec((1, 1, s // SLC_KEY_TILE, SLC_V_ROWS, SLC_KEY_TILE),
                               lambda b, k, i: (b, k, 0, 0, 0), pipeline_mode=once),
                  pl.BlockSpec((1, 1, 1, nbp, Q_BLOCK), lambda b, k, i: (b, k, i, 0, 0)),
                  pl.BlockSpec((1, nch, Q_BLOCK, cols), lambda b, k, i: (k, 0, 0, 0), pipeline_mode=once)],
        out_specs=pl.BlockSpec((1, 1, 1, HEAD_DIM, cols), lambda b, k, i: (b, k, i, 0, 0)),
        out_shape=jax.ShapeDtypeStruct((bsz, N_KV, nq, HEAD_DIM, cols), F32),
        scratch_shapes=[pltpu.VMEM((2 * ngroups, 2 * LANES, cols), BF16),
                        pltpu.VMEM((SLC_KEY_TILE, cols), F32),
                        pltpu.VMEM((SLC_KEY_TILE, cols), F32),
                        pltpu.VMEM((1, cols), F32),
                        pltpu.VMEM((1, cols), F32),
                        pltpu.VMEM((1, cols), F32),
                        pltpu.VMEM((SLC_V_ROWS, cols), F32)],
        compiler_params=_cparams(("parallel", "parallel", "arbitrary")),
        name="nsa_slc_attn",
    )(qt, far_t, k_aug, vt_aug, sel_t, tab_t)


def _win_attn_kernel(q_ref, k_ref, v_ref, tab_ref, o_ref):
    i = pl.program_id(2)
    rows = N_GRP * Q_BLOCK
    nw = WINDOW + Q_BLOCK
    q = q_ref[0, 0].reshape(rows, HEAD_DIM)
    qs = pl.multiple_of(i * Q_BLOCK, Q_BLOCK)
    k = k_ref[0, 0, pl.ds(qs, nw), :]
    v = v_ref[0, 0, pl.ds(qs, nw), :]
    s = _nt(q, k) + tab_ref[0].reshape(rows, nw)
    col = lax.broadcasted_iota(jnp.int32, (1, nw), 1)
    s = jnp.where(col >= WINDOW - i * Q_BLOCK, s, NEG_INF)
    m = jnp.max(s, axis=-1, keepdims=True)
    e = jnp.exp(s - m)
    p = e / jnp.sum(e, axis=-1, keepdims=True)
    o = jnp.dot(p.astype(BF16), v, preferred_element_type=F32)
    o_ref[0, 0] = o.reshape(N_GRP, Q_BLOCK, HEAD_DIM)


def _win_attn(q64, k_win, v_win, tab_win):
    bsz, _, _, s, _ = q64.shape
    nq = s // Q_BLOCK
    sp = k_win.shape[2]
    nw = WINDOW + Q_BLOCK
    return pl.pallas_call(
        _win_attn_kernel,
        grid=(bsz, N_KV, nq),
        in_specs=[pl.BlockSpec((1, 1, N_GRP, Q_BLOCK, HEAD_DIM), lambda b, k, i: (b, k, 0, i, 0)),
                  pl.BlockSpec((1, 1, sp, HEAD_DIM), lambda b, k, i: (b, k, 0, 0)),
                  pl.BlockSpec((1, 1, sp, HEAD_DIM), lambda b, k, i: (b, k, 0, 0)),
                  pl.BlockSpec((1, N_GRP, Q_BLOCK, nw), lambda b, k, i: (k, 0, 0, 0))],
        out_specs=pl.BlockSpec((1, 1, N_GRP, Q_BLOCK, HEAD_DIM), lambda b, k, i: (b, k, 0, i, 0)),
        out_shape=jax.ShapeDtypeStruct((bsz, N_KV, N_GRP, s, HEAD_DIM), F32),
        compiler_params=_cparams(("parallel", "parallel", "arbitrary")),
        name="nsa_win_attn",
    )(q64, k_win, v_win, tab_win)


def _ssd_kernel(xbc_ref, z_ref, dt_ref, cw_ref, cb_ref, dtb_ref, alog_ref, dskip_ref, nw_ref, o_ref,
                ext_ref, state_ref):
    ln = SSM_CHUNK
    d_ssm = z_ref.shape[2]
    nh = dt_ref.shape[2]
    gw = d_ssm // SSM_GROUPS
    hpg = nh // SSM_GROUPS
    gn = SSM_GROUPS * SSM_STATE

    @pl.when(pl.program_id(1) == 0)
    def _():
        ext_ref[0:SUBLANES, :] = jnp.zeros((SUBLANES, ext_ref.shape[1]), F32)
        state_ref[...] = jnp.zeros(state_ref.shape, F32)

    ext_ref[SUBLANES:SUBLANES + ln, :] = xbc_ref[0]
    conv = cw_ref[0:1, :] * ext_ref[SUBLANES - CONV_WIDTH + 1:SUBLANES - CONV_WIDTH + 1 + ln, :]
    for k in range(1, CONV_WIDTH):
        lo = SUBLANES - CONV_WIDTH + 1 + k
        conv = conv + cw_ref[k:k + 1, :] * ext_ref[lo:lo + ln, :]
    conv = conv + cb_ref[...]
    ext_ref[0:SUBLANES, :] = xbc_ref[0, ln - SUBLANES:ln, :]
    xc = _silu(conv)
    xs = xc[:, :d_ssm]
    bm = xc[:, d_ssm:d_ssm + gn]
    cm = xc[:, d_ssm + gn:d_ssm + 2 * gn]

    xdt = dt_ref[0] + dtb_ref[...]
    dt = jnp.maximum(xdt, 0.0) + jnp.log1p(jnp.exp(-jnp.abs(xdt)))
    a = -jnp.exp(alog_ref[...])
    da = dt * a

    row = lax.broadcasted_iota(jnp.int32, (ln, ln), 0)
    colm = lax.broadcasted_iota(jnp.int32, (ln, ln), 1)
    causal = row >= colm
    acs = jnp.dot(causal.astype(F32), da, precision=HI, preferred_element_type=F32)
    eye = (lax.broadcasted_iota(jnp.int32, (2 * nh, 2 * nh), 0)
           == lax.broadcasted_iota(jnp.int32, (2 * nh, 2 * nh), 1)).astype(F32)
    rows_t = _nt(eye, jnp.concatenate([acs, dt], axis=1), precision=HI)
    expand = (lax.broadcasted_iota(jnp.int32, (nh, d_ssm), 0)
              == lax.broadcasted_iota(jnp.int32, (nh, d_ssm), 1) // SSM_HEAD_DIM).astype(F32)
    last = acs[ln - 1:ln, :]
    exp_acs_x = jnp.dot(jnp.exp(acs), expand, precision=HI, preferred_element_type=F32)
    w_x = jnp.dot(jnp.exp(last - acs) * dt, expand, precision=HI, preferred_element_type=F32)
    exp_last_x = exp_acs_x[ln - 1:ln, :]

    ys = []
    for g in range(SSM_GROUPS):
        cg = cm[:, g * SSM_STATE:(g + 1) * SSM_STATE].astype(BF16)
        bg32 = bm[:, g * SSM_STATE:(g + 1) * SSM_STATE]
        bg = bg32.astype(BF16)
        xg = xs[:, g * gw:(g + 1) * gw]
        cb = _nt(cg, bg)
        st = state_ref[g]
        y_state = jnp.dot(cg, st.astype(BF16), preferred_element_type=F32) * exp_acs_x[:, g * gw:(g + 1) * gw]
        y_heads = []
        for j in range(hpg):
            h = g * hpg + j
            seg = acs[:, h:h + 1] - rows_t[h:h + 1, :]
            decay = jnp.exp(jnp.where(causal, seg, -jnp.inf))
            mmat = cb * decay * rows_t[nh + h:nh + h + 1, :]
            xh = xg[:, j * SSM_HEAD_DIM:(j + 1) * SSM_HEAD_DIM].astype(BF16)
            y_heads.append(jnp.dot(mmat.astype(BF16), xh, preferred_element_type=F32))
        ys.append(jnp.concatenate(y_heads, axis=1) + y_state)
        xw = (xg * w_x[:, g * gw:(g + 1) * gw]).astype(BF16)
        state_ref[g] = st * exp_last_x[:, g * gw:(g + 1) * gw] + jnp.dot(
            bg32.T.astype(BF16), xw, preferred_element_type=F32)

    y = jnp.concatenate(ys, axis=1) + dskip_ref[...] * xs
    y = y * _silu(z_ref[0])
    outs = []
    for g in range(SSM_GROUPS):
        yg = y[:, g * gw:(g + 1) * gw]
        outs.append(yg * lax.rsqrt(jnp.mean(yg * yg, axis=-1, keepdims=True) + RMS_EPS))
    o_ref[0] = jnp.concatenate(outs, axis=1) * nw_ref[...]


def _ssd(xbc, z, dt_raw, conv_w, conv_b, dt_bias, a_log, d_skip, norm_w):
    bsz, s, ch = xbc.shape
    d_ssm = z.shape[2]
    nh = dt_raw.shape[2]
    gw = d_ssm // SSM_GROUPS
    nchunks = s // SSM_CHUNK
    full = lambda shape: pl.BlockSpec(shape, lambda b, c: (0,) * len(shape))
    return pl.pallas_call(
        _ssd_kernel,
        grid=(bsz, nchunks),
        in_specs=[pl.BlockSpec((1, SSM_CHUNK, ch), lambda b, c: (b, c, 0)),
                  pl.BlockSpec((1, SSM_CHUNK, d_ssm), lambda b, c: (b, c, 0)),
                  pl.BlockSpec((1, SSM_CHUNK, nh), lambda b, c: (b, c, 0)),
                  full((CONV_WIDTH, ch)), full((1, ch)), full((1, nh)), full((1, nh)),
                  full((1, d_ssm)), full((1, d_ssm))],
        out_specs=pl.BlockSpec((1, SSM_CHUNK, d_ssm), lambda b, c: (b, c, 0)),
        out_shape=jax.ShapeDtypeStruct((bsz, s, d_ssm), F32),
        scratch_shapes=[pltpu.VMEM((SUBLANES + SSM_CHUNK, ch), F32),
                        pltpu.VMEM((SSM_GROUPS, SSM_STATE, gw), F32)],
        compiler_params=_cparams(("parallel", "arbitrary")),
        name="ssd_scan",
    )(xbc, z, dt_raw, conv_w, conv_b.reshape(1, ch), dt_bias.reshape(1, nh), a_log.reshape(1, nh),
      jnp.repeat(d_skip, SSM_HEAD_DIM).reshape(1, d_ssm), norm_w.reshape(1, d_ssm))


def _mix_out_kernel(oc_ref, os_ref, ow_ref, gl_ref, ex_ref, an_ref, ssm_ref, w_ref, x_ref, g1_ref, o_ref):
    d_attn = oc_ref.shape[2]
    sig = jax.nn.sigmoid(gl_ref[0])
    gc = jnp.dot(sig, ex_ref[0], precision=HI, preferred_element_type=F32)
    gs = jnp.dot(sig, ex_ref[1], precision=HI, preferred_element_type=F32)
    gw = jnp.dot(sig, ex_ref[2], precision=HI, preferred_element_type=F32)
    attn = gc * oc_ref[0] + gs * os_ref[0] + gw * ow_ref[0]
    attn = attn * lax.rsqrt(jnp.mean(attn * attn, axis=-1, keepdims=True) + RMS_EPS) * an_ref[...]
    mix = (jnp.dot(attn.astype(BF16), w_ref[:d_attn, :], preferred_element_type=F32)
           + jnp.dot(ssm_ref[0].astype(BF16), w_ref[d_attn:, :], preferred_element_type=F32))
    o_ref[0] = x_ref[0] + g1_ref[0] * mix


def _gate_expand():
    ex = np.zeros((3, LANES, N_HEADS * HEAD_DIM), np.float32)
    for r in range(3):
        for h in range(N_HEADS):
            ex[r, h * 3 + r, h * HEAD_DIM:(h + 1) * HEAD_DIM] = 1.0
    return ex


def _mix_out(oc, os_, ow, gl, attn_norm, ssm, w_out_bf16, x, g1, tm=256):
    bsz, s, d = x.shape
    d_attn = oc.shape[2]
    d_ssm = ssm.shape[2]
    tok = lambda w: pl.BlockSpec((1, tm, w), lambda b, i: (b, i, 0))
    return pl.pallas_call(
        _mix_out_kernel,
        grid=(bsz, s // tm),
        in_specs=[tok(d_attn), tok(d_attn), tok(d_attn), tok(LANES),
                  pl.BlockSpec((3, LANES, d_attn), lambda b, i: (0, 0, 0)),
                  pl.BlockSpec((1, d_attn), lambda b, i: (0, 0)),
                  tok(d_ssm),
                  pl.BlockSpec((d_attn + d_ssm, d), lambda b, i: (0, 0)),
                  tok(d),
                  pl.BlockSpec((1, 1, d), lambda b, i: (b, 0, 0))],
        out_specs=tok(d),
        out_shape=jax.ShapeDtypeStruct((bsz, s, d), F32),
        compiler_params=_cparams(("parallel", "parallel")),
        name="mix_out",
    )(oc, os_, ow, gl, jnp.asarray(_gate_expand()), attn_norm.reshape(1, d_attn), ssm, w_out_bf16, x, g1)


def _peer_candidates():
    k, sub = PEER_TOPK, SUBLANES
    cells = [(0, b) for b in range(k)]
    cells += [(a, b) for a in range(1, sub) for b in range(sub)]
    cells += [(a, 0) for a in range(sub, k)]
    order = np.array([a * k + b if (a + 1) * (b + 1) <= k else -1 for a, b in cells], np.float32)
    group = np.zeros((k, LANES), np.float32)
    for row, (a, _) in enumerate(cells):
        group[a, row] = 1.0
    return cells, order, group


def _peer_route_kernel(q_ref, sk_ref, order_ref, group_ref, rank2_ref, e2_ref, n1_ref, c1_ref):
    q = q_ref[...]
    half = q.shape[1] // 2
    s1 = _nt(sk_ref[0], q[:, :half], precision=HI)
    s2 = _nt(sk_ref[1], q[:, half:], precision=HI)
    k = PEER_TOPK
    rank1, v1 = _topk_mark(s1, k, axis=0)
    rank2, v2 = _topk_mark(s2, k, axis=0)
    v1_all = jnp.concatenate(v1, axis=0)
    v2_all = jnp.concatenate(v2, axis=0)
    cand = jnp.concatenate([v1[0] + v2_all] + [v1[a] + v2_all[:SUBLANES] for a in range(1, SUBLANES)]
                           + [v1_all[SUBLANES:] + v2[0]], axis=0)
    order = order_ref[...]
    cand = jnp.where(order >= 0.0, cand, -jnp.inf)
    rank_c, best = _topk_mark(cand, k, axis=0, order=order, order_bound=float(k * k))
    chosen = jnp.where(rank_c < float(k), 1.0, 0.0).astype(BF16)
    chosen = jnp.concatenate([chosen, jnp.zeros((LANES - chosen.shape[0], chosen.shape[1]), BF16)], axis=0)
    count = jnp.dot(group_ref[...], chosen, preferred_element_type=F32)
    z = best[0] * 0.0
    for r in range(k):
        z = z + jnp.exp(best[r] - best[0])
    n1 = jnp.zeros(s1.shape, F32)
    for a in range(k):
        n1 = jnp.where(rank1 == float(a), count[a:a + 1, :], n1)
    rank2_ref[0] = rank2
    e2_ref[0] = jnp.exp(s2 - v2[0])
    n1_ref[0] = n1
    c1_ref[0] = jnp.exp(s1 - v1[0]) / z


def _peer_route(q, subkeys, tm=256):
    t, width = q.shape
    kd = width // PEER_HEADS
    nk = subkeys.shape[1]
    spec = pl.BlockSpec((1, nk, tm), lambda i, h: (h, 0, i))
    shape = jax.ShapeDtypeStruct((PEER_HEADS, nk, t), F32)
    cells, order, group = _peer_candidates()
    order = jnp.asarray(np.broadcast_to(order[:, None], (len(cells), tm)))
    return pl.pallas_call(
        _peer_route_kernel,
        grid=(t // tm, PEER_HEADS),
        in_specs=[pl.BlockSpec((tm, kd), lambda i, h: (i, h)),
                  pl.BlockSpec((2, nk, kd // 2), lambda i, h: (0, 0, 0)),
                  pl.BlockSpec((len(cells), tm), lambda i, h: (0, 0)),
                  pl.BlockSpec((PEER_TOPK, LANES), lambda i, h: (0, 0))],
        out_specs=[spec, spec, spec, spec],
        out_shape=[shape, shape, shape, shape],
        compiler_params=_cparams(("parallel", "parallel")),
        name="peer_route",
    )(q, subkeys, order, jnp.asarray(group, BF16))


def _peer_dense_kernel(ht_ref, u_ref, vt_ref, rank2_ref, e2_ref, n1_ref, c1_ref, o_ref, *scratch):
    j = pl.program_id(1)
    te = u_ref.shape[0]
    nk = rank2_ref.shape[1]

    @pl.when(j == 0)
    def _():
        o_ref[...] = jnp.zeros(o_ref.shape, F32)

    tm = ht_ref.shape[1]
    strips = [slice(c * PEER_LANE_STRIP, (c + 1) * PEER_LANE_STRIP) for c in range(tm // PEER_LANE_STRIP)]

    units = [(e, c) for e in range(te // PEER_EXPERT_UNIT) for c in range(len(strips))]
    act_refs = scratch[:len(units)]
    aw_refs = scratch[len(units):]
    n_piece = PEER_EXPERT_UNIT // PEER_EXPERT_PIECE
    d_piece = o_ref.shape[0] // n_piece

    def produce(k, p):
        e, c = units[k]
        lo = e * PEER_EXPERT_UNIT + p * PEER_EXPERT_PIECE
        act_refs[k][p * PEER_EXPERT_PIECE:(p + 1) * PEER_EXPERT_PIECE, :] = jax.nn.gelu(
            jnp.dot(u_ref[lo:lo + PEER_EXPERT_PIECE, :], ht_ref[:, strips[c]], preferred_element_type=F32))

    def gate(k, r):
        e, c = units[k]
        ls = strips[c]
        i1 = (j * te + e * PEER_EXPERT_UNIT) // nk + r
        w = jnp.zeros((nk, PEER_LANE_STRIP), F32)
        for h in range(PEER_HEADS):
            n_row = n1_ref[h, pl.ds(i1, 1), ls]
            c_row = c1_ref[h, pl.ds(i1, 1), ls]
            w = w + jnp.where(rank2_ref[h, :, ls] < n_row, e2_ref[h, :, ls], 0.0) * c_row
        aw_refs[k][r * nk:(r + 1) * nk, :] = (act_refs[k][r * nk:(r + 1) * nk, :] * w).astype(BF16)

    def combine(k, m):
        e, c = units[k]
        ds_ = slice(m * d_piece, (m + 1) * d_piece)
        es = slice(e * PEER_EXPERT_UNIT, (e + 1) * PEER_EXPERT_UNIT)
        o_ref[ds_, strips[c]] += jnp.dot(vt_ref[ds_, es], aw_refs[k][...], preferred_element_type=F32)

    gates_per_piece = PEER_EXPERT_PIECE // nk
    for p in range(n_piece):
        produce(0, p)
    for k in range(len(units)):
        for p in range(n_piece):
            if k + 1 < len(units):
                produce(k + 1, p)
            for r in range(p * gates_per_piece, (p + 1) * gates_per_piece):
                gate(k, r)
                if k > 0 and r == p * gates_per_piece:
                    combine(k - 1, p)
    for m in range(n_piece):
        combine(len(units) - 1, m)


def _peer_dense(h_t, u_bf16, v_t_bf16, rank2, e2, n1, c1, tm=512, te=512):
    d, t = h_t.shape
    n_exp = u_bf16.shape[0]
    nk = rank2.shape[1]
    n_units = (te // PEER_EXPERT_UNIT) * (tm // PEER_LANE_STRIP)
    route = pl.BlockSpec((PEER_HEADS, nk, tm), lambda i, j: (0, 0, i))
    return pl.pallas_call(
        _peer_dense_kernel,
        grid=(t // tm, n_exp // te),
        in_specs=[pl.BlockSpec((d, tm), lambda i, j: (0, i)),
                  pl.BlockSpec((te, d), lambda i, j: (j, 0)),
                  pl.BlockSpec((d, te), lambda i, j: (0, j)),
                  route, route, route, route],
        out_specs=pl.BlockSpec((d, tm), lambda i, j: (0, i)),
        out_shape=jax.ShapeDtypeStruct((d, t), F32),
        scratch_shapes=([pltpu.VMEM((PEER_EXPERT_UNIT, PEER_LANE_STRIP), F32)] * n_units
                        + [pltpu.VMEM((PEER_EXPERT_UNIT, PEER_LANE_STRIP), BF16)] * n_units),
        compiler_params=_cparams(("parallel", "arbitrary")),
        name="peer_dense",
    )(h_t, u_bf16, v_t_bf16, rank2, e2, n1, c1)


def _residual_kernel(x_ref, y_ref, g_ref, o_ref):
    o_ref[0] = x_ref[0] + g_ref[0] * y_ref[0]


def _residual(x, y, g, tm=512):
    bsz, s, d = x.shape
    tok = pl.BlockSpec((1, tm, d), lambda b, i: (b, i, 0))
    return pl.pallas_call(
        _residual_kernel,
        grid=(bsz, s // tm),
        in_specs=[tok, tok, pl.BlockSpec((1, 1, d), lambda b, i: (b, 0, 0))],
        out_specs=tok,
        out_shape=jax.ShapeDtypeStruct((bsz, s, d), F32),
        compiler_params=_cparams(("parallel", "parallel")),
        name="residual",
    )(x, y, g)


def _final_norm_kernel(x_ref, w_ref, o_ref):
    x = x_ref[0]
    o_ref[0] = x * lax.rsqrt(jnp.mean(x * x, axis=-1, keepdims=True) + RMS_EPS) * w_ref[...]


def _final_norm(x, w, tm=512):
    bsz, s, d = x.shape
    tok = pl.BlockSpec((1, tm, d), lambda b, i: (b, i, 0))
    return pl.pallas_call(
        _final_norm_kernel,
        grid=(bsz, s // tm),
        in_specs=[tok, pl.BlockSpec((1, d), lambda b, i: (0, 0))],
        out_specs=tok,
        out_shape=jax.ShapeDtypeStruct((bsz, s, d), F32),
        compiler_params=_cparams(("parallel", "parallel")),
        name="final_norm",
    )(x, w.reshape(1, d))


def _pad_cols(w, width):
    return jnp.pad(w, ((0, 0), (0, width - w.shape[1])))


def _nsa(q, kv, tables, cmp_pe, cmp_w1, cmp_w2):
    tab_win, tab_cmp, tab_slc, far_col, far_t = tables
    bsz, s, _ = q.shape
    nc = s // CMP_STRIDE
    nq = s // Q_BLOCK
    n_slc = s // SLC_BLOCK
    nbp = -(-n_slc // SLC_GROUP_BLOCKS) * SLC_GROUP_BLOCKS
    scale = HEAD_DIM ** -0.5
    q6 = (q * scale).astype(BF16).reshape(bsz, nq, Q_BLOCK, N_KV, N_GRP, HEAD_DIM)
    qh = q6.transpose(0, 3, 4, 1, 2, 5).reshape(bsz, N_KV, N_GRP, s, HEAD_DIM)
    qt = q6.transpose(0, 3, 1, 5, 4, 2).reshape(bsz, N_KV, nq, HEAD_DIM, N_GRP * Q_BLOCK)
    qt = jnp.pad(qt, ((0, 0),) * 3 + ((0, LANES - HEAD_DIM), (0, 0)))
    kv6 = kv.reshape(bsz, s, 6, N_KV, HEAD_DIM)

    kv_cmp = kv6[:, :, 0:2].reshape(bsz, nc, CMP_STRIDE, 2, N_KV, HEAD_DIM)
    kv_cmp = kv_cmp.transpose(0, 3, 4, 1, 2, 5).reshape(bsz, 2, N_KV, nc, CMP_STRIDE * HEAD_DIM)
    kv_c = _compress(kv_cmp, cmp_pe, cmp_w1, cmp_w2)
    o_c, sel_t = _cmp_attn(qh, kv_c, tab_cmp, far_col, nbp)

    k_slc = kv6[:, :, 2].transpose(0, 2, 1, 3).astype(BF16)
    nkt = s // SLC_KEY_TILE
    vt = kv6[:, :, 3].astype(BF16).reshape(bsz, nkt, SLC_KEY_TILE, N_KV, HEAD_DIM).transpose(0, 3, 1, 4, 2)
    vt_aug = jnp.concatenate([
        vt, jnp.ones((bsz, N_KV, nkt, 1, SLC_KEY_TILE), BF16),
        jnp.zeros((bsz, N_KV, nkt, SLC_V_ROWS - HEAD_DIM - 1, SLC_KEY_TILE), BF16)], axis=3)
    blk = np.arange(s) // SLC_BLOCK
    onehot = (blk[:, None] % SLC_GROUP_BLOCKS == np.arange(SLC_GROUP_BLOCKS)[None, :]).astype(np.float32)
    k_aug = jnp.concatenate([
        k_slc, jnp.ones((bsz, N_KV, s, SLC_BIAS_PIECES), BF16),
        jnp.zeros((bsz, N_KV, s, LANES - HEAD_DIM - SLC_BIAS_PIECES), BF16),
        jnp.broadcast_to(jnp.asarray(onehot, BF16), (bsz, N_KV, s, SLC_GROUP_BLOCKS))], axis=-1)
    o_st = _slc_attn(qt, far_t, k_aug, vt_aug, sel_t, tab_slc)
    o_s = o_st.reshape(bsz, N_KV, nq, HEAD_DIM, N_GRP, Q_BLOCK).transpose(0, 2, 5, 1, 4, 3)
    o_s = o_s.reshape(bsz, s, N_HEADS * HEAD_DIM)

    front = ((0, 0), (0, 0), (WINDOW, 0), (0, 0))
    k_win = jnp.pad(kv6[:, :, 4].transpose(0, 2, 1, 3).astype(BF16), front)
    v_win = jnp.pad(kv6[:, :, 5].transpose(0, 2, 1, 3).astype(BF16), front)
    o_w = _win_attn(qh, k_win, v_win, tab_win)

    back = lambda o: o.transpose(0, 3, 1, 2, 4).reshape(bsz, s, N_HEADS * HEAD_DIM)
    return back(o_c), o_s, back(o_w)


def kernel(x, c, ada_w, ada_b, norm_mix, norm_ffn, w_in, cmp_pe, cmp_w1, cmp_w2, rel_bias, attn_out_norm,
           conv_w, conv_b, dt_bias, a_log, d_skip, ssm_norm, w_out, peer_wq, peer_subkeys, peer_u, peer_v,
           norm_final):
    bsz, s, d = x.shape
    depth = ada_w.shape[0]
    d_attn = N_HEADS * HEAD_DIM
    n_kv = 6 * N_KV * HEAD_DIM
    n_gate = 3 * N_HEADS
    d_ssm = ssm_norm.shape[1]
    ch = conv_w.shape[2]
    nh = dt_bias.shape[1]

    mod = _ada_mod(c, ada_w, ada_b)
    tables = _bias_tables(rel_bias)

    for l in range(depth):
        sh1, sc1, g1, sh2, sc2, g2 = [mod[l, :, i * d:(i + 1) * d].reshape(bsz, 1, d) for i in range(6)]
        cuts = np.cumsum([0, d_attn, n_kv, n_gate, d_ssm, ch, nh])
        seg = [w_in[l][:, cuts[i]:cuts[i + 1]] for i in range(6)]
        seg[2] = _pad_cols(seg[2], LANES)
        seg[5] = _pad_cols(seg[5], LANES)
        w_cat = jnp.concatenate(seg, axis=1).astype(BF16)
        pc = np.cumsum([0] + [w.shape[1] for w in seg])
        proj, _ = _norm_proj(x, norm_mix[l], sc1, sh1, w_cat)
        q = proj[:, :, pc[0]:pc[1]]
        kv = proj[:, :, pc[1]:pc[2]]
        gl = proj[:, :, pc[2]:pc[3]]
        z = proj[:, :, pc[3]:pc[4]]
        xbc = proj[:, :, pc[4]:pc[5]]
        dt_raw = proj[:, :, pc[5]:pc[5] + nh]

        o_c, o_s, o_w = _nsa(q, kv, tables, cmp_pe[l], cmp_w1[l], cmp_w2[l])
        ssm = _ssd(xbc, z, dt_raw, conv_w[l], conv_b[l], dt_bias[l], a_log[l], d_skip[l], ssm_norm[l])
        x = _mix_out(o_c, o_s, o_w, gl, attn_out_norm[l], ssm, w_out[l].astype(BF16), x, g1)

        pq, h2 = _norm_proj(x, norm_ffn[l], sc2, sh2, peer_wq[l].astype(BF16))
        rank2, e2, n1, c1 = _peer_route(pq.reshape(bsz * s, -1), peer_subkeys[l])
        h_t = h2.reshape(bsz * s, d).T
        ffn_t = _peer_dense(h_t, peer_u[l].astype(BF16), peer_v[l].T.astype(BF16), rank2, e2, n1, c1)
        x = _residual(x, ffn_t.T.reshape(bsz, s, d), g2)

    return _final_norm(x, norm_final)
```

```python
import functools
import math

import numpy as np
import jax
import jax.numpy as jnp
from jax import lax
from jax.experimental import pallas as pl
from jax.experimental.pallas import tpu as pltpu

F32 = jnp.float32
BF16 = jnp.bfloat16
HI = lax.Precision.HIGHEST

N_HEADS = 16
N_KV = 2
N_GRP = N_HEADS // N_KV
HEAD_DIM = 64
CMP_BLOCK = 32
CMP_STRIDE = 16
CMP_HIDDEN = 4 * HEAD_DIM
SLC_BLOCK = 64
SLC_TOPN = 16
WINDOW = 512
Q_BLOCK = 128
FORCE_SCORE = 1e4
NEG_INF = -1e30
REL_BUCKETS = 32
REL_MAX_DIST = 2048
SSM_HEAD_DIM = 64
SSM_GROUPS = 2
SSM_STATE = 128
CONV_WIDTH = 4
SSM_CHUNK = 256
PEER_HEADS = 8
PEER_NKEYS = 128
PEER_TOPK = 16
RMS_EPS = 1e-6

LANES = 128
SUBLANES = 8
VMEM_LIMIT = 56 * 1024 * 1024

SLC_KEY_TILE = 512
SLC_TILE_BLOCKS = SLC_KEY_TILE // SLC_BLOCK
SLC_GROUP_BLOCKS = LANES
SLC_NEAR_CHUNKS = 14
SLC_TAB_MASKED = 14
SLC_TAB_CONST = 15
MASK_BIG = 2.0 ** 100
SLC_LANE_STRIP = 256
PEER_LANE_STRIP = 256
PEER_EXPERT_PIECE = 256
PEER_EXPERT_UNIT = 512
SLC_V_ROWS = HEAD_DIM + 16
SLC_BIAS_PIECES = 3

CMP_NEAR = 128
CMP_PAD = CMP_NEAR - Q_BLOCK // CMP_STRIDE


def _cparams(sem, vmem=VMEM_LIMIT):
    return pltpu.CompilerParams(dimension_semantics=sem, vmem_limit_bytes=vmem)


def _nt(a, b, precision=None):
    return lax.dot_general(a, b, (((1,), (1,)), ((), ())), precision=precision,
                           preferred_element_type=F32)


def _silu(x):
    return x * jax.nn.sigmoid(x)


def _rel_bucket_np(d):
    d = np.maximum(np.asarray(d, np.int64), 0)
    max_exact = REL_BUCKETS // 2
    ratio = np.log(np.maximum(d, max_exact).astype(np.float64) / max_exact) / math.log(REL_MAX_DIST / max_exact)
    scaled = ratio * (REL_BUCKETS - max_exact)
    large = max_exact + np.floor(scaled).astype(np.int64)
    return np.where(d < max_exact, d, np.minimum(large, REL_BUCKETS - 1)).astype(np.int32)


def _ada_kernel(c_ref, w_ref, b_ref, o_ref):
    cond = _silu(c_ref[...])
    o_ref[0] = jnp.dot(cond, w_ref[0], precision=HI, preferred_element_type=F32) + b_ref[0]


def _ada_mod(c, ada_w, ada_b):
    depth, d, n = ada_w.shape
    bsz = c.shape[0]
    rows = SUBLANES
    c_pad = jnp.zeros((rows, d), F32).at[:bsz].set(c)
    tn = 1024
    out = pl.pallas_call(
        _ada_kernel,
        grid=(depth, n // tn),
        in_specs=[pl.BlockSpec((rows, d), lambda l, j: (0, 0)),
                  pl.BlockSpec((1, d, tn), lambda l, j: (l, 0, j)),
                  pl.BlockSpec((1, 1, tn), lambda l, j: (l, 0, j))],
        out_specs=pl.BlockSpec((1, rows, tn), lambda l, j: (l, 0, j)),
        out_shape=jax.ShapeDtypeStruct((depth, rows, n), F32),
        compiler_params=_cparams(("parallel", "parallel")),
        name="ada_mod",
    )(c_pad, ada_w, ada_b.reshape(depth, 1, n))
    return out[:, :bsz]


def _norm_proj_kernel(x_ref, nw_ref, sc_ref, sh_ref, w_ref, o_ref, h_ref, hs_ref):
    @pl.when(pl.program_id(2) == 0)
    def _():
        x = x_ref[0]
        y = x * lax.rsqrt(jnp.mean(x * x, axis=-1, keepdims=True) + RMS_EPS)
        h = (y * nw_ref[...]) * (1.0 + sc_ref[0]) + sh_ref[0]
        hs_ref[...] = h.astype(BF16)
        h_ref[0] = h.astype(BF16)

    o_ref[0] = jnp.dot(hs_ref[...], w_ref[...], preferred_element_type=F32)


def _norm_proj(x, nw, sc, sh, w_bf16, tm=512, tn=512):
    bsz, s, d = x.shape
    n = w_bf16.shape[1]
    return pl.pallas_call(
        _norm_proj_kernel,
        grid=(bsz, s // tm, n // tn),
        in_specs=[pl.BlockSpec((1, tm, d), lambda b, i, j: (b, i, 0)),
                  pl.BlockSpec((1, d), lambda b, i, j: (0, 0)),
                  pl.BlockSpec((1, 1, d), lambda b, i, j: (b, 0, 0)),
                  pl.BlockSpec((1, 1, d), lambda b, i, j: (b, 0, 0)),
                  pl.BlockSpec((d, tn), lambda b, i, j: (0, j))],
        out_specs=[pl.BlockSpec((1, tm, tn), lambda b, i, j: (b, i, j)),
                   pl.BlockSpec((1, tm, d), lambda b, i, j: (b, i, 0))],
        out_shape=[jax.ShapeDtypeStruct((bsz, s, n), F32),
                   jax.ShapeDtypeStruct((bsz, s, d), BF16)],
        scratch_shapes=[pltpu.VMEM((tm, d), BF16)],
        compiler_params=_cparams(("parallel", "parallel", "arbitrary")),
        name="norm_proj",
    )(x, nw.reshape(1, d), sc, sh, w_bf16)


def _bias_kernel(rel_ref, bk_ref, o_ref):
    h = pl.program_id(0)
    bk = bk_ref[...]
    acc = jnp.full(bk.shape, NEG_INF, F32)
    for b in range(REL_BUCKETS):
        acc = jnp.where(bk == b, rel_ref[b, h], acc)
    o_ref[0] = acc


def _bias_tables(rel_bias):
    r = np.arange(Q_BLOCK)[:, None]
    dw = r - np.arange(WINDOW + Q_BLOCK)[None, :] + WINDOW
    win = np.where((dw >= 0) & (dw < WINDOW), _rel_bucket_np(dw), -1)
    off = CMP_STRIDE * CMP_PAD - (CMP_BLOCK - 1)
    dc = r + off - CMP_STRIDE * np.arange(CMP_NEAR)[None, :]
    cmp_near = np.where(dc >= 0, _rel_bucket_np(dc), -1)
    chunks = []
    for m in range(SLC_NEAR_CHUNKS):
        ds_ = Q_BLOCK * m + r - np.arange(Q_BLOCK)[None, :]
        chunks.append(np.where(ds_ >= 0, _rel_bucket_np(ds_), -1))
    assert _rel_bucket_np(Q_BLOCK * SLC_NEAR_CHUNKS - (Q_BLOCK - 1)) == REL_BUCKETS - 1
    assert _rel_bucket_np(off + CMP_STRIDE) == REL_BUCKETS - 1
    chunks.append(np.full((Q_BLOCK, Q_BLOCK), -1))
    chunks.append(np.full((Q_BLOCK, Q_BLOCK), REL_BUCKETS - 1))
    bk = np.concatenate([win, cmp_near] + chunks, axis=1).astype(np.int32)
    cols = bk.shape[1]
    out = pl.pallas_call(
        _bias_kernel,
        grid=(N_HEADS,),
        in_specs=[pl.BlockSpec(memory_space=pltpu.SMEM),
                  pl.BlockSpec((Q_BLOCK, cols), lambda h: (0, 0))],
        out_specs=pl.BlockSpec((1, Q_BLOCK, cols), lambda h: (h, 0, 0)),
        out_shape=jax.ShapeDtypeStruct((N_HEADS, Q_BLOCK, cols), F32),
        compiler_params=_cparams(("arbitrary",)),
        name="bias_tables",
    )(rel_bias, jnp.asarray(bk))
    nw = WINDOW + Q_BLOCK
    tab_win = out[:, :, :nw].reshape(N_KV, N_GRP, Q_BLOCK, nw)
    tab_cmp = out[:, :, nw:nw + CMP_NEAR].reshape(N_KV, N_GRP, Q_BLOCK, CMP_NEAR)
    nch = SLC_NEAR_CHUNKS + 2
    tab_slc = out[:, :, nw + CMP_NEAR:].reshape(N_KV, N_GRP, Q_BLOCK, nch, Q_BLOCK)
    tab_slc = tab_slc.transpose(0, 3, 4, 1, 2).reshape(N_KV, nch, Q_BLOCK, N_GRP * Q_BLOCK)
    far = rel_bias[REL_BUCKETS - 1].reshape(N_KV, N_GRP, 1)
    far_col = jnp.broadcast_to(far, (N_KV, N_GRP, Q_BLOCK)).reshape(N_KV, N_GRP * Q_BLOCK, 1)
    far_row = far_col.reshape(N_KV, 1, N_GRP * Q_BLOCK)
    pieces, rest = [], far_row
    for _ in range(SLC_BIAS_PIECES):
        piece = rest.astype(BF16)
        pieces.append(piece)
        rest = rest - piece.astype(F32)
    zeros = lambda w: jnp.zeros((N_KV, w, N_GRP * Q_BLOCK), BF16)
    far_t = jnp.concatenate([zeros(HEAD_DIM)] + pieces + [zeros(LANES - HEAD_DIM - SLC_BIAS_PIECES)], axis=1)
    return tab_win, tab_cmp, tab_slc, far_col, far_t


def _compress_kernel(a_ref, pe_ref, w1_ref, w2_ref, o_ref):
    a = a_ref[0, 0, 0]
    half = a.shape[1]
    lo = jnp.dot((a + pe_ref[0, 0:1, :]).astype(BF16), w1_ref[0, :half, :].astype(BF16),
                 preferred_element_type=F32)
    hi = jnp.dot((a + pe_ref[0, 1:2, :]).astype(BF16), w1_ref[0, half:, :].astype(BF16),
                 preferred_element_type=F32)
    nc = a.shape[0]
    hid = jax.nn.gelu(lo + pltpu.roll(hi, nc - 1, axis=0))
    out = jnp.dot(hid.astype(BF16), w2_ref[0].astype(BF16), preferred_element_type=F32)
    o_ref[0, 0, 0] = jnp.zeros(o_ref.shape[3:], F32)
    o_ref[0, 0, 0, CMP_PAD:CMP_PAD + nc, :] = out


def _compress(kv_cmp, cmp_pe, cmp_w1, cmp_w2):
    bsz, _, _, nc, half = kv_cmp.shape
    pe = cmp_pe.reshape(2, 2, half)
    rows = CMP_PAD + nc + SUBLANES
    return pl.pallas_call(
        _compress_kernel,
        grid=(bsz, 2, N_KV),
        in_specs=[pl.BlockSpec((1, 1, 1, nc, half), lambda b, w, k: (b, w, k, 0, 0)),
                  pl.BlockSpec((1, 2, half), lambda b, w, k: (w, 0, 0)),
                  pl.BlockSpec((1, 2 * half, CMP_HIDDEN), lambda b, w, k: (w, 0, 0)),
                  pl.BlockSpec((1, CMP_HIDDEN, HEAD_DIM), lambda b, w, k: (w, 0, 0))],
        out_specs=pl.BlockSpec((1, 1, 1, rows, HEAD_DIM), lambda b, w, k: (b, w, k, 0, 0)),
        out_shape=jax.ShapeDtypeStruct((bsz, 2, N_KV, rows, HEAD_DIM), F32),
        compiler_params=_cparams(("parallel", "parallel", "parallel")),
        name="nsa_compress",
    )(kv_cmp, pe, cmp_w1, cmp_w2)


def _topk_mark(vals, k, axis, order=None, order_bound=None):
    n = vals.shape[axis] if order is None else order_bound
    iota = lax.broadcasted_iota(jnp.int32, vals.shape, axis).astype(F32) if order is None else order
    rank = jnp.full(vals.shape, float(k), F32)
    work = vals
    picked = []
    for r in range(k):
        m = jnp.max(work, axis=axis, keepdims=True)
        ix = jnp.min(jnp.where(work == m, iota, float(n)), axis=axis, keepdims=True)
        hit = iota == ix
        rank = jnp.where(hit, float(r), rank)
        work = jnp.where(hit, -jnp.inf, work)
        picked.append(m)
    return rank, picked


def _cmp_attn_kernel(q_ref, k_ref, v_ref, tab_ref, far_ref, m_ref, o_ref, sel_ref):
    i = pl.program_id(2)
    rows = N_GRP * Q_BLOCK
    nc = k_ref.shape[3] - CMP_PAD - SUBLANES
    nbp = sel_ref.shape[3]
    q = q_ref[0, 0].reshape(rows, HEAD_DIM)
    start = pl.multiple_of(i * (Q_BLOCK // CMP_STRIDE), SUBLANES)

    k_far = k_ref[0, 0, 0, CMP_PAD:CMP_PAD + nc, :].astype(BF16)
    v_far = v_ref[0, 0, 0, CMP_PAD:CMP_PAD + nc, :].astype(BF16)
    k_near = k_ref[0, 0, 0, pl.ds(start, CMP_NEAR), :].astype(BF16)
    v_near = v_ref[0, 0, 0, pl.ds(start, CMP_NEAR), :].astype(BF16)

    n_idx = lax.broadcasted_iota(jnp.int32, (1, nc), 1)
    mask_far = n_idx < (i * (Q_BLOCK // CMP_STRIDE) - CMP_PAD)
    s_far = jnp.where(mask_far, _nt(q, k_far) + far_ref[0], NEG_INF)
    tab = tab_ref[0].reshape(rows, CMP_NEAR)
    c_idx = lax.broadcasted_iota(jnp.int32, (1, CMP_NEAR), 1)
    mask_near = (c_idx >= (CMP_PAD - i * (Q_BLOCK // CMP_STRIDE))) & (tab > 0.5 * NEG_INF)
    s_near = jnp.where(mask_near, _nt(q, k_near) + tab, NEG_INF)

    m = jnp.maximum(jnp.max(s_far, axis=-1, keepdims=True), jnp.max(s_near, axis=-1, keepdims=True))
    e_far = jnp.exp(s_far - m)
    e_near = jnp.exp(s_near - m)
    l = jnp.sum(e_far, axis=-1, keepdims=True) + jnp.sum(e_near, axis=-1, keepdims=True)
    p_far = jnp.where(mask_far, e_far / l, 0.0)
    p_near = jnp.where(mask_near, e_near / l, 0.0)
    o = (jnp.dot(p_far.astype(BF16), v_far, preferred_element_type=F32)
         + jnp.dot(p_near.astype(BF16), v_near, preferred_element_type=F32))
    o_ref[0, 0] = o.reshape(N_GRP, Q_BLOCK, HEAD_DIM)

    ps_far = jnp.sum(p_far.reshape(N_GRP, Q_BLOCK, nc), axis=0)
    ps_near = jnp.sum(p_near.reshape(N_GRP, Q_BLOCK, CMP_NEAR), axis=0)
    imp = (jnp.dot(ps_far, m_ref[CMP_PAD:CMP_PAD + nc, :], precision=HI, preferred_element_type=F32)
           + jnp.dot(ps_near, m_ref[pl.ds(start, CMP_NEAR), :], precision=HI, preferred_element_type=F32))
    t = i * Q_BLOCK + lax.broadcasted_iota(jnp.int32, (Q_BLOCK, 1), 0)
    cur = t // SLC_BLOCK
    blk = lax.broadcasted_iota(jnp.int32, (1, nbp), 1)
    forced = (blk == 0) | (blk == cur) | (blk == cur - 1)
    imp = jnp.where(forced, FORCE_SCORE, jnp.where(blk <= cur, imp, -FORCE_SCORE))
    n_blocks = (nc * CMP_STRIDE) // SLC_BLOCK
    imp = jnp.where(blk < n_blocks, imp, -jnp.inf)
    rank, _ = _topk_mark(imp, min(SLC_TOPN, n_blocks), axis=1)
    sel_ref[0, 0, 0] = jnp.where(rank < float(SLC_TOPN), 1.0, 0.0).T.astype(BF16)


def _overlap_matrix(nc, nbp):
    n_cmp = nc - 1
    n_slc = nc * CMP_STRIDE // SLC_BLOCK
    j = np.arange(n_slc)
    lo = np.clip((j * SLC_BLOCK - CMP_BLOCK) // CMP_STRIDE + 1, 0, n_cmp)
    hi = np.clip(-((-(j * SLC_BLOCK + SLC_BLOCK)) // CMP_STRIDE), 0, n_cmp)
    m = np.zeros((CMP_PAD + nc + SUBLANES, nbp), np.float32)
    n = np.arange(nc)[:, None]
    m[CMP_PAD:CMP_PAD + nc, :n_slc] = (n >= lo[None, :]) & (n < hi[None, :])
    return m


def _cmp_attn(q64, kv_c, tab_cmp, far_col, nbp):
    bsz, _, _, s, _ = q64.shape
    rows_c = kv_c.shape[3]
    nc = rows_c - CMP_PAD - SUBLANES
    nq = s // Q_BLOCK
    m_pad = jnp.asarray(_overlap_matrix(nc, nbp))
    return pl.pallas_call(
        _cmp_attn_kernel,
        grid=(bsz, N_KV, nq),
        in_specs=[pl.BlockSpec((1, 1, N_GRP, Q_BLOCK, HEAD_DIM), lambda b, k, i: (b, k, 0, i, 0)),
                  pl.BlockSpec((1, 1, 1, rows_c, HEAD_DIM), lambda b, k, i: (b, 0, k, 0, 0)),
                  pl.BlockSpec((1, 1, 1, rows_c, HEAD_DIM), lambda b, k, i: (b, 1, k, 0, 0)),
                  pl.BlockSpec((1, N_GRP, Q_BLOCK, CMP_NEAR), lambda b, k, i: (k, 0, 0, 0)),
                  pl.BlockSpec((1, N_GRP * Q_BLOCK, 1), lambda b, k, i: (k, 0, 0)),
                  pl.BlockSpec((rows_c, nbp), lambda b, k, i: (0, 0))],
        out_specs=[pl.BlockSpec((1, 1, N_GRP, Q_BLOCK, HEAD_DIM), lambda b, k, i: (b, k, 0, i, 0)),
                   pl.BlockSpec((1, 1, 1, nbp, Q_BLOCK), lambda b, k, i: (b, k, i, 0, 0))],
        out_shape=[jax.ShapeDtypeStruct((bsz, N_KV, N_GRP, s, HEAD_DIM), F32),
                   jax.ShapeDtypeStruct((bsz, N_KV, nq, nbp, Q_BLOCK), BF16)],
        compiler_params=_cparams(("parallel", "parallel", "arbitrary")),
        name="nsa_cmp_attn",
    )(q64, kv_c, kv_c, tab_cmp, far_col, m_pad)


def _slc_attn_kernel(qt_ref, far_ref, ka_ref, vt_ref, selt_ref, tab_ref, o_ref,
                     qa_ref, s0_ref, s1_ref, x0_ref, x1_ref, m_ref, acc_ref):
    i = pl.program_id(2)
    cols = N_GRP * Q_BLOCK
    ngroups = qa_ref.shape[0] // 2
    qt = qt_ref[0, 0, 0]
    qt_far = qt + far_ref[0]
    selneg = ((selt_ref[0, 0, 0].astype(F32) - 1.0) * MASK_BIG).astype(BF16)
    for g in range(ngroups):
        part = selneg[g * SLC_GROUP_BLOCKS:(g + 1) * SLC_GROUP_BLOCKS, :]
        part = jnp.concatenate([part] * N_GRP, axis=1)
        qa_ref[2 * g] = jnp.concatenate([qt_far, part], axis=0)
        qa_ref[2 * g + 1] = jnp.concatenate([qt, part], axis=0)

    tiles_per_group = SLC_GROUP_BLOCKS // SLC_TILE_BLOCKS
    sub = SLC_KEY_TILE // Q_BLOCK
    last_tile = vt_ref.shape[2] - 1
    n_pairs = (i // sub + 2) // 2
    n_far = jnp.maximum(0, (i - (SLC_NEAR_CHUNKS - 1)) // sub)
    far_pairs = jnp.maximum(0, (n_far - 1) // 2)
    first_table = jnp.where(n_far > 0, 2 * far_pairs + 1, 0)

    def produce(kt, ls, s_ref, mx_ref, with_table):
        near = (kt >= first_table).astype(jnp.int32)
        kc = jnp.minimum(kt, last_tile)
        ks = pl.multiple_of(kc * SLC_KEY_TILE, SLC_KEY_TILE)
        s = jnp.dot(ka_ref[0, 0, pl.ds(ks, SLC_KEY_TILE), :], qa_ref[2 * (kc // tiles_per_group) + near, :, ls],
                    preferred_element_type=F32)
        if with_table:
            chunks = []
            for a in range(sub):
                mm = i - sub * kt - a
                idx = jnp.where(mm < 0, SLC_TAB_MASKED, jnp.where(mm >= SLC_NEAR_CHUNKS, SLC_TAB_CONST, mm))
                chunks.append(tab_ref[0, idx, :, ls])
            s = s + jnp.concatenate(chunks, axis=0)
        s_ref[:, ls] = s
        mx_ref[:, ls] = jnp.max(s, axis=0, keepdims=True)

    def consume(kt, ls, s_ref, mx_ref):
        m_old = m_ref[:, ls]
        m_new = jnp.maximum(m_old, mx_ref[:, ls])
        m_ref[:, ls] = m_new
        p = jnp.exp(s_ref[:, ls] - m_new).astype(BF16)
        acc_ref[:, ls] = jnp.exp(m_old - m_new) * acc_ref[:, ls] + jnp.dot(
            vt_ref[0, 0, jnp.minimum(kt, last_tile)], p, preferred_element_type=F32)

    strips = [slice(c * SLC_LANE_STRIP, (c + 1) * SLC_LANE_STRIP) for c in range(cols // SLC_LANE_STRIP)]

    def pair_step(j, with_table):
        for ls in strips:
            produce(2 * j + 1, ls, s1_ref, x1_ref, with_table)
            consume(2 * j, ls, s0_ref, x0_ref)
        for ls in strips:
            produce(2 * j + 2, ls, s0_ref, x0_ref, with_table)
            consume(2 * j + 1, ls, s1_ref, x1_ref)

    m_ref[...] = jnp.full(m_ref.shape, -jnp.inf, F32)
    acc_ref[...] = jnp.zeros(acc_ref.shape, F32)

    @pl.when(n_far > 0)
    def _():
        for ls in strips:
            produce(0, ls, s0_ref, x0_ref, False)

    @pl.when(n_far == 0)
    def _():
        for ls in strips:
            produce(0, ls, s0_ref, x0_ref, True)

    def far_body(j, carry):
        pair_step(j, False)
        return carry

    def near_body(j, carry):
        pair_step(j, True)
        return carry

    lax.fori_loop(0, far_pairs, far_body, 0)
    lax.fori_loop(far_pairs, n_pairs, near_body, 0)
    o_ref[0, 0, 0] = acc_ref[:HEAD_DIM, :] / acc_ref[HEAD_DIM:HEAD_DIM + 1, :]


def _slc_attn(qt, far_t, k_aug, vt_aug, sel_t, tab_t):
    bsz, _, nq, _, cols = qt.shape
    s = k_aug.shape[2]
    nbp = sel_t.shape[3]
    ngroups = nbp // SLC_GROUP_BLOCKS
    nch = tab_t.shape[1]
    once = pl.Buffered(1)
    return pl.pallas_call(
        _slc_attn_kernel,
        grid=(bsz, N_KV, nq),
        in_specs=[pl.BlockSpec((1, 1, 1, LANES, cols), lambda b, k, i: (b, k, i, 0, 0)),
                  pl.BlockSpec((1, LANES, cols), lambda b, k, i: (k, 0, 0)),
                  pl.BlockSpec((1, 1, s, 2 * LANES), lambda b, k, i: (b, k, 0, 0), pipeline_mode=once),
                  pl.BlockSpec((1, 1, s // SLC_KEY_TILE, SLC_V_ROWS, SLC_KEY_TILE),
                               lambda b, k, i: (b, k, 0, 0, 0), pipeline_mode=once),
                  pl.BlockSpec((1, 1, 1, nbp, Q_BLOCK), lambda b, k, i: (b, k, i, 0, 0)),
                  pl.BlockSpec((1, nch, Q_BLOCK, cols), lambda b, k, i: (k, 0, 0, 0), pipeline_mode=once)],
        out_specs=pl.BlockSpec((1, 1, 1, HEAD_DIM, cols), lambda b, k, i: (b, k, i, 0, 0)),
        out_shape=jax.ShapeDtypeStruct((bsz, N_KV, nq, HEAD_DIM, cols), F32),
        scratch_shapes=[pltpu.VMEM((2 * ngroups, 2 * LANES, cols), BF16),
                        pltpu.VMEM((SLC_KEY_TILE, cols), F32),
                        pltpu.VMEM((SLC_KEY_TILE, cols), F32),
                        pltpu.VMEM((1, cols), F32),
                        pltpu.VMEM((1, cols), F32),
                        pltpu.VMEM((1, cols), F32),
                        pltpu.VMEM((SLC_V_ROWS, cols), F32)],
        compiler_params=_cparams(("parallel", "parallel", "arbitrary")),
        name="nsa_slc_attn",
    )(qt, far_t, k_aug, vt_aug, sel_t, tab_t)


def _win_attn_kernel(q_ref, k_ref, v_ref, tab_ref, o_ref):
    i = pl.program_id(2)
    rows = N_GRP * Q_BLOCK
    nw = WINDOW + Q_BLOCK
    q = q_ref[0, 0].reshape(rows, HEAD_DIM)
    qs = pl.multiple_of(i * Q_BLOCK, Q_BLOCK)
    k = k_ref[0, 0, pl.ds(qs, nw), :]
    v = v_ref[0, 0, pl.ds(qs, nw), :]
    s = _nt(q, k) + tab_ref[0].reshape(rows, nw)
    col = lax.broadcasted_iota(jnp.int32, (1, nw), 1)
    s = jnp.where(col >= WINDOW - i * Q_BLOCK, s, NEG_INF)
    m = jnp.max(s, axis=-1, keepdims=True)
    e = jnp.exp(s - m)
    p = e / jnp.sum(e, axis=-1, keepdims=True)
    o = jnp.dot(p.astype(BF16), v, preferred_element_type=F32)
    o_ref[0, 0] = o.reshape(N_GRP, Q_BLOCK, HEAD_DIM)


def _win_attn(q64, k_win, v_win, tab_win):
    bsz, _, _, s, _ = q64.shape
    nq = s // Q_BLOCK
    sp = k_win.shape[2]
    nw = WINDOW + Q_BLOCK
    return pl.pallas_call(
        _win_attn_kernel,
        grid=(bsz, N_KV, nq),
        in_specs=[pl.BlockSpec((1, 1, N_GRP, Q_BLOCK, HEAD_DIM), lambda b, k, i: (b, k, 0, i, 0)),
                  pl.BlockSpec((1, 1, sp, HEAD_DIM), lambda b, k, i: (b, k, 0, 0)),
                  pl.BlockSpec((1, 1, sp, HEAD_DIM), lambda b, k, i: (b, k, 0, 0)),
                  pl.BlockSpec((1, N_GRP, Q_BLOCK, nw), lambda b, k, i: (k, 0, 0, 0))],
        out_specs=pl.BlockSpec((1, 1, N_GRP, Q_BLOCK, HEAD_DIM), lambda b, k, i: (b, k, 0, i, 0)),
        out_shape=jax.ShapeDtypeStruct((bsz, N_KV, N_GRP, s, HEAD_DIM), F32),
        compiler_params=_cparams(("parallel", "parallel", "arbitrary")),
        name="nsa_win_attn",
    )(q64, k_win, v_win, tab_win)


def _ssd_kernel(xbc_ref, z_ref, dt_ref, cw_ref, cb_ref, dtb_ref, alog_ref, dskip_ref, nw_ref, o_ref,
                ext_ref, state_ref):
    ln = SSM_CHUNK
    d_ssm = z_ref.shape[2]
    nh = dt_ref.shape[2]
    gw = d_ssm // SSM_GROUPS
    hpg = nh // SSM_GROUPS
    gn = SSM_GROUPS * SSM_STATE

    @pl.when(pl.program_id(1) == 0)
    def _():
        ext_ref[0:SUBLANES, :] = jnp.zeros((SUBLANES, ext_ref.shape[1]), F32)
        state_ref[...] = jnp.zeros(state_ref.shape, F32)

    ext_ref[SUBLANES:SUBLANES + ln, :] = xbc_ref[0]
    conv = cw_ref[0:1, :] * ext_ref[SUBLANES - CONV_WIDTH + 1:SUBLANES - CONV_WIDTH + 1 + ln, :]
    for k in range(1, CONV_WIDTH):
        lo = SUBLANES - CONV_WIDTH + 1 + k
        conv = conv + cw_ref[k:k + 1, :] * ext_ref[lo:lo + ln, :]
    conv = conv + cb_ref[...]
    ext_ref[0:SUBLANES, :] = xbc_ref[0, ln - SUBLANES:ln, :]
    xc = _silu(conv)
    xs = xc[:, :d_ssm]
    bm = xc[:, d_ssm:d_ssm + gn]
    cm = xc[:, d_ssm + gn:d_ssm + 2 * gn]

    xdt = dt_ref[0] + dtb_ref[...]
    dt = jnp.maximum(xdt, 0.0) + jnp.log1p(jnp.exp(-jnp.abs(xdt)))
    a = -jnp.exp(alog_ref[...])
    da = dt * a

    row = lax.broadcasted_iota(jnp.int32, (ln, ln), 0)
    colm = lax.broadcasted_iota(jnp.int32, (ln, ln), 1)
    causal = row >= colm
    acs = jnp.dot(causal.astype(F32), da, precision=HI, preferred_element_type=F32)
    eye = (lax.broadcasted_iota(jnp.int32, (2 * nh, 2 * nh), 0)
           == lax.broadcasted_iota(jnp.int32, (2 * nh, 2 * nh), 1)).astype(F32)
    rows_t = _nt(eye, jnp.concatenate([acs, dt], axis=1), precision=HI)
    expand = (lax.broadcasted_iota(jnp.int32, (nh, d_ssm), 0)
              == lax.broadcasted_iota(jnp.int32, (nh, d_ssm), 1) // SSM_HEAD_DIM).astype(F32)
    last = acs[ln - 1:ln, :]
    exp_acs_x = jnp.dot(jnp.exp(acs), expand, precision=HI, preferred_element_type=F32)
    w_x = jnp.dot(jnp.exp(last - acs) * dt, expand, precision=HI, preferred_element_type=F32)
    exp_last_x = exp_acs_x[ln - 1:ln, :]

    ys = []
    for g in range(SSM_GROUPS):
        cg = cm[:, g * SSM_STATE:(g + 1) * SSM_STATE].astype(BF16)
        bg32 = bm[:, g * SSM_STATE:(g + 1) * SSM_STATE]
        bg = bg32.astype(BF16)
        xg = xs[:, g * gw:(g + 1) * gw]
        cb = _nt(cg, bg)
        st = state_ref[g]
        y_state = jnp.dot(cg, st.astype(BF16), preferred_element_type=F32) * exp_acs_x[:, g * gw:(g + 1) * gw]
        y_heads = []
        for j in range(hpg):
            h = g * hpg + j
            seg = acs[:, h:h + 1] - rows_t[h:h + 1, :]
            decay = jnp.exp(jnp.where(causal, seg, -jnp.inf))
            mmat = cb * decay * rows_t[nh + h:nh + h + 1, :]
            xh = xg[:, j * SSM_HEAD_DIM:(j + 1) * SSM_HEAD_DIM].astype(BF16)
            y_heads.append(jnp.dot(mmat.astype(BF16), xh, preferred_element_type=F32))
        ys.append(jnp.concatenate(y_heads, axis=1) + y_state)
        xw = (xg * w_x[:, g * gw:(g + 1) * gw]).astype(BF16)
        state_ref[g] = st * exp_last_x[:, g * gw:(g + 1) * gw] + jnp.dot(
            bg32.T.astype(BF16), xw, preferred_element_type=F32)

    y = jnp.concatenate(ys, axis=1) + dskip_ref[...] * xs
    y = y * _silu(z_ref[0])
    outs = []
    for g in range(SSM_GROUPS):
        yg = y[:, g * gw:(g + 1) * gw]
        outs.append(yg * lax.rsqrt(jnp.mean(yg * yg, axis=-1, keepdims=True) + RMS_EPS))
    o_ref[0] = jnp.concatenate(outs, axis=1) * nw_ref[...]


def _ssd(xbc, z, dt_raw, conv_w, conv_b, dt_bias, a_log, d_skip, norm_w):
    bsz, s, ch = xbc.shape
    d_ssm = z.shape[2]
    nh = dt_raw.shape[2]
    gw = d_ssm // SSM_GROUPS
    nchunks = s // SSM_CHUNK
    full = lambda shape: pl.BlockSpec(shape, lambda b, c: (0,) * len(shape))
    return pl.pallas_call(
        _ssd_kernel,
        grid=(bsz, nchunks),
        in_specs=[pl.BlockSpec((1, SSM_CHUNK, ch), lambda b, c: (b, c, 0)),
                  pl.BlockSpec((1, SSM_CHUNK, d_ssm), lambda b, c: (b, c, 0)),
                  pl.BlockSpec((1, SSM_CHUNK, nh), lambda b, c: (b, c, 0)),
                  full((CONV_WIDTH, ch)), full((1, ch)), full((1, nh)), full((1, nh)),
                  full((1, d_ssm)), full((1, d_ssm))],
        out_specs=pl.BlockSpec((1, SSM_CHUNK, d_ssm), lambda b, c: (b, c, 0)),
        out_shape=jax.ShapeDtypeStruct((bsz, s, d_ssm), F32),
        scratch_shapes=[pltpu.VMEM((SUBLANES + SSM_CHUNK, ch), F32),
                        pltpu.VMEM((SSM_GROUPS, SSM_STATE, gw), F32)],
        compiler_params=_cparams(("parallel", "arbitrary")),
        name="ssd_scan",
    )(xbc, z, dt_raw, conv_w, conv_b.reshape(1, ch), dt_bias.reshape(1, nh), a_log.reshape(1, nh),
      jnp.repeat(d_skip, SSM_HEAD_DIM).reshape(1, d_ssm), norm_w.reshape(1, d_ssm))


def _mix_out_kernel(oc_ref, os_ref, ow_ref, gl_ref, ex_ref, an_ref, ssm_ref, w_ref, x_ref, g1_ref, o_ref):
    d_attn = oc_ref.shape[2]
    sig = jax.nn.sigmoid(gl_ref[0])
    gc = jnp.dot(sig, ex_ref[0], precision=HI, preferred_element_type=F32)
    gs = jnp.dot(sig, ex_ref[1], precision=HI, preferred_element_type=F32)
    gw = jnp.dot(sig, ex_ref[2], precision=HI, preferred_element_type=F32)
    attn = gc * oc_ref[0] + gs * os_ref[0] + gw * ow_ref[0]
    attn = attn * lax.rsqrt(jnp.mean(attn * attn, axis=-1, keepdims=True) + RMS_EPS) * an_ref[...]
    mix = (jnp.dot(attn.astype(BF16), w_ref[:d_attn, :], preferred_element_type=F32)
           + jnp.dot(ssm_ref[0].astype(BF16), w_ref[d_attn:, :], preferred_element_type=F32))
    o_ref[0] = x_ref[0] + g1_ref[0] * mix


def _gate_expand():
    ex = np.zeros((3, LANES, N_HEADS * HEAD_DIM), np.float32)
    for r in range(3):
        for h in range(N_HEADS):
            ex[r, h * 3 + r, h * HEAD_DIM:(h + 1) * HEAD_DIM] = 1.0
    return ex


def _mix_out(oc, os_, ow, gl, attn_norm, ssm, w_out_bf16, x, g1, tm=256):
    bsz, s, d = x.shape
    d_attn = oc.shape[2]
    d_ssm = ssm.shape[2]
    tok = lambda w: pl.BlockSpec((1, tm, w), lambda b, i: (b, i, 0))
    return pl.pallas_call(
        _mix_out_kernel,
        grid=(bsz, s // tm),
        in_specs=[tok(d_attn), tok(d_attn), tok(d_attn), tok(LANES),
                  pl.BlockSpec((3, LANES, d_attn), lambda b, i: (0, 0, 0)),
                  pl.BlockSpec((1, d_attn), lambda b, i: (0, 0)),
                  tok(d_ssm),
                  pl.BlockSpec((d_attn + d_ssm, d), lambda b, i: (0, 0)),
                  tok(d),
                  pl.BlockSpec((1, 1, d), lambda b, i: (b, 0, 0))],
        out_specs=tok(d),
        out_shape=jax.ShapeDtypeStruct((bsz, s, d), F32),
        compiler_params=_cparams(("parallel", "parallel")),
        name="mix_out",
    )(oc, os_, ow, gl, jnp.asarray(_gate_expand()), attn_norm.reshape(1, d_attn), ssm, w_out_bf16, x, g1)


def _peer_candidates():
    k, sub = PEER_TOPK, SUBLANES
    cells = [(0, b) for b in range(k)]
    cells += [(a, b) for a in range(1, sub) for b in range(sub)]
    cells += [(a, 0) for a in range(sub, k)]
    order = np.array([a * k + b if (a + 1) * (b + 1) <= k else -1 for a, b in cells], np.float32)
    group = np.zeros((k, LANES), np.float32)
    for row, (a, _) in enumerate(cells):
        group[a, row] = 1.0
    return cells, order, group


def _peer_route_kernel(q_ref, sk_ref, order_ref, group_ref, rank2_ref, e2_ref, n1_ref, c1_ref):
    q = q_ref[...]
    half = q.shape[1] // 2
    s1 = _nt(sk_ref[0], q[:, :half], precision=HI)
    s2 = _nt(sk_ref[1], q[:, half:], precision=HI)
    k = PEER_TOPK
    rank1, v1 = _topk_mark(s1, k, axis=0)
    rank2, v2 = _topk_mark(s2, k, axis=0)
    v1_all = jnp.concatenate(v1, axis=0)
    v2_all = jnp.concatenate(v2, axis=0)
    cand = jnp.concatenate([v1[0] + v2_all] + [v1[a] + v2_all[:SUBLANES] for a in range(1, SUBLANES)]
                           + [v1_all[SUBLANES:] + v2[0]], axis=0)
    order = order_ref[...]
    cand = jnp.where(order >= 0.0, cand, -jnp.inf)
    rank_c, best = _topk_mark(cand, k, axis=0, order=order, order_bound=float(k * k))
    chosen = jnp.where(rank_c < float(k), 1.0, 0.0).astype(BF16)
    chosen = jnp.concatenate([chosen, jnp.zeros((LANES - chosen.shape[0], chosen.shape[1]), BF16)], axis=0)
    count = jnp.dot(group_ref[...], chosen, preferred_element_type=F32)
    z = best[0] * 0.0
    for r in range(k):
        z = z + jnp.exp(best[r] - best[0])
    n1 = jnp.zeros(s1.shape, F32)
    for a in range(k):
        n1 = jnp.where(rank1 == float(a), count[a:a + 1, :], n1)
    rank2_ref[0] = rank2.astype(BF16)
    e2_ref[0] = jnp.exp(s2 - v2[0]).astype(BF16)
    n1_ref[0] = n1
    c1_ref[0] = jnp.exp(s1 - v1[0]) / z


def _peer_route(q, subkeys, tm=256):
    t, width = q.shape
    kd = width // PEER_HEADS
    nk = subkeys.shape[1]
    spec = pl.BlockSpec((1, nk, tm), lambda i, h: (h, 0, i))
    shape = jax.ShapeDtypeStruct((PEER_HEADS, nk, t), F32)
    cells, order, group = _peer_candidates()
    order = jnp.asarray(np.broadcast_to(order[:, None], (len(cells), tm)))
    return pl.pallas_call(
        _peer_route_kernel,
        grid=(t // tm, PEER_HEADS),
        in_specs=[pl.BlockSpec((tm, kd), lambda i, h: (i, h)),
                  pl.BlockSpec((2, nk, kd // 2), lambda i, h: (0, 0, 0)),
                  pl.BlockSpec((len(cells), tm), lambda i, h: (0, 0)),
                  pl.BlockSpec((PEER_TOPK, LANES), lambda i, h: (0, 0))],
        out_specs=[spec, spec, spec, spec],
        out_shape=[jax.ShapeDtypeStruct(shape.shape, BF16), jax.ShapeDtypeStruct(shape.shape, BF16), shape, shape],
        compiler_params=_cparams(("parallel", "parallel")),
        name="peer_route",
    )(q, subkeys, order, jnp.asarray(group, BF16))


def _peer_dense_kernel(ht_ref, u_ref, vt_ref, rank2_ref, e2_ref, n1_ref, c1_ref, o_ref, *scratch):
    j = pl.program_id(1)
    te = u_ref.shape[0]
    nk = rank2_ref.shape[1]

    @pl.when(j == 0)
    def _():
        o_ref[...] = jnp.zeros(o_ref.shape, F32)

    tm = ht_ref.shape[1]
    strips = [slice(c * PEER_LANE_STRIP, (c + 1) * PEER_LANE_STRIP) for c in range(tm // PEER_LANE_STRIP)]

    units = [(e, c) for e in range(te // PEER_EXPERT_UNIT) for c in range(len(strips))]
    act_refs = scratch[:len(units)]
    aw_refs = scratch[len(units):]
    n_piece = PEER_EXPERT_UNIT // PEER_EXPERT_PIECE
    d_piece = o_ref.shape[0] // n_piece

    def produce(k, p):
        e, c = units[k]
        lo = e * PEER_EXPERT_UNIT + p * PEER_EXPERT_PIECE
        act_refs[k][p * PEER_EXPERT_PIECE:(p + 1) * PEER_EXPERT_PIECE, :] = jax.nn.gelu(
            jnp.dot(u_ref[lo:lo + PEER_EXPERT_PIECE, :], ht_ref[:, strips[c]], preferred_element_type=F32)
        ).astype(BF16)

    def gate(k, r):
        e, c = units[k]
        ls = strips[c]
        i1 = (j * te + e * PEER_EXPERT_UNIT) // nk + r
        w = jnp.zeros((nk, PEER_LANE_STRIP), BF16)
        zero = jnp.zeros((nk, PEER_LANE_STRIP), BF16)
        for h in range(PEER_HEADS):
            n_row = n1_ref[h, pl.ds(i1, 1), ls].astype(BF16)
            c_row = c1_ref[h, pl.ds(i1, 1), ls].astype(BF16)
            w = w + jnp.where(rank2_ref[h, :, ls] < n_row, e2_ref[h, :, ls], zero) * c_row
        aw_refs[k][r * nk:(r + 1) * nk, :] = act_refs[k][r * nk:(r + 1) * nk, :] * w

    def combine(k, m):
        e, c = units[k]
        ds_ = slice(m * d_piece, (m + 1) * d_piece)
        es = slice(e * PEER_EXPERT_UNIT, (e + 1) * PEER_EXPERT_UNIT)
        o_ref[ds_, strips[c]] += jnp.dot(vt_ref[ds_, es], aw_refs[k][...], preferred_element_type=F32)

    gates_per_piece = PEER_EXPERT_PIECE // nk
    for p in range(n_piece):
        produce(0, p)
    for k in range(len(units)):
        for p in range(n_piece):
            if k + 1 < len(units):
                produce(k + 1, p)
            for r in range(p * gates_per_piece, (p + 1) * gates_per_piece):
                gate(k, r)
                if k > 0 and r == p * gates_per_piece:
                    combine(k - 1, p)
    for m in range(n_piece):
        combine(len(units) - 1, m)


def _peer_dense(h_t, u_bf16, v_t_bf16, rank2, e2, n1, c1, tm=512, te=512):
    d, t = h_t.shape
    n_exp = u_bf16.shape[0]
    nk = rank2.shape[1]
    n_units = (te // PEER_EXPERT_UNIT) * (tm // PEER_LANE_STRIP)
    route = pl.BlockSpec((PEER_HEADS, nk, tm), lambda i, j: (0, 0, i))
    return pl.pallas_call(
        _peer_dense_kernel,
        grid=(t // tm, n_exp // te),
        in_specs=[pl.BlockSpec((d, tm), lambda i, j: (0, i)),
                  pl.BlockSpec((te, d), lambda i, j: (j, 0)),
                  pl.BlockSpec((d, te), lambda i, j: (0, j)),
                  route, route, route, route],
        out_specs=pl.BlockSpec((d, tm), lambda i, j: (0, i)),
        out_shape=jax.ShapeDtypeStruct((d, t), F32),
        scratch_shapes=([pltpu.VMEM((PEER_EXPERT_UNIT, PEER_LANE_STRIP), BF16)] * n_units
                        + [pltpu.VMEM((PEER_EXPERT_UNIT, PEER_LANE_STRIP), BF16)] * n_units),
        compiler_params=_cparams(("parallel", "arbitrary")),
        name="peer_dense",
    )(h_t, u_bf16, v_t_bf16, rank2, e2, n1, c1)


def _residual_kernel(x_ref, y_ref, g_ref, o_ref):
    o_ref[0] = x_ref[0] + g_ref[0] * y_ref[0]


def _residual(x, y, g, tm=512):
    bsz, s, d = x.shape
    tok = pl.BlockSpec((1, tm, d), lambda b, i: (b, i, 0))
    return pl.pallas_call(
        _residual_kernel,
        grid=(bsz, s // tm),
        in_specs=[tok, tok, pl.BlockSpec((1, 1, d), lambda b, i: (b, 0, 0))],
        out_specs=tok,
        out_shape=jax.ShapeDtypeStruct((bsz, s, d), F32),
        compiler_params=_cparams(("parallel", "parallel")),
        name="residual",
    )(x, y, g)


def _final_norm_kernel(x_ref, w_ref, o_ref):
    x = x_ref[0]
    o_ref[0] = x * lax.rsqrt(jnp.mean(x * x, axis=-1, keepdims=True) + RMS_EPS) * w_ref[...]


def _final_norm(x, w, tm=512):
    bsz, s, d = x.shape
    tok = pl.BlockSpec((1, tm, d), lambda b, i: (b, i, 0))
    return pl.pallas_call(
        _final_norm_kernel,
        grid=(bsz, s // tm),
        in_specs=[tok, pl.BlockSpec((1, d), lambda b, i: (0, 0))],
        out_specs=tok,
        out_shape=jax.ShapeDtypeStruct((bsz, s, d), F32),
        compiler_params=_cparams(("parallel", "parallel")),
        name="final_norm",
    )(x, w.reshape(1, d))


def _pad_cols(w, width):
    return jnp.pad(w, ((0, 0), (0, width - w.shape[1])))


def _nsa(q, kv, tables, cmp_pe, cmp_w1, cmp_w2):
    tab_win, tab_cmp, tab_slc, far_col, far_t = tables
    bsz, s, _ = q.shape
    nc = s // CMP_STRIDE
    nq = s // Q_BLOCK
    n_slc = s // SLC_BLOCK
    nbp = -(-n_slc // SLC_GROUP_BLOCKS) * SLC_GROUP_BLOCKS
    scale = HEAD_DIM ** -0.5
    q6 = (q * scale).astype(BF16).reshape(bsz, nq, Q_BLOCK, N_KV, N_GRP, HEAD_DIM)
    qh = q6.transpose(0, 3, 4, 1, 2, 5).reshape(bsz, N_KV, N_GRP, s, HEAD_DIM)
    qt = q6.transpose(0, 3, 1, 5, 4, 2).reshape(bsz, N_KV, nq, HEAD_DIM, N_GRP * Q_BLOCK)
    qt = jnp.pad(qt, ((0, 0),) * 3 + ((0, LANES - HEAD_DIM), (0, 0)))
    kv6 = kv.reshape(bsz, s, 6, N_KV, HEAD_DIM)

    kv_cmp = kv6[:, :, 0:2].reshape(bsz, nc, CMP_STRIDE, 2, N_KV, HEAD_DIM)
    kv_cmp = kv_cmp.transpose(0, 3, 4, 1, 2, 5).reshape(bsz, 2, N_KV, nc, CMP_STRIDE * HEAD_DIM)
    kv_c = _compress(kv_cmp, cmp_pe, cmp_w1, cmp_w2)
    o_c, sel_t = _cmp_attn(qh, kv_c, tab_cmp, far_col, nbp)

    k_slc = kv6[:, :, 2].transpose(0, 2, 1, 3).astype(BF16)
    nkt = s // SLC_KEY_TILE
    vt = kv6[:, :, 3].astype(BF16).reshape(bsz, nkt, SLC_KEY_TILE, N_KV, HEAD_DIM).transpose(0, 3, 1, 4, 2)
    vt_aug = jnp.concatenate([
        vt, jnp.ones((bsz, N_KV, nkt, 1, SLC_KEY_TILE), BF16),
        jnp.zeros((bsz, N_KV, nkt, SLC_V_ROWS - HEAD_DIM - 1, SLC_KEY_TILE), BF16)], axis=3)
    blk = np.arange(s) // SLC_BLOCK
    onehot = (blk[:, None] % SLC_GROUP_BLOCKS == np.arange(SLC_GROUP_BLOCKS)[None, :]).astype(np.float32)
    k_aug = jnp.concatenate([
        k_slc, jnp.ones((bsz, N_KV, s, SLC_BIAS_PIECES), BF16),
        jnp.zeros((bsz, N_KV, s, LANES - HEAD_DIM - SLC_BIAS_PIECES), BF16),
        jnp.broadcast_to(jnp.asarray(onehot, BF16), (bsz, N_KV, s, SLC_GROUP_BLOCKS))], axis=-1)
    o_st = _slc_attn(qt, far_t, k_aug, vt_aug, sel_t, tab_slc)
    o_s = o_st.reshape(bsz, N_KV, nq, HEAD_DIM, N_GRP, Q_BLOCK).transpose(0, 2, 5, 1, 4, 3)
    o_s = o_s.reshape(bsz, s, N_HEADS * HEAD_DIM)

    front = ((0, 0), (0, 0), (WINDOW, 0), (0, 0))
    k_win = jnp.pad(kv6[:, :, 4].transpose(0, 2, 1, 3).astype(BF16), front)
    v_win = jnp.pad(kv6[:, :, 5].transpose(0, 2, 1, 3).astype(BF16), front)
    o_w = _win_attn(qh, k_win, v_win, tab_win)

    back = lambda o: o.transpose(0, 3, 1, 2, 4).reshape(bsz, s, N_HEADS * HEAD_DIM)
    return back(o_c), o_s, back(o_w)


def kernel(x, c, ada_w, ada_b, norm_mix, norm_ffn, w_in, cmp_pe, cmp_w1, cmp_w2, rel_bias, attn_out_norm,
           conv_w, conv_b, dt_bias, a_log, d_skip, ssm_norm, w_out, peer_wq, peer_subkeys, peer_u, peer_v,
           norm_final):
    bsz, s, d = x.shape
    depth = ada_w.shape[0]
    d_attn = N_HEADS * HEAD_DIM
    n_kv = 6 * N_KV * HEAD_DIM
    n_gate = 3 * N_HEADS
    d_ssm = ssm_norm.shape[1]
    ch = conv_w.shape[2]
    nh = dt_bias.shape[1]

    mod = _ada_mod(c, ada_w, ada_b)
    tables = _bias_tables(rel_bias)

    for l in range(depth):
        sh1, sc1, g1, sh2, sc2, g2 = [mod[l, :, i * d:(i + 1) * d].reshape(bsz, 1, d) for i in range(6)]
        cuts = np.cumsum([0, d_attn, n_kv, n_gate, d_ssm, ch, nh])
        seg = [w_in[l][:, cuts[i]:cuts[i + 1]] for i in range(6)]
        seg[2] = _pad_cols(seg[2], LANES)
        seg[5] = _pad_cols(seg[5], LANES)
        w_cat = jnp.concatenate(seg, axis=1).astype(BF16)
        pc = np.cumsum([0] + [w.shape[1] for w in seg])
        proj, _ = _norm_proj(x, norm_mix[l], sc1, sh1, w_cat)
        q = proj[:, :, pc[0]:pc[1]]
        kv = proj[:, :, pc[1]:pc[2]]
        gl = proj[:, :, pc[2]:pc[3]]
        z = proj[:, :, pc[3]:pc[4]]
        xbc = proj[:, :, pc[4]:pc[5]]
        dt_raw = proj[:, :, pc[5]:pc[5] + nh]

        o_c, o_s, o_w = _nsa(q, kv, tables, cmp_pe[l], cmp_w1[l], cmp_w2[l])
        ssm = _ssd(xbc, z, dt_raw, conv_w[l], conv_b[l], dt_bias[l], a_log[l], d_skip[l], ssm_norm[l])
        x = _mix_out(o_c, o_s, o_w, gl, attn_out_norm[l], ssm, w_out[l].astype(BF16), x, g1)

        pq, h2 = _norm_proj(x, norm_ffn[l], sc2, sh2, peer_wq[l].astype(BF16))
        rank2, e2, n1, c1 = _peer_route(pq.reshape(bsz * s, -1), peer_subkeys[l])
        h_t = h2.reshape(bsz * s, d).T
        ffn_t = _peer_dense(h_t, peer_u[l].astype(BF16), peer_v[l].T.astype(BF16), rank2, e2, n1, c1)
        x = _residual(x, ffn_t.T.reshape(bsz, s, d), g2)

    return _final_norm(x, norm_final)
```

```python
import functools
import math

import numpy as np
import jax
import jax.numpy as jnp
from jax import lax
from jax.experimental import pallas as pl
from jax.experimental.pallas import tpu as pltpu

F32 = jnp.float32
BF16 = jnp.bfloat16
HI = lax.Precision.HIGHEST

N_HEADS = 16
N_KV = 2
N_GRP = N_HEADS // N_KV
HEAD_DIM = 64
CMP_BLOCK = 32
CMP_STRIDE = 16
CMP_HIDDEN = 4 * HEAD_DIM
SLC_BLOCK = 64
SLC_TOPN = 16
WINDOW = 512
Q_BLOCK = 128
FORCE_SCORE = 1e4
NEG_INF = -1e30
REL_BUCKETS = 32
REL_MAX_DIST = 2048
SSM_HEAD_DIM = 64
SSM_GROUPS = 2
SSM_STATE = 128
CONV_WIDTH = 4
SSM_CHUNK = 256
PEER_HEADS = 8
PEER_NKEYS = 128
PEER_TOPK = 16
RMS_EPS = 1e-6

LANES = 128
SUBLANES = 8
VMEM_LIMIT = 56 * 1024 * 1024

SLC_KEY_TILE = 512
SLC_TILE_BLOCKS = SLC_KEY_TILE // SLC_BLOCK
SLC_GROUP_BLOCKS = LANES
SLC_NEAR_CHUNKS = 14
SLC_TAB_MASKED = 14
SLC_TAB_CONST = 15
MASK_BIG = 2.0 ** 100
SLC_LANE_STRIP = 256
PEER_LANE_STRIP = 256
PEER_EXPERT_PIECE = 256
PEER_EXPERT_UNIT = 512
SLC_V_ROWS = HEAD_DIM + 16
SLC_BIAS_PIECES = 3

CMP_FAR_STEP = 256
CMP_NEAR = 128
CMP_PAD = CMP_NEAR - Q_BLOCK // CMP_STRIDE


def _cparams(sem, vmem=VMEM_LIMIT):
    return pltpu.CompilerParams(dimension_semantics=sem, vmem_limit_bytes=vmem)


def _nt(a, b, precision=None):
    return lax.dot_general(a, b, (((1,), (1,)), ((), ())), precision=precision,
                           preferred_element_type=F32)


def _silu(x):
    return x * jax.nn.sigmoid(x)


def _rel_bucket_np(d):
    d = np.maximum(np.asarray(d, np.int64), 0)
    max_exact = REL_BUCKETS // 2
    ratio = np.log(np.maximum(d, max_exact).astype(np.float64) / max_exact) / math.log(REL_MAX_DIST / max_exact)
    scaled = ratio * (REL_BUCKETS - max_exact)
    large = max_exact + np.floor(scaled).astype(np.int64)
    return np.where(d < max_exact, d, np.minimum(large, REL_BUCKETS - 1)).astype(np.int32)


def _ada_kernel(c_ref, w_ref, b_ref, o_ref):
    cond = _silu(c_ref[...])
    o_ref[0] = jnp.dot(cond, w_ref[0], precision=HI, preferred_element_type=F32) + b_ref[0]


def _ada_mod(c, ada_w, ada_b):
    depth, d, n = ada_w.shape
    bsz = c.shape[0]
    rows = SUBLANES
    c_pad = jnp.zeros((rows, d), F32).at[:bsz].set(c)
    tn = 1024
    out = pl.pallas_call(
        _ada_kernel,
        grid=(depth, n // tn),
        in_specs=[pl.BlockSpec((rows, d), lambda l, j: (0, 0)),
                  pl.BlockSpec((1, d, tn), lambda l, j: (l, 0, j)),
                  pl.BlockSpec((1, 1, tn), lambda l, j: (l, 0, j))],
        out_specs=pl.BlockSpec((1, rows, tn), lambda l, j: (l, 0, j)),
        out_shape=jax.ShapeDtypeStruct((depth, rows, n), F32),
        compiler_params=_cparams(("parallel", "parallel")),
        name="ada_mod",
    )(c_pad, ada_w, ada_b.reshape(depth, 1, n))
    return out[:, :bsz]


def _norm_proj_kernel(x_ref, nw_ref, sc_ref, sh_ref, w_ref, o_ref, h_ref, hs_ref):
    @pl.when(pl.program_id(2) == 0)
    def _():
        x = x_ref[0]
        y = x * lax.rsqrt(jnp.mean(x * x, axis=-1, keepdims=True) + RMS_EPS)
        h = (y * nw_ref[...]) * (1.0 + sc_ref[0]) + sh_ref[0]
        hs_ref[...] = h.astype(BF16)
        h_ref[0] = h.astype(BF16)

    o_ref[0] = jnp.dot(hs_ref[...], w_ref[...], preferred_element_type=F32)


def _norm_proj(x, nw, sc, sh, w_bf16, tm=512, tn=512):
    bsz, s, d = x.shape
    n = w_bf16.shape[1]
    return pl.pallas_call(
        _norm_proj_kernel,
        grid=(bsz, s // tm, n // tn),
        in_specs=[pl.BlockSpec((1, tm, d), lambda b, i, j: (b, i, 0)),
                  pl.BlockSpec((1, d), lambda b, i, j: (0, 0)),
                  pl.BlockSpec((1, 1, d), lambda b, i, j: (b, 0, 0)),
                  pl.BlockSpec((1, 1, d), lambda b, i, j: (b, 0, 0)),
                  pl.BlockSpec((d, tn), lambda b, i, j: (0, j))],
        out_specs=[pl.BlockSpec((1, tm, tn), lambda b, i, j: (b, i, j)),
                   pl.BlockSpec((1, tm, d), lambda b, i, j: (b, i, 0))],
        out_shape=[jax.ShapeDtypeStruct((bsz, s, n), F32),
                   jax.ShapeDtypeStruct((bsz, s, d), BF16)],
        scratch_shapes=[pltpu.VMEM((tm, d), BF16)],
        compiler_params=_cparams(("parallel", "parallel", "arbitrary")),
        name="norm_proj",
    )(x, nw.reshape(1, d), sc, sh, w_bf16)


def _bias_kernel(rel_ref, bk_ref, o_ref):
    h = pl.program_id(0)
    bk = bk_ref[...]
    acc = jnp.full(bk.shape, NEG_INF, F32)
    for b in range(REL_BUCKETS):
        acc = jnp.where(bk == b, rel_ref[b, h], acc)
    o_ref[0] = acc


def _bias_tables(rel_bias):
    r = np.arange(Q_BLOCK)[:, None]
    dw = r - np.arange(WINDOW + Q_BLOCK)[None, :] + WINDOW
    win = np.where((dw >= 0) & (dw < WINDOW), _rel_bucket_np(dw), -1)
    off = CMP_STRIDE * CMP_PAD - (CMP_BLOCK - 1)
    dc = r + off - CMP_STRIDE * np.arange(CMP_NEAR)[None, :]
    cmp_near = np.where(dc >= 0, _rel_bucket_np(dc), -1)
    chunks = []
    for m in range(SLC_NEAR_CHUNKS):
        ds_ = Q_BLOCK * m + r - np.arange(Q_BLOCK)[None, :]
        chunks.append(np.where(ds_ >= 0, _rel_bucket_np(ds_), -1))
    assert _rel_bucket_np(Q_BLOCK * SLC_NEAR_CHUNKS - (Q_BLOCK - 1)) == REL_BUCKETS - 1
    assert _rel_bucket_np(off + CMP_STRIDE) == REL_BUCKETS - 1
    chunks.append(np.full((Q_BLOCK, Q_BLOCK), -1))
    chunks.append(np.full((Q_BLOCK, Q_BLOCK), REL_BUCKETS - 1))
    bk = np.concatenate([win, cmp_near] + chunks, axis=1).astype(np.int32)
    cols = bk.shape[1]
    out = pl.pallas_call(
        _bias_kernel,
        grid=(N_HEADS,),
        in_specs=[pl.BlockSpec(memory_space=pltpu.SMEM),
                  pl.BlockSpec((Q_BLOCK, cols), lambda h: (0, 0))],
        out_specs=pl.BlockSpec((1, Q_BLOCK, cols), lambda h: (h, 0, 0)),
        out_shape=jax.ShapeDtypeStruct((N_HEADS, Q_BLOCK, cols), F32),
        compiler_params=_cparams(("arbitrary",)),
        name="bias_tables",
    )(rel_bias, jnp.asarray(bk))
    nw = WINDOW + Q_BLOCK
    tab_win = out[:, :, :nw].reshape(N_KV, N_GRP, Q_BLOCK, nw)
    tab_cmp = out[:, :, nw:nw + CMP_NEAR].reshape(N_KV, N_GRP, Q_BLOCK, CMP_NEAR)
    nch = SLC_NEAR_CHUNKS + 2
    tab_slc = out[:, :, nw + CMP_NEAR:].reshape(N_KV, N_GRP, Q_BLOCK, nch, Q_BLOCK)
    tab_slc = tab_slc.transpose(0, 3, 4, 1, 2).reshape(N_KV, nch, Q_BLOCK, N_GRP * Q_BLOCK)
    far = rel_bias[REL_BUCKETS - 1].reshape(N_KV, N_GRP, 1)
    far_col = jnp.broadcast_to(far, (N_KV, N_GRP, Q_BLOCK)).reshape(N_KV, N_GRP * Q_BLOCK, 1)
    far_row = far_col.reshape(N_KV, 1, N_GRP * Q_BLOCK)
    pieces, rest = [], far_row
    for _ in range(SLC_BIAS_PIECES):
        piece = rest.astype(BF16)
        pieces.append(piece)
        rest = rest - piece.astype(F32)
    zeros = lambda w: jnp.zeros((N_KV, w, N_GRP * Q_BLOCK), BF16)
    far_t = jnp.concatenate([zeros(HEAD_DIM)] + pieces + [zeros(LANES - HEAD_DIM - SLC_BIAS_PIECES)], axis=1)
    return tab_win, tab_cmp, tab_slc, far_col, far_t


def _compress_kernel(a_ref, pe_ref, w1_ref, w2_ref, o_ref):
    a = a_ref[0, 0, 0]
    half = a.shape[1]
    lo = jnp.dot((a + pe_ref[0, 0:1, :]).astype(BF16), w1_ref[0, :half, :].astype(BF16),
                 preferred_element_type=F32)
    hi = jnp.dot((a + pe_ref[0, 1:2, :]).astype(BF16), w1_ref[0, half:, :].astype(BF16),
                 preferred_element_type=F32)
    nc = a.shape[0]
    hid = jax.nn.gelu(lo + pltpu.roll(hi, nc - 1, axis=0))
    out = jnp.dot(hid.astype(BF16), w2_ref[0].astype(BF16), preferred_element_type=F32)
    o_ref[0, 0, 0] = jnp.zeros(o_ref.shape[3:], F32)
    o_ref[0, 0, 0, CMP_PAD:CMP_PAD + nc, :] = out


def _compress(kv_cmp, cmp_pe, cmp_w1, cmp_w2):
    bsz, _, _, nc, half = kv_cmp.shape
    pe = cmp_pe.reshape(2, 2, half)
    rows = CMP_PAD + nc + SUBLANES
    return pl.pallas_call(
        _compress_kernel,
        grid=(bsz, 2, N_KV),
        in_specs=[pl.BlockSpec((1, 1, 1, nc, half), lambda b, w, k: (b, w, k, 0, 0)),
                  pl.BlockSpec((1, 2, half), lambda b, w, k: (w, 0, 0)),
                  pl.BlockSpec((1, 2 * half, CMP_HIDDEN), lambda b, w, k: (w, 0, 0)),
                  pl.BlockSpec((1, CMP_HIDDEN, HEAD_DIM), lambda b, w, k: (w, 0, 0))],
        out_specs=pl.BlockSpec((1, 1, 1, rows, HEAD_DIM), lambda b, w, k: (b, w, k, 0, 0)),
        out_shape=jax.ShapeDtypeStruct((bsz, 2, N_KV, rows, HEAD_DIM), F32),
        compiler_params=_cparams(("parallel", "parallel", "parallel")),
        name="nsa_compress",
    )(kv_cmp, pe, cmp_w1, cmp_w2)


def _topk_mark(vals, k, axis, order=None, order_bound=None):
    n = vals.shape[axis] if order is None else order_bound
    iota = lax.broadcasted_iota(jnp.int32, vals.shape, axis).astype(F32) if order is None else order
    rank = jnp.full(vals.shape, float(k), F32)
    work = vals
    picked = []
    for r in range(k):
        m = jnp.max(work, axis=axis, keepdims=True)
        ix = jnp.min(jnp.where(work == m, iota, float(n)), axis=axis, keepdims=True)
        hit = iota == ix
        rank = jnp.where(hit, float(r), rank)
        work = jnp.where(hit, -jnp.inf, work)
        picked.append(m)
    return rank, picked


def _cmp_attn_kernel(q_ref, k_ref, v_ref, tab_ref, far_ref, m_ref, o_ref, sel_ref, imp_ref):
    i = pl.program_id(2)
    rows = N_GRP * Q_BLOCK
    nc = k_ref.shape[3] - CMP_PAD - SUBLANES
    nbp = sel_ref.shape[3]
    start = pl.multiple_of(i * (Q_BLOCK // CMP_STRIDE), SUBLANES)
    n_far = i * (Q_BLOCK // CMP_STRIDE) - CMP_PAD

    def attend(wf):
        q = q_ref[0, 0].reshape(rows, HEAD_DIM)
        k_far = k_ref[0, 0, 0, CMP_PAD:CMP_PAD + wf, :].astype(BF16)
        v_far = v_ref[0, 0, 0, CMP_PAD:CMP_PAD + wf, :].astype(BF16)
        k_near = k_ref[0, 0, 0, pl.ds(start, CMP_NEAR), :].astype(BF16)
        v_near = v_ref[0, 0, 0, pl.ds(start, CMP_NEAR), :].astype(BF16)

        n_idx = lax.broadcasted_iota(jnp.int32, (1, wf), 1)
        mask_far = n_idx < n_far
        s_far = jnp.where(mask_far, _nt(q, k_far) + far_ref[0], NEG_INF)
        tab = tab_ref[0].reshape(rows, CMP_NEAR)
        c_idx = lax.broadcasted_iota(jnp.int32, (1, CMP_NEAR), 1)
        mask_near = (c_idx >= -n_far) & (tab > 0.5 * NEG_INF)
        s_near = jnp.where(mask_near, _nt(q, k_near) + tab, NEG_INF)

        m = jnp.maximum(jnp.max(s_far, axis=-1, keepdims=True), jnp.max(s_near, axis=-1, keepdims=True))
        e_far = jnp.exp(s_far - m)
        e_near = jnp.exp(s_near - m)
        l = jnp.sum(e_far, axis=-1, keepdims=True) + jnp.sum(e_near, axis=-1, keepdims=True)
        p_far = jnp.where(mask_far, e_far / l, 0.0)
        p_near = jnp.where(mask_near, e_near / l, 0.0)
        o = (jnp.dot(p_far.astype(BF16), v_far, preferred_element_type=F32)
             + jnp.dot(p_near.astype(BF16), v_near, preferred_element_type=F32))
        o_ref[0, 0] = o.reshape(N_GRP, Q_BLOCK, HEAD_DIM)

        ps_far = jnp.sum(p_far.reshape(N_GRP, Q_BLOCK, wf), axis=0)
        ps_near = jnp.sum(p_near.reshape(N_GRP, Q_BLOCK, CMP_NEAR), axis=0)
        imp_ref[...] = (
            jnp.dot(ps_far, m_ref[CMP_PAD:CMP_PAD + wf, :], precision=HI, preferred_element_type=F32)
            + jnp.dot(ps_near, m_ref[pl.ds(start, CMP_NEAR), :], precision=HI, preferred_element_type=F32))

    step = min(CMP_FAR_STEP, nc)
    widths = list(range(step, nc + 1, step))
    for b, wf in enumerate(widths):
        lower = n_far > widths[b - 1] if b > 0 else True
        upper = n_far <= wf if b + 1 < len(widths) else True
        pl.when(jnp.logical_and(lower, upper))(functools.partial(attend, wf))

    imp = imp_ref[...].T
    t = i * Q_BLOCK + lax.broadcasted_iota(jnp.int32, (1, Q_BLOCK), 1)
    cur = t // SLC_BLOCK
    blk = lax.broadcasted_iota(jnp.int32, (nbp, 1), 0)
    forced = (blk == 0) | (blk == cur) | (blk == cur - 1)
    imp = jnp.where(forced, FORCE_SCORE, jnp.where(blk <= cur, imp, -FORCE_SCORE))
    n_blocks = (nc * CMP_STRIDE) // SLC_BLOCK
    imp = jnp.where(blk < n_blocks, imp, -jnp.inf)
    rank, _ = _topk_mark(imp, min(SLC_TOPN, n_blocks), axis=0)
    sel_ref[0, 0, 0] = jnp.where(rank < float(SLC_TOPN), 1.0, 0.0).astype(BF16)


def _overlap_matrix(nc, nbp):
    n_cmp = nc - 1
    n_slc = nc * CMP_STRIDE // SLC_BLOCK
    j = np.arange(n_slc)
    lo = np.clip((j * SLC_BLOCK - CMP_BLOCK) // CMP_STRIDE + 1, 0, n_cmp)
    hi = np.clip(-((-(j * SLC_BLOCK + SLC_BLOCK)) // CMP_STRIDE), 0, n_cmp)
    m = np.zeros((CMP_PAD + nc + SUBLANES, nbp), np.float32)
    n = np.arange(nc)[:, None]
    m[CMP_PAD:CMP_PAD + nc, :n_slc] = (n >= lo[None, :]) & (n < hi[None, :])
    return m


def _cmp_attn(q64, kv_c, tab_cmp, far_col, nbp):
    bsz, _, _, s, _ = q64.shape
    rows_c = kv_c.shape[3]
    nc = rows_c - CMP_PAD - SUBLANES
    nq = s // Q_BLOCK
    m_pad = jnp.asarray(_overlap_matrix(nc, nbp))
    return pl.pallas_call(
        _cmp_attn_kernel,
        grid=(bsz, N_KV, nq),
        in_specs=[pl.BlockSpec((1, 1, N_GRP, Q_BLOCK, HEAD_DIM), lambda b, k, i: (b, k, 0, i, 0)),
                  pl.BlockSpec((1, 1, 1, rows_c, HEAD_DIM), lambda b, k, i: (b, 0, k, 0, 0)),
                  pl.BlockSpec((1, 1, 1, rows_c, HEAD_DIM), lambda b, k, i: (b, 1, k, 0, 0)),
                  pl.BlockSpec((1, N_GRP, Q_BLOCK, CMP_NEAR), lambda b, k, i: (k, 0, 0, 0)),
                  pl.BlockSpec((1, N_GRP * Q_BLOCK, 1), lambda b, k, i: (k, 0, 0)),
                  pl.BlockSpec((rows_c, nbp), lambda b, k, i: (0, 0))],
        out_specs=[pl.BlockSpec((1, 1, N_GRP, Q_BLOCK, HEAD_DIM), lambda b, k, i: (b, k, 0, i, 0)),
                   pl.BlockSpec((1, 1, 1, nbp, Q_BLOCK), lambda b, k, i: (b, k, i, 0, 0))],
        out_shape=[jax.ShapeDtypeStruct((bsz, N_KV, N_GRP, s, HEAD_DIM), F32),
                   jax.ShapeDtypeStruct((bsz, N_KV, nq, nbp, Q_BLOCK), BF16)],
        scratch_shapes=[pltpu.VMEM((Q_BLOCK, nbp), F32)],
        compiler_params=_cparams(("parallel", "parallel", "arbitrary")),
        name="nsa_cmp_attn",
    )(q64, kv_c, kv_c, tab_cmp, far_col, m_pad)


def _slc_attn_kernel(qt_ref, far_ref, ka_ref, vt_ref, selt_ref, tab_ref, o_ref,
                     qa_ref, s0_ref, s1_ref, x0_ref, x1_ref, m_ref, acc_ref):
    i = pl.program_id(2)
    cols = N_GRP * Q_BLOCK
    ngroups = qa_ref.shape[0] // 2
    qt = qt_ref[0, 0, 0]
    qt_far = qt + far_ref[0]
    selneg = ((selt_ref[0, 0, 0].astype(F32) - 1.0) * MASK_BIG).astype(BF16)
    for g in range(ngroups):
        part = selneg[g * SLC_GROUP_BLOCKS:(g + 1) * SLC_GROUP_BLOCKS, :]
        part = jnp.concatenate([part] * N_GRP, axis=1)
        qa_ref[2 * g] = jnp.concatenate([qt_far, part], axis=0)
        qa_ref[2 * g + 1] = jnp.concatenate([qt, part], axis=0)

    tiles_per_group = SLC_GROUP_BLOCKS // SLC_TILE_BLOCKS
    sub = SLC_KEY_TILE // Q_BLOCK
    last_tile = vt_ref.shape[2] - 1
    n_pairs = (i // sub + 2) // 2
    n_far = jnp.maximum(0, (i - (SLC_NEAR_CHUNKS - 1)) // sub)
    far_pairs = jnp.maximum(0, (n_far - 1) // 2)
    first_table = jnp.where(n_far > 0, 2 * far_pairs + 1, 0)

    def produce(kt, ls, s_ref, mx_ref, with_table):
        near = (kt >= first_table).astype(jnp.int32)
        kc = jnp.minimum(kt, last_tile)
        ks = pl.multiple_of(kc * SLC_KEY_TILE, SLC_KEY_TILE)
        s = jnp.dot(ka_ref[0, 0, pl.ds(ks, SLC_KEY_TILE), :], qa_ref[2 * (kc // tiles_per_group) + near, :, ls],
                    preferred_element_type=F32)
        if with_table:
            chunks = []
            for a in range(sub):
                mm = i - sub * kt - a
                idx = jnp.where(mm < 0, SLC_TAB_MASKED, jnp.where(mm >= SLC_NEAR_CHUNKS, SLC_TAB_CONST, mm))
                chunks.append(tab_ref[0, idx, :, ls])
            s = s + jnp.concatenate(chunks, axis=0)
        s_ref[:, ls] = s
        mx_ref[:, ls] = jnp.max(s, axis=0, keepdims=True)

    def consume(kt, ls, s_ref, mx_ref):
        m_old = m_ref[:, ls]
        m_new = jnp.maximum(m_old, mx_ref[:, ls])
        m_ref[:, ls] = m_new
        p = jnp.exp(s_ref[:, ls] - m_new).astype(BF16)
        acc_ref[:, ls] = jnp.exp(m_old - m_new) * acc_ref[:, ls] + jnp.dot(
            vt_ref[0, 0, jnp.minimum(kt, last_tile)], p, preferred_element_type=F32)

    strips = [slice(c * SLC_LANE_STRIP, (c + 1) * SLC_LANE_STRIP) for c in range(cols // SLC_LANE_STRIP)]

    def pair_step(j, with_table):
        for ls in strips:
            produce(2 * j + 1, ls, s1_ref, x1_ref, with_table)
            consume(2 * j, ls, s0_ref, x0_ref)
        for ls in strips:
            produce(2 * j + 2, ls, s0_ref, x0_ref, with_table)
            consume(2 * j + 1, ls, s1_ref, x1_ref)

    m_ref[...] = jnp.full(m_ref.shape, -jnp.inf, F32)
    acc_ref[...] = jnp.zeros(acc_ref.shape, F32)

    @pl.when(n_far > 0)
    def _():
        for ls in strips:
            produce(0, ls, s0_ref, x0_ref, False)

    @pl.when(n_far == 0)
    def _():
        for ls in strips:
            produce(0, ls, s0_ref, x0_ref, True)

    def far_body(j, carry):
        pair_step(j, False)
        return carry

    def near_body(j, carry):
        pair_step(j, True)
        return carry

    lax.fori_loop(0, far_pairs, far_body, 0)
    lax.fori_loop(far_pairs, n_pairs, near_body, 0)
    o_ref[0, 0, 0] = acc_ref[:HEAD_DIM, :] / acc_ref[HEAD_DIM:HEAD_DIM + 1, :]


def _slc_attn(qt, far_t, k_aug, vt_aug, sel_t, tab_t):
    bsz, _, nq, _, cols = qt.shape
    s = k_aug.shape[2]
    nbp = sel_t.shape[3]
    ngroups = nbp // SLC_GROUP_BLOCKS
    nch = tab_t.shape[1]
    once = pl.Buffered(1)
    return pl.pallas_call(
        _slc_attn_kernel,
        grid=(bsz, N_KV, nq),
        in_specs=[pl.BlockSpec((1, 1, 1, LANES, cols), lambda b, k, i: (b, k, i, 0, 0)),
                  pl.BlockSpec((1, LANES, cols), lambda b, k, i: (k, 0, 0)),
                  pl.BlockSpec((1, 1, s, 2 * LANES), lambda b, k, i: (b, k, 0, 0), pipeline_mode=once),
                  pl.BlockSpec((1, 1, s // SLC_KEY_TILE, SLC_V_ROWS, SLC_KEY_TILE),
                               lambda b, k, i: (b, k, 0, 0, 0), pipeline_mode=once),
                  pl.BlockSpec((1, 1, 1, nbp, Q_BLOCK), lambda b, k, i: (b, k, i, 0, 0)),
                  pl.BlockSpec((1, nch, Q_BLOCK, cols), lambda b, k, i: (k, 0, 0, 0), pipeline_mode=once)],
        out_specs=pl.BlockSpec((1, 1, 1, HEAD_DIM, cols), lambda b, k, i: (b, k, i, 0, 0)),
        out_shape=jax.ShapeDtypeStruct((bsz, N_KV, nq, HEAD_DIM, cols), F32),
        scratch_shapes=[pltpu.VMEM((2 * ngroups, 2 * LANES, cols), BF16),
                        pltpu.VMEM((SLC_KEY_TILE, cols), F32),
                        pltpu.VMEM((SLC_KEY_TILE, cols), F32),
                        pltpu.VMEM((1, cols), F32),
                        pltpu.VMEM((1, cols), F32),
                        pltpu.VMEM((1, cols), F32),
                        pltpu.VMEM((SLC_V_ROWS, cols), F32)],
        compiler_params=_cparams(("parallel", "parallel", "arbitrary")),
        name="nsa_slc_attn",
    )(qt, far_t, k_aug, vt_aug, sel_t, tab_t)


def _win_attn_kernel(q_ref, k_ref, v_ref, tab_ref, o_ref):
    i = pl.program_id(2)
    rows = N_GRP * Q_BLOCK
    nw = WINDOW + Q_BLOCK
    q = q_ref[0, 0].reshape(rows, HEAD_DIM)
    qs = pl.multiple_of(i * Q_BLOCK, Q_BLOCK)
    k = k_ref[0, 0, pl.ds(qs, nw), :]
    v = v_ref[0, 0, pl.ds(qs, nw), :]
    s = _nt(q, k) + tab_ref[0].reshape(rows, nw)
    col = lax.broadcasted_iota(jnp.int32, (1, nw), 1)
    s = jnp.where(col >= WINDOW - i * Q_BLOCK, s, NEG_INF)
    m = jnp.max(s, axis=-1, keepdims=True)
    e = jnp.exp(s - m)
    p = e / jnp.sum(e, axis=-1, keepdims=True)
    o = jnp.dot(p.astype(BF16), v, preferred_element_type=F32)
    o_ref[0, 0] = o.reshape(N_GRP, Q_BLOCK, HEAD_DIM)


def _win_attn(q64, k_win, v_win, tab_win):
    bsz, _, _, s, _ = q64.shape
    nq = s // Q_BLOCK
    sp = k_win.shape[2]
    nw = WINDOW + Q_BLOCK
    return pl.pallas_call(
        _win_attn_kernel,
        grid=(bsz, N_KV, nq),
        in_specs=[pl.BlockSpec((1, 1, N_GRP, Q_BLOCK, HEAD_DIM), lambda b, k, i: (b, k, 0, i, 0)),
                  pl.BlockSpec((1, 1, sp, HEAD_DIM), lambda b, k, i: (b, k, 0, 0)),
                  pl.BlockSpec((1, 1, sp, HEAD_DIM), lambda b, k, i: (b, k, 0, 0)),
                  pl.BlockSpec((1, N_GRP, Q_BLOCK, nw), lambda b, k, i: (k, 0, 0, 0))],
        out_specs=pl.BlockSpec((1, 1, N_GRP, Q_BLOCK, HEAD_DIM), lambda b, k, i: (b, k, 0, i, 0)),
        out_shape=jax.ShapeDtypeStruct((bsz, N_KV, N_GRP, s, HEAD_DIM), F32),
        compiler_params=_cparams(("parallel", "parallel", "arbitrary")),
        name="nsa_win_attn",
    )(q64, k_win, v_win, tab_win)


def _ssd_kernel(xbc_ref, z_ref, dt_ref, cw_ref, cb_ref, dtb_ref, alog_ref, dskip_ref, nw_ref, o_ref,
                ext_ref, state_ref):
    ln = SSM_CHUNK
    d_ssm = z_ref.shape[2]
    nh = dt_ref.shape[2]
    gw = d_ssm // SSM_GROUPS
    hpg = nh // SSM_GROUPS
    gn = SSM_GROUPS * SSM_STATE

    @pl.when(pl.program_id(1) == 0)
    def _():
        ext_ref[0:SUBLANES, :] = jnp.zeros((SUBLANES, ext_ref.shape[1]), F32)
        state_ref[...] = jnp.zeros(state_ref.shape, F32)

    ext_ref[SUBLANES:SUBLANES + ln, :] = xbc_ref[0]
    conv = cw_ref[0:1, :] * ext_ref[SUBLANES - CONV_WIDTH + 1:SUBLANES - CONV_WIDTH + 1 + ln, :]
    for k in range(1, CONV_WIDTH):
        lo = SUBLANES - CONV_WIDTH + 1 + k
        conv = conv + cw_ref[k:k + 1, :] * ext_ref[lo:lo + ln, :]
    conv = conv + cb_ref[...]
    ext_ref[0:SUBLANES, :] = xbc_ref[0, ln - SUBLANES:ln, :]
    xc = _silu(conv)
    xs = xc[:, :d_ssm]
    bm = xc[:, d_ssm:d_ssm + gn]
    cm = xc[:, d_ssm + gn:d_ssm + 2 * gn]

    xdt = dt_ref[0] + dtb_ref[...]
    dt = jnp.maximum(xdt, 0.0) + jnp.log1p(jnp.exp(-jnp.abs(xdt)))
    a = -jnp.exp(alog_ref[...])
    da = dt * a

    row = lax.broadcasted_iota(jnp.int32, (ln, ln), 0)
    colm = lax.broadcasted_iota(jnp.int32, (ln, ln), 1)
    causal = row >= colm
    acs = jnp.dot(causal.astype(F32), da, precision=HI, preferred_element_type=F32)
    eye = (lax.broadcasted_iota(jnp.int32, (2 * nh, 2 * nh), 0)
           == lax.broadcasted_iota(jnp.int32, (2 * nh, 2 * nh), 1)).astype(F32)
    rows_t = _nt(eye, jnp.concatenate([acs, dt], axis=1), precision=HI)
    expand = (lax.broadcasted_iota(jnp.int32, (nh, d_ssm), 0)
              == lax.broadcasted_iota(jnp.int32, (nh, d_ssm), 1) // SSM_HEAD_DIM).astype(F32)
    last = acs[ln - 1:ln, :]
    exp_acs_x = jnp.dot(jnp.exp(acs), expand, precision=HI, preferred_element_type=F32)
    w_x = jnp.dot(jnp.exp(last - acs) * dt, expand, precision=HI, preferred_element_type=F32)
    exp_last_x = exp_acs_x[ln - 1:ln, :]

    ys = []
    for g in range(SSM_GROUPS):
        cg = cm[:, g * SSM_STATE:(g + 1) * SSM_STATE].astype(BF16)
        bg32 = bm[:, g * SSM_STATE:(g + 1) * SSM_STATE]
        bg = bg32.astype(BF16)
        xg = xs[:, g * gw:(g + 1) * gw]
        cb = _nt(cg, bg)
        st = state_ref[g]
        y_state = jnp.dot(cg, st.astype(BF16), preferred_element_type=F32) * exp_acs_x[:, g * gw:(g + 1) * gw]
        y_heads = []
        for j in range(hpg):
            h = g * hpg + j
            seg = acs[:, h:h + 1] - rows_t[h:h + 1, :]
            decay = jnp.exp(jnp.where(causal, seg, -jnp.inf))
            mmat = cb * decay * rows_t[nh + h:nh + h + 1, :]
            xh = xg[:, j * SSM_HEAD_DIM:(j + 1) * SSM_HEAD_DIM].astype(BF16)
            y_heads.append(jnp.dot(mmat.astype(BF16), xh, preferred_element_type=F32))
        ys.append(jnp.concatenate(y_heads, axis=1) + y_state)
        xw = (xg * w_x[:, g * gw:(g + 1) * gw]).astype(BF16)
        state_ref[g] = st * exp_last_x[:, g * gw:(g + 1) * gw] + jnp.dot(
            bg32.T.astype(BF16), xw, preferred_element_type=F32)

    y = jnp.concatenate(ys, axis=1) + dskip_ref[...] * xs
    y = y * _silu(z_ref[0])
    outs = []
    for g in range(SSM_GROUPS):
        yg = y[:, g * gw:(g + 1) * gw]
        outs.append(yg * lax.rsqrt(jnp.mean(yg * yg, axis=-1, keepdims=True) + RMS_EPS))
    o_ref[0] = jnp.concatenate(outs, axis=1) * nw_ref[...]


def _ssd(xbc, z, dt_raw, conv_w, conv_b, dt_bias, a_log, d_skip, norm_w):
    bsz, s, ch = xbc.shape
    d_ssm = z.shape[2]
    nh = dt_raw.shape[2]
    gw = d_ssm // SSM_GROUPS
    nchunks = s // SSM_CHUNK
    full = lambda shape: pl.BlockSpec(shape, lambda b, c: (0,) * len(shape))
    return pl.pallas_call(
        _ssd_kernel,
        grid=(bsz, nchunks),
        in_specs=[pl.BlockSpec((1, SSM_CHUNK, ch), lambda b, c: (b, c, 0)),
                  pl.BlockSpec((1, SSM_CHUNK, d_ssm), lambda b, c: (b, c, 0)),
                  pl.BlockSpec((1, SSM_CHUNK, nh), lambda b, c: (b, c, 0)),
                  full((CONV_WIDTH, ch)), full((1, ch)), full((1, nh)), full((1, nh)),
                  full((1, d_ssm)), full((1, d_ssm))],
        out_specs=pl.BlockSpec((1, SSM_CHUNK, d_ssm), lambda b, c: (b, c, 0)),
        out_shape=jax.ShapeDtypeStruct((bsz, s, d_ssm), F32),
        scratch_shapes=[pltpu.VMEM((SUBLANES + SSM_CHUNK, ch), F32),
                        pltpu.VMEM((SSM_GROUPS, SSM_STATE, gw), F32)],
        compiler_params=_cparams(("parallel", "arbitrary")),
        name="ssd_scan",
    )(xbc, z, dt_raw, conv_w, conv_b.reshape(1, ch), dt_bias.reshape(1, nh), a_log.reshape(1, nh),
      jnp.repeat(d_skip, SSM_HEAD_DIM).reshape(1, d_ssm), norm_w.reshape(1, d_ssm))


def _mix_out_kernel(oc_ref, os_ref, ow_ref, gl_ref, ex_ref, an_ref, ssm_ref, w_ref, x_ref, g1_ref, o_ref):
    d_attn = oc_ref.shape[2]
    sig = jax.nn.sigmoid(gl_ref[0])
    gc = jnp.dot(sig, ex_ref[0], precision=HI, preferred_element_type=F32)
    gs = jnp.dot(sig, ex_ref[1], precision=HI, preferred_element_type=F32)
    gw = jnp.dot(sig, ex_ref[2], precision=HI, preferred_element_type=F32)
    attn = gc * oc_ref[0] + gs * os_ref[0] + gw * ow_ref[0]
    attn = attn * lax.rsqrt(jnp.mean(attn * attn, axis=-1, keepdims=True) + RMS_EPS) * an_ref[...]
    mix = (jnp.dot(attn.astype(BF16), w_ref[:d_attn, :], preferred_element_type=F32)
           + jnp.dot(ssm_ref[0].astype(BF16), w_ref[d_attn:, :], preferred_element_type=F32))
    o_ref[0] = x_ref[0] + g1_ref[0] * mix


def _gate_expand():
    ex = np.zeros((3, LANES, N_HEADS * HEAD_DIM), np.float32)
    for r in range(3):
        for h in range(N_HEADS):
            ex[r, h * 3 + r, h * HEAD_DIM:(h + 1) * HEAD_DIM] = 1.0
    return ex


def _mix_out(oc, os_, ow, gl, attn_norm, ssm, w_out_bf16, x, g1, tm=256):
    bsz, s, d = x.shape
    d_attn = oc.shape[2]
    d_ssm = ssm.shape[2]
    tok = lambda w: pl.BlockSpec((1, tm, w), lambda b, i: (b, i, 0))
    return pl.pallas_call(
        _mix_out_kernel,
        grid=(bsz, s // tm),
        in_specs=[tok(d_attn), tok(d_attn), tok(d_attn), tok(LANES),
                  pl.BlockSpec((3, LANES, d_attn), lambda b, i: (0, 0, 0)),
                  pl.BlockSpec((1, d_attn), lambda b, i: (0, 0)),
                  tok(d_ssm),
                  pl.BlockSpec((d_attn + d_ssm, d), lambda b, i: (0, 0)),
                  tok(d),
                  pl.BlockSpec((1, 1, d), lambda b, i: (b, 0, 0))],
        out_specs=tok(d),
        out_shape=jax.ShapeDtypeStruct((bsz, s, d), F32),
        compiler_params=_cparams(("parallel", "parallel")),
        name="mix_out",
    )(oc, os_, ow, gl, jnp.asarray(_gate_expand()), attn_norm.reshape(1, d_attn), ssm, w_out_bf16, x, g1)


def _peer_candidates():
    k, sub = PEER_TOPK, SUBLANES
    cells = [(0, b) for b in range(k)]
    cells += [(a, b) for a in range(1, sub) for b in range(sub)]
    cells += [(a, 0) for a in range(sub, k)]
    order = np.array([a * k + b if (a + 1) * (b + 1) <= k else -1 for a, b in cells], np.float32)
    group = np.zeros((k, LANES), np.float32)
    for row, (a, _) in enumerate(cells):
        group[a, row] = 1.0
    return cells, order, group


def _peer_route_kernel(q_ref, sk_ref, order_ref, group_ref, rank2_ref, e2_ref, n1_ref, c1_ref):
    q = q_ref[...]
    half = q.shape[1] // 2
    s1 = _nt(sk_ref[0], q[:, :half], precision=HI)
    s2 = _nt(sk_ref[1], q[:, half:], precision=HI)
    k = PEER_TOPK
    rank1, v1 = _topk_mark(s1, k, axis=0)
    rank2, v2 = _topk_mark(s2, k, axis=0)
    v1_all = jnp.concatenate(v1, axis=0)
    v2_all = jnp.concatenate(v2, axis=0)
    cand = jnp.concatenate([v1[0] + v2_all] + [v1[a] + v2_all[:SUBLANES] for a in range(1, SUBLANES)]
                           + [v1_all[SUBLANES:] + v2[0]], axis=0)
    order = order_ref[...]
    cand = jnp.where(order >= 0.0, cand, -jnp.inf)
    rank_c, best = _topk_mark(cand, k, axis=0, order=order, order_bound=float(k * k))
    chosen = jnp.where(rank_c < float(k), 1.0, 0.0).astype(BF16)
    chosen = jnp.concatenate([chosen, jnp.zeros((LANES - chosen.shape[0], chosen.shape[1]), BF16)], axis=0)
    count = jnp.dot(group_ref[...], chosen, preferred_element_type=F32)
    z = best[0] * 0.0
    for r in range(k):
        z = z + jnp.exp(best[r] - best[0])
    n1 = jnp.zeros(s1.shape, F32)
    for a in range(k):
        n1 = jnp.where(rank1 == float(a), count[a:a + 1, :], n1)
    rank2_ref[0] = rank2.astype(BF16)
    e2_ref[0] = jnp.exp(s2 - v2[0]).astype(BF16)
    n1_ref[0] = n1
    c1_ref[0] = jnp.exp(s1 - v1[0]) / z


def _peer_route(q, subkeys, tm=256):
    t, width = q.shape
    kd = width // PEER_HEADS
    nk = subkeys.shape[1]
    spec = pl.BlockSpec((1, nk, tm), lambda i, h: (h, 0, i))
    shape = jax.ShapeDtypeStruct((PEER_HEADS, nk, t), F32)
    cells, order, group = _peer_candidates()
    order = jnp.asarray(np.broadcast_to(order[:, None], (len(cells), tm)))
    return pl.pallas_call(
        _peer_route_kernel,
        grid=(t // tm, PEER_HEADS),
        in_specs=[pl.BlockSpec((tm, kd), lambda i, h: (i, h)),
                  pl.BlockSpec((2, nk, kd // 2), lambda i, h: (0, 0, 0)),
                  pl.BlockSpec((len(cells), tm), lambda i, h: (0, 0)),
                  pl.BlockSpec((PEER_TOPK, LANES), lambda i, h: (0, 0))],
        out_specs=[spec, spec, spec, spec],
        out_shape=[jax.ShapeDtypeStruct(shape.shape, BF16), jax.ShapeDtypeStruct(shape.shape, BF16), shape, shape],
        compiler_params=_cparams(("parallel", "parallel")),
        name="peer_route",
    )(q, subkeys, order, jnp.asarray(group, BF16))


def _peer_dense_kernel(ht_ref, u_ref, vt_ref, rank2_ref, e2_ref, n1_ref, c1_ref, o_ref, *scratch):
    j = pl.program_id(1)
    te = u_ref.shape[0]
    nk = rank2_ref.shape[1]

    @pl.when(j == 0)
    def _():
        o_ref[...] = jnp.zeros(o_ref.shape, F32)

    tm = ht_ref.shape[1]
    strips = [slice(c * PEER_LANE_STRIP, (c + 1) * PEER_LANE_STRIP) for c in range(tm // PEER_LANE_STRIP)]

    units = [(e, c) for e in range(te // PEER_EXPERT_UNIT) for c in range(len(strips))]
    act_refs = scratch[:len(units)]
    aw_refs = scratch[len(units):]
    n_piece = PEER_EXPERT_UNIT // PEER_EXPERT_PIECE
    d_piece = o_ref.shape[0] // n_piece

    def produce(k, p):
        e, c = units[k]
        lo = e * PEER_EXPERT_UNIT + p * PEER_EXPERT_PIECE
        act_refs[k][p * PEER_EXPERT_PIECE:(p + 1) * PEER_EXPERT_PIECE, :] = jax.nn.gelu(
            jnp.dot(u_ref[lo:lo + PEER_EXPERT_PIECE, :], ht_ref[:, strips[c]], preferred_element_type=F32)
        ).astype(BF16)

    def gate(k, r):
        e, c = units[k]
        ls = strips[c]
        i1 = (j * te + e * PEER_EXPERT_UNIT) // nk + r
        w = jnp.zeros((nk, PEER_LANE_STRIP), BF16)
        zero = jnp.zeros((nk, PEER_LANE_STRIP), BF16)
        for h in range(PEER_HEADS):
            n_row = n1_ref[h, pl.ds(i1, 1), ls].astype(BF16)
            c_row = c1_ref[h, pl.ds(i1, 1), ls].astype(BF16)
            w = w + jnp.where(rank2_ref[h, :, ls] < n_row, e2_ref[h, :, ls], zero) * c_row
        aw_refs[k][r * nk:(r + 1) * nk, :] = act_refs[k][r * nk:(r + 1) * nk, :] * w

    def combine(k, m):
        e, c = units[k]
        ds_ = slice(m * d_piece, (m + 1) * d_piece)
        es = slice(e * PEER_EXPERT_UNIT, (e + 1) * PEER_EXPERT_UNIT)
        o_ref[ds_, strips[c]] += jnp.dot(vt_ref[ds_, es], aw_refs[k][...], preferred_element_type=F32)

    gates_per_piece = PEER_EXPERT_PIECE // nk
    for p in range(n_piece):
        produce(0, p)
    for k in range(len(units)):
        for p in range(n_piece):
            if k + 1 < len(units):
                produce(k + 1, p)
            for r in range(p * gates_per_piece, (p + 1) * gates_per_piece):
                gate(k, r)
                if k > 0 and r == p * gates_per_piece:
                    combine(k - 1, p)
    for m in range(n_piece):
        combine(len(units) - 1, m)


def _peer_dense(h_t, u_bf16, v_t_bf16, rank2, e2, n1, c1, tm=512, te=512):
    d, t = h_t.shape
    n_exp = u_bf16.shape[0]
    nk = rank2.shape[1]
    n_units = (te // PEER_EXPERT_UNIT) * (tm // PEER_LANE_STRIP)
    route = pl.BlockSpec((PEER_HEADS, nk, tm), lambda i, j: (0, 0, i))
    return pl.pallas_call(
        _peer_dense_kernel,
        grid=(t // tm, n_exp // te),
        in_specs=[pl.BlockSpec((d, tm), lambda i, j: (0, i)),
                  pl.BlockSpec((te, d), lambda i, j: (j, 0)),
                  pl.BlockSpec((d, te), lambda i, j: (0, j)),
                  route, route, route, route],
        out_specs=pl.BlockSpec((d, tm), lambda i, j: (0, i)),
        out_shape=jax.ShapeDtypeStruct((d, t), F32),
        scratch_shapes=([pltpu.VMEM((PEER_EXPERT_UNIT, PEER_LANE_STRIP), BF16)] * n_units
                        + [pltpu.VMEM((PEER_EXPERT_UNIT, PEER_LANE_STRIP), BF16)] * n_units),
        compiler_params=_cparams(("parallel", "arbitrary")),
        name="peer_dense",
    )(h_t, u_bf16, v_t_bf16, rank2, e2, n1, c1)


def _residual_kernel(x_ref, y_ref, g_ref, o_ref):
    o_ref[0] = x_ref[0] + g_ref[0] * y_ref[0]


def _residual(x, y, g, tm=512):
    bsz, s, d = x.shape
    tok = pl.BlockSpec((1, tm, d), lambda b, i: (b, i, 0))
    return pl.pallas_call(
        _residual_kernel,
        grid=(bsz, s // tm),
        in_specs=[tok, tok, pl.BlockSpec((1, 1, d), lambda b, i: (b, 0, 0))],
        out_specs=tok,
        out_shape=jax.ShapeDtypeStruct((bsz, s, d), F32),
        compiler_params=_cparams(("parallel", "parallel")),
        name="residual",
    )(x, y, g)


def _final_norm_kernel(x_ref, w_ref, o_ref):
    x = x_ref[0]
    o_ref[0] = x * lax.rsqrt(jnp.mean(x * x, axis=-1, keepdims=True) + RMS_EPS) * w_ref[...]


def _final_norm(x, w, tm=512):
    bsz, s, d = x.shape
    tok = pl.BlockSpec((1, tm, d), lambda b, i: (b, i, 0))
    return pl.pallas_call(
        _final_norm_kernel,
        grid=(bsz, s // tm),
        in_specs=[tok, pl.BlockSpec((1, d), lambda b, i: (0, 0))],
        out_specs=tok,
        out_shape=jax.ShapeDtypeStruct((bsz, s, d), F32),
        compiler_params=_cparams(("parallel", "parallel")),
        name="final_norm",
    )(x, w.reshape(1, d))


def _pad_cols(w, width):
    return jnp.pad(w, ((0, 0), (0, width - w.shape[1])))


def _nsa(q, kv, tables, cmp_pe, cmp_w1, cmp_w2):
    tab_win, tab_cmp, tab_slc, far_col, far_t = tables
    bsz, s, _ = q.shape
    nc = s // CMP_STRIDE
    nq = s // Q_BLOCK
    n_slc = s // SLC_BLOCK
    nbp = -(-n_slc // SLC_GROUP_BLOCKS) * SLC_GROUP_BLOCKS
    scale = HEAD_DIM ** -0.5
    q6 = (q * scale).astype(BF16).reshape(bsz, nq, Q_BLOCK, N_KV, N_GRP, HEAD_DIM)
    qh = q6.transpose(0, 3, 4, 1, 2, 5).reshape(bsz, N_KV, N_GRP, s, HEAD_DIM)
    qt = q6.transpose(0, 3, 1, 5, 4, 2).reshape(bsz, N_KV, nq, HEAD_DIM, N_GRP * Q_BLOCK)
    qt = jnp.pad(qt, ((0, 0),) * 3 + ((0, LANES - HEAD_DIM), (0, 0)))
    kv6 = kv.reshape(bsz, s, 6, N_KV, HEAD_DIM)

    kv_cmp = kv6[:, :, 0:2].reshape(bsz, nc, CMP_STRIDE, 2, N_KV, HEAD_DIM)
    kv_cmp = kv_cmp.transpose(0, 3, 4, 1, 2, 5).reshape(bsz, 2, N_KV, nc, CMP_STRIDE * HEAD_DIM)
    kv_c = _compress(kv_cmp, cmp_pe, cmp_w1, cmp_w2)
    o_c, sel_t = _cmp_attn(qh, kv_c, tab_cmp, far_col, nbp)

    k_slc = kv6[:, :, 2].transpose(0, 2, 1, 3).astype(BF16)
    nkt = s // SLC_KEY_TILE
    vt = kv6[:, :, 3].astype(BF16).reshape(bsz, nkt, SLC_KEY_TILE, N_KV, HEAD_DIM).transpose(0, 3, 1, 4, 2)
    vt_aug = jnp.concatenate([
        vt, jnp.ones((bsz, N_KV, nkt, 1, SLC_KEY_TILE), BF16),
        jnp.zeros((bsz, N_KV, nkt, SLC_V_ROWS - HEAD_DIM - 1, SLC_KEY_TILE), BF16)], axis=3)
    blk = np.arange(s) // SLC_BLOCK
    onehot = (blk[:, None] % SLC_GROUP_BLOCKS == np.arange(SLC_GROUP_BLOCKS)[None, :]).astype(np.float32)
    k_aug = jnp.concatenate([
        k_slc, jnp.ones((bsz, N_KV, s, SLC_BIAS_PIECES), BF16),
        jnp.zeros((bsz, N_KV, s, LANES - HEAD_DIM - SLC_BIAS_PIECES), BF16),
        jnp.broadcast_to(jnp.asarray(onehot, BF16), (bsz, N_KV, s, SLC_GROUP_BLOCKS))], axis=-1)
    o_st = _slc_attn(qt, far_t, k_aug, vt_aug, sel_t, tab_slc)
    o_s = o_st.reshape(bsz, N_KV, nq, HEAD_DIM, N_GRP, Q_BLOCK).transpose(0, 2, 5, 1, 4, 3)
    o_s = o_s.reshape(bsz, s, N_HEADS * HEAD_DIM)

    front = ((0, 0), (0, 0), (WINDOW, 0), (0, 0))
    k_win = jnp.pad(kv6[:, :, 4].transpose(0, 2, 1, 3).astype(BF16), front)
    v_win = jnp.pad(kv6[:, :, 5].transpose(0, 2, 1, 3).astype(BF16), front)
    o_w = _win_attn(qh, k_win, v_win, tab_win)

    back = lambda o: o.transpose(0, 3, 1, 2, 4).reshape(bsz, s, N_HEADS * HEAD_DIM)
    return back(o_c), o_s, back(o_w)


def kernel(x, c, ada_w, ada_b, norm_mix, norm_ffn, w_in, cmp_pe, cmp_w1, cmp_w2, rel_bias, attn_out_norm,
           conv_w, conv_b, dt_bias, a_log, d_skip, ssm_norm, w_out, peer_wq, peer_subkeys, peer_u, peer_v,
           norm_final):
    bsz, s, d = x.shape
    depth = ada_w.shape[0]
    d_attn = N_HEADS * HEAD_DIM
    n_kv = 6 * N_KV * HEAD_DIM
    n_gate = 3 * N_HEADS
    d_ssm = ssm_norm.shape[1]
    ch = conv_w.shape[2]
    nh = dt_bias.shape[1]

    mod = _ada_mod(c, ada_w, ada_b)
    tables = _bias_tables(rel_bias)

    for l in range(depth):
        sh1, sc1, g1, sh2, sc2, g2 = [mod[l, :, i * d:(i + 1) * d].reshape(bsz, 1, d) for i in range(6)]
        cuts = np.cumsum([0, d_attn, n_kv, n_gate, d_ssm, ch, nh])
        seg = [w_in[l][:, cuts[i]:cuts[i + 1]] for i in range(6)]
        seg[2] = _pad_cols(seg[2], LANES)
        seg[5] = _pad_cols(seg[5], LANES)
        w_cat = jnp.concatenate(seg, axis=1).astype(BF16)
        pc = np.cumsum([0] + [w.shape[1] for w in seg])
        proj, _ = _norm_proj(x, norm_mix[l], sc1, sh1, w_cat)
        q = proj[:, :, pc[0]:pc[1]]
        kv = proj[:, :, pc[1]:pc[2]]
        gl = proj[:, :, pc[2]:pc[3]]
        z = proj[:, :, pc[3]:pc[4]]
        xbc = proj[:, :, pc[4]:pc[5]]
        dt_raw = proj[:, :, pc[5]:pc[5] + nh]

        o_c, o_s, o_w = _nsa(q, kv, tables, cmp_pe[l], cmp_w1[l], cmp_w2[l])
        ssm = _ssd(xbc, z, dt_raw, conv_w[l], conv_b[l], dt_bias[l], a_log[l], d_skip[l], ssm_norm[l])
        x = _mix_out(o_c, o_s, o_w, gl, attn_out_norm[l], ssm, w_out[l].astype(BF16), x, g1)

        pq, h2 = _norm_proj(x, norm_ffn[l], sc2, sh2, peer_wq[l].astype(BF16))
        rank2, e2, n1, c1 = _peer_route(pq.reshape(bsz * s, -1), peer_subkeys[l])
        h_t = h2.reshape(bsz * s, d).T
        ffn_t = _peer_dense(h_t, peer_u[l].astype(BF16), peer_v[l].T.astype(BF16), rank2, e2, n1, c1)
        x = _residual(x, ffn_t.T.reshape(bsz, s, d), g2)

    return _final_norm(x, norm_final)
```

```python
import functools
import math

import numpy as np
import jax
import jax.numpy as jnp
from jax import lax
from jax.experimental import pallas as pl
from jax.experimental.pallas import tpu as pltpu

F32 = jnp.float32
BF16 = jnp.bfloat16
HI = lax.Precision.HIGHEST

N_HEADS = 16
N_KV = 2
N_GRP = N_HEADS // N_KV
HEAD_DIM = 64
CMP_BLOCK = 32
CMP_STRIDE = 16
CMP_HIDDEN = 4 * HEAD_DIM
SLC_BLOCK = 64
SLC_TOPN = 16
WINDOW = 512
Q_BLOCK = 128
FORCE_SCORE = 1e4
NEG_INF = -1e30
REL_BUCKETS = 32
REL_MAX_DIST = 2048
SSM_HEAD_DIM = 64
SSM_GROUPS = 2
SSM_STATE = 128
CONV_WIDTH = 4
SSM_CHUNK = 256
PEER_HEADS = 8
PEER_NKEYS = 128
PEER_TOPK = 16
RMS_EPS = 1e-6

LANES = 128
SUBLANES = 8
VMEM_LIMIT = 56 * 1024 * 1024

SLC_KEY_TILE = 512
SLC_TILE_BLOCKS = SLC_KEY_TILE // SLC_BLOCK
SLC_GROUP_BLOCKS = LANES
SLC_NEAR_CHUNKS = 14
SLC_TAB_MASKED = 14
SLC_TAB_CONST = 15
MASK_BIG = 2.0 ** 100
SLC_LANE_STRIP = 256
PEER_LANE_STRIP = 256
PEER_EXPERT_PIECE = 256
PEER_EXPERT_UNIT = 512
SLC_V_ROWS = HEAD_DIM + 16
SLC_BIAS_PIECES = 3
GATE_PIECES = 3

CMP_FAR_STEP = 256
CMP_NEAR = 128
CMP_PAD = CMP_NEAR - Q_BLOCK // CMP_STRIDE


def _cparams(sem, vmem=VMEM_LIMIT):
    return pltpu.CompilerParams(dimension_semantics=sem, vmem_limit_bytes=vmem)


def _nt(a, b, precision=None):
    return lax.dot_general(a, b, (((1,), (1,)), ((), ())), precision=precision,
                           preferred_element_type=F32)


def _silu(x):
    return x * jax.nn.sigmoid(x)


def _rel_bucket_np(d):
    d = np.maximum(np.asarray(d, np.int64), 0)
    max_exact = REL_BUCKETS // 2
    ratio = np.log(np.maximum(d, max_exact).astype(np.float64) / max_exact) / math.log(REL_MAX_DIST / max_exact)
    scaled = ratio * (REL_BUCKETS - max_exact)
    large = max_exact + np.floor(scaled).astype(np.int64)
    return np.where(d < max_exact, d, np.minimum(large, REL_BUCKETS - 1)).astype(np.int32)


def _ada_kernel(c_ref, w_ref, b_ref, o_ref):
    cond = _silu(c_ref[...])
    o_ref[0] = jnp.dot(cond, w_ref[0], precision=HI, preferred_element_type=F32) + b_ref[0]


def _ada_mod(c, ada_w, ada_b):
    depth, d, n = ada_w.shape
    bsz = c.shape[0]
    rows = SUBLANES
    c_pad = jnp.zeros((rows, d), F32).at[:bsz].set(c)
    tn = 1024
    out = pl.pallas_call(
        _ada_kernel,
        grid=(depth, n // tn),
        in_specs=[pl.BlockSpec((rows, d), lambda l, j: (0, 0)),
                  pl.BlockSpec((1, d, tn), lambda l, j: (l, 0, j)),
                  pl.BlockSpec((1, 1, tn), lambda l, j: (l, 0, j))],
        out_specs=pl.BlockSpec((1, rows, tn), lambda l, j: (l, 0, j)),
        out_shape=jax.ShapeDtypeStruct((depth, rows, n), F32),
        compiler_params=_cparams(("parallel", "parallel")),
        name="ada_mod",
    )(c_pad, ada_w, ada_b.reshape(depth, 1, n))
    return out[:, :bsz]


def _norm_proj_kernel(x_ref, nw_ref, sc_ref, sh_ref, w_ref, o_ref, h_ref, hs_ref):
    @pl.when(pl.program_id(2) == 0)
    def _():
        x = x_ref[0]
        y = x * lax.rsqrt(jnp.mean(x * x, axis=-1, keepdims=True) + RMS_EPS)
        h = (y * nw_ref[...]) * (1.0 + sc_ref[0]) + sh_ref[0]
        hs_ref[...] = h.astype(BF16)
        h_ref[0] = h.astype(BF16)

    o_ref[0] = jnp.dot(hs_ref[...], w_ref[...], preferred_element_type=F32)


def _norm_proj(x, nw, sc, sh, w_bf16, tm=1024, tn=512):
    bsz, s, d = x.shape
    n = w_bf16.shape[1]
    return pl.pallas_call(
        _norm_proj_kernel,
        grid=(bsz, s // tm, n // tn),
        in_specs=[pl.BlockSpec((1, tm, d), lambda b, i, j: (b, i, 0)),
                  pl.BlockSpec((1, d), lambda b, i, j: (0, 0)),
                  pl.BlockSpec((1, 1, d), lambda b, i, j: (b, 0, 0)),
                  pl.BlockSpec((1, 1, d), lambda b, i, j: (b, 0, 0)),
                  pl.BlockSpec((d, tn), lambda b, i, j: (0, j))],
        out_specs=[pl.BlockSpec((1, tm, tn), lambda b, i, j: (b, i, j)),
                   pl.BlockSpec((1, tm, d), lambda b, i, j: (b, i, 0))],
        out_shape=[jax.ShapeDtypeStruct((bsz, s, n), F32),
                   jax.ShapeDtypeStruct((bsz, s, d), BF16)],
        scratch_shapes=[pltpu.VMEM((tm, d), BF16)],
        compiler_params=_cparams(("parallel", "parallel", "arbitrary")),
        name="norm_proj",
    )(x, nw.reshape(1, d), sc, sh, w_bf16)


def _bias_kernel(rel_ref, bk_ref, o_ref):
    h = pl.program_id(0)
    bk = bk_ref[...]
    acc = jnp.full(bk.shape, NEG_INF, F32)
    for b in range(REL_BUCKETS):
        acc = jnp.where(bk == b, rel_ref[b, h], acc)
    o_ref[0] = acc


def _bias_tables(rel_bias):
    r = np.arange(Q_BLOCK)[:, None]
    dw = r - np.arange(WINDOW + Q_BLOCK)[None, :] + WINDOW
    win = np.where((dw >= 0) & (dw < WINDOW), _rel_bucket_np(dw), -1)
    off = CMP_STRIDE * CMP_PAD - (CMP_BLOCK - 1)
    dc = r + off - CMP_STRIDE * np.arange(CMP_NEAR)[None, :]
    cmp_near = np.where(dc >= 0, _rel_bucket_np(dc), -1)
    chunks = []
    for m in range(SLC_NEAR_CHUNKS):
        ds_ = Q_BLOCK * m + r - np.arange(Q_BLOCK)[None, :]
        chunks.append(np.where(ds_ >= 0, _rel_bucket_np(ds_), -1))
    assert _rel_bucket_np(Q_BLOCK * SLC_NEAR_CHUNKS - (Q_BLOCK - 1)) == REL_BUCKETS - 1
    assert _rel_bucket_np(off + CMP_STRIDE) == REL_BUCKETS - 1
    chunks.append(np.full((Q_BLOCK, Q_BLOCK), -1))
    chunks.append(np.full((Q_BLOCK, Q_BLOCK), REL_BUCKETS - 1))
    bk = np.concatenate([win, cmp_near] + chunks, axis=1).astype(np.int32)
    cols = bk.shape[1]
    out = pl.pallas_call(
        _bias_kernel,
        grid=(N_HEADS,),
        in_specs=[pl.BlockSpec(memory_space=pltpu.SMEM),
                  pl.BlockSpec((Q_BLOCK, cols), lambda h: (0, 0))],
        out_specs=pl.BlockSpec((1, Q_BLOCK, cols), lambda h: (h, 0, 0)),
        out_shape=jax.ShapeDtypeStruct((N_HEADS, Q_BLOCK, cols), F32),
        compiler_params=_cparams(("arbitrary",)),
        name="bias_tables",
    )(rel_bias, jnp.asarray(bk))
    nw = WINDOW + Q_BLOCK
    tab_win = out[:, :, :nw].reshape(N_KV, N_GRP, Q_BLOCK, nw)
    tab_cmp = out[:, :, nw:nw + CMP_NEAR].reshape(N_KV, N_GRP, Q_BLOCK, CMP_NEAR)
    nch = SLC_NEAR_CHUNKS + 2
    tab_slc = out[:, :, nw + CMP_NEAR:].reshape(N_KV, N_GRP, Q_BLOCK, nch, Q_BLOCK)
    tab_slc = tab_slc.transpose(0, 3, 4, 1, 2).reshape(N_KV, nch, Q_BLOCK, N_GRP * Q_BLOCK)
    far = rel_bias[REL_BUCKETS - 1].reshape(N_KV, N_GRP, 1)
    far_col = jnp.broadcast_to(far, (N_KV, N_GRP, Q_BLOCK)).reshape(N_KV, N_GRP * Q_BLOCK, 1)
    far_row = far_col.reshape(N_KV, 1, N_GRP * Q_BLOCK)
    pieces, rest = [], far_row
    for _ in range(SLC_BIAS_PIECES):
        piece = rest.astype(BF16)
        pieces.append(piece)
        rest = rest - piece.astype(F32)
    zeros = lambda w: jnp.zeros((N_KV, w, N_GRP * Q_BLOCK), BF16)
    far_t = jnp.concatenate([zeros(HEAD_DIM)] + pieces + [zeros(LANES - HEAD_DIM - SLC_BIAS_PIECES)], axis=1)
    return tab_win, tab_cmp, tab_slc, far_col, far_t


def _compress_kernel(a_ref, pe_ref, w1_ref, w2_ref, o_ref):
    a = a_ref[0, 0, 0]
    half = a.shape[1]
    lo = jnp.dot((a + pe_ref[0, 0:1, :]).astype(BF16), w1_ref[0, :half, :].astype(BF16),
                 preferred_element_type=F32)
    hi = jnp.dot((a + pe_ref[0, 1:2, :]).astype(BF16), w1_ref[0, half:, :].astype(BF16),
                 preferred_element_type=F32)
    nc = a.shape[0]
    hid = jax.nn.gelu(lo + pltpu.roll(hi, nc - 1, axis=0))
    out = jnp.dot(hid.astype(BF16), w2_ref[0].astype(BF16), preferred_element_type=F32)
    o_ref[0, 0, 0] = jnp.zeros(o_ref.shape[3:], F32)
    o_ref[0, 0, 0, CMP_PAD:CMP_PAD + nc, :] = out


def _compress(kv_cmp, cmp_pe, cmp_w1, cmp_w2):
    bsz, _, _, nc, half = kv_cmp.shape
    pe = cmp_pe.reshape(2, 2, half)
    rows = CMP_PAD + nc + SUBLANES
    return pl.pallas_call(
        _compress_kernel,
        grid=(bsz, 2, N_KV),
        in_specs=[pl.BlockSpec((1, 1, 1, nc, half), lambda b, w, k: (b, w, k, 0, 0)),
                  pl.BlockSpec((1, 2, half), lambda b, w, k: (w, 0, 0)),
                  pl.BlockSpec((1, 2 * half, CMP_HIDDEN), lambda b, w, k: (w, 0, 0)),
                  pl.BlockSpec((1, CMP_HIDDEN, HEAD_DIM), lambda b, w, k: (w, 0, 0))],
        out_specs=pl.BlockSpec((1, 1, 1, rows, HEAD_DIM), lambda b, w, k: (b, w, k, 0, 0)),
        out_shape=jax.ShapeDtypeStruct((bsz, 2, N_KV, rows, HEAD_DIM), F32),
        compiler_params=_cparams(("parallel", "parallel", "parallel")),
        name="nsa_compress",
    )(kv_cmp, pe, cmp_w1, cmp_w2)


def _topk_mark(vals, k, axis, order=None, order_bound=None):
    n = vals.shape[axis] if order is None else order_bound
    iota = lax.broadcasted_iota(jnp.int32, vals.shape, axis).astype(F32) if order is None else order
    rank = jnp.full(vals.shape, float(k), F32)
    work = vals
    picked = []
    for r in range(k):
        m = jnp.max(work, axis=axis, keepdims=True)
        ix = jnp.min(jnp.where(work == m, iota, float(n)), axis=axis, keepdims=True)
        hit = iota == ix
        rank = jnp.where(hit, float(r), rank)
        work = jnp.where(hit, -jnp.inf, work)
        picked.append(m)
    return rank, picked


def _cmp_attn_kernel(q_ref, k_ref, v_ref, tab_ref, far_ref, m_ref, o_ref, sel_ref, imp_ref):
    i = pl.program_id(2)
    rows = N_GRP * Q_BLOCK
    nc = k_ref.shape[3] - CMP_PAD - SUBLANES
    nbp = sel_ref.shape[3]
    start = pl.multiple_of(i * (Q_BLOCK // CMP_STRIDE), SUBLANES)
    n_far = i * (Q_BLOCK // CMP_STRIDE) - CMP_PAD

    def attend(wf):
        q = q_ref[0, 0].reshape(rows, HEAD_DIM)
        k_far = k_ref[0, 0, 0, CMP_PAD:CMP_PAD + wf, :].astype(BF16)
        v_far = v_ref[0, 0, 0, CMP_PAD:CMP_PAD + wf, :].astype(BF16)
        k_near = k_ref[0, 0, 0, pl.ds(start, CMP_NEAR), :].astype(BF16)
        v_near = v_ref[0, 0, 0, pl.ds(start, CMP_NEAR), :].astype(BF16)

        n_idx = lax.broadcasted_iota(jnp.int32, (1, wf), 1)
        mask_far = n_idx < n_far
        s_far = jnp.where(mask_far, _nt(q, k_far) + far_ref[0], NEG_INF)
        tab = tab_ref[0].reshape(rows, CMP_NEAR)
        c_idx = lax.broadcasted_iota(jnp.int32, (1, CMP_NEAR), 1)
        mask_near = (c_idx >= -n_far) & (tab > 0.5 * NEG_INF)
        s_near = jnp.where(mask_near, _nt(q, k_near) + tab, NEG_INF)

        m = jnp.maximum(jnp.max(s_far, axis=-1, keepdims=True), jnp.max(s_near, axis=-1, keepdims=True))
        e_far = jnp.exp(s_far - m)
        e_near = jnp.exp(s_near - m)
        l = jnp.sum(e_far, axis=-1, keepdims=True) + jnp.sum(e_near, axis=-1, keepdims=True)
        p_far = jnp.where(mask_far, e_far / l, 0.0)
        p_near = jnp.where(mask_near, e_near / l, 0.0)
        o = (jnp.dot(p_far.astype(BF16), v_far, preferred_element_type=F32)
             + jnp.dot(p_near.astype(BF16), v_near, preferred_element_type=F32))
        o_ref[0, 0] = o.reshape(N_GRP, Q_BLOCK, HEAD_DIM)

        ps_far = jnp.sum(p_far.reshape(N_GRP, Q_BLOCK, wf), axis=0)
        ps_near = jnp.sum(p_near.reshape(N_GRP, Q_BLOCK, CMP_NEAR), axis=0)
        imp_ref[...] = (
            jnp.dot(ps_far, m_ref[CMP_PAD:CMP_PAD + wf, :], precision=HI, preferred_element_type=F32)
            + jnp.dot(ps_near, m_ref[pl.ds(start, CMP_NEAR), :], precision=HI, preferred_element_type=F32))

    step = min(CMP_FAR_STEP, nc)
    widths = list(range(step, nc + 1, step))
    for b, wf in enumerate(widths):
        lower = n_far > widths[b - 1] if b > 0 else True
        upper = n_far <= wf if b + 1 < len(widths) else True
        pl.when(jnp.logical_and(lower, upper))(functools.partial(attend, wf))

    imp = imp_ref[...].T
    t = i * Q_BLOCK + lax.broadcasted_iota(jnp.int32, (1, Q_BLOCK), 1)
    cur = t // SLC_BLOCK
    blk = lax.broadcasted_iota(jnp.int32, (nbp, 1), 0)
    forced = (blk == 0) | (blk == cur) | (blk == cur - 1)
    imp = jnp.where(forced, FORCE_SCORE, jnp.where(blk <= cur, imp, -FORCE_SCORE))
    n_blocks = (nc * CMP_STRIDE) // SLC_BLOCK
    imp = jnp.where(blk < n_blocks, imp, -jnp.inf)
    rank, _ = _topk_mark(imp, min(SLC_TOPN, n_blocks), axis=0)
    sel_ref[0, 0, 0] = jnp.where(rank < float(SLC_TOPN), 1.0, 0.0).astype(BF16)


def _overlap_matrix(nc, nbp):
    n_cmp = nc - 1
    n_slc = nc * CMP_STRIDE // SLC_BLOCK
    j = np.arange(n_slc)
    lo = np.clip((j * SLC_BLOCK - CMP_BLOCK) // CMP_STRIDE + 1, 0, n_cmp)
    hi = np.clip(-((-(j * SLC_BLOCK + SLC_BLOCK)) // CMP_STRIDE), 0, n_cmp)
    m = np.zeros((CMP_PAD + nc + SUBLANES, nbp), np.float32)
    n = np.arange(nc)[:, None]
    m[CMP_PAD:CMP_PAD + nc, :n_slc] = (n >= lo[None, :]) & (n < hi[None, :])
    return m


def _cmp_attn(q64, kv_c, tab_cmp, far_col, nbp):
    bsz, _, _, s, _ = q64.shape
    rows_c = kv_c.shape[3]
    nc = rows_c - CMP_PAD - SUBLANES
    nq = s // Q_BLOCK
    m_pad = jnp.asarray(_overlap_matrix(nc, nbp))
    return pl.pallas_call(
        _cmp_attn_kernel,
        grid=(bsz, N_KV, nq),
        in_specs=[pl.BlockSpec((1, 1, N_GRP, Q_BLOCK, HEAD_DIM), lambda b, k, i: (b, k, 0, i, 0)),
                  pl.BlockSpec((1, 1, 1, rows_c, HEAD_DIM), lambda b, k, i: (b, 0, k, 0, 0)),
                  pl.BlockSpec((1, 1, 1, rows_c, HEAD_DIM), lambda b, k, i: (b, 1, k, 0, 0)),
                  pl.BlockSpec((1, N_GRP, Q_BLOCK, CMP_NEAR), lambda b, k, i: (k, 0, 0, 0)),
                  pl.BlockSpec((1, N_GRP * Q_BLOCK, 1), lambda b, k, i: (k, 0, 0)),
                  pl.BlockSpec((rows_c, nbp), lambda b, k, i: (0, 0))],
        out_specs=[pl.BlockSpec((1, 1, N_GRP, Q_BLOCK, HEAD_DIM), lambda b, k, i: (b, k, 0, i, 0)),
                   pl.BlockSpec((1, 1, 1, nbp, Q_BLOCK), lambda b, k, i: (b, k, i, 0, 0))],
        out_shape=[jax.ShapeDtypeStruct((bsz, N_KV, N_GRP, s, HEAD_DIM), F32),
                   jax.ShapeDtypeStruct((bsz, N_KV, nq, nbp, Q_BLOCK), BF16)],
        scratch_shapes=[pltpu.VMEM((Q_BLOCK, nbp), F32)],
        compiler_params=_cparams(("parallel", "parallel", "arbitrary")),
        name="nsa_cmp_attn",
    )(q64, kv_c, kv_c, tab_cmp, far_col, m_pad)


def _slc_attn_kernel(qt_ref, far_ref, ka_ref, vt_ref, selt_ref, tab_ref, o_ref,
                     qa_ref, s0_ref, s1_ref, x0_ref, x1_ref, m_ref, acc_ref):
    i = pl.program_id(2)
    cols = N_GRP * Q_BLOCK
    ngroups = qa_ref.shape[0] // 2
    qt = qt_ref[0, 0, 0]
    qt_far = qt + far_ref[0]
    selneg = ((selt_ref[0, 0, 0].astype(F32) - 1.0) * MASK_BIG).astype(BF16)
    for g in range(ngroups):
        part = selneg[g * SLC_GROUP_BLOCKS:(g + 1) * SLC_GROUP_BLOCKS, :]
        part = jnp.concatenate([part] * N_GRP, axis=1)
        qa_ref[2 * g] = jnp.concatenate([qt_far, part], axis=0)
        qa_ref[2 * g + 1] = jnp.concatenate([qt, part], axis=0)

    tiles_per_group = SLC_GROUP_BLOCKS // SLC_TILE_BLOCKS
    sub = SLC_KEY_TILE // Q_BLOCK
    last_tile = vt_ref.shape[2] - 1
    n_pairs = (i // sub + 2) // 2
    n_far = jnp.maximum(0, (i - (SLC_NEAR_CHUNKS - 1)) // sub)
    far_pairs = jnp.maximum(0, (n_far - 1) // 2)
    first_table = jnp.where(n_far > 0, 2 * far_pairs + 1, 0)

    def produce(kt, ls, s_ref, mx_ref, with_table):
        near = (kt >= first_table).astype(jnp.int32)
        kc = jnp.minimum(kt, last_tile)
        ks = pl.multiple_of(kc * SLC_KEY_TILE, SLC_KEY_TILE)
        s = jnp.dot(ka_ref[0, 0, pl.ds(ks, SLC_KEY_TILE), :], qa_ref[2 * (kc // tiles_per_group) + near, :, ls],
                    preferred_element_type=F32)
        if with_table:
            chunks = []
            for a in range(sub):
                mm = i - sub * kt - a
                idx = jnp.where(mm < 0, SLC_TAB_MASKED, jnp.where(mm >= SLC_NEAR_CHUNKS, SLC_TAB_CONST, mm))
                chunks.append(tab_ref[0, idx, :, ls])
            s = s + jnp.concatenate(chunks, axis=0)
        s_ref[:, ls] = s
        mx_ref[:, ls] = jnp.max(s, axis=0, keepdims=True)

    def consume(kt, ls, s_ref, mx_ref):
        m_old = m_ref[:, ls]
        m_new = jnp.maximum(m_old, mx_ref[:, ls])
        m_ref[:, ls] = m_new
        p = jnp.exp(s_ref[:, ls] - m_new).astype(BF16)
        acc_ref[:, ls] = jnp.exp(m_old - m_new) * acc_ref[:, ls] + jnp.dot(
            vt_ref[0, 0, jnp.minimum(kt, last_tile)], p, preferred_element_type=F32)

    strips = [slice(c * SLC_LANE_STRIP, (c + 1) * SLC_LANE_STRIP) for c in range(cols // SLC_LANE_STRIP)]

    def pair_step(j, with_table):
        for ls in strips:
            produce(2 * j + 1, ls, s1_ref, x1_ref, with_table)
            consume(2 * j, ls, s0_ref, x0_ref)
        for ls in strips:
            produce(2 * j + 2, ls, s0_ref, x0_ref, with_table)
            consume(2 * j + 1, ls, s1_ref, x1_ref)

    m_ref[...] = jnp.full(m_ref.shape, -jnp.inf, F32)
    acc_ref[...] = jnp.zeros(acc_ref.shape, F32)

    @pl.when(n_far > 0)
    def _():
        for ls in strips:
            produce(0, ls, s0_ref, x0_ref, False)

    @pl.when(n_far == 0)
    def _():
        for ls in strips:
            produce(0, ls, s0_ref, x0_ref, True)

    def far_body(j, carry):
        pair_step(j, False)
        return carry

    def near_body(j, carry):
        pair_step(j, True)
        return carry

    lax.fori_loop(0, far_pairs, far_body, 0)
    lax.fori_loop(far_pairs, n_pairs, near_body, 0)
    o_ref[0, 0, 0] = acc_ref[:HEAD_DIM, :] / acc_ref[HEAD_DIM:HEAD_DIM + 1, :]


def _slc_attn(qt, far_t, k_aug, vt_aug, sel_t, tab_t):
    bsz, _, nq, _, cols = qt.shape
    s = k_aug.shape[2]
    nbp = sel_t.shape[3]
    ngroups = nbp // SLC_GROUP_BLOCKS
    nch = tab_t.shape[1]
    once = pl.Buffered(1)
    return pl.pallas_call(
        _slc_attn_kernel,
        grid=(bsz, N_KV, nq),
        in_specs=[pl.BlockSpec((1, 1, 1, LANES, cols), lambda b, k, i: (b, k, i, 0, 0)),
                  pl.BlockSpec((1, LANES, cols), lambda b, k, i: (k, 0, 0)),
                  pl.BlockSpec((1, 1, s, 2 * LANES), lambda b, k, i: (b, k, 0, 0), pipeline_mode=once),
                  pl.BlockSpec((1, 1, s // SLC_KEY_TILE, SLC_V_ROWS, SLC_KEY_TILE),
                               lambda b, k, i: (b, k, 0, 0, 0), pipeline_mode=once),
                  pl.BlockSpec((1, 1, 1, nbp, Q_BLOCK), lambda b, k, i: (b, k, i, 0, 0)),
                  pl.BlockSpec((1, nch, Q_BLOCK, cols), lambda b, k, i: (k, 0, 0, 0), pipeline_mode=once)],
        out_specs=pl.BlockSpec((1, 1, 1, HEAD_DIM, cols), lambda b, k, i: (b, k, i, 0, 0)),
        out_shape=jax.ShapeDtypeStruct((bsz, N_KV, nq, HEAD_DIM, cols), F32),
        scratch_shapes=[pltpu.VMEM((2 * ngroups, 2 * LANES, cols), BF16),
                        pltpu.VMEM((SLC_KEY_TILE, cols), F32),
                        pltpu.VMEM((SLC_KEY_TILE, cols), F32),
                        pltpu.VMEM((1, cols), F32),
                        pltpu.VMEM((1, cols), F32),
                        pltpu.VMEM((1, cols), F32),
                        pltpu.VMEM((SLC_V_ROWS, cols), F32)],
        compiler_params=_cparams(("parallel", "parallel", "arbitrary")),
        name="nsa_slc_attn",
    )(qt, far_t, k_aug, vt_aug, sel_t, tab_t)


def _win_attn_kernel(q_ref, k_ref, v_ref, tab_ref, o_ref):
    i = pl.program_id(2)
    rows = N_GRP * Q_BLOCK
    nw = WINDOW + Q_BLOCK
    q = q_ref[0, 0].reshape(rows, HEAD_DIM)
    qs = pl.multiple_of(i * Q_BLOCK, Q_BLOCK)
    k = k_ref[0, 0, pl.ds(qs, nw), :]
    v = v_ref[0, 0, pl.ds(qs, nw), :]
    s = _nt(q, k) + tab_ref[0].reshape(rows, nw)
    col = lax.broadcasted_iota(jnp.int32, (1, nw), 1)
    s = jnp.where(col >= WINDOW - i * Q_BLOCK, s, NEG_INF)
    m = jnp.max(s, axis=-1, keepdims=True)
    e = jnp.exp(s - m)
    p = e / jnp.sum(e, axis=-1, keepdims=True)
    o = jnp.dot(p.astype(BF16), v, preferred_element_type=F32)
    o_ref[0, 0] = o.reshape(N_GRP, Q_BLOCK, HEAD_DIM)


def _win_attn(q64, k_win, v_win, tab_win):
    bsz, _, _, s, _ = q64.shape
    nq = s // Q_BLOCK
    sp = k_win.shape[2]
    nw = WINDOW + Q_BLOCK
    return pl.pallas_call(
        _win_attn_kernel,
        grid=(bsz, N_KV, nq),
        in_specs=[pl.BlockSpec((1, 1, N_GRP, Q_BLOCK, HEAD_DIM), lambda b, k, i: (b, k, 0, i, 0)),
                  pl.BlockSpec((1, 1, sp, HEAD_DIM), lambda b, k, i: (b, k, 0, 0)),
                  pl.BlockSpec((1, 1, sp, HEAD_DIM), lambda b, k, i: (b, k, 0, 0)),
                  pl.BlockSpec((1, N_GRP, Q_BLOCK, nw), lambda b, k, i: (k, 0, 0, 0))],
        out_specs=pl.BlockSpec((1, 1, N_GRP, Q_BLOCK, HEAD_DIM), lambda b, k, i: (b, k, 0, i, 0)),
        out_shape=jax.ShapeDtypeStruct((bsz, N_KV, N_GRP, s, HEAD_DIM), F32),
        compiler_params=_cparams(("parallel", "parallel", "arbitrary")),
        name="nsa_win_attn",
    )(q64, k_win, v_win, tab_win)


def _ssd_kernel(xbc_ref, z_ref, dt_ref, cw_ref, cb_ref, dtb_ref, alog_ref, dskip_ref, nw_ref, o_ref,
                ext_ref, state_ref):
    ln = SSM_CHUNK
    d_ssm = z_ref.shape[2]
    nh = dtb_ref.shape[1]
    gw = d_ssm // SSM_GROUPS
    hpg = nh // SSM_GROUPS
    gn = SSM_GROUPS * SSM_STATE

    @pl.when(pl.program_id(1) == 0)
    def _():
        ext_ref[0:SUBLANES, :] = jnp.zeros((SUBLANES, ext_ref.shape[1]), F32)
        state_ref[...] = jnp.zeros(state_ref.shape, F32)

    ext_ref[SUBLANES:SUBLANES + ln, :] = xbc_ref[0]
    conv = cw_ref[0:1, :] * ext_ref[SUBLANES - CONV_WIDTH + 1:SUBLANES - CONV_WIDTH + 1 + ln, :]
    for k in range(1, CONV_WIDTH):
        lo = SUBLANES - CONV_WIDTH + 1 + k
        conv = conv + cw_ref[k:k + 1, :] * ext_ref[lo:lo + ln, :]
    conv = conv + cb_ref[...]
    ext_ref[0:SUBLANES, :] = xbc_ref[0, ln - SUBLANES:ln, :]
    xc = _silu(conv)
    xs = xc[:, :d_ssm]
    bm = xc[:, d_ssm:d_ssm + gn]
    cm = xc[:, d_ssm + gn:d_ssm + 2 * gn]

    xdt = dt_ref[0, :, :nh] + dtb_ref[...]
    dt = jnp.maximum(xdt, 0.0) + jnp.log1p(jnp.exp(-jnp.abs(xdt)))
    a = -jnp.exp(alog_ref[...])
    da = dt * a

    row = lax.broadcasted_iota(jnp.int32, (ln, ln), 0)
    colm = lax.broadcasted_iota(jnp.int32, (ln, ln), 1)
    causal = row >= colm
    acs = jnp.dot(causal.astype(F32), da, precision=HI, preferred_element_type=F32)
    eye = (lax.broadcasted_iota(jnp.int32, (2 * nh, 2 * nh), 0)
           == lax.broadcasted_iota(jnp.int32, (2 * nh, 2 * nh), 1)).astype(F32)
    rows_t = _nt(eye, jnp.concatenate([acs, dt], axis=1), precision=HI)
    expand = (lax.broadcasted_iota(jnp.int32, (nh, d_ssm), 0)
              == lax.broadcasted_iota(jnp.int32, (nh, d_ssm), 1) // SSM_HEAD_DIM).astype(F32)
    last = acs[ln - 1:ln, :]
    exp_acs_x = jnp.dot(jnp.exp(acs), expand, precision=HI, preferred_element_type=F32)
    w_x = jnp.dot(jnp.exp(last - acs) * dt, expand, precision=HI, preferred_element_type=F32)
    exp_last_x = exp_acs_x[ln - 1:ln, :]

    ys = []
    for g in range(SSM_GROUPS):
        cg = cm[:, g * SSM_STATE:(g + 1) * SSM_STATE].astype(BF16)
        bg32 = bm[:, g * SSM_STATE:(g + 1) * SSM_STATE]
        bg = bg32.astype(BF16)
        xg = xs[:, g * gw:(g + 1) * gw]
        cb = _nt(cg, bg)
        st = state_ref[g]
        y_state = jnp.dot(cg, st.astype(BF16), preferred_element_type=F32) * exp_acs_x[:, g * gw:(g + 1) * gw]
        y_heads = []
        for j in range(hpg):
            h = g * hpg + j
            seg = acs[:, h:h + 1] - rows_t[h:h + 1, :]
            decay = jnp.exp(jnp.where(causal, seg, -jnp.inf))
            mmat = cb * decay * rows_t[nh + h:nh + h + 1, :]
            xh = xg[:, j * SSM_HEAD_DIM:(j + 1) * SSM_HEAD_DIM].astype(BF16)
            y_heads.append(jnp.dot(mmat.astype(BF16), xh, preferred_element_type=F32))
        ys.append(jnp.concatenate(y_heads, axis=1) + y_state)
        xw = (xg * w_x[:, g * gw:(g + 1) * gw]).astype(BF16)
        state_ref[g] = st * exp_last_x[:, g * gw:(g + 1) * gw] + jnp.dot(
            bg32.T.astype(BF16), xw, preferred_element_type=F32)

    y = jnp.concatenate(ys, axis=1) + dskip_ref[...] * xs
    y = y * _silu(z_ref[0])
    outs = []
    for g in range(SSM_GROUPS):
        yg = y[:, g * gw:(g + 1) * gw]
        outs.append(yg * lax.rsqrt(jnp.mean(yg * yg, axis=-1, keepdims=True) + RMS_EPS))
    o_ref[0] = jnp.concatenate(outs, axis=1) * nw_ref[...]


def _ssd(proj, xbc_blk, z_blk, dt_blk, conv_w, conv_b, dt_bias, a_log, d_skip, norm_w):
    bsz, s, _ = proj.shape
    ch = xbc_blk[1]
    d_ssm = z_blk[1]
    nh = dt_bias.shape[0]
    gw = d_ssm // SSM_GROUPS
    nchunks = s // SSM_CHUNK
    full = lambda shape: pl.BlockSpec(shape, lambda b, c: (0,) * len(shape))
    cols = lambda blk: pl.BlockSpec((1, SSM_CHUNK, blk[1]), lambda b, c: (b, c, blk[0]))
    xbc = z = dt_raw = proj
    return pl.pallas_call(
        _ssd_kernel,
        grid=(bsz, nchunks),
        in_specs=[cols(xbc_blk), cols(z_blk), cols(dt_blk),
                  full((CONV_WIDTH, ch)), full((1, ch)), full((1, nh)), full((1, nh)),
                  full((1, d_ssm)), full((1, d_ssm))],
        out_specs=pl.BlockSpec((1, SSM_CHUNK, d_ssm), lambda b, c: (b, c, 0)),
        out_shape=jax.ShapeDtypeStruct((bsz, s, d_ssm), F32),
        scratch_shapes=[pltpu.VMEM((SUBLANES + SSM_CHUNK, ch), F32),
                        pltpu.VMEM((SSM_GROUPS, SSM_STATE, gw), F32)],
        compiler_params=_cparams(("parallel", "arbitrary")),
        name="ssd_scan",
    )(xbc, z, dt_raw, conv_w, conv_b.reshape(1, ch), dt_bias.reshape(1, nh), a_log.reshape(1, nh),
      jnp.repeat(d_skip, SSM_HEAD_DIM).reshape(1, d_ssm), norm_w.reshape(1, d_ssm))


def _mix_out_kernel(oc_ref, os_ref, ow_ref, gl_ref, ex_ref, an_ref, ssm_ref, w_ref, x_ref, g1_ref, o_ref):
    d_attn = oc_ref.shape[2]
    sig = jax.nn.sigmoid(gl_ref[0])
    pieces, rest = [], sig
    for _ in range(GATE_PIECES):
        piece = rest.astype(BF16)
        pieces.append(piece)
        rest = rest - piece.astype(F32)
    sig3 = jnp.concatenate(pieces, axis=1)
    gc = jnp.dot(sig3, ex_ref[0], preferred_element_type=F32)
    gs = jnp.dot(sig3, ex_ref[1], preferred_element_type=F32)
    gw = jnp.dot(sig3, ex_ref[2], preferred_element_type=F32)
    attn = gc * oc_ref[0] + gs * os_ref[0] + gw * ow_ref[0]
    attn = attn * lax.rsqrt(jnp.mean(attn * attn, axis=-1, keepdims=True) + RMS_EPS) * an_ref[...]
    mix = (jnp.dot(attn.astype(BF16), w_ref[:d_attn, :], preferred_element_type=F32)
           + jnp.dot(ssm_ref[0].astype(BF16), w_ref[d_attn:, :], preferred_element_type=F32))
    o_ref[0] = x_ref[0] + g1_ref[0] * mix


def _gate_expand():
    ex = np.zeros((3, LANES, N_HEADS * HEAD_DIM), np.float32)
    for r in range(3):
        for h in range(N_HEADS):
            ex[r, h * 3 + r, h * HEAD_DIM:(h + 1) * HEAD_DIM] = 1.0
    return np.concatenate([ex] * GATE_PIECES, axis=1)


def _mix_out(oc, os_, ow, gl, gl_blk, attn_norm, ssm, w_out_bf16, x, g1, tm=256):
    bsz, s, d = x.shape
    d_attn = oc.shape[2]
    d_ssm = ssm.shape[2]
    tok = lambda w: pl.BlockSpec((1, tm, w), lambda b, i: (b, i, 0))
    return pl.pallas_call(
        _mix_out_kernel,
        grid=(bsz, s // tm),
        in_specs=[tok(d_attn), tok(d_attn), tok(d_attn),
                  pl.BlockSpec((1, tm, gl_blk[1]), lambda b, i: (b, i, gl_blk[0])),
                  pl.BlockSpec((3, GATE_PIECES * LANES, d_attn), lambda b, i: (0, 0, 0)),
                  pl.BlockSpec((1, d_attn), lambda b, i: (0, 0)),
                  tok(d_ssm),
                  pl.BlockSpec((d_attn + d_ssm, d), lambda b, i: (0, 0)),
                  tok(d),
                  pl.BlockSpec((1, 1, d), lambda b, i: (b, 0, 0))],
        out_specs=tok(d),
        out_shape=jax.ShapeDtypeStruct((bsz, s, d), F32),
        compiler_params=_cparams(("parallel", "parallel")),
        name="mix_out",
    )(oc, os_, ow, gl, jnp.asarray(_gate_expand(), BF16), attn_norm.reshape(1, d_attn), ssm, w_out_bf16, x, g1)


def _peer_candidates():
    k, sub = PEER_TOPK, SUBLANES
    cells = [(0, b) for b in range(k)]
    cells += [(a, b) for a in range(1, sub) for b in range(sub)]
    cells += [(a, 0) for a in range(sub, k)]
    order = np.array([a * k + b if (a + 1) * (b + 1) <= k else -1 for a, b in cells], np.float32)
    group = np.zeros((k, LANES), np.float32)
    for row, (a, _) in enumerate(cells):
        group[a, row] = 1.0
    return cells, order, group


def _peer_route_kernel(q_ref, sk_ref, order_ref, group_ref, rank2_ref, e2_ref, n1_ref, c1_ref):
    q = q_ref[...]
    half = q.shape[1] // 2
    s1 = _nt(sk_ref[0], q[:, :half], precision=HI)
    s2 = _nt(sk_ref[1], q[:, half:], precision=HI)
    k = PEER_TOPK
    rank1, v1 = _topk_mark(s1, k, axis=0)
    rank2, v2 = _topk_mark(s2, k, axis=0)
    v1_all = jnp.concatenate(v1, axis=0)
    v2_all = jnp.concatenate(v2, axis=0)
    cand = jnp.concatenate([v1[0] + v2_all] + [v1[a] + v2_all[:SUBLANES] for a in range(1, SUBLANES)]
                           + [v1_all[SUBLANES:] + v2[0]], axis=0)
    order = order_ref[...]
    cand = jnp.where(order >= 0.0, cand, -jnp.inf)
    rank_c, best = _topk_mark(cand, k, axis=0, order=order, order_bound=float(k * k))
    chosen = jnp.where(rank_c < float(k), 1.0, 0.0).astype(BF16)
    chosen = jnp.concatenate([chosen, jnp.zeros((LANES - chosen.shape[0], chosen.shape[1]), BF16)], axis=0)
    count = jnp.dot(group_ref[...], chosen, preferred_element_type=F32)
    z = best[0] * 0.0
    for r in range(k):
        z = z + jnp.exp(best[r] - best[0])
    n1 = jnp.zeros(s1.shape, F32)
    for a in range(k):
        n1 = jnp.where(rank1 == float(a), count[a:a + 1, :], n1)
    rank2_ref[0] = rank2.astype(BF16)
    e2_ref[0] = jnp.exp(s2 - v2[0]).astype(BF16)
    n1_ref[0] = n1
    c1_ref[0] = jnp.exp(s1 - v1[0]) / z


def _peer_route(q, subkeys, tm=256):
    t, width = q.shape
    kd = width // PEER_HEADS
    nk = subkeys.shape[1]
    spec = pl.BlockSpec((1, nk, tm), lambda i, h: (h, 0, i))
    shape = jax.ShapeDtypeStruct((PEER_HEADS, nk, t), F32)
    cells, order, group = _peer_candidates()
    order = jnp.asarray(np.broadcast_to(order[:, None], (len(cells), tm)))
    return pl.pallas_call(
        _peer_route_kernel,
        grid=(t // tm, PEER_HEADS),
        in_specs=[pl.BlockSpec((tm, kd), lambda i, h: (i, h)),
                  pl.BlockSpec((2, nk, kd // 2), lambda i, h: (0, 0, 0)),
                  pl.BlockSpec((len(cells), tm), lambda i, h: (0, 0)),
                  pl.BlockSpec((PEER_TOPK, LANES), lambda i, h: (0, 0))],
        out_specs=[spec, spec, spec, spec],
        out_shape=[jax.ShapeDtypeStruct(shape.shape, BF16), jax.ShapeDtypeStruct(shape.shape, BF16), shape, shape],
        compiler_params=_cparams(("parallel", "parallel")),
        name="peer_route",
    )(q, subkeys, order, jnp.asarray(group, BF16))


def _peer_dense_kernel(ht_ref, u_ref, vt_ref, rank2_ref, e2_ref, n1_ref, c1_ref, o_ref, *scratch):
    j = pl.program_id(1)
    te = u_ref.shape[0]
    nk = rank2_ref.shape[1]

    @pl.when(j == 0)
    def _():
        o_ref[...] = jnp.zeros(o_ref.shape, F32)

    tm = ht_ref.shape[1]
    strips = [slice(c * PEER_LANE_STRIP, (c + 1) * PEER_LANE_STRIP) for c in range(tm // PEER_LANE_STRIP)]

    units = [(e, c) for e in range(te // PEER_EXPERT_UNIT) for c in range(len(strips))]
    act_refs = scratch[:len(units)]
    aw_refs = scratch[len(units):]
    n_piece = PEER_EXPERT_UNIT // PEER_EXPERT_PIECE
    d_piece = o_ref.shape[0] // n_piece

    def produce(k, p):
        e, c = units[k]
        lo = e * PEER_EXPERT_UNIT + p * PEER_EXPERT_PIECE
        act_refs[k][p * PEER_EXPERT_PIECE:(p + 1) * PEER_EXPERT_PIECE, :] = jax.nn.gelu(
            jnp.dot(u_ref[lo:lo + PEER_EXPERT_PIECE, :], ht_ref[:, strips[c]], preferred_element_type=F32)
        ).astype(BF16)

    def gate(k, r):
        e, c = units[k]
        ls = strips[c]
        i1 = (j * te + e * PEER_EXPERT_UNIT) // nk + r
        w = jnp.zeros((nk, PEER_LANE_STRIP), BF16)
        zero = jnp.zeros((nk, PEER_LANE_STRIP), BF16)
        for h in range(PEER_HEADS):
            n_row = n1_ref[h, pl.ds(i1, 1), ls].astype(BF16)
            c_row = c1_ref[h, pl.ds(i1, 1), ls].astype(BF16)
            w = w + jnp.where(rank2_ref[h, :, ls] < n_row, e2_ref[h, :, ls], zero) * c_row
        aw_refs[k][r * nk:(r + 1) * nk, :] = act_refs[k][r * nk:(r + 1) * nk, :] * w

    def combine(k, m):
        e, c = units[k]
        ds_ = slice(m * d_piece, (m + 1) * d_piece)
        es = slice(e * PEER_EXPERT_UNIT, (e + 1) * PEER_EXPERT_UNIT)
        o_ref[ds_, strips[c]] += jnp.dot(vt_ref[ds_, es], aw_refs[k][...], preferred_element_type=F32)

    gates_per_piece = PEER_EXPERT_PIECE // nk
    for p in range(n_piece):
        produce(0, p)
    for k in range(len(units)):
        for p in range(n_piece):
            if k + 1 < len(units):
                produce(k + 1, p)
            for r in range(p * gates_per_piece, (p + 1) * gates_per_piece):
                gate(k, r)
                if k > 0 and r == p * gates_per_piece:
                    combine(k - 1, p)
    for m in range(n_piece):
        combine(len(units) - 1, m)


def _peer_dense(h_t, u_bf16, v_t_bf16, rank2, e2, n1, c1, tm=512, te=512):
    d, t = h_t.shape
    n_exp = u_bf16.shape[0]
    nk = rank2.shape[1]
    n_units = (te // PEER_EXPERT_UNIT) * (tm // PEER_LANE_STRIP)
    route = pl.BlockSpec((PEER_HEADS, nk, tm), lambda i, j: (0, 0, i))
    return pl.pallas_call(
        _peer_dense_kernel,
        grid=(t // tm, n_exp // te),
        in_specs=[pl.BlockSpec((d, tm), lambda i, j: (0, i)),
                  pl.BlockSpec((te, d), lambda i, j: (j, 0)),
                  pl.BlockSpec((d, te), lambda i, j: (0, j)),
                  route, route, route, route],
        out_specs=pl.BlockSpec((d, tm), lambda i, j: (0, i)),
        out_shape=jax.ShapeDtypeStruct((d, t), F32),
        scratch_shapes=([pltpu.VMEM((PEER_EXPERT_UNIT, PEER_LANE_STRIP), BF16)] * n_units
                        + [pltpu.VMEM((PEER_EXPERT_UNIT, PEER_LANE_STRIP), BF16)] * n_units),
        compiler_params=_cparams(("parallel", "arbitrary")),
        name="peer_dense",
    )(h_t, u_bf16, v_t_bf16, rank2, e2, n1, c1)


def _residual_kernel(x_ref, yt_ref, g_ref, o_ref):
    o_ref[0] = x_ref[0] + g_ref[0] * yt_ref[...].T


def _residual(x, y_t, g, tm=512):
    bsz, s, d = x.shape
    tok = pl.BlockSpec((1, tm, d), lambda b, i: (b, i, 0))
    return pl.pallas_call(
        _residual_kernel,
        grid=(bsz, s // tm),
        in_specs=[tok, pl.BlockSpec((d, tm), lambda b, i: (0, b * (s // tm) + i)),
                  pl.BlockSpec((1, 1, d), lambda b, i: (b, 0, 0))],
        out_specs=tok,
        out_shape=jax.ShapeDtypeStruct((bsz, s, d), F32),
        compiler_params=_cparams(("parallel", "parallel")),
        name="residual",
    )(x, y_t, g)


def _final_norm_kernel(x_ref, w_ref, o_ref):
    x = x_ref[0]
    o_ref[0] = x * lax.rsqrt(jnp.mean(x * x, axis=-1, keepdims=True) + RMS_EPS) * w_ref[...]


def _final_norm(x, w, tm=512):
    bsz, s, d = x.shape
    tok = pl.BlockSpec((1, tm, d), lambda b, i: (b, i, 0))
    return pl.pallas_call(
        _final_norm_kernel,
        grid=(bsz, s // tm),
        in_specs=[tok, pl.BlockSpec((1, d), lambda b, i: (0, 0))],
        out_specs=tok,
        out_shape=jax.ShapeDtypeStruct((bsz, s, d), F32),
        compiler_params=_cparams(("parallel", "parallel")),
        name="final_norm",
    )(x, w.reshape(1, d))


def _pad_cols(w, width):
    return jnp.pad(w, ((0, 0), (0, width - w.shape[1])))


def _nsa(q, kv, tables, cmp_pe, cmp_w1, cmp_w2):
    tab_win, tab_cmp, tab_slc, far_col, far_t = tables
    bsz, s, _ = q.shape
    nc = s // CMP_STRIDE
    nq = s // Q_BLOCK
    n_slc = s // SLC_BLOCK
    nbp = -(-n_slc // SLC_GROUP_BLOCKS) * SLC_GROUP_BLOCKS
    scale = HEAD_DIM ** -0.5
    q6 = (q * scale).astype(BF16).reshape(bsz, nq, Q_BLOCK, N_KV, N_GRP, HEAD_DIM)
    qh = q6.transpose(0, 3, 4, 1, 2, 5).reshape(bsz, N_KV, N_GRP, s, HEAD_DIM)
    qt = q6.transpose(0, 3, 1, 5, 4, 2).reshape(bsz, N_KV, nq, HEAD_DIM, N_GRP * Q_BLOCK)
    qt = jnp.pad(qt, ((0, 0),) * 3 + ((0, LANES - HEAD_DIM), (0, 0)))
    kv6 = kv.reshape(bsz, s, 6, N_KV, HEAD_DIM)

    kv_cmp = kv6[:, :, 0:2].reshape(bsz, nc, CMP_STRIDE, 2, N_KV, HEAD_DIM)
    kv_cmp = kv_cmp.transpose(0, 3, 4, 1, 2, 5).reshape(bsz, 2, N_KV, nc, CMP_STRIDE * HEAD_DIM)
    kv_c = _compress(kv_cmp, cmp_pe, cmp_w1, cmp_w2)
    o_c, sel_t = _cmp_attn(qh, kv_c, tab_cmp, far_col, nbp)

    k_slc = kv6[:, :, 2].transpose(0, 2, 1, 3).astype(BF16)
    nkt = s // SLC_KEY_TILE
    vt = kv6[:, :, 3].astype(BF16).reshape(bsz, nkt, SLC_KEY_TILE, N_KV, HEAD_DIM).transpose(0, 3, 1, 4, 2)
    vt_aug = jnp.concatenate([
        vt, jnp.ones((bsz, N_KV, nkt, 1, SLC_KEY_TILE), BF16),
        jnp.zeros((bsz, N_KV, nkt, SLC_V_ROWS - HEAD_DIM - 1, SLC_KEY_TILE), BF16)], axis=3)
    blk = np.arange(s) // SLC_BLOCK
    onehot = (blk[:, None] % SLC_GROUP_BLOCKS == np.arange(SLC_GROUP_BLOCKS)[None, :]).astype(np.float32)
    k_aug = jnp.concatenate([
        k_slc, jnp.ones((bsz, N_KV, s, SLC_BIAS_PIECES), BF16),
        jnp.zeros((bsz, N_KV, s, LANES - HEAD_DIM - SLC_BIAS_PIECES), BF16),
        jnp.broadcast_to(jnp.asarray(onehot, BF16), (bsz, N_KV, s, SLC_GROUP_BLOCKS))], axis=-1)
    o_st = _slc_attn(qt, far_t, k_aug, vt_aug, sel_t, tab_slc)
    o_s = o_st.reshape(bsz, N_KV, nq, HEAD_DIM, N_GRP, Q_BLOCK).transpose(0, 2, 5, 1, 4, 3)
    o_s = o_s.reshape(bsz, s, N_HEADS * HEAD_DIM)

    front = ((0, 0), (0, 0), (WINDOW, 0), (0, 0))
    k_win = jnp.pad(kv6[:, :, 4].transpose(0, 2, 1, 3).astype(BF16), front)
    v_win = jnp.pad(kv6[:, :, 5].transpose(0, 2, 1, 3).astype(BF16), front)
    o_w = _win_attn(qh, k_win, v_win, tab_win)

    back = lambda o: o.transpose(0, 3, 1, 2, 4).reshape(bsz, s, N_HEADS * HEAD_DIM)
    return back(o_c), o_s, back(o_w)


def kernel(x, c, ada_w, ada_b, norm_mix, norm_ffn, w_in, cmp_pe, cmp_w1, cmp_w2, rel_bias, attn_out_norm,
           conv_w, conv_b, dt_bias, a_log, d_skip, ssm_norm, w_out, peer_wq, peer_subkeys, peer_u, peer_v,
           norm_final):
    bsz, s, d = x.shape
    depth = ada_w.shape[0]
    d_attn = N_HEADS * HEAD_DIM
    n_kv = 6 * N_KV * HEAD_DIM
    n_gate = 3 * N_HEADS
    d_ssm = ssm_norm.shape[1]
    ch = conv_w.shape[2]
    nh = dt_bias.shape[1]

    mod = _ada_mod(c, ada_w, ada_b)
    tables = _bias_tables(rel_bias)

    for l in range(depth):
        sh1, sc1, g1, sh2, sc2, g2 = [mod[l, :, i * d:(i + 1) * d].reshape(bsz, 1, d) for i in range(6)]
        cuts = np.cumsum([0, d_attn, n_kv, n_gate, d_ssm, ch, nh])
        seg = dict(zip(("q", "kv", "gl", "z", "xbc", "dt"),
                       [w_in[l][:, cuts[i]:cuts[i + 1]] for i in range(6)]))
        seg["gl"] = _pad_cols(seg["gl"], LANES)
        seg["dt"] = _pad_cols(seg["dt"], LANES)
        order = ("q", "z", "kv", "gl", "dt", "xbc")
        w_cat = jnp.concatenate([seg[name] for name in order], axis=1).astype(BF16)
        start = dict(zip(order, np.cumsum([0] + [seg[name].shape[1] for name in order])[:-1]))
        blk = {name: (int(start[name]) // seg[name].shape[1], seg[name].shape[1]) for name in order}
        assert all(start[name] % seg[name].shape[1] == 0 for name in ("z", "gl", "dt", "xbc"))
        proj, _ = _norm_proj(x, norm_mix[l], sc1, sh1, w_cat)
        q = proj[:, :, start["q"]:start["q"] + d_attn]
        kv = proj[:, :, start["kv"]:start["kv"] + n_kv]

        o_c, o_s, o_w = _nsa(q, kv, tables, cmp_pe[l], cmp_w1[l], cmp_w2[l])
        ssm = _ssd(proj, blk["xbc"], blk["z"], blk["dt"], conv_w[l], conv_b[l], dt_bias[l], a_log[l], d_skip[l],
                   ssm_norm[l])
        x = _mix_out(o_c, o_s, o_w, proj, blk["gl"], attn_out_norm[l], ssm, w_out[l].astype(BF16), x, g1)

        pq, h2 = _norm_proj(x, norm_ffn[l], sc2, sh2, peer_wq[l].astype(BF16))
        rank2, e2, n1, c1 = _peer_route(pq.reshape(bsz * s, -1), peer_subkeys[l])
        h_t = h2.reshape(bsz * s, d).T
        ffn_t = _peer_dense(h_t, peer_u[l].astype(BF16), peer_v[l].T.astype(BF16), rank2, e2, n1, c1)
        x = _residual(x, ffn_t, g2)

    return _final_norm(x, norm_final)
```

```python
import functools
import math

import numpy as np
import jax
import jax.numpy as jnp
from jax import lax
from jax.experimental import pallas as pl
from jax.experimental.pallas import tpu as pltpu

F32 = jnp.float32
BF16 = jnp.bfloat16
HI = lax.Precision.HIGHEST

N_HEADS = 16
N_KV = 2
N_GRP = N_HEADS // N_KV
HEAD_DIM = 64
CMP_BLOCK = 32
CMP_STRIDE = 16
CMP_HIDDEN = 4 * HEAD_DIM
SLC_BLOCK = 64
SLC_TOPN = 16
WINDOW = 512
Q_BLOCK = 128
FORCE_SCORE = 1e4
NEG_INF = -1e30
REL_BUCKETS = 32
REL_MAX_DIST = 2048
SSM_HEAD_DIM = 64
SSM_GROUPS = 2
SSM_STATE = 128
CONV_WIDTH = 4
SSM_CHUNK = 256
PEER_HEADS = 8
PEER_NKEYS = 128
PEER_TOPK = 16
RMS_EPS = 1e-6

LANES = 128
SUBLANES = 8
VMEM_LIMIT = 56 * 1024 * 1024

SLC_KEY_TILE = 512
SLC_TILE_BLOCKS = SLC_KEY_TILE // SLC_BLOCK
SLC_GROUP_BLOCKS = LANES
SLC_NEAR_CHUNKS = 14
SLC_TAB_MASKED = 14
SLC_TAB_CONST = 15
MASK_BIG = 2.0 ** 100
SLC_LANE_STRIP = 256
PEER_LANE_STRIP = 256
PEER_EXPERT_PIECE = 256
PEER_EXPERT_UNIT = 512
SLC_V_ROWS = HEAD_DIM + 16
SLC_BIAS_PIECES = 3
GATE_PIECES = 3

CMP_FAR_STEP = 256
CMP_NEAR = 128
CMP_PAD = CMP_NEAR - Q_BLOCK // CMP_STRIDE


def _cparams(sem, vmem=VMEM_LIMIT):
    return pltpu.CompilerParams(dimension_semantics=sem, vmem_limit_bytes=vmem)


def _nt(a, b, precision=None):
    return lax.dot_general(a, b, (((1,), (1,)), ((), ())), precision=precision,
                           preferred_element_type=F32)


def _silu(x):
    return x * jax.nn.sigmoid(x)


def _rel_bucket_np(d):
    d = np.maximum(np.asarray(d, np.int64), 0)
    max_exact = REL_BUCKETS // 2
    ratio = np.log(np.maximum(d, max_exact).astype(np.float64) / max_exact) / math.log(REL_MAX_DIST / max_exact)
    scaled = ratio * (REL_BUCKETS - max_exact)
    large = max_exact + np.floor(scaled).astype(np.int64)
    return np.where(d < max_exact, d, np.minimum(large, REL_BUCKETS - 1)).astype(np.int32)


def _ada_kernel(c_ref, w_ref, b_ref, o_ref):
    cond = _silu(c_ref[...])
    o_ref[0] = jnp.dot(cond, w_ref[0], precision=HI, preferred_element_type=F32) + b_ref[0]


def _ada_mod(c, ada_w, ada_b):
    depth, d, n = ada_w.shape
    bsz = c.shape[0]
    rows = SUBLANES
    c_pad = jnp.zeros((rows, d), F32).at[:bsz].set(c)
    tn = 1024
    out = pl.pallas_call(
        _ada_kernel,
        grid=(depth, n // tn),
        in_specs=[pl.BlockSpec((rows, d), lambda l, j: (0, 0)),
                  pl.BlockSpec((1, d, tn), lambda l, j: (l, 0, j)),
                  pl.BlockSpec((1, 1, tn), lambda l, j: (l, 0, j))],
        out_specs=pl.BlockSpec((1, rows, tn), lambda l, j: (l, 0, j)),
        out_shape=jax.ShapeDtypeStruct((depth, rows, n), F32),
        compiler_params=_cparams(("parallel", "parallel")),
        name="ada_mod",
    )(c_pad, ada_w, ada_b.reshape(depth, 1, n))
    return out[:, :bsz]


def _norm_proj_kernel(x_ref, nw_ref, sc_ref, sh_ref, w_ref, o_ref, h_ref, hs_ref):
    @pl.when(pl.program_id(2) == 0)
    def _():
        x = x_ref[0]
        y = x * lax.rsqrt(jnp.mean(x * x, axis=-1, keepdims=True) + RMS_EPS)
        h = (y * nw_ref[...]) * (1.0 + sc_ref[0]) + sh_ref[0]
        hs_ref[...] = h.astype(BF16)
        h_ref[0] = h.astype(BF16)

    o_ref[0] = jnp.dot(hs_ref[...], w_ref[...], preferred_element_type=F32)


def _norm_proj(x, nw, sc, sh, w_bf16, tm=1024, tn=512):
    bsz, s, d = x.shape
    n = w_bf16.shape[1]
    return pl.pallas_call(
        _norm_proj_kernel,
        grid=(bsz, s // tm, n // tn),
        in_specs=[pl.BlockSpec((1, tm, d), lambda b, i, j: (b, i, 0)),
                  pl.BlockSpec((1, d), lambda b, i, j: (0, 0)),
                  pl.BlockSpec((1, 1, d), lambda b, i, j: (b, 0, 0)),
                  pl.BlockSpec((1, 1, d), lambda b, i, j: (b, 0, 0)),
                  pl.BlockSpec((d, tn), lambda b, i, j: (0, j))],
        out_specs=[pl.BlockSpec((1, tm, tn), lambda b, i, j: (b, i, j)),
                   pl.BlockSpec((1, tm, d), lambda b, i, j: (b, i, 0))],
        out_shape=[jax.ShapeDtypeStruct((bsz, s, n), F32),
                   jax.ShapeDtypeStruct((bsz, s, d), BF16)],
        scratch_shapes=[pltpu.VMEM((tm, d), BF16)],
        compiler_params=_cparams(("parallel", "parallel", "arbitrary")),
        name="norm_proj",
    )(x, nw.reshape(1, d), sc, sh, w_bf16)


def _bias_kernel(rel_ref, bk_ref, o_ref):
    h = pl.program_id(0)
    bk = bk_ref[...]
    acc = jnp.full(bk.shape, NEG_INF, F32)
    for b in range(REL_BUCKETS):
        acc = jnp.where(bk == b, rel_ref[b, h], acc)
    o_ref[0] = acc


def _bias_tables(rel_bias):
    r = np.arange(Q_BLOCK)[:, None]
    dw = r - np.arange(WINDOW + Q_BLOCK)[None, :] + WINDOW
    win = np.where((dw >= 0) & (dw < WINDOW), _rel_bucket_np(dw), -1)
    off = CMP_STRIDE * CMP_PAD - (CMP_BLOCK - 1)
    dc = r + off - CMP_STRIDE * np.arange(CMP_NEAR)[None, :]
    cmp_near = np.where(dc >= 0, _rel_bucket_np(dc), -1)
    chunks = []
    for m in range(SLC_NEAR_CHUNKS):
        ds_ = Q_BLOCK * m + r - np.arange(Q_BLOCK)[None, :]
        chunks.append(np.where(ds_ >= 0, _rel_bucket_np(ds_), -1))
    assert _rel_bucket_np(Q_BLOCK * SLC_NEAR_CHUNKS - (Q_BLOCK - 1)) == REL_BUCKETS - 1
    assert _rel_bucket_np(off + CMP_STRIDE) == REL_BUCKETS - 1
    chunks.append(np.full((Q_BLOCK, Q_BLOCK), -1))
    chunks.append(np.full((Q_BLOCK, Q_BLOCK), REL_BUCKETS - 1))
    bk = np.concatenate([win, cmp_near] + chunks, axis=1).astype(np.int32)
    cols = bk.shape[1]
    out = pl.pallas_call(
        _bias_kernel,
        grid=(N_HEADS,),
        in_specs=[pl.BlockSpec(memory_space=pltpu.SMEM),
                  pl.BlockSpec((Q_BLOCK, cols), lambda h: (0, 0))],
        out_specs=pl.BlockSpec((1, Q_BLOCK, cols), lambda h: (h, 0, 0)),
        out_shape=jax.ShapeDtypeStruct((N_HEADS, Q_BLOCK, cols), F32),
        compiler_params=_cparams(("arbitrary",)),
        name="bias_tables",
    )(rel_bias, jnp.asarray(bk))
    nw = WINDOW + Q_BLOCK
    tab_win = out[:, :, :nw].reshape(N_KV, N_GRP, Q_BLOCK, nw)
    tab_cmp = out[:, :, nw:nw + CMP_NEAR].reshape(N_KV, N_GRP, Q_BLOCK, CMP_NEAR)
    nch = SLC_NEAR_CHUNKS + 2
    tab_slc = out[:, :, nw + CMP_NEAR:].reshape(N_KV, N_GRP, Q_BLOCK, nch, Q_BLOCK)
    tab_slc = tab_slc.transpose(0, 3, 4, 1, 2).reshape(N_KV, nch, Q_BLOCK, N_GRP * Q_BLOCK)
    far = rel_bias[REL_BUCKETS - 1].reshape(N_KV, N_GRP, 1)
    far_col = jnp.broadcast_to(far, (N_KV, N_GRP, Q_BLOCK)).reshape(N_KV, N_GRP * Q_BLOCK, 1)
    far_row = far_col.reshape(N_KV, 1, N_GRP * Q_BLOCK)
    pieces, rest = [], far_row
    for _ in range(SLC_BIAS_PIECES):
        piece = rest.astype(BF16)
        pieces.append(piece)
        rest = rest - piece.astype(F32)
    zeros = lambda w: jnp.zeros((N_KV, w, N_GRP * Q_BLOCK), BF16)
    far_t = jnp.concatenate([zeros(HEAD_DIM)] + pieces + [zeros(LANES - HEAD_DIM - SLC_BIAS_PIECES)], axis=1)
    return tab_win, tab_cmp, tab_slc, far_col, far_t


def _compress_kernel(a_ref, pe_ref, w1_ref, w2_ref, o_ref):
    a = a_ref[0, 0, 0]
    half = a.shape[1]
    lo = jnp.dot((a + pe_ref[0, 0:1, :]).astype(BF16), w1_ref[0, :half, :].astype(BF16),
                 preferred_element_type=F32)
    hi = jnp.dot((a + pe_ref[0, 1:2, :]).astype(BF16), w1_ref[0, half:, :].astype(BF16),
                 preferred_element_type=F32)
    nc = a.shape[0]
    hid = jax.nn.gelu(lo + pltpu.roll(hi, nc - 1, axis=0))
    out = jnp.dot(hid.astype(BF16), w2_ref[0].astype(BF16), preferred_element_type=F32)
    o_ref[0, 0, 0] = jnp.zeros(o_ref.shape[3:], F32)
    o_ref[0, 0, 0, CMP_PAD:CMP_PAD + nc, :] = out


def _compress(kv_cmp, cmp_pe, cmp_w1, cmp_w2):
    bsz, _, _, nc, half = kv_cmp.shape
    pe = cmp_pe.reshape(2, 2, half)
    rows = CMP_PAD + nc + SUBLANES
    return pl.pallas_call(
        _compress_kernel,
        grid=(bsz, 2, N_KV),
        in_specs=[pl.BlockSpec((1, 1, 1, nc, half), lambda b, w, k: (b, w, k, 0, 0)),
                  pl.BlockSpec((1, 2, half), lambda b, w, k: (w, 0, 0)),
                  pl.BlockSpec((1, 2 * half, CMP_HIDDEN), lambda b, w, k: (w, 0, 0)),
                  pl.BlockSpec((1, CMP_HIDDEN, HEAD_DIM), lambda b, w, k: (w, 0, 0))],
        out_specs=pl.BlockSpec((1, 1, 1, rows, HEAD_DIM), lambda b, w, k: (b, w, k, 0, 0)),
        out_shape=jax.ShapeDtypeStruct((bsz, 2, N_KV, rows, HEAD_DIM), F32),
        compiler_params=_cparams(("parallel", "parallel", "parallel")),
        name="nsa_compress",
    )(kv_cmp, pe, cmp_w1, cmp_w2)


def _topk_mark(vals, k, axis, order=None, order_bound=None, assume_distinct=False):
    n = vals.shape[axis] if order is None else order_bound
    if not assume_distinct:
        iota = lax.broadcasted_iota(jnp.int32, vals.shape, axis).astype(F32) if order is None else order
    rank = jnp.full(vals.shape, float(k), F32)
    work = vals
    picked = []
    for r in range(k):
        m = jnp.max(work, axis=axis, keepdims=True)
        if assume_distinct:
            hit = work == m
        else:
            ix = jnp.min(jnp.where(work == m, iota, float(n)), axis=axis, keepdims=True)
            hit = iota == ix
        rank = jnp.where(hit, float(r), rank)
        work = jnp.where(hit, -jnp.inf, work)
        picked.append(m)
    return rank, picked


def _topk_clean(rank, k, axis):
    marks = jnp.sum(jnp.where(rank < float(k), 1.0, 0.0), axis=axis, keepdims=True)
    return jnp.max(jnp.abs(marks - float(k)))


def _cmp_attn_kernel(q_ref, k_ref, v_ref, tab_ref, far_ref, m_ref, o_ref, sel_ref, imp_ref):
    i = pl.program_id(2)
    rows = N_GRP * Q_BLOCK
    nc = k_ref.shape[3] - CMP_PAD - SUBLANES
    nbp = sel_ref.shape[3]
    start = pl.multiple_of(i * (Q_BLOCK // CMP_STRIDE), SUBLANES)
    n_far = i * (Q_BLOCK // CMP_STRIDE) - CMP_PAD

    def attend(wf):
        q = q_ref[0, 0].reshape(rows, HEAD_DIM)
        k_far = k_ref[0, 0, 0, CMP_PAD:CMP_PAD + wf, :].astype(BF16)
        v_far = v_ref[0, 0, 0, CMP_PAD:CMP_PAD + wf, :].astype(BF16)
        k_near = k_ref[0, 0, 0, pl.ds(start, CMP_NEAR), :].astype(BF16)
        v_near = v_ref[0, 0, 0, pl.ds(start, CMP_NEAR), :].astype(BF16)

        n_idx = lax.broadcasted_iota(jnp.int32, (1, wf), 1)
        mask_far = n_idx < n_far
        s_far = jnp.where(mask_far, _nt(q, k_far) + far_ref[0], NEG_INF)
        tab = tab_ref[0].reshape(rows, CMP_NEAR)
        c_idx = lax.broadcasted_iota(jnp.int32, (1, CMP_NEAR), 1)
        mask_near = (c_idx >= -n_far) & (tab > 0.5 * NEG_INF)
        s_near = jnp.where(mask_near, _nt(q, k_near) + tab, NEG_INF)

        m = jnp.maximum(jnp.max(s_far, axis=-1, keepdims=True), jnp.max(s_near, axis=-1, keepdims=True))
        e_far = jnp.exp(s_far - m)
        e_near = jnp.exp(s_near - m)
        l = jnp.sum(e_far, axis=-1, keepdims=True) + jnp.sum(e_near, axis=-1, keepdims=True)
        p_far = jnp.where(mask_far, e_far / l, 0.0)
        p_near = jnp.where(mask_near, e_near / l, 0.0)
        o = (jnp.dot(p_far.astype(BF16), v_far, preferred_element_type=F32)
             + jnp.dot(p_near.astype(BF16), v_near, preferred_element_type=F32))
        o_ref[0, 0] = o.reshape(N_GRP, Q_BLOCK, HEAD_DIM)

        ps_far = jnp.sum(p_far.reshape(N_GRP, Q_BLOCK, wf), axis=0)
        ps_near = jnp.sum(p_near.reshape(N_GRP, Q_BLOCK, CMP_NEAR), axis=0)
        imp_ref[...] = (
            jnp.dot(ps_far, m_ref[CMP_PAD:CMP_PAD + wf, :], precision=HI, preferred_element_type=F32)
            + jnp.dot(ps_near, m_ref[pl.ds(start, CMP_NEAR), :], precision=HI, preferred_element_type=F32))

    step = min(CMP_FAR_STEP, nc)
    widths = list(range(step, nc + 1, step))
    for b, wf in enumerate(widths):
        lower = n_far > widths[b - 1] if b > 0 else True
        upper = n_far <= wf if b + 1 < len(widths) else True
        pl.when(jnp.logical_and(lower, upper))(functools.partial(attend, wf))

    imp = imp_ref[...].T
    t = i * Q_BLOCK + lax.broadcasted_iota(jnp.int32, (1, Q_BLOCK), 1)
    cur = t // SLC_BLOCK
    blk = lax.broadcasted_iota(jnp.int32, (nbp, 1), 0)
    forced = (blk == 0) | (blk == cur) | (blk == cur - 1)
    imp = jnp.where(forced, FORCE_SCORE, jnp.where(blk <= cur, imp, -FORCE_SCORE))
    n_blocks = (nc * CMP_STRIDE) // SLC_BLOCK
    imp = jnp.where(blk < n_blocks, imp, -jnp.inf)
    rank, _ = _topk_mark(imp, min(SLC_TOPN, n_blocks), axis=0)
    sel_ref[0, 0, 0] = jnp.where(rank < float(SLC_TOPN), 1.0, 0.0).astype(BF16)


def _overlap_matrix(nc, nbp):
    n_cmp = nc - 1
    n_slc = nc * CMP_STRIDE // SLC_BLOCK
    j = np.arange(n_slc)
    lo = np.clip((j * SLC_BLOCK - CMP_BLOCK) // CMP_STRIDE + 1, 0, n_cmp)
    hi = np.clip(-((-(j * SLC_BLOCK + SLC_BLOCK)) // CMP_STRIDE), 0, n_cmp)
    m = np.zeros((CMP_PAD + nc + SUBLANES, nbp), np.float32)
    n = np.arange(nc)[:, None]
    m[CMP_PAD:CMP_PAD + nc, :n_slc] = (n >= lo[None, :]) & (n < hi[None, :])
    return m


def _cmp_attn(q64, kv_c, tab_cmp, far_col, nbp):
    bsz, _, _, s, _ = q64.shape
    rows_c = kv_c.shape[3]
    nc = rows_c - CMP_PAD - SUBLANES
    nq = s // Q_BLOCK
    m_pad = jnp.asarray(_overlap_matrix(nc, nbp))
    return pl.pallas_call(
        _cmp_attn_kernel,
        grid=(bsz, N_KV, nq),
        in_specs=[pl.BlockSpec((1, 1, N_GRP, Q_BLOCK, HEAD_DIM), lambda b, k, i: (b, k, 0, i, 0)),
                  pl.BlockSpec((1, 1, 1, rows_c, HEAD_DIM), lambda b, k, i: (b, 0, k, 0, 0)),
                  pl.BlockSpec((1, 1, 1, rows_c, HEAD_DIM), lambda b, k, i: (b, 1, k, 0, 0)),
                  pl.BlockSpec((1, N_GRP, Q_BLOCK, CMP_NEAR), lambda b, k, i: (k, 0, 0, 0)),
                  pl.BlockSpec((1, N_GRP * Q_BLOCK, 1), lambda b, k, i: (k, 0, 0)),
                  pl.BlockSpec((rows_c, nbp), lambda b, k, i: (0, 0))],
        out_specs=[pl.BlockSpec((1, 1, N_GRP, Q_BLOCK, HEAD_DIM), lambda b, k, i: (b, k, 0, i, 0)),
                   pl.BlockSpec((1, 1, 1, nbp, Q_BLOCK), lambda b, k, i: (b, k, i, 0, 0))],
        out_shape=[jax.ShapeDtypeStruct((bsz, N_KV, N_GRP, s, HEAD_DIM), F32),
                   jax.ShapeDtypeStruct((bsz, N_KV, nq, nbp, Q_BLOCK), BF16)],
        scratch_shapes=[pltpu.VMEM((Q_BLOCK, nbp), F32)],
        compiler_params=_cparams(("parallel", "parallel", "arbitrary")),
        name="nsa_cmp_attn",
    )(q64, kv_c, kv_c, tab_cmp, far_col, m_pad)


def _slc_attn_kernel(qt_ref, far_ref, ka_ref, vt_ref, selt_ref, tab_ref, o_ref,
                     qa_ref, s0_ref, s1_ref, x0_ref, x1_ref, m_ref, acc_ref):
    i = pl.program_id(2)
    cols = N_GRP * Q_BLOCK
    ngroups = qa_ref.shape[0] // 2
    qt = qt_ref[0, 0, 0]
    qt_far = qt + far_ref[0]
    selneg = ((selt_ref[0, 0, 0].astype(F32) - 1.0) * MASK_BIG).astype(BF16)
    for g in range(ngroups):
        part = selneg[g * SLC_GROUP_BLOCKS:(g + 1) * SLC_GROUP_BLOCKS, :]
        part = jnp.concatenate([part] * N_GRP, axis=1)
        qa_ref[2 * g] = jnp.concatenate([qt_far, part], axis=0)
        qa_ref[2 * g + 1] = jnp.concatenate([qt, part], axis=0)

    tiles_per_group = SLC_GROUP_BLOCKS // SLC_TILE_BLOCKS
    sub = SLC_KEY_TILE // Q_BLOCK
    last_tile = vt_ref.shape[2] - 1
    n_pairs = (i // sub + 2) // 2
    n_far = jnp.maximum(0, (i - (SLC_NEAR_CHUNKS - 1)) // sub)
    far_pairs = jnp.maximum(0, (n_far - 1) // 2)
    first_table = jnp.where(n_far > 0, 2 * far_pairs + 1, 0)

    def produce(kt, ls, s_ref, mx_ref, with_table):
        near = (kt >= first_table).astype(jnp.int32)
        kc = jnp.minimum(kt, last_tile)
        ks = pl.multiple_of(kc * SLC_KEY_TILE, SLC_KEY_TILE)
        s = jnp.dot(ka_ref[0, 0, pl.ds(ks, SLC_KEY_TILE), :], qa_ref[2 * (kc // tiles_per_group) + near, :, ls],
                    preferred_element_type=F32)
        if with_table:
            chunks = []
            for a in range(sub):
                mm = i - sub * kt - a
                idx = jnp.where(mm < 0, SLC_TAB_MASKED, jnp.where(mm >= SLC_NEAR_CHUNKS, SLC_TAB_CONST, mm))
                chunks.append(tab_ref[0, idx, :, ls])
            s = s + jnp.concatenate(chunks, axis=0)
        s_ref[:, ls] = s
        mx_ref[:, ls] = jnp.max(s, axis=0, keepdims=True)

    def consume(kt, ls, s_ref, mx_ref):
        m_old = m_ref[:, ls]
        m_new = jnp.maximum(m_old, mx_ref[:, ls])
        m_ref[:, ls] = m_new
        p = jnp.exp(s_ref[:, ls] - m_new).astype(BF16)
        acc_ref[:, ls] = jnp.exp(m_old - m_new) * acc_ref[:, ls] + jnp.dot(
            vt_ref[0, 0, jnp.minimum(kt, last_tile)], p, preferred_element_type=F32)

    strips = [slice(c * SLC_LANE_STRIP, (c + 1) * SLC_LANE_STRIP) for c in range(cols // SLC_LANE_STRIP)]

    def pair_step(j, with_table):
        for ls in strips:
            produce(2 * j + 1, ls, s1_ref, x1_ref, with_table)
            consume(2 * j, ls, s0_ref, x0_ref)
        for ls in strips:
            produce(2 * j + 2, ls, s0_ref, x0_ref, with_table)
            consume(2 * j + 1, ls, s1_ref, x1_ref)

    m_ref[...] = jnp.full(m_ref.shape, -jnp.inf, F32)
    acc_ref[...] = jnp.zeros(acc_ref.shape, F32)

    @pl.when(n_far > 0)
    def _():
        for ls in strips:
            produce(0, ls, s0_ref, x0_ref, False)

    @pl.when(n_far == 0)
    def _():
        for ls in strips:
            produce(0, ls, s0_ref, x0_ref, True)

    def far_body(j, carry):
        pair_step(j, False)
        return carry

    def near_body(j, carry):
        pair_step(j, True)
        return carry

    lax.fori_loop(0, far_pairs, far_body, 0)
    lax.fori_loop(far_pairs, n_pairs, near_body, 0)
    o_ref[0, 0, 0] = acc_ref[:HEAD_DIM, :] / acc_ref[HEAD_DIM:HEAD_DIM + 1, :]


def _slc_attn(qt, far_t, k_aug, vt_aug, sel_t, tab_t):
    bsz, _, nq, _, cols = qt.shape
    s = k_aug.shape[2]
    nbp = sel_t.shape[3]
    ngroups = nbp // SLC_GROUP_BLOCKS
    nch = tab_t.shape[1]
    once = pl.Buffered(1)
    return pl.pallas_call(
        _slc_attn_kernel,
        grid=(bsz, N_KV, nq),
        in_specs=[pl.BlockSpec((1, 1, 1, LANES, cols), lambda b, k, i: (b, k, i, 0, 0)),
                  pl.BlockSpec((1, LANES, cols), lambda b, k, i: (k, 0, 0)),
                  pl.BlockSpec((1, 1, s, 2 * LANES), lambda b, k, i: (b, k, 0, 0), pipeline_mode=once),
                  pl.BlockSpec((1, 1, s // SLC_KEY_TILE, SLC_V_ROWS, SLC_KEY_TILE),
                               lambda b, k, i: (b, k, 0, 0, 0), pipeline_mode=once),
                  pl.BlockSpec((1, 1, 1, nbp, Q_BLOCK), lambda b, k, i: (b, k, i, 0, 0)),
                  pl.BlockSpec((1, nch, Q_BLOCK, cols), lambda b, k, i: (k, 0, 0, 0), pipeline_mode=once)],
        out_specs=pl.BlockSpec((1, 1, 1, HEAD_DIM, cols), lambda b, k, i: (b, k, i, 0, 0)),
        out_shape=jax.ShapeDtypeStruct((bsz, N_KV, nq, HEAD_DIM, cols), F32),
        scratch_shapes=[pltpu.VMEM((2 * ngroups, 2 * LANES, cols), BF16),
                        pltpu.VMEM((SLC_KEY_TILE, cols), F32),
                        pltpu.VMEM((SLC_KEY_TILE, cols), F32),
                        pltpu.VMEM((1, cols), F32),
                        pltpu.VMEM((1, cols), F32),
                        pltpu.VMEM((1, cols), F32),
                        pltpu.VMEM((SLC_V_ROWS, cols), F32)],
        compiler_params=_cparams(("parallel", "parallel", "arbitrary")),
        name="nsa_slc_attn",
    )(qt, far_t, k_aug, vt_aug, sel_t, tab_t)


def _win_attn_kernel(q_ref, k_ref, v_ref, tab_ref, o_ref):
    i = pl.program_id(2)
    rows = N_GRP * Q_BLOCK
    nw = WINDOW + Q_BLOCK
    q = q_ref[0, 0].reshape(rows, HEAD_DIM)
    qs = pl.multiple_of(i * Q_BLOCK, Q_BLOCK)
    k = k_ref[0, 0, pl.ds(qs, nw), :]
    v = v_ref[0, 0, pl.ds(qs, nw), :]
    s = _nt(q, k) + tab_ref[0].reshape(rows, nw)
    col = lax.broadcasted_iota(jnp.int32, (1, nw), 1)
    s = jnp.where(col >= WINDOW - i * Q_BLOCK, s, NEG_INF)
    m = jnp.max(s, axis=-1, keepdims=True)
    e = jnp.exp(s - m)
    p = e / jnp.sum(e, axis=-1, keepdims=True)
    o = jnp.dot(p.astype(BF16), v, preferred_element_type=F32)
    o_ref[0, 0] = o.reshape(N_GRP, Q_BLOCK, HEAD_DIM)


def _win_attn(q64, k_win, v_win, tab_win):
    bsz, _, _, s, _ = q64.shape
    nq = s // Q_BLOCK
    sp = k_win.shape[2]
    nw = WINDOW + Q_BLOCK
    return pl.pallas_call(
        _win_attn_kernel,
        grid=(bsz, N_KV, nq),
        in_specs=[pl.BlockSpec((1, 1, N_GRP, Q_BLOCK, HEAD_DIM), lambda b, k, i: (b, k, 0, i, 0)),
                  pl.BlockSpec((1, 1, sp, HEAD_DIM), lambda b, k, i: (b, k, 0, 0)),
                  pl.BlockSpec((1, 1, sp, HEAD_DIM), lambda b, k, i: (b, k, 0, 0)),
                  pl.BlockSpec((1, N_GRP, Q_BLOCK, nw), lambda b, k, i: (k, 0, 0, 0))],
        out_specs=pl.BlockSpec((1, 1, N_GRP, Q_BLOCK, HEAD_DIM), lambda b, k, i: (b, k, 0, i, 0)),
        out_shape=jax.ShapeDtypeStruct((bsz, N_KV, N_GRP, s, HEAD_DIM), F32),
        compiler_params=_cparams(("parallel", "parallel", "arbitrary")),
        name="nsa_win_attn",
    )(q64, k_win, v_win, tab_win)


def _ssd_kernel(xbc_ref, z_ref, dt_ref, cw_ref, cb_ref, dtb_ref, alog_ref, dskip_ref, nw_ref, o_ref,
                ext_ref, state_ref):
    ln = SSM_CHUNK
    d_ssm = z_ref.shape[2]
    nh = dtb_ref.shape[1]
    gw = d_ssm // SSM_GROUPS
    hpg = nh // SSM_GROUPS
    gn = SSM_GROUPS * SSM_STATE

    @pl.when(pl.program_id(1) == 0)
    def _():
        ext_ref[0:SUBLANES, :] = jnp.zeros((SUBLANES, ext_ref.shape[1]), F32)
        state_ref[...] = jnp.zeros(state_ref.shape, F32)

    ext_ref[SUBLANES:SUBLANES + ln, :] = xbc_ref[0]
    conv = cw_ref[0:1, :] * ext_ref[SUBLANES - CONV_WIDTH + 1:SUBLANES - CONV_WIDTH + 1 + ln, :]
    for k in range(1, CONV_WIDTH):
        lo = SUBLANES - CONV_WIDTH + 1 + k
        conv = conv + cw_ref[k:k + 1, :] * ext_ref[lo:lo + ln, :]
    conv = conv + cb_ref[...]
    ext_ref[0:SUBLANES, :] = xbc_ref[0, ln - SUBLANES:ln, :]
    xc = _silu(conv)
    xs = xc[:, :d_ssm]
    bm = xc[:, d_ssm:d_ssm + gn]
    cm = xc[:, d_ssm + gn:d_ssm + 2 * gn]

    xdt = dt_ref[0, :, :nh] + dtb_ref[...]
    dt = jnp.maximum(xdt, 0.0) + jnp.log1p(jnp.exp(-jnp.abs(xdt)))
    a = -jnp.exp(alog_ref[...])
    da = dt * a

    row = lax.broadcasted_iota(jnp.int32, (ln, ln), 0)
    colm = lax.broadcasted_iota(jnp.int32, (ln, ln), 1)
    causal = row >= colm
    acs = jnp.dot(causal.astype(F32), da, precision=HI, preferred_element_type=F32)
    eye = (lax.broadcasted_iota(jnp.int32, (2 * nh, 2 * nh), 0)
           == lax.broadcasted_iota(jnp.int32, (2 * nh, 2 * nh), 1)).astype(F32)
    rows_t = _nt(eye, jnp.concatenate([acs, dt], axis=1), precision=HI)
    expand = (lax.broadcasted_iota(jnp.int32, (nh, d_ssm), 0)
              == lax.broadcasted_iota(jnp.int32, (nh, d_ssm), 1) // SSM_HEAD_DIM).astype(F32)
    last = acs[ln - 1:ln, :]
    exp_acs_x = jnp.dot(jnp.exp(acs), expand, precision=HI, preferred_element_type=F32)
    w_x = jnp.dot(jnp.exp(last - acs) * dt, expand, precision=HI, preferred_element_type=F32)
    exp_last_x = exp_acs_x[ln - 1:ln, :]

    ys = []
    for g in range(SSM_GROUPS):
        cg = cm[:, g * SSM_STATE:(g + 1) * SSM_STATE].astype(BF16)
        bg32 = bm[:, g * SSM_STATE:(g + 1) * SSM_STATE]
        bg = bg32.astype(BF16)
        xg = xs[:, g * gw:(g + 1) * gw]
        cb = _nt(cg, bg)
        st = state_ref[g]
        y_state = jnp.dot(cg, st.astype(BF16), preferred_element_type=F32) * exp_acs_x[:, g * gw:(g + 1) * gw]
        y_heads = []
        for j in range(hpg):
            h = g * hpg + j
            seg = acs[:, h:h + 1] - rows_t[h:h + 1, :]
            decay = jnp.exp(jnp.where(causal, seg, -jnp.inf))
            mmat = cb * decay * rows_t[nh + h:nh + h + 1, :]
            xh = xg[:, j * SSM_HEAD_DIM:(j + 1) * SSM_HEAD_DIM].astype(BF16)
            y_heads.append(jnp.dot(mmat.astype(BF16), xh, preferred_element_type=F32))
        ys.append(jnp.concatenate(y_heads, axis=1) + y_state)
        xw = (xg * w_x[:, g * gw:(g + 1) * gw]).astype(BF16)
        state_ref[g] = st * exp_last_x[:, g * gw:(g + 1) * gw] + jnp.dot(
            bg32.T.astype(BF16), xw, preferred_element_type=F32)

    y = jnp.concatenate(ys, axis=1) + dskip_ref[...] * xs
    y = y * _silu(z_ref[0])
    outs = []
    for g in range(SSM_GROUPS):
        yg = y[:, g * gw:(g + 1) * gw]
        outs.append(yg * lax.rsqrt(jnp.mean(yg * yg, axis=-1, keepdims=True) + RMS_EPS))
    o_ref[0] = jnp.concatenate(outs, axis=1) * nw_ref[...]


def _ssd(proj, xbc_blk, z_blk, dt_blk, conv_w, conv_b, dt_bias, a_log, d_skip, norm_w):
    bsz, s, _ = proj.shape
    ch = xbc_blk[1]
    d_ssm = z_blk[1]
    nh = dt_bias.shape[0]
    gw = d_ssm // SSM_GROUPS
    nchunks = s // SSM_CHUNK
    full = lambda shape: pl.BlockSpec(shape, lambda b, c: (0,) * len(shape))
    cols = lambda blk: pl.BlockSpec((1, SSM_CHUNK, blk[1]), lambda b, c: (b, c, blk[0]))
    xbc = z = dt_raw = proj
    return pl.pallas_call(
        _ssd_kernel,
        grid=(bsz, nchunks),
        in_specs=[cols(xbc_blk), cols(z_blk), cols(dt_blk),
                  full((CONV_WIDTH, ch)), full((1, ch)), full((1, nh)), full((1, nh)),
                  full((1, d_ssm)), full((1, d_ssm))],
        out_specs=pl.BlockSpec((1, SSM_CHUNK, d_ssm), lambda b, c: (b, c, 0)),
        out_shape=jax.ShapeDtypeStruct((bsz, s, d_ssm), F32),
        scratch_shapes=[pltpu.VMEM((SUBLANES + SSM_CHUNK, ch), F32),
                        pltpu.VMEM((SSM_GROUPS, SSM_STATE, gw), F32)],
        compiler_params=_cparams(("parallel", "arbitrary")),
        name="ssd_scan",
    )(xbc, z, dt_raw, conv_w, conv_b.reshape(1, ch), dt_bias.reshape(1, nh), a_log.reshape(1, nh),
      jnp.repeat(d_skip, SSM_HEAD_DIM).reshape(1, d_ssm), norm_w.reshape(1, d_ssm))


def _mix_out_kernel(oc_ref, os_ref, ow_ref, gl_ref, ex_ref, an_ref, ssm_ref, w_ref, x_ref, g1_ref, o_ref):
    d_attn = oc_ref.shape[2]
    sig = jax.nn.sigmoid(gl_ref[0])
    pieces, rest = [], sig
    for _ in range(GATE_PIECES):
        piece = rest.astype(BF16)
        pieces.append(piece)
        rest = rest - piece.astype(F32)
    sig3 = jnp.concatenate(pieces, axis=1)
    gc = jnp.dot(sig3, ex_ref[0], preferred_element_type=F32)
    gs = jnp.dot(sig3, ex_ref[1], preferred_element_type=F32)
    gw = jnp.dot(sig3, ex_ref[2], preferred_element_type=F32)
    attn = gc * oc_ref[0] + gs * os_ref[0] + gw * ow_ref[0]
    attn = attn * lax.rsqrt(jnp.mean(attn * attn, axis=-1, keepdims=True) + RMS_EPS) * an_ref[...]
    mix = (jnp.dot(attn.astype(BF16), w_ref[:d_attn, :], preferred_element_type=F32)
           + jnp.dot(ssm_ref[0].astype(BF16), w_ref[d_attn:, :], preferred_element_type=F32))
    o_ref[0] = x_ref[0] + g1_ref[0] * mix


def _gate_expand():
    ex = np.zeros((3, LANES, N_HEADS * HEAD_DIM), np.float32)
    for r in range(3):
        for h in range(N_HEADS):
            ex[r, h * 3 + r, h * HEAD_DIM:(h + 1) * HEAD_DIM] = 1.0
    return np.concatenate([ex] * GATE_PIECES, axis=1)


def _mix_out(oc, os_, ow, gl, gl_blk, attn_norm, ssm, w_out_bf16, x, g1, tm=256):
    bsz, s, d = x.shape
    d_attn = oc.shape[2]
    d_ssm = ssm.shape[2]
    tok = lambda w: pl.BlockSpec((1, tm, w), lambda b, i: (b, i, 0))
    return pl.pallas_call(
        _mix_out_kernel,
        grid=(bsz, s // tm),
        in_specs=[tok(d_attn), tok(d_attn), tok(d_attn),
                  pl.BlockSpec((1, tm, gl_blk[1]), lambda b, i: (b, i, gl_blk[0])),
                  pl.BlockSpec((3, GATE_PIECES * LANES, d_attn), lambda b, i: (0, 0, 0)),
                  pl.BlockSpec((1, d_attn), lambda b, i: (0, 0)),
                  tok(d_ssm),
                  pl.BlockSpec((d_attn + d_ssm, d), lambda b, i: (0, 0)),
                  tok(d),
                  pl.BlockSpec((1, 1, d), lambda b, i: (b, 0, 0))],
        out_specs=tok(d),
        out_shape=jax.ShapeDtypeStruct((bsz, s, d), F32),
        compiler_params=_cparams(("parallel", "parallel")),
        name="mix_out",
    )(oc, os_, ow, gl, jnp.asarray(_gate_expand(), BF16), attn_norm.reshape(1, d_attn), ssm, w_out_bf16, x, g1)


def _peer_candidates():
    k, sub = PEER_TOPK, SUBLANES
    cells = [(0, b) for b in range(k)]
    cells += [(a, b) for a in range(1, sub) for b in range(sub)]
    cells += [(a, 0) for a in range(sub, k)]
    order = np.array([a * k + b if (a + 1) * (b + 1) <= k else -1 for a, b in cells], np.float32)
    group = np.zeros((k, LANES), np.float32)
    for row, (a, _) in enumerate(cells):
        group[a, row] = 1.0
    return cells, order, group


def _peer_route_kernel(q_ref, sk_ref, order_ref, group_ref, rank2_ref, e2_ref, n1_ref, c1_ref):
    k = PEER_TOPK

    def route(assume_distinct):
        q = q_ref[...]
        half = q.shape[1] // 2
        s1 = _nt(sk_ref[0], q[:, :half], precision=HI)
        s2 = _nt(sk_ref[1], q[:, half:], precision=HI)
        rank1, v1 = _topk_mark(s1, k, axis=0, assume_distinct=assume_distinct)
        rank2, v2 = _topk_mark(s2, k, axis=0, assume_distinct=assume_distinct)
        v1_all = jnp.concatenate(v1, axis=0)
        v2_all = jnp.concatenate(v2, axis=0)
        cand = jnp.concatenate([v1[0] + v2_all] + [v1[a] + v2_all[:SUBLANES] for a in range(1, SUBLANES)]
                               + [v1_all[SUBLANES:] + v2[0]], axis=0)
        order = order_ref[...]
        cand = jnp.where(order >= 0.0, cand, -jnp.inf)
        rank_c, best = _topk_mark(cand, k, axis=0, order=order, order_bound=float(k * k),
                                  assume_distinct=assume_distinct)
        chosen = jnp.where(rank_c < float(k), 1.0, 0.0).astype(BF16)
        chosen = jnp.concatenate([chosen, jnp.zeros((LANES - chosen.shape[0], chosen.shape[1]), BF16)], axis=0)
        count = jnp.dot(group_ref[...], chosen, preferred_element_type=F32)
        z = best[0] * 0.0
        for r in range(k):
            z = z + jnp.exp(best[r] - best[0])
        n1 = jnp.zeros(s1.shape, F32)
        for a in range(k):
            n1 = jnp.where(rank1 == float(a), count[a:a + 1, :], n1)
        rank2_ref[0] = rank2.astype(BF16)
        e2_ref[0] = jnp.exp(s2 - v2[0]).astype(BF16)
        n1_ref[0] = n1
        c1_ref[0] = jnp.exp(s1 - v1[0]) / z
        return jnp.maximum(jnp.maximum(_topk_clean(rank1, k, 0), _topk_clean(rank2, k, 0)),
                           _topk_clean(rank_c, k, 0))

    dirty = route(True)

    @pl.when(dirty > 0.0)
    def _():
        route(False)


def _peer_route(q, subkeys, tm=256):
    t, width = q.shape
    kd = width // PEER_HEADS
    nk = subkeys.shape[1]
    spec = pl.BlockSpec((1, nk, tm), lambda i, h: (h, 0, i))
    shape = jax.ShapeDtypeStruct((PEER_HEADS, nk, t), F32)
    cells, order, group = _peer_candidates()
    order = jnp.asarray(np.broadcast_to(order[:, None], (len(cells), tm)))
    return pl.pallas_call(
        _peer_route_kernel,
        grid=(t // tm, PEER_HEADS),
        in_specs=[pl.BlockSpec((tm, kd), lambda i, h: (i, h)),
                  pl.BlockSpec((2, nk, kd // 2), lambda i, h: (0, 0, 0)),
                  pl.BlockSpec((len(cells), tm), lambda i, h: (0, 0)),
                  pl.BlockSpec((PEER_TOPK, LANES), lambda i, h: (0, 0))],
        out_specs=[spec, spec, spec, spec],
        out_shape=[jax.ShapeDtypeStruct(shape.shape, BF16), jax.ShapeDtypeStruct(shape.shape, BF16), shape, shape],
        compiler_params=_cparams(("parallel", "parallel")),
        name="peer_route",
    )(q, subkeys, order, jnp.asarray(group, BF16))


def _peer_dense_kernel(ht_ref, u_ref, vt_ref, rank2_ref, e2_ref, n1_ref, c1_ref, o_ref, *scratch):
    j = pl.program_id(1)
    te = u_ref.shape[0]
    nk = rank2_ref.shape[1]

    @pl.when(j == 0)
    def _():
        o_ref[...] = jnp.zeros(o_ref.shape, F32)

    tm = ht_ref.shape[1]
    strips = [slice(c * PEER_LANE_STRIP, (c + 1) * PEER_LANE_STRIP) for c in range(tm // PEER_LANE_STRIP)]

    units = [(e, c) for e in range(te // PEER_EXPERT_UNIT) for c in range(len(strips))]
    act_refs = scratch[:len(units)]
    aw_refs = scratch[len(units):]
    n_piece = PEER_EXPERT_UNIT // PEER_EXPERT_PIECE
    d_piece = o_ref.shape[0] // n_piece

    def produce(k, p):
        e, c = units[k]
        lo = e * PEER_EXPERT_UNIT + p * PEER_EXPERT_PIECE
        act_refs[k][p * PEER_EXPERT_PIECE:(p + 1) * PEER_EXPERT_PIECE, :] = jax.nn.gelu(
            jnp.dot(u_ref[lo:lo + PEER_EXPERT_PIECE, :], ht_ref[:, strips[c]], preferred_element_type=F32)
        ).astype(BF16)

    def gate(k, r):
        e, c = units[k]
        ls = strips[c]
        i1 = (j * te + e * PEER_EXPERT_UNIT) // nk + r
        w = jnp.zeros((nk, PEER_LANE_STRIP), BF16)
        zero = jnp.zeros((nk, PEER_LANE_STRIP), BF16)
        for h in range(PEER_HEADS):
            n_row = n1_ref[h, pl.ds(i1, 1), ls].astype(BF16)
            c_row = c1_ref[h, pl.ds(i1, 1), ls].astype(BF16)
            w = w + jnp.where(rank2_ref[h, :, ls] < n_row, e2_ref[h, :, ls], zero) * c_row
        aw_refs[k][r * nk:(r + 1) * nk, :] = act_refs[k][r * nk:(r + 1) * nk, :] * w

    def combine(k, m):
        e, c = units[k]
        ds_ = slice(m * d_piece, (m + 1) * d_piece)
        es = slice(e * PEER_EXPERT_UNIT, (e + 1) * PEER_EXPERT_UNIT)
        o_ref[ds_, strips[c]] += jnp.dot(vt_ref[ds_, es], aw_refs[k][...], preferred_element_type=F32)

    gates_per_piece = PEER_EXPERT_PIECE // nk
    for p in range(n_piece):
        produce(0, p)
    for k in range(len(units)):
        for p in range(n_piece):
            if k + 1 < len(units):
                produce(k + 1, p)
            for r in range(p * gates_per_piece, (p + 1) * gates_per_piece):
                gate(k, r)
                if k > 0 and r == p * gates_per_piece:
                    combine(k - 1, p)
    for m in range(n_piece):
        combine(len(units) - 1, m)


def _peer_dense(h_t, u_bf16, v_t_bf16, rank2, e2, n1, c1, tm=512, te=512):
    d, t = h_t.shape
    n_exp = u_bf16.shape[0]
    nk = rank2.shape[1]
    n_units = (te // PEER_EXPERT_UNIT) * (tm // PEER_LANE_STRIP)
    route = pl.BlockSpec((PEER_HEADS, nk, tm), lambda i, j: (0, 0, i))
    return pl.pallas_call(
        _peer_dense_kernel,
        grid=(t // tm, n_exp // te),
        in_specs=[pl.BlockSpec((d, tm), lambda i, j: (0, i)),
                  pl.BlockSpec((te, d), lambda i, j: (j, 0)),
                  pl.BlockSpec((d, te), lambda i, j: (0, j)),
                  route, route, route, route],
        out_specs=pl.BlockSpec((d, tm), lambda i, j: (0, i)),
        out_shape=jax.ShapeDtypeStruct((d, t), F32),
        scratch_shapes=([pltpu.VMEM((PEER_EXPERT_UNIT, PEER_LANE_STRIP), BF16)] * n_units
                        + [pltpu.VMEM((PEER_EXPERT_UNIT, PEER_LANE_STRIP), BF16)] * n_units),
        compiler_params=_cparams(("parallel", "arbitrary")),
        name="peer_dense",
    )(h_t, u_bf16, v_t_bf16, rank2, e2, n1, c1)


def _residual_kernel(x_ref, yt_ref, g_ref, o_ref):
    o_ref[0] = x_ref[0] + g_ref[0] * yt_ref[...].T


def _residual(x, y_t, g, tm=512):
    bsz, s, d = x.shape
    tok = pl.BlockSpec((1, tm, d), lambda b, i: (b, i, 0))
    return pl.pallas_call(
        _residual_kernel,
        grid=(bsz, s // tm),
        in_specs=[tok, pl.BlockSpec((d, tm), lambda b, i: (0, b * (s // tm) + i)),
                  pl.BlockSpec((1, 1, d), lambda b, i: (b, 0, 0))],
        out_specs=tok,
        out_shape=jax.ShapeDtypeStruct((bsz, s, d), F32),
        compiler_params=_cparams(("parallel", "parallel")),
        name="residual",
    )(x, y_t, g)


def _final_norm_kernel(x_ref, w_ref, o_ref):
    x = x_ref[0]
    o_ref[0] = x * lax.rsqrt(jnp.mean(x * x, axis=-1, keepdims=True) + RMS_EPS) * w_ref[...]


def _final_norm(x, w, tm=512):
    bsz, s, d = x.shape
    tok = pl.BlockSpec((1, tm, d), lambda b, i: (b, i, 0))
    return pl.pallas_call(
        _final_norm_kernel,
        grid=(bsz, s // tm),
        in_specs=[tok, pl.BlockSpec((1, d), lambda b, i: (0, 0))],
        out_specs=tok,
        out_shape=jax.ShapeDtypeStruct((bsz, s, d), F32),
        compiler_params=_cparams(("parallel", "parallel")),
        name="final_norm",
    )(x, w.reshape(1, d))


def _pad_cols(w, width):
    return jnp.pad(w, ((0, 0), (0, width - w.shape[1])))


def _nsa(q, kv, tables, cmp_pe, cmp_w1, cmp_w2):
    tab_win, tab_cmp, tab_slc, far_col, far_t = tables
    bsz, s, _ = q.shape
    nc = s // CMP_STRIDE
    nq = s // Q_BLOCK
    n_slc = s // SLC_BLOCK
    nbp = -(-n_slc // SLC_GROUP_BLOCKS) * SLC_GROUP_BLOCKS
    scale = HEAD_DIM ** -0.5
    q6 = (q * scale).astype(BF16).reshape(bsz, nq, Q_BLOCK, N_KV, N_GRP, HEAD_DIM)
    qh = q6.transpose(0, 3, 4, 1, 2, 5).reshape(bsz, N_KV, N_GRP, s, HEAD_DIM)
    qt = q6.transpose(0, 3, 1, 5, 4, 2).reshape(bsz, N_KV, nq, HEAD_DIM, N_GRP * Q_BLOCK)
    qt = jnp.pad(qt, ((0, 0),) * 3 + ((0, LANES - HEAD_DIM), (0, 0)))
    kv6 = kv.reshape(bsz, s, 6, N_KV, HEAD_DIM)

    kv_cmp = kv6[:, :, 0:2].reshape(bsz, nc, CMP_STRIDE, 2, N_KV, HEAD_DIM)
    kv_cmp = kv_cmp.transpose(0, 3, 4, 1, 2, 5).reshape(bsz, 2, N_KV, nc, CMP_STRIDE * HEAD_DIM)
    kv_c = _compress(kv_cmp, cmp_pe, cmp_w1, cmp_w2)
    o_c, sel_t = _cmp_attn(qh, kv_c, tab_cmp, far_col, nbp)

    k_slc = kv6[:, :, 2].transpose(0, 2, 1, 3).astype(BF16)
    nkt = s // SLC_KEY_TILE
    vt = kv6[:, :, 3].astype(BF16).reshape(bsz, nkt, SLC_KEY_TILE, N_KV, HEAD_DIM).transpose(0, 3, 1, 4, 2)
    vt_aug = jnp.concatenate([
        vt, jnp.ones((bsz, N_KV, nkt, 1, SLC_KEY_TILE), BF16),
        jnp.zeros((bsz, N_KV, nkt, SLC_V_ROWS - HEAD_DIM - 1, SLC_KEY_TILE), BF16)], axis=3)
    blk = np.arange(s) // SLC_BLOCK
    onehot = (blk[:, None] % SLC_GROUP_BLOCKS == np.arange(SLC_GROUP_BLOCKS)[None, :]).astype(np.float32)
    k_aug = jnp.concatenate([
        k_slc, jnp.ones((bsz, N_KV, s, SLC_BIAS_PIECES), BF16),
        jnp.zeros((bsz, N_KV, s, LANES - HEAD_DIM - SLC_BIAS_PIECES), BF16),
        jnp.broadcast_to(jnp.asarray(onehot, BF16), (bsz, N_KV, s, SLC_GROUP_BLOCKS))], axis=-1)
    o_st = _slc_attn(qt, far_t, k_aug, vt_aug, sel_t, tab_slc)
    o_s = o_st.reshape(bsz, N_KV, nq, HEAD_DIM, N_GRP, Q_BLOCK).transpose(0, 2, 5, 1, 4, 3)
    o_s = o_s.reshape(bsz, s, N_HEADS * HEAD_DIM)

    front = ((0, 0), (0, 0), (WINDOW, 0), (0, 0))
    k_win = jnp.pad(kv6[:, :, 4].transpose(0, 2, 1, 3).astype(BF16), front)
    v_win = jnp.pad(kv6[:, :, 5].transpose(0, 2, 1, 3).astype(BF16), front)
    o_w = _win_attn(qh, k_win, v_win, tab_win)

    back = lambda o: o.transpose(0, 3, 1, 2, 4).reshape(bsz, s, N_HEADS * HEAD_DIM)
    return back(o_c), o_s, back(o_w)


def kernel(x, c, ada_w, ada_b, norm_mix, norm_ffn, w_in, cmp_pe, cmp_w1, cmp_w2, rel_bias, attn_out_norm,
           conv_w, conv_b, dt_bias, a_log, d_skip, ssm_norm, w_out, peer_wq, peer_subkeys, peer_u, peer_v,
           norm_final):
    bsz, s, d = x.shape
    depth = ada_w.shape[0]
    d_attn = N_HEADS * HEAD_DIM
    n_kv = 6 * N_KV * HEAD_DIM
    n_gate = 3 * N_HEADS
    d_ssm = ssm_norm.shape[1]
    ch = conv_w.shape[2]
    nh = dt_bias.shape[1]

    mod = _ada_mod(c, ada_w, ada_b)
    tables = _bias_tables(rel_bias)

    for l in range(depth):
        sh1, sc1, g1, sh2, sc2, g2 = [mod[l, :, i * d:(i + 1) * d].reshape(bsz, 1, d) for i in range(6)]
        cuts = np.cumsum([0, d_attn, n_kv, n_gate, d_ssm, ch, nh])
        seg = dict(zip(("q", "kv", "gl", "z", "xbc", "dt"),
                       [w_in[l][:, cuts[i]:cuts[i + 1]] for i in range(6)]))
        seg["gl"] = _pad_cols(seg["gl"], LANES)
        seg["dt"] = _pad_cols(seg["dt"], LANES)
        order = ("q", "z", "kv", "gl", "dt", "xbc")
        w_cat = jnp.concatenate([seg[name] for name in order], axis=1).astype(BF16)
        start = dict(zip(order, np.cumsum([0] + [seg[name].shape[1] for name in order])[:-1]))
        blk = {name: (int(start[name]) // seg[name].shape[1], seg[name].shape[1]) for name in order}
        assert all(start[name] % seg[name].shape[1] == 0 for name in ("z", "gl", "dt", "xbc"))
        proj, _ = _norm_proj(x, norm_mix[l], sc1, sh1, w_cat)
        q = proj[:, :, start["q"]:start["q"] + d_attn]
        kv = proj[:, :, start["kv"]:start["kv"] + n_kv]

        o_c, o_s, o_w = _nsa(q, kv, tables, cmp_pe[l], cmp_w1[l], cmp_w2[l])
        ssm = _ssd(proj, blk["xbc"], blk["z"], blk["dt"], conv_w[l], conv_b[l], dt_bias[l], a_log[l], d_skip[l],
                   ssm_norm[l])
        x = _mix_out(o_c, o_s, o_w, proj, blk["gl"], attn_out_norm[l], ssm, w_out[l].astype(BF16), x, g1)

        pq, h2 = _norm_proj(x, norm_ffn[l], sc2, sh2, peer_wq[l].astype(BF16))
        rank2, e2, n1, c1 = _peer_route(pq.reshape(bsz * s, -1), peer_subkeys[l])
        h_t = h2.reshape(bsz * s, d).T
        ffn_t = _peer_dense(h_t, peer_u[l].astype(BF16), peer_v[l].T.astype(BF16), rank2, e2, n1, c1)
        x = _residual(x, ffn_t, g2)

    return _final_norm(x, norm_final)
```

```python
import functools
import math

import numpy as np
import jax
import jax.numpy as jnp
from jax import lax
from jax.experimental import pallas as pl
from jax.experimental.pallas import tpu as pltpu

F32 = jnp.float32
BF16 = jnp.bfloat16
HI = lax.Precision.HIGHEST

N_HEADS = 16
N_KV = 2
N_GRP = N_HEADS // N_KV
HEAD_DIM = 64
CMP_BLOCK = 32
CMP_STRIDE = 16
CMP_HIDDEN = 4 * HEAD_DIM
SLC_BLOCK = 64
SLC_TOPN = 16
WINDOW = 512
Q_BLOCK = 128
FORCE_SCORE = 1e4
NEG_INF = -1e30
REL_BUCKETS = 32
REL_MAX_DIST = 2048
SSM_HEAD_DIM = 64
SSM_GROUPS = 2
SSM_STATE = 128
CONV_WIDTH = 4
SSM_CHUNK = 256
PEER_HEADS = 8
PEER_NKEYS = 128
PEER_TOPK = 16
RMS_EPS = 1e-6

LANES = 128
SUBLANES = 8
VMEM_LIMIT = 56 * 1024 * 1024

SLC_KEY_TILE = 512
SLC_TILE_BLOCKS = SLC_KEY_TILE // SLC_BLOCK
SLC_GROUP_BLOCKS = LANES
SLC_NEAR_CHUNKS = 14
SLC_TAB_MASKED = 14
SLC_TAB_CONST = 15
MASK_BIG = 2.0 ** 100
SLC_LANE_STRIP = 256
PEER_LANE_STRIP = 256
PEER_EXPERT_PIECE = 256
PEER_EXPERT_UNIT = 512
SLC_V_ROWS = HEAD_DIM + 16
SLC_BIAS_PIECES = 3
GATE_PIECES = 3

CMP_FAR_STEP = 256
CMP_NEAR = 128
CMP_PAD = CMP_NEAR - Q_BLOCK // CMP_STRIDE


def _cparams(sem, vmem=VMEM_LIMIT):
    return pltpu.CompilerParams(dimension_semantics=sem, vmem_limit_bytes=vmem)


def _nt(a, b, precision=None):
    return lax.dot_general(a, b, (((1,), (1,)), ((), ())), precision=precision,
                           preferred_element_type=F32)


def _silu(x):
    return x * jax.nn.sigmoid(x)


def _rel_bucket_np(d):
    d = np.maximum(np.asarray(d, np.int64), 0)
    max_exact = REL_BUCKETS // 2
    ratio = np.log(np.maximum(d, max_exact).astype(np.float64) / max_exact) / math.log(REL_MAX_DIST / max_exact)
    scaled = ratio * (REL_BUCKETS - max_exact)
    large = max_exact + np.floor(scaled).astype(np.int64)
    return np.where(d < max_exact, d, np.minimum(large, REL_BUCKETS - 1)).astype(np.int32)


def _ada_kernel(c_ref, w_ref, b_ref, o_ref):
    cond = _silu(c_ref[...])
    o_ref[0] = jnp.dot(cond, w_ref[0], precision=HI, preferred_element_type=F32) + b_ref[0]


def _ada_mod(c, ada_w, ada_b):
    depth, d, n = ada_w.shape
    bsz = c.shape[0]
    rows = SUBLANES
    c_pad = jnp.zeros((rows, d), F32).at[:bsz].set(c)
    tn = 1024
    out = pl.pallas_call(
        _ada_kernel,
        grid=(depth, n // tn),
        in_specs=[pl.BlockSpec((rows, d), lambda l, j: (0, 0)),
                  pl.BlockSpec((1, d, tn), lambda l, j: (l, 0, j)),
                  pl.BlockSpec((1, 1, tn), lambda l, j: (l, 0, j))],
        out_specs=pl.BlockSpec((1, rows, tn), lambda l, j: (l, 0, j)),
        out_shape=jax.ShapeDtypeStruct((depth, rows, n), F32),
        compiler_params=_cparams(("parallel", "parallel")),
        name="ada_mod",
    )(c_pad, ada_w, ada_b.reshape(depth, 1, n))
    return out[:, :bsz]


def _norm_proj_kernel(x_ref, nw_ref, sc_ref, sh_ref, w_ref, o_ref, h_ref, hs_ref):
    @pl.when(pl.program_id(2) == 0)
    def _():
        x = x_ref[0]
        y = x * lax.rsqrt(jnp.mean(x * x, axis=-1, keepdims=True) + RMS_EPS)
        h = (y * nw_ref[...]) * (1.0 + sc_ref[0]) + sh_ref[0]
        hs_ref[...] = h.astype(BF16)
        h_ref[0] = h.astype(BF16)

    o_ref[0] = jnp.dot(hs_ref[...], w_ref[...], preferred_element_type=F32)


def _norm_proj(x, nw, sc, sh, w_bf16, tm=1024, tn=512):
    bsz, s, d = x.shape
    n = w_bf16.shape[1]
    return pl.pallas_call(
        _norm_proj_kernel,
        grid=(bsz, s // tm, n // tn),
        in_specs=[pl.BlockSpec((1, tm, d), lambda b, i, j: (b, i, 0)),
                  pl.BlockSpec((1, d), lambda b, i, j: (0, 0)),
                  pl.BlockSpec((1, 1, d), lambda b, i, j: (b, 0, 0)),
                  pl.BlockSpec((1, 1, d), lambda b, i, j: (b, 0, 0)),
                  pl.BlockSpec((d, tn), lambda b, i, j: (0, j))],
        out_specs=[pl.BlockSpec((1, tm, tn), lambda b, i, j: (b, i, j)),
                   pl.BlockSpec((1, tm, d), lambda b, i, j: (b, i, 0))],
        out_shape=[jax.ShapeDtypeStruct((bsz, s, n), F32),
                   jax.ShapeDtypeStruct((bsz, s, d), BF16)],
        scratch_shapes=[pltpu.VMEM((tm, d), BF16)],
        compiler_params=_cparams(("parallel", "parallel", "arbitrary")),
        name="norm_proj",
    )(x, nw.reshape(1, d), sc, sh, w_bf16)


def _bias_kernel(rel_ref, bk_ref, o_ref):
    h = pl.program_id(0)
    bk = bk_ref[...]
    acc = jnp.full(bk.shape, NEG_INF, F32)
    for b in range(REL_BUCKETS):
        acc = jnp.where(bk == b, rel_ref[b, h], acc)
    o_ref[0] = acc


def _bias_tables(rel_bias):
    r = np.arange(Q_BLOCK)[:, None]
    dw = r - np.arange(WINDOW + Q_BLOCK)[None, :] + WINDOW
    win = np.where((dw >= 0) & (dw < WINDOW), _rel_bucket_np(dw), -1)
    off = CMP_STRIDE * CMP_PAD - (CMP_BLOCK - 1)
    dc = r + off - CMP_STRIDE * np.arange(CMP_NEAR)[None, :]
    cmp_near = np.where(dc >= 0, _rel_bucket_np(dc), -1)
    chunks = []
    for m in range(SLC_NEAR_CHUNKS):
        ds_ = Q_BLOCK * m + r - np.arange(Q_BLOCK)[None, :]
        chunks.append(np.where(ds_ >= 0, _rel_bucket_np(ds_), -1))
    assert _rel_bucket_np(Q_BLOCK * SLC_NEAR_CHUNKS - (Q_BLOCK - 1)) == REL_BUCKETS - 1
    assert _rel_bucket_np(off + CMP_STRIDE) == REL_BUCKETS - 1
    chunks.append(np.full((Q_BLOCK, Q_BLOCK), -1))
    chunks.append(np.full((Q_BLOCK, Q_BLOCK), REL_BUCKETS - 1))
    bk = np.concatenate([win, cmp_near] + chunks, axis=1).astype(np.int32)
    cols = bk.shape[1]
    out = pl.pallas_call(
        _bias_kernel,
        grid=(N_HEADS,),
        in_specs=[pl.BlockSpec(memory_space=pltpu.SMEM),
                  pl.BlockSpec((Q_BLOCK, cols), lambda h: (0, 0))],
        out_specs=pl.BlockSpec((1, Q_BLOCK, cols), lambda h: (h, 0, 0)),
        out_shape=jax.ShapeDtypeStruct((N_HEADS, Q_BLOCK, cols), F32),
        compiler_params=_cparams(("arbitrary",)),
        name="bias_tables",
    )(rel_bias, jnp.asarray(bk))
    nw = WINDOW + Q_BLOCK
    tab_win = out[:, :, :nw].reshape(N_KV, N_GRP, Q_BLOCK, nw)
    tab_cmp = out[:, :, nw:nw + CMP_NEAR].reshape(N_KV, N_GRP, Q_BLOCK, CMP_NEAR)
    nch = SLC_NEAR_CHUNKS + 2
    tab_slc = out[:, :, nw + CMP_NEAR:].reshape(N_KV, N_GRP, Q_BLOCK, nch, Q_BLOCK)
    tab_slc = tab_slc.transpose(0, 3, 4, 1, 2).reshape(N_KV, nch, Q_BLOCK, N_GRP * Q_BLOCK)
    far = rel_bias[REL_BUCKETS - 1].reshape(N_KV, N_GRP, 1)
    far_col = jnp.broadcast_to(far, (N_KV, N_GRP, Q_BLOCK)).reshape(N_KV, N_GRP * Q_BLOCK, 1)
    far_row = far_col.reshape(N_KV, 1, N_GRP * Q_BLOCK)
    pieces, rest = [], far_row
    for _ in range(SLC_BIAS_PIECES):
        piece = rest.astype(BF16)
        pieces.append(piece)
        rest = rest - piece.astype(F32)
    zeros = lambda w: jnp.zeros((N_KV, w, N_GRP * Q_BLOCK), BF16)
    far_t = jnp.concatenate([zeros(HEAD_DIM)] + pieces + [zeros(LANES - HEAD_DIM - SLC_BIAS_PIECES)], axis=1)
    return tab_win, tab_cmp, tab_slc, far_col, far_t


def _compress_kernel(a_ref, pe_ref, w1_ref, w2_ref, o_ref):
    a = a_ref[0, 0, 0]
    half = a.shape[1]
    lo = jnp.dot((a + pe_ref[0, 0:1, :]).astype(BF16), w1_ref[0, :half, :].astype(BF16),
                 preferred_element_type=F32)
    hi = jnp.dot((a + pe_ref[0, 1:2, :]).astype(BF16), w1_ref[0, half:, :].astype(BF16),
                 preferred_element_type=F32)
    nc = a.shape[0]
    hid = jax.nn.gelu(lo + pltpu.roll(hi, nc - 1, axis=0))
    out = jnp.dot(hid.astype(BF16), w2_ref[0].astype(BF16), preferred_element_type=F32)
    o_ref[0, 0, 0] = jnp.zeros(o_ref.shape[3:], F32)
    o_ref[0, 0, 0, CMP_PAD:CMP_PAD + nc, :] = out


def _compress(kv_cmp, cmp_pe, cmp_w1, cmp_w2):
    bsz, _, _, nc, half = kv_cmp.shape
    pe = cmp_pe.reshape(2, 2, half)
    rows = CMP_PAD + nc + SUBLANES
    return pl.pallas_call(
        _compress_kernel,
        grid=(bsz, 2, N_KV),
        in_specs=[pl.BlockSpec((1, 1, 1, nc, half), lambda b, w, k: (b, w, k, 0, 0)),
                  pl.BlockSpec((1, 2, half), lambda b, w, k: (w, 0, 0)),
                  pl.BlockSpec((1, 2 * half, CMP_HIDDEN), lambda b, w, k: (w, 0, 0)),
                  pl.BlockSpec((1, CMP_HIDDEN, HEAD_DIM), lambda b, w, k: (w, 0, 0))],
        out_specs=pl.BlockSpec((1, 1, 1, rows, HEAD_DIM), lambda b, w, k: (b, w, k, 0, 0)),
        out_shape=jax.ShapeDtypeStruct((bsz, 2, N_KV, rows, HEAD_DIM), F32),
        compiler_params=_cparams(("parallel", "parallel", "parallel")),
        name="nsa_compress",
    )(kv_cmp, pe, cmp_w1, cmp_w2)


def _topk_mark(vals, k, axis, order=None, order_bound=None, assume_distinct=False):
    n = vals.shape[axis] if order is None else order_bound
    if not assume_distinct:
        iota = lax.broadcasted_iota(jnp.int32, vals.shape, axis).astype(F32) if order is None else order
    rank = jnp.full(vals.shape, float(k), F32)
    work = vals
    picked = []
    for r in range(k):
        m = jnp.max(work, axis=axis, keepdims=True)
        if assume_distinct:
            hit = work == m
        else:
            ix = jnp.min(jnp.where(work == m, iota, float(n)), axis=axis, keepdims=True)
            hit = iota == ix
        rank = jnp.where(hit, float(r), rank)
        work = jnp.where(hit, -jnp.inf, work)
        picked.append(m)
    return rank, picked


def _topk_clean(rank, k, axis):
    marks = jnp.sum(jnp.where(rank < float(k), 1.0, 0.0), axis=axis, keepdims=True)
    return jnp.max(jnp.abs(marks - float(k)))


def _cmp_attn_kernel(q_ref, k_ref, v_ref, tab_ref, far_ref, m_ref, o_ref, sel_ref, imp_ref):
    i = pl.program_id(2)
    rows = N_GRP * Q_BLOCK
    nc = k_ref.shape[3] - CMP_PAD - SUBLANES
    nbp = sel_ref.shape[3]
    start = pl.multiple_of(i * (Q_BLOCK // CMP_STRIDE), SUBLANES)
    n_far = i * (Q_BLOCK // CMP_STRIDE) - CMP_PAD

    def attend(wf):
        q = q_ref[0, 0].reshape(rows, HEAD_DIM)
        k_far = k_ref[0, 0, 0, CMP_PAD:CMP_PAD + wf, :].astype(BF16)
        v_far = v_ref[0, 0, 0, CMP_PAD:CMP_PAD + wf, :].astype(BF16)
        k_near = k_ref[0, 0, 0, pl.ds(start, CMP_NEAR), :].astype(BF16)
        v_near = v_ref[0, 0, 0, pl.ds(start, CMP_NEAR), :].astype(BF16)

        n_idx = lax.broadcasted_iota(jnp.int32, (1, wf), 1)
        mask_far = n_idx < n_far
        s_far = jnp.where(mask_far, _nt(q, k_far) + far_ref[0], NEG_INF)
        tab = tab_ref[0].reshape(rows, CMP_NEAR)
        c_idx = lax.broadcasted_iota(jnp.int32, (1, CMP_NEAR), 1)
        mask_near = (c_idx >= -n_far) & (tab > 0.5 * NEG_INF)
        s_near = jnp.where(mask_near, _nt(q, k_near) + tab, NEG_INF)

        m = jnp.maximum(jnp.max(s_far, axis=-1, keepdims=True), jnp.max(s_near, axis=-1, keepdims=True))
        e_far = jnp.exp(s_far - m)
        e_near = jnp.exp(s_near - m)
        l = jnp.sum(e_far, axis=-1, keepdims=True) + jnp.sum(e_near, axis=-1, keepdims=True)
        p_far = jnp.where(mask_far, e_far / l, 0.0)
        p_near = jnp.where(mask_near, e_near / l, 0.0)
        o = (jnp.dot(p_far.astype(BF16), v_far, preferred_element_type=F32)
             + jnp.dot(p_near.astype(BF16), v_near, preferred_element_type=F32))
        o_ref[0, 0] = o.reshape(N_GRP, Q_BLOCK, HEAD_DIM)

        ps_far = jnp.sum(p_far.reshape(N_GRP, Q_BLOCK, wf), axis=0)
        ps_near = jnp.sum(p_near.reshape(N_GRP, Q_BLOCK, CMP_NEAR), axis=0)
        imp_ref[...] = (
            jnp.dot(ps_far, m_ref[CMP_PAD:CMP_PAD + wf, :], precision=HI, preferred_element_type=F32)
            + jnp.dot(ps_near, m_ref[pl.ds(start, CMP_NEAR), :], precision=HI, preferred_element_type=F32))

    step = min(CMP_FAR_STEP, nc)
    widths = list(range(step, nc + 1, step))
    for b, wf in enumerate(widths):
        lower = n_far > widths[b - 1] if b > 0 else True
        upper = n_far <= wf if b + 1 < len(widths) else True
        pl.when(jnp.logical_and(lower, upper))(functools.partial(attend, wf))

    imp = imp_ref[...].T
    t = i * Q_BLOCK + lax.broadcasted_iota(jnp.int32, (1, Q_BLOCK), 1)
    cur = t // SLC_BLOCK
    blk = lax.broadcasted_iota(jnp.int32, (nbp, 1), 0)
    forced = (blk == 0) | (blk == cur) | (blk == cur - 1)
    imp = jnp.where(forced, FORCE_SCORE, jnp.where(blk <= cur, imp, -FORCE_SCORE))
    n_blocks = (nc * CMP_STRIDE) // SLC_BLOCK
    imp = jnp.where(blk < n_blocks, imp, -jnp.inf)
    rank, _ = _topk_mark(imp, min(SLC_TOPN, n_blocks), axis=0)
    sel_ref[0, 0, 0] = jnp.where(rank < float(SLC_TOPN), 1.0, 0.0).astype(BF16)


def _overlap_matrix(nc, nbp):
    n_cmp = nc - 1
    n_slc = nc * CMP_STRIDE // SLC_BLOCK
    j = np.arange(n_slc)
    lo = np.clip((j * SLC_BLOCK - CMP_BLOCK) // CMP_STRIDE + 1, 0, n_cmp)
    hi = np.clip(-((-(j * SLC_BLOCK + SLC_BLOCK)) // CMP_STRIDE), 0, n_cmp)
    m = np.zeros((CMP_PAD + nc + SUBLANES, nbp), np.float32)
    n = np.arange(nc)[:, None]
    m[CMP_PAD:CMP_PAD + nc, :n_slc] = (n >= lo[None, :]) & (n < hi[None, :])
    return m


def _cmp_attn(q64, kv_c, tab_cmp, far_col, nbp):
    bsz, _, _, s, _ = q64.shape
    rows_c = kv_c.shape[3]
    nc = rows_c - CMP_PAD - SUBLANES
    nq = s // Q_BLOCK
    m_pad = jnp.asarray(_overlap_matrix(nc, nbp))
    return pl.pallas_call(
        _cmp_attn_kernel,
        grid=(bsz, N_KV, nq),
        in_specs=[pl.BlockSpec((1, 1, N_GRP, Q_BLOCK, HEAD_DIM), lambda b, k, i: (b, k, 0, i, 0)),
                  pl.BlockSpec((1, 1, 1, rows_c, HEAD_DIM), lambda b, k, i: (b, 0, k, 0, 0)),
                  pl.BlockSpec((1, 1, 1, rows_c, HEAD_DIM), lambda b, k, i: (b, 1, k, 0, 0)),
                  pl.BlockSpec((1, N_GRP, Q_BLOCK, CMP_NEAR), lambda b, k, i: (k, 0, 0, 0)),
                  pl.BlockSpec((1, N_GRP * Q_BLOCK, 1), lambda b, k, i: (k, 0, 0)),
                  pl.BlockSpec((rows_c, nbp), lambda b, k, i: (0, 0))],
        out_specs=[pl.BlockSpec((1, 1, N_GRP, Q_BLOCK, HEAD_DIM), lambda b, k, i: (b, k, 0, i, 0)),
                   pl.BlockSpec((1, 1, 1, nbp, Q_BLOCK), lambda b, k, i: (b, k, i, 0, 0))],
        out_shape=[jax.ShapeDtypeStruct((bsz, N_KV, N_GRP, s, HEAD_DIM), F32),
                   jax.ShapeDtypeStruct((bsz, N_KV, nq, nbp, Q_BLOCK), BF16)],
        scratch_shapes=[pltpu.VMEM((Q_BLOCK, nbp), F32)],
        compiler_params=_cparams(("parallel", "parallel", "arbitrary")),
        name="nsa_cmp_attn",
    )(q64, kv_c, kv_c, tab_cmp, far_col, m_pad)


def _slc_attn_kernel(qt_ref, far_ref, ka_ref, vt_ref, selt_ref, tab_ref, o_ref,
                     qa_ref, s0_ref, s1_ref, x0_ref, x1_ref, m_ref, acc_ref):
    i = pl.program_id(2)
    cols = N_GRP * Q_BLOCK
    ngroups = qa_ref.shape[0] // 2
    qt = qt_ref[0, 0, 0]
    qt_far = qt + far_ref[0]
    selneg = ((selt_ref[0, 0, 0].astype(F32) - 1.0) * MASK_BIG).astype(BF16)
    for g in range(ngroups):
        part = selneg[g * SLC_GROUP_BLOCKS:(g + 1) * SLC_GROUP_BLOCKS, :]
        part = jnp.concatenate([part] * N_GRP, axis=1)
        qa_ref[2 * g] = jnp.concatenate([qt_far, part], axis=0)
        qa_ref[2 * g + 1] = jnp.concatenate([qt, part], axis=0)

    tiles_per_group = SLC_GROUP_BLOCKS // SLC_TILE_BLOCKS
    sub = SLC_KEY_TILE // Q_BLOCK
    last_tile = vt_ref.shape[2] - 1
    n_pairs = (i // sub + 2) // 2
    n_far = jnp.maximum(0, (i - (SLC_NEAR_CHUNKS - 1)) // sub)
    far_pairs = jnp.maximum(0, (n_far - 1) // 2)
    first_table = jnp.where(n_far > 0, 2 * far_pairs + 1, 0)

    def produce(kt, ls, s_ref, mx_ref, with_table):
        near = (kt >= first_table).astype(jnp.int32)
        kc = jnp.minimum(kt, last_tile)
        ks = pl.multiple_of(kc * SLC_KEY_TILE, SLC_KEY_TILE)
        s = jnp.dot(ka_ref[0, 0, pl.ds(ks, SLC_KEY_TILE), :], qa_ref[2 * (kc // tiles_per_group) + near, :, ls],
                    preferred_element_type=F32)
        if with_table:
            chunks = []
            for a in range(sub):
                mm = i - sub * kt - a
                idx = jnp.where(mm < 0, SLC_TAB_MASKED, jnp.where(mm >= SLC_NEAR_CHUNKS, SLC_TAB_CONST, mm))
                chunks.append(tab_ref[0, idx, :, ls])
            s = s + jnp.concatenate(chunks, axis=0)
        s_ref[:, ls] = s
        mx_ref[:, ls] = jnp.max(s, axis=0, keepdims=True)

    def consume(kt, ls, s_ref, mx_ref):
        m_old = m_ref[:, ls]
        m_new = jnp.maximum(m_old, mx_ref[:, ls])
        m_ref[:, ls] = m_new
        p = jnp.exp(s_ref[:, ls] - m_new).astype(BF16)
        acc_ref[:, ls] = jnp.exp(m_old - m_new) * acc_ref[:, ls] + jnp.dot(
            vt_ref[0, 0, jnp.minimum(kt, last_tile)], p, preferred_element_type=F32)

    strips = [slice(c * SLC_LANE_STRIP, (c + 1) * SLC_LANE_STRIP) for c in range(cols // SLC_LANE_STRIP)]

    def pair_step(j, with_table):
        for ls in strips:
            produce(2 * j + 1, ls, s1_ref, x1_ref, with_table)
            consume(2 * j, ls, s0_ref, x0_ref)
        for ls in strips:
            produce(2 * j + 2, ls, s0_ref, x0_ref, with_table)
            consume(2 * j + 1, ls, s1_ref, x1_ref)

    m_ref[...] = jnp.full(m_ref.shape, -jnp.inf, F32)
    acc_ref[...] = jnp.zeros(acc_ref.shape, F32)

    @pl.when(n_far > 0)
    def _():
        for ls in strips:
            produce(0, ls, s0_ref, x0_ref, False)

    @pl.when(n_far == 0)
    def _():
        for ls in strips:
            produce(0, ls, s0_ref, x0_ref, True)

    def far_body(j, carry):
        pair_step(j, False)
        return carry

    def near_body(j, carry):
        pair_step(j, True)
        return carry

    lax.fori_loop(0, far_pairs, far_body, 0)
    lax.fori_loop(far_pairs, n_pairs, near_body, 0)
    o_ref[0, 0, 0] = acc_ref[:HEAD_DIM, :] / acc_ref[HEAD_DIM:HEAD_DIM + 1, :]


def _slc_attn(qt, far_t, k_aug, vt_aug, sel_t, tab_t):
    bsz, _, nq, _, cols = qt.shape
    s = k_aug.shape[2]
    nbp = sel_t.shape[3]
    ngroups = nbp // SLC_GROUP_BLOCKS
    nch = tab_t.shape[1]
    once = pl.Buffered(1)
    return pl.pallas_call(
        _slc_attn_kernel,
        grid=(bsz, N_KV, nq),
        in_specs=[pl.BlockSpec((1, 1, 1, LANES, cols), lambda b, k, i: (b, k, i, 0, 0)),
                  pl.BlockSpec((1, LANES, cols), lambda b, k, i: (k, 0, 0)),
                  pl.BlockSpec((1, 1, s, 2 * LANES), lambda b, k, i: (b, k, 0, 0), pipeline_mode=once),
                  pl.BlockSpec((1, 1, s // SLC_KEY_TILE, SLC_V_ROWS, SLC_KEY_TILE),
                               lambda b, k, i: (b, k, 0, 0, 0), pipeline_mode=once),
                  pl.BlockSpec((1, 1, 1, nbp, Q_BLOCK), lambda b, k, i: (b, k, i, 0, 0)),
                  pl.BlockSpec((1, nch, Q_BLOCK, cols), lambda b, k, i: (k, 0, 0, 0), pipeline_mode=once)],
        out_specs=pl.BlockSpec((1, 1, 1, HEAD_DIM, cols), lambda b, k, i: (b, k, i, 0, 0)),
        out_shape=jax.ShapeDtypeStruct((bsz, N_KV, nq, HEAD_DIM, cols), F32),
        scratch_shapes=[pltpu.VMEM((2 * ngroups, 2 * LANES, cols), BF16),
                        pltpu.VMEM((SLC_KEY_TILE, cols), F32),
                        pltpu.VMEM((SLC_KEY_TILE, cols), F32),
                        pltpu.VMEM((1, cols), F32),
                        pltpu.VMEM((1, cols), F32),
                        pltpu.VMEM((1, cols), F32),
                        pltpu.VMEM((SLC_V_ROWS, cols), F32)],
        compiler_params=_cparams(("parallel", "parallel", "arbitrary")),
        name="nsa_slc_attn",
    )(qt, far_t, k_aug, vt_aug, sel_t, tab_t)


def _win_attn_kernel(q_ref, k_ref, v_ref, tab_ref, o_ref):
    i = pl.program_id(2)
    rows = N_GRP * Q_BLOCK
    nw = WINDOW + Q_BLOCK
    q = q_ref[0, 0].reshape(rows, HEAD_DIM)
    qs = pl.multiple_of(i * Q_BLOCK, Q_BLOCK)
    k = k_ref[0, 0, pl.ds(qs, nw), :]
    v = v_ref[0, 0, pl.ds(qs, nw), :]
    s = _nt(q, k) + tab_ref[0].reshape(rows, nw)
    col = lax.broadcasted_iota(jnp.int32, (1, nw), 1)
    s = jnp.where(col >= WINDOW - i * Q_BLOCK, s, NEG_INF)
    m = jnp.max(s, axis=-1, keepdims=True)
    e = jnp.exp(s - m)
    p = e / jnp.sum(e, axis=-1, keepdims=True)
    o = jnp.dot(p.astype(BF16), v, preferred_element_type=F32)
    o_ref[0, 0] = o.reshape(N_GRP, Q_BLOCK, HEAD_DIM)


def _win_attn(q64, k_win, v_win, tab_win):
    bsz, _, _, s, _ = q64.shape
    nq = s // Q_BLOCK
    sp = k_win.shape[2]
    nw = WINDOW + Q_BLOCK
    return pl.pallas_call(
        _win_attn_kernel,
        grid=(bsz, N_KV, nq),
        in_specs=[pl.BlockSpec((1, 1, N_GRP, Q_BLOCK, HEAD_DIM), lambda b, k, i: (b, k, 0, i, 0)),
                  pl.BlockSpec((1, 1, sp, HEAD_DIM), lambda b, k, i: (b, k, 0, 0)),
                  pl.BlockSpec((1, 1, sp, HEAD_DIM), lambda b, k, i: (b, k, 0, 0)),
                  pl.BlockSpec((1, N_GRP, Q_BLOCK, nw), lambda b, k, i: (k, 0, 0, 0))],
        out_specs=pl.BlockSpec((1, 1, N_GRP, Q_BLOCK, HEAD_DIM), lambda b, k, i: (b, k, 0, i, 0)),
        out_shape=jax.ShapeDtypeStruct((bsz, N_KV, N_GRP, s, HEAD_DIM), F32),
        compiler_params=_cparams(("parallel", "parallel", "arbitrary")),
        name="nsa_win_attn",
    )(q64, k_win, v_win, tab_win)


def _ssd_kernel(xbc_ref, z_ref, dt_ref, cw_ref, cb_ref, dtb_ref, alog_ref, dskip_ref, nw_ref, o_ref,
                ext_ref, state_ref):
    ln = SSM_CHUNK
    d_ssm = z_ref.shape[2]
    nh = dtb_ref.shape[1]
    gw = d_ssm // SSM_GROUPS
    hpg = nh // SSM_GROUPS
    gn = SSM_GROUPS * SSM_STATE

    @pl.when(pl.program_id(1) == 0)
    def _():
        ext_ref[0:SUBLANES, :] = jnp.zeros((SUBLANES, ext_ref.shape[1]), F32)
        state_ref[...] = jnp.zeros(state_ref.shape, F32)

    ext_ref[SUBLANES:SUBLANES + ln, :] = xbc_ref[0]
    conv = cw_ref[0:1, :] * ext_ref[SUBLANES - CONV_WIDTH + 1:SUBLANES - CONV_WIDTH + 1 + ln, :]
    for k in range(1, CONV_WIDTH):
        lo = SUBLANES - CONV_WIDTH + 1 + k
        conv = conv + cw_ref[k:k + 1, :] * ext_ref[lo:lo + ln, :]
    conv = conv + cb_ref[...]
    ext_ref[0:SUBLANES, :] = xbc_ref[0, ln - SUBLANES:ln, :]
    xc = _silu(conv)
    xs = xc[:, :d_ssm]
    bm = xc[:, d_ssm:d_ssm + gn]
    cm = xc[:, d_ssm + gn:d_ssm + 2 * gn]

    xdt = dt_ref[0, :, :nh] + dtb_ref[...]
    dt = jnp.maximum(xdt, 0.0) + jnp.log1p(jnp.exp(-jnp.abs(xdt)))
    a = -jnp.exp(alog_ref[...])
    da = dt * a

    row = lax.broadcasted_iota(jnp.int32, (ln, ln), 0)
    colm = lax.broadcasted_iota(jnp.int32, (ln, ln), 1)
    causal = row >= colm
    acs = jnp.dot(causal.astype(F32), da, precision=HI, preferred_element_type=F32)
    eye = (lax.broadcasted_iota(jnp.int32, (2 * nh, 2 * nh), 0)
           == lax.broadcasted_iota(jnp.int32, (2 * nh, 2 * nh), 1)).astype(F32)
    rows_t = _nt(eye, jnp.concatenate([acs, dt], axis=1), precision=HI)
    expand = (lax.broadcasted_iota(jnp.int32, (nh, d_ssm), 0)
              == lax.broadcasted_iota(jnp.int32, (nh, d_ssm), 1) // SSM_HEAD_DIM).astype(F32)
    last = acs[ln - 1:ln, :]
    exp_acs_x = jnp.dot(jnp.exp(acs), expand, precision=HI, preferred_element_type=F32)
    w_x = jnp.dot(jnp.exp(last - acs) * dt, expand, precision=HI, preferred_element_type=F32)
    exp_last_x = exp_acs_x[ln - 1:ln, :]

    ys = []
    for g in range(SSM_GROUPS):
        cg = cm[:, g * SSM_STATE:(g + 1) * SSM_STATE].astype(BF16)
        bg32 = bm[:, g * SSM_STATE:(g + 1) * SSM_STATE]
        bg = bg32.astype(BF16)
        xg = xs[:, g * gw:(g + 1) * gw]
        cb = _nt(cg, bg)
        st = state_ref[g]
        y_state = jnp.dot(cg, st.astype(BF16), preferred_element_type=F32) * exp_acs_x[:, g * gw:(g + 1) * gw]
        y_heads = []
        for j in range(hpg):
            h = g * hpg + j
            seg = acs[:, h:h + 1] - rows_t[h:h + 1, :]
            decay = jnp.exp(jnp.where(causal, seg, -jnp.inf))
            mmat = cb * decay * rows_t[nh + h:nh + h + 1, :]
            xh = xg[:, j * SSM_HEAD_DIM:(j + 1) * SSM_HEAD_DIM].astype(BF16)
            y_heads.append(jnp.dot(mmat.astype(BF16), xh, preferred_element_type=F32))
        ys.append(jnp.concatenate(y_heads, axis=1) + y_state)
        xw = (xg * w_x[:, g * gw:(g + 1) * gw]).astype(BF16)
        state_ref[g] = st * exp_last_x[:, g * gw:(g + 1) * gw] + jnp.dot(
            bg32.T.astype(BF16), xw, preferred_element_type=F32)

    y = jnp.concatenate(ys, axis=1) + dskip_ref[...] * xs
    y = y * _silu(z_ref[0])
    outs = []
    for g in range(SSM_GROUPS):
        yg = y[:, g * gw:(g + 1) * gw]
        outs.append(yg * lax.rsqrt(jnp.mean(yg * yg, axis=-1, keepdims=True) + RMS_EPS))
    o_ref[0] = jnp.concatenate(outs, axis=1) * nw_ref[...]


def _ssd(proj, xbc_blk, z_blk, dt_blk, conv_w, conv_b, dt_bias, a_log, d_skip, norm_w):
    bsz, s, _ = proj.shape
    ch = xbc_blk[1]
    d_ssm = z_blk[1]
    nh = dt_bias.shape[0]
    gw = d_ssm // SSM_GROUPS
    nchunks = s // SSM_CHUNK
    full = lambda shape: pl.BlockSpec(shape, lambda b, c: (0,) * len(shape))
    cols = lambda blk: pl.BlockSpec((1, SSM_CHUNK, blk[1]), lambda b, c: (b, c, blk[0]))
    xbc = z = dt_raw = proj
    return pl.pallas_call(
        _ssd_kernel,
        grid=(bsz, nchunks),
        in_specs=[cols(xbc_blk), cols(z_blk), cols(dt_blk),
                  full((CONV_WIDTH, ch)), full((1, ch)), full((1, nh)), full((1, nh)),
                  full((1, d_ssm)), full((1, d_ssm))],
        out_specs=pl.BlockSpec((1, SSM_CHUNK, d_ssm), lambda b, c: (b, c, 0)),
        out_shape=jax.ShapeDtypeStruct((bsz, s, d_ssm), F32),
        scratch_shapes=[pltpu.VMEM((SUBLANES + SSM_CHUNK, ch), F32),
                        pltpu.VMEM((SSM_GROUPS, SSM_STATE, gw), F32)],
        compiler_params=_cparams(("parallel", "arbitrary")),
        name="ssd_scan",
    )(xbc, z, dt_raw, conv_w, conv_b.reshape(1, ch), dt_bias.reshape(1, nh), a_log.reshape(1, nh),
      jnp.repeat(d_skip, SSM_HEAD_DIM).reshape(1, d_ssm), norm_w.reshape(1, d_ssm))


def _mix_out_kernel(oc_ref, os_ref, ow_ref, gl_ref, ex_ref, an_ref, ssm_ref, w_ref, x_ref, g1_ref, o_ref):
    d_attn = oc_ref.shape[2]
    sig = jax.nn.sigmoid(gl_ref[0])
    pieces, rest = [], sig
    for _ in range(GATE_PIECES):
        piece = rest.astype(BF16)
        pieces.append(piece)
        rest = rest - piece.astype(F32)
    sig3 = jnp.concatenate(pieces, axis=1)
    gc = jnp.dot(sig3, ex_ref[0], preferred_element_type=F32)
    gs = jnp.dot(sig3, ex_ref[1], preferred_element_type=F32)
    gw = jnp.dot(sig3, ex_ref[2], preferred_element_type=F32)
    attn = gc * oc_ref[0] + gs * os_ref[0] + gw * ow_ref[0]
    attn = attn * lax.rsqrt(jnp.mean(attn * attn, axis=-1, keepdims=True) + RMS_EPS) * an_ref[...]
    mix = (jnp.dot(attn.astype(BF16), w_ref[:d_attn, :], preferred_element_type=F32)
           + jnp.dot(ssm_ref[0].astype(BF16), w_ref[d_attn:, :], preferred_element_type=F32))
    o_ref[0] = x_ref[0] + g1_ref[0] * mix


def _gate_expand():
    ex = np.zeros((3, LANES, N_HEADS * HEAD_DIM), np.float32)
    for r in range(3):
        for h in range(N_HEADS):
            ex[r, h * 3 + r, h * HEAD_DIM:(h + 1) * HEAD_DIM] = 1.0
    return np.concatenate([ex] * GATE_PIECES, axis=1)


def _mix_out(oc, os_, ow, gl, gl_blk, attn_norm, ssm, w_out_bf16, x, g1, tm=256):
    bsz, s, d = x.shape
    d_attn = oc.shape[2]
    d_ssm = ssm.shape[2]
    tok = lambda w: pl.BlockSpec((1, tm, w), lambda b, i: (b, i, 0))
    return pl.pallas_call(
        _mix_out_kernel,
        grid=(bsz, s // tm),
        in_specs=[tok(d_attn), tok(d_attn), tok(d_attn),
                  pl.BlockSpec((1, tm, gl_blk[1]), lambda b, i: (b, i, gl_blk[0])),
                  pl.BlockSpec((3, GATE_PIECES * LANES, d_attn), lambda b, i: (0, 0, 0)),
                  pl.BlockSpec((1, d_attn), lambda b, i: (0, 0)),
                  tok(d_ssm),
                  pl.BlockSpec((d_attn + d_ssm, d), lambda b, i: (0, 0)),
                  tok(d),
                  pl.BlockSpec((1, 1, d), lambda b, i: (b, 0, 0))],
        out_specs=tok(d),
        out_shape=jax.ShapeDtypeStruct((bsz, s, d), F32),
        compiler_params=_cparams(("parallel", "parallel")),
        name="mix_out",
    )(oc, os_, ow, gl, jnp.asarray(_gate_expand(), BF16), attn_norm.reshape(1, d_attn), ssm, w_out_bf16, x, g1)


def _peer_candidates():
    k, sub = PEER_TOPK, SUBLANES
    cells = [(0, b) for b in range(k)]
    cells += [(a, b) for a in range(1, sub) for b in range(sub)]
    cells += [(a, 0) for a in range(sub, k)]
    order = np.array([a * k + b if (a + 1) * (b + 1) <= k else -1 for a, b in cells], np.float32)
    group = np.zeros((k, LANES), np.float32)
    for row, (a, _) in enumerate(cells):
        group[a, row] = 1.0
    return cells, order, group


def _peer_route_kernel(q_ref, sk_ref, order_ref, group_ref, rank2_ref, e2_ref, n1_ref, c1_ref):
    k = PEER_TOPK

    def route(assume_distinct):
        q = q_ref[...]
        half = q.shape[1] // 2
        s1 = _nt(sk_ref[0], q[:, :half], precision=HI)
        s2 = _nt(sk_ref[1], q[:, half:], precision=HI)
        rank1, v1 = _topk_mark(s1, k, axis=0, assume_distinct=assume_distinct)
        rank2, v2 = _topk_mark(s2, k, axis=0, assume_distinct=assume_distinct)
        v1_all = jnp.concatenate(v1, axis=0)
        v2_all = jnp.concatenate(v2, axis=0)
        cand = jnp.concatenate([v1[0] + v2_all] + [v1[a] + v2_all[:SUBLANES] for a in range(1, SUBLANES)]
                               + [v1_all[SUBLANES:] + v2[0]], axis=0)
        order = order_ref[...]
        cand = jnp.where(order >= 0.0, cand, -jnp.inf)
        rank_c, best = _topk_mark(cand, k, axis=0, order=order, order_bound=float(k * k),
                                  assume_distinct=assume_distinct)
        chosen = jnp.where(rank_c < float(k), 1.0, 0.0).astype(BF16)
        chosen = jnp.concatenate([chosen, jnp.zeros((LANES - chosen.shape[0], chosen.shape[1]), BF16)], axis=0)
        count = jnp.dot(group_ref[...], chosen, preferred_element_type=F32)
        z = best[0] * 0.0
        for r in range(k):
            z = z + jnp.exp(best[r] - best[0])
        n1 = jnp.zeros(s1.shape, F32)
        for a in range(k):
            n1 = jnp.where(rank1 == float(a), count[a:a + 1, :], n1)
        rank2_ref[0] = rank2.astype(BF16)
        e2_ref[0] = jnp.exp(s2 - v2[0]).astype(BF16)
        n1_ref[0] = n1
        c1_ref[0] = jnp.exp(s1 - v1[0]) / z
        return jnp.maximum(jnp.maximum(_topk_clean(rank1, k, 0), _topk_clean(rank2, k, 0)),
                           _topk_clean(rank_c, k, 0))

    dirty = route(True)

    @pl.when(dirty > 0.0)
    def _():
        route(False)


def _peer_route(q, subkeys, tm=256):
    t, width = q.shape
    kd = width // PEER_HEADS
    nk = subkeys.shape[1]
    spec = pl.BlockSpec((1, nk, tm), lambda i, h: (h, 0, i))
    shape = jax.ShapeDtypeStruct((PEER_HEADS, nk, t), F32)
    cells, order, group = _peer_candidates()
    order = jnp.asarray(np.broadcast_to(order[:, None], (len(cells), tm)))
    return pl.pallas_call(
        _peer_route_kernel,
        grid=(t // tm, PEER_HEADS),
        in_specs=[pl.BlockSpec((tm, kd), lambda i, h: (i, h)),
                  pl.BlockSpec((2, nk, kd // 2), lambda i, h: (0, 0, 0)),
                  pl.BlockSpec((len(cells), tm), lambda i, h: (0, 0)),
                  pl.BlockSpec((PEER_TOPK, LANES), lambda i, h: (0, 0))],
        out_specs=[spec, spec, spec, spec],
        out_shape=[jax.ShapeDtypeStruct(shape.shape, BF16), jax.ShapeDtypeStruct(shape.shape, BF16), shape, shape],
        compiler_params=_cparams(("parallel", "parallel")),
        name="peer_route",
    )(q, subkeys, order, jnp.asarray(group, BF16))


def _peer_dense_kernel(ht_ref, u0_ref, un_ref, vt_ref, rank2_ref, e2_ref, n1_ref, c1_ref, o_ref, *scratch):
    j = pl.program_id(1)
    te = un_ref.shape[0]
    nk = rank2_ref.shape[1]
    tm = ht_ref.shape[1]
    strips = [slice(c * PEER_LANE_STRIP, (c + 1) * PEER_LANE_STRIP) for c in range(tm // PEER_LANE_STRIP)]

    units = [(e, c) for e in range(te // PEER_EXPERT_UNIT) for c in range(len(strips))]
    act_sets = (scratch[:len(units)], scratch[len(units):2 * len(units)])
    aw_refs = scratch[2 * len(units):]
    n_piece = PEER_EXPERT_UNIT // PEER_EXPERT_PIECE
    d_piece = o_ref.shape[0] // n_piece

    def produce(u_ref, act_refs, k, p):
        e, c = units[k]
        lo = e * PEER_EXPERT_UNIT + p * PEER_EXPERT_PIECE
        act_refs[k][p * PEER_EXPERT_PIECE:(p + 1) * PEER_EXPERT_PIECE, :] = jax.nn.gelu(
            jnp.dot(u_ref[lo:lo + PEER_EXPERT_PIECE, :], ht_ref[:, strips[c]], preferred_element_type=F32)
        ).astype(BF16)

    def gate(act_refs, k, r):
        e, c = units[k]
        ls = strips[c]
        i1 = (j * te + e * PEER_EXPERT_UNIT) // nk + r
        w = jnp.zeros((nk, PEER_LANE_STRIP), BF16)
        zero = jnp.zeros((nk, PEER_LANE_STRIP), BF16)
        for h in range(PEER_HEADS):
            n_row = n1_ref[h, pl.ds(i1, 1), ls].astype(BF16)
            c_row = c1_ref[h, pl.ds(i1, 1), ls].astype(BF16)
            w = w + jnp.where(rank2_ref[h, :, ls] < n_row, e2_ref[h, :, ls], zero) * c_row
        aw_refs[k][r * nk:(r + 1) * nk, :] = act_refs[k][r * nk:(r + 1) * nk, :] * w

    def combine(k, m):
        e, c = units[k]
        ds_ = slice(m * d_piece, (m + 1) * d_piece)
        es = slice(e * PEER_EXPERT_UNIT, (e + 1) * PEER_EXPERT_UNIT)
        o_ref[ds_, strips[c]] += jnp.dot(vt_ref[ds_, es], aw_refs[k][...], preferred_element_type=F32)

    @pl.when(j == 0)
    def _():
        o_ref[...] = jnp.zeros(o_ref.shape, F32)
        for k in range(len(units)):
            for p in range(n_piece):
                produce(u0_ref, act_sets[0], k, p)

    gates_per_piece = PEER_EXPERT_PIECE // nk

    def step(cur, nxt):
        for k in range(len(units)):
            for p in range(n_piece):
                produce(un_ref, nxt, k, p)
                for r in range(p * gates_per_piece, (p + 1) * gates_per_piece):
                    gate(cur, k, r)
                    if k > 0 and r == p * gates_per_piece:
                        combine(k - 1, p)
        for m in range(n_piece):
            combine(len(units) - 1, m)

    pl.when(j % 2 == 0)(functools.partial(step, act_sets[0], act_sets[1]))
    pl.when(j % 2 == 1)(functools.partial(step, act_sets[1], act_sets[0]))


def _peer_dense(h_t, u_bf16, v_t_bf16, rank2, e2, n1, c1, tm=512, te=512):
    d, t = h_t.shape
    n_exp = u_bf16.shape[0]
    nk = rank2.shape[1]
    n_units = (te // PEER_EXPERT_UNIT) * (tm // PEER_LANE_STRIP)
    n_chunks = n_exp // te
    assert n_chunks % 2 == 0
    route = pl.BlockSpec((PEER_HEADS, nk, tm), lambda i, j: (0, 0, i))
    return pl.pallas_call(
        _peer_dense_kernel,
        grid=(t // tm, n_chunks),
        in_specs=[pl.BlockSpec((d, tm), lambda i, j: (0, i)),
                  pl.BlockSpec((te, d), lambda i, j: (0, 0)),
                  pl.BlockSpec((te, d), lambda i, j: (jnp.minimum(j + 1, n_chunks - 1), 0)),
                  pl.BlockSpec((d, te), lambda i, j: (0, j)),
                  route, route, route, route],
        out_specs=pl.BlockSpec((d, tm), lambda i, j: (0, i)),
        out_shape=jax.ShapeDtypeStruct((d, t), F32),
        scratch_shapes=[pltpu.VMEM((PEER_EXPERT_UNIT, PEER_LANE_STRIP), BF16)] * (3 * n_units),
        compiler_params=_cparams(("parallel", "arbitrary")),
        name="peer_dense",
    )(h_t, u_bf16, u_bf16, v_t_bf16, rank2, e2, n1, c1)


def _residual_kernel(x_ref, yt_ref, g_ref, o_ref):
    o_ref[0] = x_ref[0] + g_ref[0] * yt_ref[...].T


def _residual(x, y_t, g, tm=512):
    bsz, s, d = x.shape
    tok = pl.BlockSpec((1, tm, d), lambda b, i: (b, i, 0))
    return pl.pallas_call(
        _residual_kernel,
        grid=(bsz, s // tm),
        in_specs=[tok, pl.BlockSpec((d, tm), lambda b, i: (0, b * (s // tm) + i)),
                  pl.BlockSpec((1, 1, d), lambda b, i: (b, 0, 0))],
        out_specs=tok,
        out_shape=jax.ShapeDtypeStruct((bsz, s, d), F32),
        compiler_params=_cparams(("parallel", "parallel")),
        name="residual",
    )(x, y_t, g)


def _final_norm_kernel(x_ref, w_ref, o_ref):
    x = x_ref[0]
    o_ref[0] = x * lax.rsqrt(jnp.mean(x * x, axis=-1, keepdims=True) + RMS_EPS) * w_ref[...]


def _final_norm(x, w, tm=512):
    bsz, s, d = x.shape
    tok = pl.BlockSpec((1, tm, d), lambda b, i: (b, i, 0))
    return pl.pallas_call(
        _final_norm_kernel,
        grid=(bsz, s // tm),
        in_specs=[tok, pl.BlockSpec((1, d), lambda b, i: (0, 0))],
        out_specs=tok,
        out_shape=jax.ShapeDtypeStruct((bsz, s, d), F32),
        compiler_params=_cparams(("parallel", "parallel")),
        name="final_norm",
    )(x, w.reshape(1, d))


def _pad_cols(w, width):
    return jnp.pad(w, ((0, 0), (0, width - w.shape[1])))


def _nsa(q, kv, tables, cmp_pe, cmp_w1, cmp_w2):
    tab_win, tab_cmp, tab_slc, far_col, far_t = tables
    bsz, s, _ = q.shape
    nc = s // CMP_STRIDE
    nq = s // Q_BLOCK
    n_slc = s // SLC_BLOCK
    nbp = -(-n_slc // SLC_GROUP_BLOCKS) * SLC_GROUP_BLOCKS
    scale = HEAD_DIM ** -0.5
    q6 = (q * scale).astype(BF16).reshape(bsz, nq, Q_BLOCK, N_KV, N_GRP, HEAD_DIM)
    qh = q6.transpose(0, 3, 4, 1, 2, 5).reshape(bsz, N_KV, N_GRP, s, HEAD_DIM)
    qt = q6.transpose(0, 3, 1, 5, 4, 2).reshape(bsz, N_KV, nq, HEAD_DIM, N_GRP * Q_BLOCK)
    qt = jnp.pad(qt, ((0, 0),) * 3 + ((0, LANES - HEAD_DIM), (0, 0)))
    kv6 = kv.reshape(bsz, s, 6, N_KV, HEAD_DIM)

    kv_cmp = kv6[:, :, 0:2].reshape(bsz, nc, CMP_STRIDE, 2, N_KV, HEAD_DIM)
    kv_cmp = kv_cmp.transpose(0, 3, 4, 1, 2, 5).reshape(bsz, 2, N_KV, nc, CMP_STRIDE * HEAD_DIM)
    kv_c = _compress(kv_cmp, cmp_pe, cmp_w1, cmp_w2)
    o_c, sel_t = _cmp_attn(qh, kv_c, tab_cmp, far_col, nbp)

    k_slc = kv6[:, :, 2].transpose(0, 2, 1, 3).astype(BF16)
    nkt = s // SLC_KEY_TILE
    vt = kv6[:, :, 3].astype(BF16).reshape(bsz, nkt, SLC_KEY_TILE, N_KV, HEAD_DIM).transpose(0, 3, 1, 4, 2)
    vt_aug = jnp.concatenate([
        vt, jnp.ones((bsz, N_KV, nkt, 1, SLC_KEY_TILE), BF16),
        jnp.zeros((bsz, N_KV, nkt, SLC_V_ROWS - HEAD_DIM - 1, SLC_KEY_TILE), BF16)], axis=3)
    blk = np.arange(s) // SLC_BLOCK
    onehot = (blk[:, None] % SLC_GROUP_BLOCKS == np.arange(SLC_GROUP_BLOCKS)[None, :]).astype(np.float32)
    k_aug = jnp.concatenate([
        k_slc, jnp.ones((bsz, N_KV, s, SLC_BIAS_PIECES), BF16),
        jnp.zeros((bsz, N_KV, s, LANES - HEAD_DIM - SLC_BIAS_PIECES), BF16),
        jnp.broadcast_to(jnp.asarray(onehot, BF16), (bsz, N_KV, s, SLC_GROUP_BLOCKS))], axis=-1)
    o_st = _slc_attn(qt, far_t, k_aug, vt_aug, sel_t, tab_slc)
    o_s = o_st.reshape(bsz, N_KV, nq, HEAD_DIM, N_GRP, Q_BLOCK).transpose(0, 2, 5, 1, 4, 3)
    o_s = o_s.reshape(bsz, s, N_HEADS * HEAD_DIM)

    front = ((0, 0), (0, 0), (WINDOW, 0), (0, 0))
    k_win = jnp.pad(kv6[:, :, 4].transpose(0, 2, 1, 3).astype(BF16), front)
    v_win = jnp.pad(kv6[:, :, 5].transpose(0, 2, 1, 3).astype(BF16), front)
    o_w = _win_attn(qh, k_win, v_win, tab_win)

    back = lambda o: o.transpose(0, 3, 1, 2, 4).reshape(bsz, s, N_HEADS * HEAD_DIM)
    return back(o_c), o_s, back(o_w)


def kernel(x, c, ada_w, ada_b, norm_mix, norm_ffn, w_in, cmp_pe, cmp_w1, cmp_w2, rel_bias, attn_out_norm,
           conv_w, conv_b, dt_bias, a_log, d_skip, ssm_norm, w_out, peer_wq, peer_subkeys, peer_u, peer_v,
           norm_final):
    bsz, s, d = x.shape
    depth = ada_w.shape[0]
    d_attn = N_HEADS * HEAD_DIM
    n_kv = 6 * N_KV * HEAD_DIM
    n_gate = 3 * N_HEADS
    d_ssm = ssm_norm.shape[1]
    ch = conv_w.shape[2]
    nh = dt_bias.shape[1]

    mod = _ada_mod(c, ada_w, ada_b)
    tables = _bias_tables(rel_bias)

    for l in range(depth):
        sh1, sc1, g1, sh2, sc2, g2 = [mod[l, :, i * d:(i + 1) * d].reshape(bsz, 1, d) for i in range(6)]
        cuts = np.cumsum([0, d_attn, n_kv, n_gate, d_ssm, ch, nh])
        seg = dict(zip(("q", "kv", "gl", "z", "xbc", "dt"),
                       [w_in[l][:, cuts[i]:cuts[i + 1]] for i in range(6)]))
        seg["gl"] = _pad_cols(seg["gl"], LANES)
        seg["dt"] = _pad_cols(seg["dt"], LANES)
        order = ("q", "z", "kv", "gl", "dt", "xbc")
        w_cat = jnp.concatenate([seg[name] for name in order], axis=1).astype(BF16)
        start = dict(zip(order, np.cumsum([0] + [seg[name].shape[1] for name in order])[:-1]))
        blk = {name: (int(start[name]) // seg[name].shape[1], seg[name].shape[1]) for name in order}
        assert all(start[name] % seg[name].shape[1] == 0 for name in ("z", "gl", "dt", "xbc"))
        proj, _ = _norm_proj(x, norm_mix[l], sc1, sh1, w_cat)
        q = proj[:, :, start["q"]:start["q"] + d_attn]
        kv = proj[:, :, start["kv"]:start["kv"] + n_kv]

        o_c, o_s, o_w = _nsa(q, kv, tables, cmp_pe[l], cmp_w1[l], cmp_w2[l])
        ssm = _ssd(proj, blk["xbc"], blk["z"], blk["dt"], conv_w[l], conv_b[l], dt_bias[l], a_log[l], d_skip[l],
                   ssm_norm[l])
        x = _mix_out(o_c, o_s, o_w, proj, blk["gl"], attn_out_norm[l], ssm, w_out[l].astype(BF16), x, g1)

        pq, h2 = _norm_proj(x, norm_ffn[l], sc2, sh2, peer_wq[l].astype(BF16))
        rank2, e2, n1, c1 = _peer_route(pq.reshape(bsz * s, -1), peer_subkeys[l])
        h_t = h2.reshape(bsz * s, d).T
        ffn_t = _peer_dense(h_t, peer_u[l].astype(BF16), peer_v[l].astype(BF16).T, rank2, e2, n1, c1)
        x = _residual(x, ffn_t, g2)

    return _final_norm(x, norm_final)
```

```python
import functools
import math

import numpy as np
import jax
import jax.numpy as jnp
from jax import lax
from jax.experimental import pallas as pl
from jax.experimental.pallas import tpu as pltpu

F32 = jnp.float32
BF16 = jnp.bfloat16
HI = lax.Precision.HIGHEST

N_HEADS = 16
N_KV = 2
N_GRP = N_HEADS // N_KV
HEAD_DIM = 64
CMP_BLOCK = 32
CMP_STRIDE = 16
CMP_HIDDEN = 4 * HEAD_DIM
SLC_BLOCK = 64
SLC_TOPN = 16
WINDOW = 512
Q_BLOCK = 128
FORCE_SCORE = 1e4
NEG_INF = -1e30
REL_BUCKETS = 32
REL_MAX_DIST = 2048
SSM_HEAD_DIM = 64
SSM_GROUPS = 2
SSM_STATE = 128
CONV_WIDTH = 4
SSM_CHUNK = 256
PEER_HEADS = 8
PEER_NKEYS = 128
PEER_TOPK = 16
RMS_EPS = 1e-6

LANES = 128
SUBLANES = 8
VMEM_LIMIT = 56 * 1024 * 1024

SLC_KEY_TILE = 512
SLC_TILE_BLOCKS = SLC_KEY_TILE // SLC_BLOCK
SLC_GROUP_BLOCKS = LANES
SLC_NEAR_CHUNKS = 14
SLC_TAB_MASKED = 14
SLC_TAB_CONST = 15
MASK_BIG = 2.0 ** 100
SLC_LANE_STRIP = 256
PEER_LANE_STRIP = 256
PEER_EXPERT_PIECE = 256
PEER_EXPERT_UNIT = 512
SLC_V_ROWS = HEAD_DIM + 16
SLC_BIAS_PIECES = 3
GATE_PIECES = 3

CMP_FAR_STEP = 256
CMP_NEAR = 128
CMP_PAD = CMP_NEAR - Q_BLOCK // CMP_STRIDE


def _cparams(sem, vmem=VMEM_LIMIT):
    return pltpu.CompilerParams(dimension_semantics=sem, vmem_limit_bytes=vmem)


def _nt(a, b, precision=None):
    return lax.dot_general(a, b, (((1,), (1,)), ((), ())), precision=precision,
                           preferred_element_type=F32)


def _silu(x):
    return x * jax.nn.sigmoid(x)


def _rel_bucket_np(d):
    d = np.maximum(np.asarray(d, np.int64), 0)
    max_exact = REL_BUCKETS // 2
    ratio = np.log(np.maximum(d, max_exact).astype(np.float64) / max_exact) / math.log(REL_MAX_DIST / max_exact)
    scaled = ratio * (REL_BUCKETS - max_exact)
    large = max_exact + np.floor(scaled).astype(np.int64)
    return np.where(d < max_exact, d, np.minimum(large, REL_BUCKETS - 1)).astype(np.int32)


def _ada_kernel(c_ref, w_ref, b_ref, o_ref):
    cond = _silu(c_ref[...])
    o_ref[0] = jnp.dot(cond, w_ref[0], precision=HI, preferred_element_type=F32) + b_ref[0]


def _ada_mod(c, ada_w, ada_b):
    depth, d, n = ada_w.shape
    bsz = c.shape[0]
    rows = SUBLANES
    c_pad = jnp.zeros((rows, d), F32).at[:bsz].set(c)
    tn = 1024
    out = pl.pallas_call(
        _ada_kernel,
        grid=(depth, n // tn),
        in_specs=[pl.BlockSpec((rows, d), lambda l, j: (0, 0)),
                  pl.BlockSpec((1, d, tn), lambda l, j: (l, 0, j)),
                  pl.BlockSpec((1, 1, tn), lambda l, j: (l, 0, j))],
        out_specs=pl.BlockSpec((1, rows, tn), lambda l, j: (l, 0, j)),
        out_shape=jax.ShapeDtypeStruct((depth, rows, n), F32),
        compiler_params=_cparams(("parallel", "parallel")),
        name="ada_mod",
    )(c_pad, ada_w, ada_b.reshape(depth, 1, n))
    return out[:, :bsz]


def _norm_proj_kernel(x_ref, nw_ref, sc_ref, sh_ref, w_ref, o_ref, h_ref, hs_ref):
    @pl.when(pl.program_id(2) == 0)
    def _():
        x = x_ref[0]
        y = x * lax.rsqrt(jnp.mean(x * x, axis=-1, keepdims=True) + RMS_EPS)
        h = (y * nw_ref[...]) * (1.0 + sc_ref[0]) + sh_ref[0]
        hs_ref[...] = h.astype(BF16)
        h_ref[0] = h.astype(BF16)

    o_ref[0] = jnp.dot(hs_ref[...], w_ref[0], preferred_element_type=F32)


def _norm_proj(x, nw, sc, sh, w_bf16, tm=1024, tn=512):
    bsz, s, d = x.shape
    n = w_bf16.shape[1]
    return pl.pallas_call(
        _norm_proj_kernel,
        grid=(bsz, s // tm, n // tn),
        in_specs=[pl.BlockSpec((1, tm, d), lambda b, i, j: (b, i, 0)),
                  pl.BlockSpec((1, d), lambda b, i, j: (0, 0)),
                  pl.BlockSpec((1, 1, d), lambda b, i, j: (b, 0, 0)),
                  pl.BlockSpec((1, 1, d), lambda b, i, j: (b, 0, 0)),
                  pl.BlockSpec((1, d, tn), lambda b, i, j: (j, 0, 0))],
        out_specs=[pl.BlockSpec((1, tm, tn), lambda b, i, j: (b, i, j)),
                   pl.BlockSpec((1, tm, d), lambda b, i, j: (b, i, 0))],
        out_shape=[jax.ShapeDtypeStruct((bsz, s, n), F32),
                   jax.ShapeDtypeStruct((bsz, s, d), BF16)],
        scratch_shapes=[pltpu.VMEM((tm, d), BF16)],
        compiler_params=_cparams(("parallel", "parallel", "arbitrary")),
        name="norm_proj",
    )(x, nw.reshape(1, d), sc, sh, w_bf16.reshape(d, n // tn, tn).transpose(1, 0, 2))


def _bias_kernel(rel_ref, bk_ref, o_ref):
    h = pl.program_id(0)
    bk = bk_ref[...]
    acc = jnp.full(bk.shape, NEG_INF, F32)
    for b in range(REL_BUCKETS):
        acc = jnp.where(bk == b, rel_ref[b, h], acc)
    o_ref[0] = acc


def _bias_tables(rel_bias):
    r = np.arange(Q_BLOCK)[:, None]
    dw = r - np.arange(WINDOW + Q_BLOCK)[None, :] + WINDOW
    win = np.where((dw >= 0) & (dw < WINDOW), _rel_bucket_np(dw), -1)
    off = CMP_STRIDE * CMP_PAD - (CMP_BLOCK - 1)
    dc = r + off - CMP_STRIDE * np.arange(CMP_NEAR)[None, :]
    cmp_near = np.where(dc >= 0, _rel_bucket_np(dc), -1)
    chunks = []
    for m in range(SLC_NEAR_CHUNKS):
        ds_ = Q_BLOCK * m + r - np.arange(Q_BLOCK)[None, :]
        chunks.append(np.where(ds_ >= 0, _rel_bucket_np(ds_), -1))
    assert _rel_bucket_np(Q_BLOCK * SLC_NEAR_CHUNKS - (Q_BLOCK - 1)) == REL_BUCKETS - 1
    assert _rel_bucket_np(off + CMP_STRIDE) == REL_BUCKETS - 1
    chunks.append(np.full((Q_BLOCK, Q_BLOCK), -1))
    chunks.append(np.full((Q_BLOCK, Q_BLOCK), REL_BUCKETS - 1))
    bk = np.concatenate([win, cmp_near] + chunks, axis=1).astype(np.int32)
    cols = bk.shape[1]
    out = pl.pallas_call(
        _bias_kernel,
        grid=(N_HEADS,),
        in_specs=[pl.BlockSpec(memory_space=pltpu.SMEM),
                  pl.BlockSpec((Q_BLOCK, cols), lambda h: (0, 0))],
        out_specs=pl.BlockSpec((1, Q_BLOCK, cols), lambda h: (h, 0, 0)),
        out_shape=jax.ShapeDtypeStruct((N_HEADS, Q_BLOCK, cols), F32),
        compiler_params=_cparams(("arbitrary",)),
        name="bias_tables",
    )(rel_bias, jnp.asarray(bk))
    nw = WINDOW + Q_BLOCK
    tab_win = out[:, :, :nw].reshape(N_KV, N_GRP, Q_BLOCK, nw)
    tab_cmp = out[:, :, nw:nw + CMP_NEAR].reshape(N_KV, N_GRP, Q_BLOCK, CMP_NEAR)
    nch = SLC_NEAR_CHUNKS + 2
    tab_slc = out[:, :, nw + CMP_NEAR:].reshape(N_KV, N_GRP, Q_BLOCK, nch, Q_BLOCK)
    tab_slc = tab_slc.transpose(0, 3, 4, 1, 2).reshape(N_KV, nch, Q_BLOCK, N_GRP * Q_BLOCK)
    far = rel_bias[REL_BUCKETS - 1].reshape(N_KV, N_GRP, 1)
    far_col = jnp.broadcast_to(far, (N_KV, N_GRP, Q_BLOCK)).reshape(N_KV, N_GRP * Q_BLOCK, 1)
    far_row = far_col.reshape(N_KV, 1, N_GRP * Q_BLOCK)
    pieces, rest = [], far_row
    for _ in range(SLC_BIAS_PIECES):
        piece = rest.astype(BF16)
        pieces.append(piece)
        rest = rest - piece.astype(F32)
    zeros = lambda w: jnp.zeros((N_KV, w, N_GRP * Q_BLOCK), BF16)
    far_t = jnp.concatenate([zeros(HEAD_DIM)] + pieces + [zeros(LANES - HEAD_DIM - SLC_BIAS_PIECES)], axis=1)
    return tab_win, tab_cmp, tab_slc, far_col, far_t


def _compress_kernel(a_ref, pe_ref, w1_ref, w2_ref, o_ref):
    a = a_ref[0, 0, 0]
    half = a.shape[1]
    lo = jnp.dot((a + pe_ref[0, 0:1, :]).astype(BF16), w1_ref[0, :half, :].astype(BF16),
                 preferred_element_type=F32)
    hi = jnp.dot((a + pe_ref[0, 1:2, :]).astype(BF16), w1_ref[0, half:, :].astype(BF16),
                 preferred_element_type=F32)
    nc = a.shape[0]
    hid = jax.nn.gelu(lo + pltpu.roll(hi, nc - 1, axis=0))
    out = jnp.dot(hid.astype(BF16), w2_ref[0].astype(BF16), preferred_element_type=F32)
    o_ref[0, 0, 0] = jnp.zeros(o_ref.shape[3:], F32)
    o_ref[0, 0, 0, CMP_PAD:CMP_PAD + nc, :] = out


def _compress(kv_cmp, cmp_pe, cmp_w1, cmp_w2):
    bsz, _, _, nc, half = kv_cmp.shape
    pe = cmp_pe.reshape(2, 2, half)
    rows = CMP_PAD + nc + SUBLANES
    return pl.pallas_call(
        _compress_kernel,
        grid=(bsz, 2, N_KV),
        in_specs=[pl.BlockSpec((1, 1, 1, nc, half), lambda b, w, k: (b, w, k, 0, 0)),
                  pl.BlockSpec((1, 2, half), lambda b, w, k: (w, 0, 0)),
                  pl.BlockSpec((1, 2 * half, CMP_HIDDEN), lambda b, w, k: (w, 0, 0)),
                  pl.BlockSpec((1, CMP_HIDDEN, HEAD_DIM), lambda b, w, k: (w, 0, 0))],
        out_specs=pl.BlockSpec((1, 1, 1, rows, HEAD_DIM), lambda b, w, k: (b, w, k, 0, 0)),
        out_shape=jax.ShapeDtypeStruct((bsz, 2, N_KV, rows, HEAD_DIM), F32),
        compiler_params=_cparams(("parallel", "parallel", "parallel")),
        name="nsa_compress",
    )(kv_cmp, pe, cmp_w1, cmp_w2)


def _topk_mark(vals, k, axis, order=None, order_bound=None, assume_distinct=False):
    n = vals.shape[axis] if order is None else order_bound
    if not assume_distinct:
        iota = lax.broadcasted_iota(jnp.int32, vals.shape, axis).astype(F32) if order is None else order
    rank = jnp.full(vals.shape, float(k), F32)
    work = vals
    picked = []
    for r in range(k):
        m = jnp.max(work, axis=axis, keepdims=True)
        if assume_distinct:
            hit = work == m
        else:
            ix = jnp.min(jnp.where(work == m, iota, float(n)), axis=axis, keepdims=True)
            hit = iota == ix
        rank = jnp.where(hit, float(r), rank)
        work = jnp.where(hit, -jnp.inf, work)
        picked.append(m)
    return rank, picked


def _topk_clean(rank, k, axis):
    marks = jnp.sum(jnp.where(rank < float(k), 1.0, 0.0), axis=axis, keepdims=True)
    return jnp.max(jnp.abs(marks - float(k)))


def _cmp_attn_kernel(q_ref, k_ref, v_ref, tab_ref, far_ref, m_ref, o_ref, sel_ref, imp_ref):
    i = pl.program_id(2)
    rows = N_GRP * Q_BLOCK
    nc = k_ref.shape[3] - CMP_PAD - SUBLANES
    nbp = sel_ref.shape[3]
    start = pl.multiple_of(i * (Q_BLOCK // CMP_STRIDE), SUBLANES)
    n_far = i * (Q_BLOCK // CMP_STRIDE) - CMP_PAD

    def attend(wf):
        q = q_ref[0, 0].reshape(rows, HEAD_DIM)
        k_far = k_ref[0, 0, 0, CMP_PAD:CMP_PAD + wf, :].astype(BF16)
        v_far = v_ref[0, 0, 0, CMP_PAD:CMP_PAD + wf, :].astype(BF16)
        k_near = k_ref[0, 0, 0, pl.ds(start, CMP_NEAR), :].astype(BF16)
        v_near = v_ref[0, 0, 0, pl.ds(start, CMP_NEAR), :].astype(BF16)

        n_idx = lax.broadcasted_iota(jnp.int32, (1, wf), 1)
        mask_far = n_idx < n_far
        s_far = jnp.where(mask_far, _nt(q, k_far) + far_ref[0], NEG_INF)
        tab = tab_ref[0].reshape(rows, CMP_NEAR)
        c_idx = lax.broadcasted_iota(jnp.int32, (1, CMP_NEAR), 1)
        mask_near = (c_idx >= -n_far) & (tab > 0.5 * NEG_INF)
        s_near = jnp.where(mask_near, _nt(q, k_near) + tab, NEG_INF)

        m = jnp.maximum(jnp.max(s_far, axis=-1, keepdims=True), jnp.max(s_near, axis=-1, keepdims=True))
        e_far = jnp.exp(s_far - m)
        e_near = jnp.exp(s_near - m)
        l = jnp.sum(e_far, axis=-1, keepdims=True) + jnp.sum(e_near, axis=-1, keepdims=True)
        p_far = jnp.where(mask_far, e_far / l, 0.0)
        p_near = jnp.where(mask_near, e_near / l, 0.0)
        o = (jnp.dot(p_far.astype(BF16), v_far, preferred_element_type=F32)
             + jnp.dot(p_near.astype(BF16), v_near, preferred_element_type=F32))
        o_ref[0, 0] = o.reshape(N_GRP, Q_BLOCK, HEAD_DIM)

        ps_far = jnp.sum(p_far.reshape(N_GRP, Q_BLOCK, wf), axis=0)
        ps_near = jnp.sum(p_near.reshape(N_GRP, Q_BLOCK, CMP_NEAR), axis=0)
        imp_ref[...] = (
            jnp.dot(ps_far, m_ref[CMP_PAD:CMP_PAD + wf, :], precision=HI, preferred_element_type=F32)
            + jnp.dot(ps_near, m_ref[pl.ds(start, CMP_NEAR), :], precision=HI, preferred_element_type=F32))

    step = min(CMP_FAR_STEP, nc)
    widths = list(range(step, nc + 1, step))
    for b, wf in enumerate(widths):
        lower = n_far > widths[b - 1] if b > 0 else True
        upper = n_far <= wf if b + 1 < len(widths) else True
        pl.when(jnp.logical_and(lower, upper))(functools.partial(attend, wf))

    imp = imp_ref[...].T
    t = i * Q_BLOCK + lax.broadcasted_iota(jnp.int32, (1, Q_BLOCK), 1)
    cur = t // SLC_BLOCK
    blk = lax.broadcasted_iota(jnp.int32, (nbp, 1), 0)
    forced = (blk == 0) | (blk == cur) | (blk == cur - 1)
    imp = jnp.where(forced, FORCE_SCORE, jnp.where(blk <= cur, imp, -FORCE_SCORE))
    n_blocks = (nc * CMP_STRIDE) // SLC_BLOCK
    imp = jnp.where(blk < n_blocks, imp, -jnp.inf)
    rank, _ = _topk_mark(imp, min(SLC_TOPN, n_blocks), axis=0)
    sel_ref[0, 0, 0] = jnp.where(rank < float(SLC_TOPN), 1.0, 0.0).astype(BF16)


def _overlap_matrix(nc, nbp):
    n_cmp = nc - 1
    n_slc = nc * CMP_STRIDE // SLC_BLOCK
    j = np.arange(n_slc)
    lo = np.clip((j * SLC_BLOCK - CMP_BLOCK) // CMP_STRIDE + 1, 0, n_cmp)
    hi = np.clip(-((-(j * SLC_BLOCK + SLC_BLOCK)) // CMP_STRIDE), 0, n_cmp)
    m = np.zeros((CMP_PAD + nc + SUBLANES, nbp), np.float32)
    n = np.arange(nc)[:, None]
    m[CMP_PAD:CMP_PAD + nc, :n_slc] = (n >= lo[None, :]) & (n < hi[None, :])
    return m


def _cmp_attn(q64, kv_c, tab_cmp, far_col, nbp):
    bsz, _, _, s, _ = q64.shape
    rows_c = kv_c.shape[3]
    nc = rows_c - CMP_PAD - SUBLANES
    nq = s // Q_BLOCK
    m_pad = jnp.asarray(_overlap_matrix(nc, nbp))
    return pl.pallas_call(
        _cmp_attn_kernel,
        grid=(bsz, N_KV, nq),
        in_specs=[pl.BlockSpec((1, 1, N_GRP, Q_BLOCK, HEAD_DIM), lambda b, k, i: (b, k, 0, i, 0)),
                  pl.BlockSpec((1, 1, 1, rows_c, HEAD_DIM), lambda b, k, i: (b, 0, k, 0, 0)),
                  pl.BlockSpec((1, 1, 1, rows_c, HEAD_DIM), lambda b, k, i: (b, 1, k, 0, 0)),
                  pl.BlockSpec((1, N_GRP, Q_BLOCK, CMP_NEAR), lambda b, k, i: (k, 0, 0, 0)),
                  pl.BlockSpec((1, N_GRP * Q_BLOCK, 1), lambda b, k, i: (k, 0, 0)),
                  pl.BlockSpec((rows_c, nbp), lambda b, k, i: (0, 0))],
        out_specs=[pl.BlockSpec((1, 1, N_GRP, Q_BLOCK, HEAD_DIM), lambda b, k, i: (b, k, 0, i, 0)),
                   pl.BlockSpec((1, 1, 1, nbp, Q_BLOCK), lambda b, k, i: (b, k, i, 0, 0))],
        out_shape=[jax.ShapeDtypeStruct((bsz, N_KV, N_GRP, s, HEAD_DIM), F32),
                   jax.ShapeDtypeStruct((bsz, N_KV, nq, nbp, Q_BLOCK), BF16)],
        scratch_shapes=[pltpu.VMEM((Q_BLOCK, nbp), F32)],
        compiler_params=_cparams(("parallel", "parallel", "arbitrary")),
        name="nsa_cmp_attn",
    )(q64, kv_c, kv_c, tab_cmp, far_col, m_pad)


def _slc_attn_kernel(qt_ref, far_ref, ka_ref, vt_ref, selt_ref, tab_ref, o_ref,
                     qa_ref, s0_ref, s1_ref, x0_ref, x1_ref, m_ref, acc_ref):
    i = pl.program_id(2)
    cols = N_GRP * Q_BLOCK
    ngroups = qa_ref.shape[0] // 2
    qt = qt_ref[0, 0, 0]
    qt_far = qt + far_ref[0]
    selneg = ((selt_ref[0, 0, 0].astype(F32) - 1.0) * MASK_BIG).astype(BF16)
    for g in range(ngroups):
        part = selneg[g * SLC_GROUP_BLOCKS:(g + 1) * SLC_GROUP_BLOCKS, :]
        part = jnp.concatenate([part] * N_GRP, axis=1)
        qa_ref[2 * g] = jnp.concatenate([qt_far, part], axis=0)
        qa_ref[2 * g + 1] = jnp.concatenate([qt, part], axis=0)

    tiles_per_group = SLC_GROUP_BLOCKS // SLC_TILE_BLOCKS
    sub = SLC_KEY_TILE // Q_BLOCK
    last_tile = vt_ref.shape[2] - 1
    n_pairs = (i // sub + 2) // 2
    n_far = jnp.maximum(0, (i - (SLC_NEAR_CHUNKS - 1)) // sub)
    far_pairs = jnp.maximum(0, (n_far - 1) // 2)
    first_table = jnp.where(n_far > 0, 2 * far_pairs + 1, 0)

    def produce(kt, ls, s_ref, mx_ref, with_table):
        near = (kt >= first_table).astype(jnp.int32)
        kc = jnp.minimum(kt, last_tile)
        ks = pl.multiple_of(kc * SLC_KEY_TILE, SLC_KEY_TILE)
        s = jnp.dot(ka_ref[0, 0, pl.ds(ks, SLC_KEY_TILE), :], qa_ref[2 * (kc // tiles_per_group) + near, :, ls],
                    preferred_element_type=F32)
        if with_table:
            chunks = []
            for a in range(sub):
                mm = i - sub * kt - a
                idx = jnp.where(mm < 0, SLC_TAB_MASKED, jnp.where(mm >= SLC_NEAR_CHUNKS, SLC_TAB_CONST, mm))
                chunks.append(tab_ref[0, idx, :, ls])
            s = s + jnp.concatenate(chunks, axis=0)
        s_ref[:, ls] = s
        mx_ref[:, ls] = jnp.max(s, axis=0, keepdims=True)

    def consume(kt, ls, s_ref, mx_ref):
        m_old = m_ref[:, ls]
        m_new = jnp.maximum(m_old, mx_ref[:, ls])
        m_ref[:, ls] = m_new
        p = jnp.exp(s_ref[:, ls] - m_new).astype(BF16)
        acc_ref[:, ls] = jnp.exp(m_old - m_new) * acc_ref[:, ls] + jnp.dot(
            vt_ref[0, 0, jnp.minimum(kt, last_tile)], p, preferred_element_type=F32)

    strips = [slice(c * SLC_LANE_STRIP, (c + 1) * SLC_LANE_STRIP) for c in range(cols // SLC_LANE_STRIP)]

    def pair_step(j, with_table):
        for ls in strips:
            produce(2 * j + 1, ls, s1_ref, x1_ref, with_table)
            consume(2 * j, ls, s0_ref, x0_ref)
        for ls in strips:
            produce(2 * j + 2, ls, s0_ref, x0_ref, with_table)
            consume(2 * j + 1, ls, s1_ref, x1_ref)

    m_ref[...] = jnp.full(m_ref.shape, -jnp.inf, F32)
    acc_ref[...] = jnp.zeros(acc_ref.shape, F32)

    @pl.when(n_far > 0)
    def _():
        for ls in strips:
            produce(0, ls, s0_ref, x0_ref, False)

    @pl.when(n_far == 0)
    def _():
        for ls in strips:
            produce(0, ls, s0_ref, x0_ref, True)

    def far_body(j, carry):
        pair_step(j, False)
        return carry

    def near_body(j, carry):
        pair_step(j, True)
        return carry

    lax.fori_loop(0, far_pairs, far_body, 0)
    lax.fori_loop(far_pairs, n_pairs, near_body, 0)
    o_ref[0, 0, 0] = acc_ref[:HEAD_DIM, :] / acc_ref[HEAD_DIM:HEAD_DIM + 1, :]


def _slc_attn(qt, far_t, k_aug, vt_aug, sel_t, tab_t):
    bsz, _, nq, _, cols = qt.shape
    s = k_aug.shape[2]
    nbp = sel_t.shape[3]
    ngroups = nbp // SLC_GROUP_BLOCKS
    nch = tab_t.shape[1]
    once = pl.Buffered(1)
    return pl.pallas_call(
        _slc_attn_kernel,
        grid=(bsz, N_KV, nq),
        in_specs=[pl.BlockSpec((1, 1, 1, LANES, cols), lambda b, k, i: (b, k, i, 0, 0)),
                  pl.BlockSpec((1, LANES, cols), lambda b, k, i: (k, 0, 0)),
                  pl.BlockSpec((1, 1, s, 2 * LANES), lambda b, k, i: (b, k, 0, 0), pipeline_mode=once),
                  pl.BlockSpec((1, 1, s // SLC_KEY_TILE, SLC_V_ROWS, SLC_KEY_TILE),
                               lambda b, k, i: (b, k, 0, 0, 0), pipeline_mode=once),
                  pl.BlockSpec((1, 1, 1, nbp, Q_BLOCK), lambda b, k, i: (b, k, i, 0, 0)),
                  pl.BlockSpec((1, nch, Q_BLOCK, cols), lambda b, k, i: (k, 0, 0, 0), pipeline_mode=once)],
        out_specs=pl.BlockSpec((1, 1, 1, HEAD_DIM, cols), lambda b, k, i: (b, k, i, 0, 0)),
        out_shape=jax.ShapeDtypeStruct((bsz, N_KV, nq, HEAD_DIM, cols), F32),
        scratch_shapes=[pltpu.VMEM((2 * ngroups, 2 * LANES, cols), BF16),
                        pltpu.VMEM((SLC_KEY_TILE, cols), F32),
                        pltpu.VMEM((SLC_KEY_TILE, cols), F32),
                        pltpu.VMEM((1, cols), F32),
                        pltpu.VMEM((1, cols), F32),
                        pltpu.VMEM((1, cols), F32),
                        pltpu.VMEM((SLC_V_ROWS, cols), F32)],
        compiler_params=_cparams(("parallel", "parallel", "arbitrary")),
        name="nsa_slc_attn",
    )(qt, far_t, k_aug, vt_aug, sel_t, tab_t)


def _win_attn_kernel(q_ref, k_ref, v_ref, tab_ref, o_ref):
    i = pl.program_id(2)
    rows = N_GRP * Q_BLOCK
    nw = WINDOW + Q_BLOCK
    q = q_ref[0, 0].reshape(rows, HEAD_DIM)
    qs = pl.multiple_of(i * Q_BLOCK, Q_BLOCK)
    k = k_ref[0, 0, pl.ds(qs, nw), :]
    v = v_ref[0, 0, pl.ds(qs, nw), :]
    s = _nt(q, k) + tab_ref[0].reshape(rows, nw)
    col = lax.broadcasted_iota(jnp.int32, (1, nw), 1)
    s = jnp.where(col >= WINDOW - i * Q_BLOCK, s, NEG_INF)
    m = jnp.max(s, axis=-1, keepdims=True)
    e = jnp.exp(s - m)
    p = e / jnp.sum(e, axis=-1, keepdims=True)
    o = jnp.dot(p.astype(BF16), v, preferred_element_type=F32)
    o_ref[0, 0] = o.reshape(N_GRP, Q_BLOCK, HEAD_DIM)


def _win_attn(q64, k_win, v_win, tab_win):
    bsz, _, _, s, _ = q64.shape
    nq = s // Q_BLOCK
    sp = k_win.shape[2]
    nw = WINDOW + Q_BLOCK
    return pl.pallas_call(
        _win_attn_kernel,
        grid=(bsz, N_KV, nq),
        in_specs=[pl.BlockSpec((1, 1, N_GRP, Q_BLOCK, HEAD_DIM), lambda b, k, i: (b, k, 0, i, 0)),
                  pl.BlockSpec((1, 1, sp, HEAD_DIM), lambda b, k, i: (b, k, 0, 0)),
                  pl.BlockSpec((1, 1, sp, HEAD_DIM), lambda b, k, i: (b, k, 0, 0)),
                  pl.BlockSpec((1, N_GRP, Q_BLOCK, nw), lambda b, k, i: (k, 0, 0, 0))],
        out_specs=pl.BlockSpec((1, 1, N_GRP, Q_BLOCK, HEAD_DIM), lambda b, k, i: (b, k, 0, i, 0)),
        out_shape=jax.ShapeDtypeStruct((bsz, N_KV, N_GRP, s, HEAD_DIM), F32),
        compiler_params=_cparams(("parallel", "parallel", "arbitrary")),
        name="nsa_win_attn",
    )(q64, k_win, v_win, tab_win)


def _ssd_kernel(xbc_ref, z_ref, dt_ref, cw_ref, cb_ref, dtb_ref, alog_ref, dskip_ref, nw_ref, o_ref,
                ext_ref, state_ref):
    ln = SSM_CHUNK
    d_ssm = z_ref.shape[2]
    nh = dtb_ref.shape[1]
    gw = d_ssm // SSM_GROUPS
    hpg = nh // SSM_GROUPS
    gn = SSM_GROUPS * SSM_STATE

    @pl.when(pl.program_id(1) == 0)
    def _():
        ext_ref[0:SUBLANES, :] = jnp.zeros((SUBLANES, ext_ref.shape[1]), F32)
        state_ref[...] = jnp.zeros(state_ref.shape, F32)

    ext_ref[SUBLANES:SUBLANES + ln, :] = xbc_ref[0]
    conv = cw_ref[0:1, :] * ext_ref[SUBLANES - CONV_WIDTH + 1:SUBLANES - CONV_WIDTH + 1 + ln, :]
    for k in range(1, CONV_WIDTH):
        lo = SUBLANES - CONV_WIDTH + 1 + k
        conv = conv + cw_ref[k:k + 1, :] * ext_ref[lo:lo + ln, :]
    conv = conv + cb_ref[...]
    ext_ref[0:SUBLANES, :] = xbc_ref[0, ln - SUBLANES:ln, :]
    xc = _silu(conv)
    xs = xc[:, :d_ssm]
    bm = xc[:, d_ssm:d_ssm + gn]
    cm = xc[:, d_ssm + gn:d_ssm + 2 * gn]

    xdt = dt_ref[0, :, :nh] + dtb_ref[...]
    dt = jnp.maximum(xdt, 0.0) + jnp.log1p(jnp.exp(-jnp.abs(xdt)))
    a = -jnp.exp(alog_ref[...])
    da = dt * a

    row = lax.broadcasted_iota(jnp.int32, (ln, ln), 0)
    colm = lax.broadcasted_iota(jnp.int32, (ln, ln), 1)
    causal = row >= colm
    acs = jnp.dot(causal.astype(F32), da, precision=HI, preferred_element_type=F32)
    eye = (lax.broadcasted_iota(jnp.int32, (2 * nh, 2 * nh), 0)
           == lax.broadcasted_iota(jnp.int32, (2 * nh, 2 * nh), 1)).astype(F32)
    rows_t = _nt(eye, jnp.concatenate([acs, dt], axis=1), precision=HI)
    expand = (lax.broadcasted_iota(jnp.int32, (nh, d_ssm), 0)
              == lax.broadcasted_iota(jnp.int32, (nh, d_ssm), 1) // SSM_HEAD_DIM).astype(F32)
    last = acs[ln - 1:ln, :]
    exp_acs_x = jnp.dot(jnp.exp(acs), expand, precision=HI, preferred_element_type=F32)
    w_x = jnp.dot(jnp.exp(last - acs) * dt, expand, precision=HI, preferred_element_type=F32)
    exp_last_x = exp_acs_x[ln - 1:ln, :]

    ys = []
    for g in range(SSM_GROUPS):
        cg = cm[:, g * SSM_STATE:(g + 1) * SSM_STATE].astype(BF16)
        bg32 = bm[:, g * SSM_STATE:(g + 1) * SSM_STATE]
        bg = bg32.astype(BF16)
        xg = xs[:, g * gw:(g + 1) * gw]
        cb = _nt(cg, bg)
        st = state_ref[g]
        y_state = jnp.dot(cg, st.astype(BF16), preferred_element_type=F32) * exp_acs_x[:, g * gw:(g + 1) * gw]
        y_heads = []
        for j in range(hpg):
            h = g * hpg + j
            seg = acs[:, h:h + 1] - rows_t[h:h + 1, :]
            decay = jnp.exp(jnp.where(causal, seg, -jnp.inf))
            mmat = cb * decay * rows_t[nh + h:nh + h + 1, :]
            xh = xg[:, j * SSM_HEAD_DIM:(j + 1) * SSM_HEAD_DIM].astype(BF16)
            y_heads.append(jnp.dot(mmat.astype(BF16), xh, preferred_element_type=F32))
        ys.append(jnp.concatenate(y_heads, axis=1) + y_state)
        xw = (xg * w_x[:, g * gw:(g + 1) * gw]).astype(BF16)
        state_ref[g] = st * exp_last_x[:, g * gw:(g + 1) * gw] + jnp.dot(
            bg32.T.astype(BF16), xw, preferred_element_type=F32)

    y = jnp.concatenate(ys, axis=1) + dskip_ref[...] * xs
    y = y * _silu(z_ref[0])
    outs = []
    for g in range(SSM_GROUPS):
        yg = y[:, g * gw:(g + 1) * gw]
        outs.append(yg * lax.rsqrt(jnp.mean(yg * yg, axis=-1, keepdims=True) + RMS_EPS))
    o_ref[0] = jnp.concatenate(outs, axis=1) * nw_ref[...]


def _ssd(proj, xbc_blk, z_blk, dt_blk, conv_w, conv_b, dt_bias, a_log, d_skip, norm_w):
    bsz, s, _ = proj.shape
    ch = xbc_blk[1]
    d_ssm = z_blk[1]
    nh = dt_bias.shape[0]
    gw = d_ssm // SSM_GROUPS
    nchunks = s // SSM_CHUNK
    full = lambda shape: pl.BlockSpec(shape, lambda b, c: (0,) * len(shape))
    cols = lambda blk: pl.BlockSpec((1, SSM_CHUNK, blk[1]), lambda b, c: (b, c, blk[0]))
    xbc = z = dt_raw = proj
    return pl.pallas_call(
        _ssd_kernel,
        grid=(bsz, nchunks),
        in_specs=[cols(xbc_blk), cols(z_blk), cols(dt_blk),
                  full((CONV_WIDTH, ch)), full((1, ch)), full((1, nh)), full((1, nh)),
                  full((1, d_ssm)), full((1, d_ssm))],
        out_specs=pl.BlockSpec((1, SSM_CHUNK, d_ssm), lambda b, c: (b, c, 0)),
        out_shape=jax.ShapeDtypeStruct((bsz, s, d_ssm), F32),
        scratch_shapes=[pltpu.VMEM((SUBLANES + SSM_CHUNK, ch), F32),
                        pltpu.VMEM((SSM_GROUPS, SSM_STATE, gw), F32)],
        compiler_params=_cparams(("parallel", "arbitrary")),
        name="ssd_scan",
    )(xbc, z, dt_raw, conv_w, conv_b.reshape(1, ch), dt_bias.reshape(1, nh), a_log.reshape(1, nh),
      jnp.repeat(d_skip, SSM_HEAD_DIM).reshape(1, d_ssm), norm_w.reshape(1, d_ssm))


def _mix_out_kernel(oc_ref, os_ref, ow_ref, gl_ref, ex_ref, an_ref, ssm_ref, w_ref, x_ref, g1_ref, o_ref):
    d_attn = oc_ref.shape[2]
    sig = jax.nn.sigmoid(gl_ref[0])
    pieces, rest = [], sig
    for _ in range(GATE_PIECES):
        piece = rest.astype(BF16)
        pieces.append(piece)
        rest = rest - piece.astype(F32)
    sig3 = jnp.concatenate(pieces, axis=1)
    gc = jnp.dot(sig3, ex_ref[0], preferred_element_type=F32)
    gs = jnp.dot(sig3, ex_ref[1], preferred_element_type=F32)
    gw = jnp.dot(sig3, ex_ref[2], preferred_element_type=F32)
    attn = gc * oc_ref[0] + gs * os_ref[0] + gw * ow_ref[0]
    attn = attn * lax.rsqrt(jnp.mean(attn * attn, axis=-1, keepdims=True) + RMS_EPS) * an_ref[...]
    mix = (jnp.dot(attn.astype(BF16), w_ref[:d_attn, :], preferred_element_type=F32)
           + jnp.dot(ssm_ref[0].astype(BF16), w_ref[d_attn:, :], preferred_element_type=F32))
    o_ref[0] = x_ref[0] + g1_ref[0] * mix


def _gate_expand():
    ex = np.zeros((3, LANES, N_HEADS * HEAD_DIM), np.float32)
    for r in range(3):
        for h in range(N_HEADS):
            ex[r, h * 3 + r, h * HEAD_DIM:(h + 1) * HEAD_DIM] = 1.0
    return np.concatenate([ex] * GATE_PIECES, axis=1)


def _mix_out(oc, os_, ow, gl, gl_blk, attn_norm, ssm, w_out_bf16, x, g1, tm=256):
    bsz, s, d = x.shape
    d_attn = oc.shape[2]
    d_ssm = ssm.shape[2]
    tok = lambda w: pl.BlockSpec((1, tm, w), lambda b, i: (b, i, 0))
    return pl.pallas_call(
        _mix_out_kernel,
        grid=(bsz, s // tm),
        in_specs=[tok(d_attn), tok(d_attn), tok(d_attn),
                  pl.BlockSpec((1, tm, gl_blk[1]), lambda b, i: (b, i, gl_blk[0])),
                  pl.BlockSpec((3, GATE_PIECES * LANES, d_attn), lambda b, i: (0, 0, 0)),
                  pl.BlockSpec((1, d_attn), lambda b, i: (0, 0)),
                  tok(d_ssm),
                  pl.BlockSpec((d_attn + d_ssm, d), lambda b, i: (0, 0)),
                  tok(d),
                  pl.BlockSpec((1, 1, d), lambda b, i: (b, 0, 0))],
        out_specs=tok(d),
        out_shape=jax.ShapeDtypeStruct((bsz, s, d), F32),
        compiler_params=_cparams(("parallel", "parallel")),
        name="mix_out",
    )(oc, os_, ow, gl, jnp.asarray(_gate_expand(), BF16), attn_norm.reshape(1, d_attn), ssm, w_out_bf16, x, g1)


def _peer_candidates():
    k, sub = PEER_TOPK, SUBLANES
    cells = [(0, b) for b in range(k)]
    cells += [(a, b) for a in range(1, sub) for b in range(sub)]
    cells += [(a, 0) for a in range(sub, k)]
    order = np.array([a * k + b if (a + 1) * (b + 1) <= k else -1 for a, b in cells], np.float32)
    group = np.zeros((k, LANES), np.float32)
    for row, (a, _) in enumerate(cells):
        group[a, row] = 1.0
    return cells, order, group


def _peer_route_kernel(q_ref, sk_ref, order_ref, group_ref, rank2_ref, e2_ref, n1_ref, c1_ref):
    k = PEER_TOPK

    def route(assume_distinct):
        q = q_ref[...]
        half = q.shape[1] // 2
        s1 = _nt(sk_ref[0], q[:, :half], precision=HI)
        s2 = _nt(sk_ref[1], q[:, half:], precision=HI)
        rank1, v1 = _topk_mark(s1, k, axis=0, assume_distinct=assume_distinct)
        rank2, v2 = _topk_mark(s2, k, axis=0, assume_distinct=assume_distinct)
        v1_all = jnp.concatenate(v1, axis=0)
        v2_all = jnp.concatenate(v2, axis=0)
        cand = jnp.concatenate([v1[0] + v2_all] + [v1[a] + v2_all[:SUBLANES] for a in range(1, SUBLANES)]
                               + [v1_all[SUBLANES:] + v2[0]], axis=0)
        order = order_ref[...]
        cand = jnp.where(order >= 0.0, cand, -jnp.inf)
        rank_c, best = _topk_mark(cand, k, axis=0, order=order, order_bound=float(k * k),
                                  assume_distinct=assume_distinct)
        chosen = jnp.where(rank_c < float(k), 1.0, 0.0).astype(BF16)
        chosen = jnp.concatenate([chosen, jnp.zeros((LANES - chosen.shape[0], chosen.shape[1]), BF16)], axis=0)
        count = jnp.dot(group_ref[...], chosen, preferred_element_type=F32)
        z = best[0] * 0.0
        for r in range(k):
            z = z + jnp.exp(best[r] - best[0])
        n1 = jnp.zeros(s1.shape, F32)
        for a in range(k):
            n1 = jnp.where(rank1 == float(a), count[a:a + 1, :], n1)
        rank2_ref[0] = rank2.astype(BF16)
        e2_ref[0] = jnp.exp(s2 - v2[0]).astype(BF16)
        n1_ref[0] = n1
        c1_ref[0] = jnp.exp(s1 - v1[0]) / z
        return jnp.maximum(jnp.maximum(_topk_clean(rank1, k, 0), _topk_clean(rank2, k, 0)),
                           _topk_clean(rank_c, k, 0))

    dirty = route(True)

    @pl.when(dirty > 0.0)
    def _():
        route(False)


def _peer_route(q, subkeys, tm=256):
    t, width = q.shape
    kd = width // PEER_HEADS
    nk = subkeys.shape[1]
    spec = pl.BlockSpec((1, nk, tm), lambda i, h: (h, 0, i))
    shape = jax.ShapeDtypeStruct((PEER_HEADS, nk, t), F32)
    cells, order, group = _peer_candidates()
    order = jnp.asarray(np.broadcast_to(order[:, None], (len(cells), tm)))
    return pl.pallas_call(
        _peer_route_kernel,
        grid=(t // tm, PEER_HEADS),
        in_specs=[pl.BlockSpec((tm, kd), lambda i, h: (i, h)),
                  pl.BlockSpec((2, nk, kd // 2), lambda i, h: (0, 0, 0)),
                  pl.BlockSpec((len(cells), tm), lambda i, h: (0, 0)),
                  pl.BlockSpec((PEER_TOPK, LANES), lambda i, h: (0, 0))],
        out_specs=[spec, spec, spec, spec],
        out_shape=[jax.ShapeDtypeStruct(shape.shape, BF16), jax.ShapeDtypeStruct(shape.shape, BF16), shape, shape],
        compiler_params=_cparams(("parallel", "parallel")),
        name="peer_route",
    )(q, subkeys, order, jnp.asarray(group, BF16))


def _peer_dense_kernel(ht_ref, u0_ref, un_ref, vt_ref, rank2_ref, e2_ref, n1_ref, c1_ref, o_ref, *scratch):
    j = pl.program_id(1)
    te = un_ref.shape[0]
    nk = rank2_ref.shape[1]
    tm = ht_ref.shape[1]
    strips = [slice(c * PEER_LANE_STRIP, (c + 1) * PEER_LANE_STRIP) for c in range(tm // PEER_LANE_STRIP)]

    units = [(e, c) for e in range(te // PEER_EXPERT_UNIT) for c in range(len(strips))]
    act_sets = (scratch[:len(units)], scratch[len(units):2 * len(units)])
    aw_refs = scratch[2 * len(units):]
    n_piece = PEER_EXPERT_UNIT // PEER_EXPERT_PIECE
    d_piece = o_ref.shape[0] // n_piece

    def produce(u_ref, act_refs, k, p):
        e, c = units[k]
        lo = e * PEER_EXPERT_UNIT + p * PEER_EXPERT_PIECE
        act_refs[k][p * PEER_EXPERT_PIECE:(p + 1) * PEER_EXPERT_PIECE, :] = jax.nn.gelu(
            jnp.dot(u_ref[lo:lo + PEER_EXPERT_PIECE, :], ht_ref[:, strips[c]], preferred_element_type=F32)
        ).astype(BF16)

    def gate(act_refs, k, r):
        e, c = units[k]
        ls = strips[c]
        i1 = (j * te + e * PEER_EXPERT_UNIT) // nk + r
        w = jnp.zeros((nk, PEER_LANE_STRIP), BF16)
        zero = jnp.zeros((nk, PEER_LANE_STRIP), BF16)
        for h in range(PEER_HEADS):
            n_row = n1_ref[h, pl.ds(i1, 1), ls].astype(BF16)
            c_row = c1_ref[h, pl.ds(i1, 1), ls].astype(BF16)
            w = w + jnp.where(rank2_ref[h, :, ls] < n_row, e2_ref[h, :, ls], zero) * c_row
        aw_refs[k][r * nk:(r + 1) * nk, :] = act_refs[k][r * nk:(r + 1) * nk, :] * w

    def combine(k, m):
        e, c = units[k]
        ds_ = slice(m * d_piece, (m + 1) * d_piece)
        es = slice(e * PEER_EXPERT_UNIT, (e + 1) * PEER_EXPERT_UNIT)
        o_ref[ds_, strips[c]] += jnp.dot(vt_ref[0, ds_, es], aw_refs[k][...], preferred_element_type=F32)

    @pl.when(j == 0)
    def _():
        o_ref[...] = jnp.zeros(o_ref.shape, F32)
        for k in range(len(units)):
            for p in range(n_piece):
                produce(u0_ref, act_sets[0], k, p)

    gates_per_piece = PEER_EXPERT_PIECE // nk

    def step(cur, nxt):
        for k in range(len(units)):
            for p in range(n_piece):
                produce(un_ref, nxt, k, p)
                for r in range(p * gates_per_piece, (p + 1) * gates_per_piece):
                    gate(cur, k, r)
                    if k > 0 and r == p * gates_per_piece:
                        combine(k - 1, p)
        for m in range(n_piece):
            combine(len(units) - 1, m)

    pl.when(j % 2 == 0)(functools.partial(step, act_sets[0], act_sets[1]))
    pl.when(j % 2 == 1)(functools.partial(step, act_sets[1], act_sets[0]))


def _peer_dense(h_t, u_bf16, v_bf16, rank2, e2, n1, c1, tm=512, te=512):
    d, t = h_t.shape
    n_exp = u_bf16.shape[0]
    nk = rank2.shape[1]
    v_t_bf16 = v_bf16.reshape(n_exp // te, te, d).transpose(0, 2, 1)
    n_units = (te // PEER_EXPERT_UNIT) * (tm // PEER_LANE_STRIP)
    n_chunks = n_exp // te
    assert n_chunks % 2 == 0
    route = pl.BlockSpec((PEER_HEADS, nk, tm), lambda i, j: (0, 0, i))
    return pl.pallas_call(
        _peer_dense_kernel,
        grid=(t // tm, n_chunks),
        in_specs=[pl.BlockSpec((d, tm), lambda i, j: (0, i)),
                  pl.BlockSpec((te, d), lambda i, j: (0, 0)),
                  pl.BlockSpec((te, d), lambda i, j: (jnp.minimum(j + 1, n_chunks - 1), 0)),
                  pl.BlockSpec((1, d, te), lambda i, j: (j, 0, 0)),
                  route, route, route, route],
        out_specs=pl.BlockSpec((d, tm), lambda i, j: (0, i)),
        out_shape=jax.ShapeDtypeStruct((d, t), F32),
        scratch_shapes=[pltpu.VMEM((PEER_EXPERT_UNIT, PEER_LANE_STRIP), BF16)] * (3 * n_units),
        compiler_params=_cparams(("parallel", "arbitrary")),
        name="peer_dense",
    )(h_t, u_bf16, u_bf16, v_t_bf16, rank2, e2, n1, c1)


def _residual_kernel(x_ref, yt_ref, g_ref, o_ref):
    o_ref[0] = x_ref[0] + g_ref[0] * yt_ref[...].T


def _residual(x, y_t, g, tm=512):
    bsz, s, d = x.shape
    tok = pl.BlockSpec((1, tm, d), lambda b, i: (b, i, 0))
    return pl.pallas_call(
        _residual_kernel,
        grid=(bsz, s // tm),
        in_specs=[tok, pl.BlockSpec((d, tm), lambda b, i: (0, b * (s // tm) + i)),
                  pl.BlockSpec((1, 1, d), lambda b, i: (b, 0, 0))],
        out_specs=tok,
        out_shape=jax.ShapeDtypeStruct((bsz, s, d), F32),
        compiler_params=_cparams(("parallel", "parallel")),
        name="residual",
    )(x, y_t, g)


def _final_norm_kernel(x_ref, w_ref, o_ref):
    x = x_ref[0]
    o_ref[0] = x * lax.rsqrt(jnp.mean(x * x, axis=-1, keepdims=True) + RMS_EPS) * w_ref[...]


def _final_norm(x, w, tm=512):
    bsz, s, d = x.shape
    tok = pl.BlockSpec((1, tm, d), lambda b, i: (b, i, 0))
    return pl.pallas_call(
        _final_norm_kernel,
        grid=(bsz, s // tm),
        in_specs=[tok, pl.BlockSpec((1, d), lambda b, i: (0, 0))],
        out_specs=tok,
        out_shape=jax.ShapeDtypeStruct((bsz, s, d), F32),
        compiler_params=_cparams(("parallel", "parallel")),
        name="final_norm",
    )(x, w.reshape(1, d))


def _pad_cols(w, width):
    return jnp.pad(w, ((0, 0), (0, width - w.shape[1])))


def _nsa(q, kv, tables, cmp_pe, cmp_w1, cmp_w2):
    tab_win, tab_cmp, tab_slc, far_col, far_t = tables
    bsz, s, _ = q.shape
    nc = s // CMP_STRIDE
    nq = s // Q_BLOCK
    n_slc = s // SLC_BLOCK
    nbp = -(-n_slc // SLC_GROUP_BLOCKS) * SLC_GROUP_BLOCKS
    scale = HEAD_DIM ** -0.5
    q6 = (q * scale).astype(BF16).reshape(bsz, nq, Q_BLOCK, N_KV, N_GRP, HEAD_DIM)
    qh = q6.transpose(0, 3, 4, 1, 2, 5).reshape(bsz, N_KV, N_GRP, s, HEAD_DIM)
    qt = q6.transpose(0, 3, 1, 5, 4, 2).reshape(bsz, N_KV, nq, HEAD_DIM, N_GRP * Q_BLOCK)
    qt = jnp.pad(qt, ((0, 0),) * 3 + ((0, LANES - HEAD_DIM), (0, 0)))
    kv6 = kv.reshape(bsz, s, 6, N_KV, HEAD_DIM)

    kv_cmp = kv6[:, :, 0:2].reshape(bsz, nc, CMP_STRIDE, 2, N_KV, HEAD_DIM)
    kv_cmp = kv_cmp.transpose(0, 3, 4, 1, 2, 5).reshape(bsz, 2, N_KV, nc, CMP_STRIDE * HEAD_DIM)
    kv_c = _compress(kv_cmp, cmp_pe, cmp_w1, cmp_w2)
    o_c, sel_t = _cmp_attn(qh, kv_c, tab_cmp, far_col, nbp)

    k_slc = kv6[:, :, 2].transpose(0, 2, 1, 3).astype(BF16)
    nkt = s // SLC_KEY_TILE
    vt = kv6[:, :, 3].astype(BF16).reshape(bsz, nkt, SLC_KEY_TILE, N_KV, HEAD_DIM).transpose(0, 3, 1, 4, 2)
    vt_aug = jnp.concatenate([
        vt, jnp.ones((bsz, N_KV, nkt, 1, SLC_KEY_TILE), BF16),
        jnp.zeros((bsz, N_KV, nkt, SLC_V_ROWS - HEAD_DIM - 1, SLC_KEY_TILE), BF16)], axis=3)
    blk = np.arange(s) // SLC_BLOCK
    onehot = (blk[:, None] % SLC_GROUP_BLOCKS == np.arange(SLC_GROUP_BLOCKS)[None, :]).astype(np.float32)
    k_aug = jnp.concatenate([
        k_slc, jnp.ones((bsz, N_KV, s, SLC_BIAS_PIECES), BF16),
        jnp.zeros((bsz, N_KV, s, LANES - HEAD_DIM - SLC_BIAS_PIECES), BF16),
        jnp.broadcast_to(jnp.asarray(onehot, BF16), (bsz, N_KV, s, SLC_GROUP_BLOCKS))], axis=-1)
    o_st = _slc_attn(qt, far_t, k_aug, vt_aug, sel_t, tab_slc)
    o_s = o_st.reshape(bsz, N_KV, nq, HEAD_DIM, N_GRP, Q_BLOCK).transpose(0, 2, 5, 1, 4, 3)
    o_s = o_s.reshape(bsz, s, N_HEADS * HEAD_DIM)

    front = ((0, 0), (0, 0), (WINDOW, 0), (0, 0))
    k_win = jnp.pad(kv6[:, :, 4].transpose(0, 2, 1, 3).astype(BF16), front)
    v_win = jnp.pad(kv6[:, :, 5].transpose(0, 2, 1, 3).astype(BF16), front)
    o_w = _win_attn(qh, k_win, v_win, tab_win)

    back = lambda o: o.transpose(0, 3, 1, 2, 4).reshape(bsz, s, N_HEADS * HEAD_DIM)
    return back(o_c), o_s, back(o_w)


def kernel(x, c, ada_w, ada_b, norm_mix, norm_ffn, w_in, cmp_pe, cmp_w1, cmp_w2, rel_bias, attn_out_norm,
           conv_w, conv_b, dt_bias, a_log, d_skip, ssm_norm, w_out, peer_wq, peer_subkeys, peer_u, peer_v,
           norm_final):
    bsz, s, d = x.shape
    depth = ada_w.shape[0]
    d_attn = N_HEADS * HEAD_DIM
    n_kv = 6 * N_KV * HEAD_DIM
    n_gate = 3 * N_HEADS
    d_ssm = ssm_norm.shape[1]
    ch = conv_w.shape[2]
    nh = dt_bias.shape[1]

    mod = _ada_mod(c, ada_w, ada_b)
    tables = _bias_tables(rel_bias)

    for l in range(depth):
        sh1, sc1, g1, sh2, sc2, g2 = [mod[l, :, i * d:(i + 1) * d].reshape(bsz, 1, d) for i in range(6)]
        cuts = np.cumsum([0, d_attn, n_kv, n_gate, d_ssm, ch, nh])
        seg = dict(zip(("q", "kv", "gl", "z", "xbc", "dt"),
                       [w_in[l][:, cuts[i]:cuts[i + 1]] for i in range(6)]))
        seg["gl"] = _pad_cols(seg["gl"], LANES)
        seg["dt"] = _pad_cols(seg["dt"], LANES)
        order = ("q", "z", "kv", "gl", "dt", "xbc")
        w_cat = jnp.concatenate([seg[name] for name in order], axis=1).astype(BF16)
        start = dict(zip(order, np.cumsum([0] + [seg[name].shape[1] for name in order])[:-1]))
        blk = {name: (int(start[name]) // seg[name].shape[1], seg[name].shape[1]) for name in order}
        assert all(start[name] % seg[name].shape[1] == 0 for name in ("z", "gl", "dt", "xbc"))
        proj, _ = _norm_proj(x, norm_mix[l], sc1, sh1, w_cat)
        q = proj[:, :, start["q"]:start["q"] + d_attn]
        kv = proj[:, :, start["kv"]:start["kv"] + n_kv]

        o_c, o_s, o_w = _nsa(q, kv, tables, cmp_pe[l], cmp_w1[l], cmp_w2[l])
        ssm = _ssd(proj, blk["xbc"], blk["z"], blk["dt"], conv_w[l], conv_b[l], dt_bias[l], a_log[l], d_skip[l],
                   ssm_norm[l])
        x = _mix_out(o_c, o_s, o_w, proj, blk["gl"], attn_out_norm[l], ssm, w_out[l].astype(BF16), x, g1)

        pq, h2 = _norm_proj(x, norm_ffn[l], sc2, sh2, peer_wq[l].astype(BF16))
        rank2, e2, n1, c1 = _peer_route(pq.reshape(bsz * s, -1), peer_subkeys[l])
        h_t = h2.reshape(bsz * s, d).T
        ffn_t = _peer_dense(h_t, peer_u[l].astype(BF16), peer_v[l].astype(BF16), rank2, e2, n1, c1)
        x = _residual(x, ffn_t, g2)

    return _final_norm(x, norm_final)
```

```python
import functools
import math

import numpy as np
import jax
import jax.numpy as jnp
from jax import lax
from jax.experimental import pallas as pl
from jax.experimental.pallas import tpu as pltpu

F32 = jnp.float32
BF16 = jnp.bfloat16
HI = lax.Precision.HIGHEST

N_HEADS = 16
N_KV = 2
N_GRP = N_HEADS // N_KV
HEAD_DIM = 64
CMP_BLOCK = 32
CMP_STRIDE = 16
CMP_HIDDEN = 4 * HEAD_DIM
SLC_BLOCK = 64
SLC_TOPN = 16
WINDOW = 512
Q_BLOCK = 128
FORCE_SCORE = 1e4
NEG_INF = -1e30
REL_BUCKETS = 32
REL_MAX_DIST = 2048
SSM_HEAD_DIM = 64
SSM_GROUPS = 2
SSM_STATE = 128
CONV_WIDTH = 4
SSM_CHUNK = 256
PEER_HEADS = 8
PEER_NKEYS = 128
PEER_TOPK = 16
RMS_EPS = 1e-6

LANES = 128
SUBLANES = 8
VMEM_LIMIT = 56 * 1024 * 1024

SLC_KEY_TILE = 512
SLC_TILE_BLOCKS = SLC_KEY_TILE // SLC_BLOCK
SLC_GROUP_BLOCKS = LANES
SLC_NEAR_CHUNKS = 14
SLC_TAB_MASKED = 14
SLC_TAB_CONST = 15
MASK_BIG = 2.0 ** 100
SLC_LANE_STRIP = 256
PEER_LANE_STRIP = 256
PEER_EXPERT_PIECE = 256
PEER_EXPERT_UNIT = 512
SLC_V_ROWS = HEAD_DIM + 16
SLC_BIAS_PIECES = 3
GATE_PIECES = 3

CMP_FAR_STEP = 256
CMP_NEAR = 128
CMP_PAD = CMP_NEAR - Q_BLOCK // CMP_STRIDE


def _cparams(sem, vmem=VMEM_LIMIT):
    return pltpu.CompilerParams(dimension_semantics=sem, vmem_limit_bytes=vmem)


def _nt(a, b, precision=None):
    return lax.dot_general(a, b, (((1,), (1,)), ((), ())), precision=precision,
                           preferred_element_type=F32)


def _silu(x):
    return x * jax.nn.sigmoid(x)


def _rel_bucket_np(d):
    d = np.maximum(np.asarray(d, np.int64), 0)
    max_exact = REL_BUCKETS // 2
    ratio = np.log(np.maximum(d, max_exact).astype(np.float64) / max_exact) / math.log(REL_MAX_DIST / max_exact)
    scaled = ratio * (REL_BUCKETS - max_exact)
    large = max_exact + np.floor(scaled).astype(np.int64)
    return np.where(d < max_exact, d, np.minimum(large, REL_BUCKETS - 1)).astype(np.int32)


def _ada_kernel(c_ref, w_ref, b_ref, o_ref):
    cond = _silu(c_ref[...])
    o_ref[0] = jnp.dot(cond, w_ref[0], precision=HI, preferred_element_type=F32) + b_ref[0]


def _ada_mod(c, ada_w, ada_b):
    depth, d, n = ada_w.shape
    bsz = c.shape[0]
    rows = SUBLANES
    c_pad = jnp.zeros((rows, d), F32).at[:bsz].set(c)
    tn = 1024
    out = pl.pallas_call(
        _ada_kernel,
        grid=(depth, n // tn),
        in_specs=[pl.BlockSpec((rows, d), lambda l, j: (0, 0)),
                  pl.BlockSpec((1, d, tn), lambda l, j: (l, 0, j)),
                  pl.BlockSpec((1, 1, tn), lambda l, j: (l, 0, j))],
        out_specs=pl.BlockSpec((1, rows, tn), lambda l, j: (l, 0, j)),
        out_shape=jax.ShapeDtypeStruct((depth, rows, n), F32),
        compiler_params=_cparams(("parallel", "parallel")),
        name="ada_mod",
    )(c_pad, ada_w, ada_b.reshape(depth, 1, n))
    return out[:, :bsz]


def _norm_proj_kernel(x_ref, nw_ref, sc_ref, sh_ref, w_ref, o_ref, h_ref, hs_ref):
    @pl.when(pl.program_id(2) == 0)
    def _():
        x = x_ref[0]
        y = x * lax.rsqrt(jnp.mean(x * x, axis=-1, keepdims=True) + RMS_EPS)
        h = (y * nw_ref[...]) * (1.0 + sc_ref[0]) + sh_ref[0]
        hs_ref[...] = h.astype(BF16)
        h_ref[0] = h.astype(BF16)

    o_ref[0] = jnp.dot(hs_ref[...], w_ref[0], preferred_element_type=F32)


def _norm_proj(x, nw, sc, sh, w_bf16, tm=1024, tn=512):
    bsz, s, d = x.shape
    n = w_bf16.shape[1]
    return pl.pallas_call(
        _norm_proj_kernel,
        grid=(bsz, s // tm, n // tn),
        in_specs=[pl.BlockSpec((1, tm, d), lambda b, i, j: (b, i, 0)),
                  pl.BlockSpec((1, d), lambda b, i, j: (0, 0)),
                  pl.BlockSpec((1, 1, d), lambda b, i, j: (b, 0, 0)),
                  pl.BlockSpec((1, 1, d), lambda b, i, j: (b, 0, 0)),
                  pl.BlockSpec((1, d, tn), lambda b, i, j: (j, 0, 0))],
        out_specs=[pl.BlockSpec((1, tm, tn), lambda b, i, j: (b, i, j)),
                   pl.BlockSpec((1, tm, d), lambda b, i, j: (b, i, 0))],
        out_shape=[jax.ShapeDtypeStruct((bsz, s, n), F32),
                   jax.ShapeDtypeStruct((bsz, s, d), BF16)],
        scratch_shapes=[pltpu.VMEM((tm, d), BF16)],
        compiler_params=_cparams(("parallel", "parallel", "arbitrary")),
        name="norm_proj",
    )(x, nw.reshape(1, d), sc, sh, w_bf16.reshape(d, n // tn, tn).transpose(1, 0, 2))


def _bias_kernel(rel_ref, bk_ref, o_ref):
    h = pl.program_id(0)
    bk = bk_ref[...]
    acc = jnp.full(bk.shape, NEG_INF, F32)
    for b in range(REL_BUCKETS):
        acc = jnp.where(bk == b, rel_ref[b, h], acc)
    o_ref[0] = acc


def _bias_tables(rel_bias):
    r = np.arange(Q_BLOCK)[:, None]
    dw = r - np.arange(WINDOW + Q_BLOCK)[None, :] + WINDOW
    win = np.where((dw >= 0) & (dw < WINDOW), _rel_bucket_np(dw), -1)
    off = CMP_STRIDE * CMP_PAD - (CMP_BLOCK - 1)
    dc = r + off - CMP_STRIDE * np.arange(CMP_NEAR)[None, :]
    cmp_near = np.where(dc >= 0, _rel_bucket_np(dc), -1)
    chunks = []
    for m in range(SLC_NEAR_CHUNKS):
        ds_ = Q_BLOCK * m + r - np.arange(Q_BLOCK)[None, :]
        chunks.append(np.where(ds_ >= 0, _rel_bucket_np(ds_), -1))
    assert _rel_bucket_np(Q_BLOCK * SLC_NEAR_CHUNKS - (Q_BLOCK - 1)) == REL_BUCKETS - 1
    assert _rel_bucket_np(off + CMP_STRIDE) == REL_BUCKETS - 1
    chunks.append(np.full((Q_BLOCK, Q_BLOCK), -1))
    chunks.append(np.full((Q_BLOCK, Q_BLOCK), REL_BUCKETS - 1))
    bk = np.concatenate([win, cmp_near] + chunks, axis=1).astype(np.int32)
    cols = bk.shape[1]
    out = pl.pallas_call(
        _bias_kernel,
        grid=(N_HEADS,),
        in_specs=[pl.BlockSpec(memory_space=pltpu.SMEM),
                  pl.BlockSpec((Q_BLOCK, cols), lambda h: (0, 0))],
        out_specs=pl.BlockSpec((1, Q_BLOCK, cols), lambda h: (h, 0, 0)),
        out_shape=jax.ShapeDtypeStruct((N_HEADS, Q_BLOCK, cols), F32),
        compiler_params=_cparams(("arbitrary",)),
        name="bias_tables",
    )(rel_bias, jnp.asarray(bk))
    nw = WINDOW + Q_BLOCK
    tab_win = out[:, :, :nw].reshape(N_KV, N_GRP, Q_BLOCK, nw)
    tab_cmp = out[:, :, nw:nw + CMP_NEAR].reshape(N_KV, N_GRP, Q_BLOCK, CMP_NEAR)
    nch = SLC_NEAR_CHUNKS + 2
    tab_slc = out[:, :, nw + CMP_NEAR:].reshape(N_KV, N_GRP, Q_BLOCK, nch, Q_BLOCK)
    tab_slc = tab_slc.transpose(0, 3, 4, 1, 2).reshape(N_KV, nch, Q_BLOCK, N_GRP * Q_BLOCK)
    far = rel_bias[REL_BUCKETS - 1].reshape(N_KV, N_GRP, 1)
    far_col = jnp.broadcast_to(far, (N_KV, N_GRP, Q_BLOCK)).reshape(N_KV, N_GRP * Q_BLOCK, 1)
    far_row = far_col.reshape(N_KV, 1, N_GRP * Q_BLOCK)
    pieces, rest = [], far_row
    for _ in range(SLC_BIAS_PIECES):
        piece = rest.astype(BF16)
        pieces.append(piece)
        rest = rest - piece.astype(F32)
    zeros = lambda w: jnp.zeros((N_KV, w, N_GRP * Q_BLOCK), BF16)
    far_t = jnp.concatenate([zeros(HEAD_DIM)] + pieces + [zeros(LANES - HEAD_DIM - SLC_BIAS_PIECES)], axis=1)
    return tab_win, tab_cmp, tab_slc, far_col, far_t


def _compress_kernel(a_ref, pe_ref, w1_ref, w2_ref, o_ref):
    a = a_ref[0, 0, 0]
    half = a.shape[1]
    lo = jnp.dot((a + pe_ref[0, 0:1, :]).astype(BF16), w1_ref[0, :half, :].astype(BF16),
                 preferred_element_type=F32)
    hi = jnp.dot((a + pe_ref[0, 1:2, :]).astype(BF16), w1_ref[0, half:, :].astype(BF16),
                 preferred_element_type=F32)
    nc = a.shape[0]
    hid = jax.nn.gelu(lo + pltpu.roll(hi, nc - 1, axis=0))
    out = jnp.dot(hid.astype(BF16), w2_ref[0].astype(BF16), preferred_element_type=F32)
    o_ref[0, 0, 0] = jnp.zeros(o_ref.shape[3:], F32)
    o_ref[0, 0, 0, CMP_PAD:CMP_PAD + nc, :] = out


def _compress(kv_cmp, cmp_pe, cmp_w1, cmp_w2):
    bsz, _, _, nc, half = kv_cmp.shape
    pe = cmp_pe.reshape(2, 2, half)
    rows = CMP_PAD + nc + SUBLANES
    return pl.pallas_call(
        _compress_kernel,
        grid=(bsz, 2, N_KV),
        in_specs=[pl.BlockSpec((1, 1, 1, nc, half), lambda b, w, k: (b, w, k, 0, 0)),
                  pl.BlockSpec((1, 2, half), lambda b, w, k: (w, 0, 0)),
                  pl.BlockSpec((1, 2 * half, CMP_HIDDEN), lambda b, w, k: (w, 0, 0)),
                  pl.BlockSpec((1, CMP_HIDDEN, HEAD_DIM), lambda b, w, k: (w, 0, 0))],
        out_specs=pl.BlockSpec((1, 1, 1, rows, HEAD_DIM), lambda b, w, k: (b, w, k, 0, 0)),
        out_shape=jax.ShapeDtypeStruct((bsz, 2, N_KV, rows, HEAD_DIM), F32),
        compiler_params=_cparams(("parallel", "parallel", "parallel")),
        name="nsa_compress",
    )(kv_cmp, pe, cmp_w1, cmp_w2)


def _topk_mark(vals, k, axis, order=None, order_bound=None, assume_distinct=False):
    n = vals.shape[axis] if order is None else order_bound
    if not assume_distinct:
        iota = lax.broadcasted_iota(jnp.int32, vals.shape, axis).astype(F32) if order is None else order
    rank = jnp.full(vals.shape, float(k), F32)
    work = vals
    picked = []
    for r in range(k):
        m = jnp.max(work, axis=axis, keepdims=True)
        if assume_distinct:
            hit = work == m
        else:
            ix = jnp.min(jnp.where(work == m, iota, float(n)), axis=axis, keepdims=True)
            hit = iota == ix
        rank = jnp.where(hit, float(r), rank)
        work = jnp.where(hit, -jnp.inf, work)
        picked.append(m)
    return rank, picked


def _topk_clean(rank, k, axis):
    marks = jnp.sum(jnp.where(rank < float(k), 1.0, 0.0), axis=axis, keepdims=True)
    return jnp.max(jnp.abs(marks - float(k)))


def _cmp_attn_kernel(q_ref, k_ref, v_ref, tab_ref, far_ref, m_ref, o_ref, sel_ref, imp_ref):
    i = pl.program_id(2)
    rows = N_GRP * Q_BLOCK
    nc = k_ref.shape[3] - CMP_PAD - SUBLANES
    nbp = sel_ref.shape[3]
    start = pl.multiple_of(i * (Q_BLOCK // CMP_STRIDE), SUBLANES)
    n_far = i * (Q_BLOCK // CMP_STRIDE) - CMP_PAD

    def attend(wf):
        q = q_ref[0, 0].reshape(rows, HEAD_DIM)
        k_far = k_ref[0, 0, 0, CMP_PAD:CMP_PAD + wf, :].astype(BF16)
        v_far = v_ref[0, 0, 0, CMP_PAD:CMP_PAD + wf, :].astype(BF16)
        k_near = k_ref[0, 0, 0, pl.ds(start, CMP_NEAR), :].astype(BF16)
        v_near = v_ref[0, 0, 0, pl.ds(start, CMP_NEAR), :].astype(BF16)

        n_idx = lax.broadcasted_iota(jnp.int32, (1, wf), 1)
        mask_far = n_idx < n_far
        s_far = jnp.where(mask_far, _nt(q, k_far) + far_ref[0], NEG_INF)
        tab = tab_ref[0].reshape(rows, CMP_NEAR)
        c_idx = lax.broadcasted_iota(jnp.int32, (1, CMP_NEAR), 1)
        mask_near = (c_idx >= -n_far) & (tab > 0.5 * NEG_INF)
        s_near = jnp.where(mask_near, _nt(q, k_near) + tab, NEG_INF)

        m = jnp.maximum(jnp.max(s_far, axis=-1, keepdims=True), jnp.max(s_near, axis=-1, keepdims=True))
        e_far = jnp.exp(s_far - m)
        e_near = jnp.exp(s_near - m)
        l = jnp.sum(e_far, axis=-1, keepdims=True) + jnp.sum(e_near, axis=-1, keepdims=True)
        p_far = jnp.where(mask_far, e_far / l, 0.0)
        p_near = jnp.where(mask_near, e_near / l, 0.0)
        o = (jnp.dot(p_far.astype(BF16), v_far, preferred_element_type=F32)
             + jnp.dot(p_near.astype(BF16), v_near, preferred_element_type=F32))
        o_ref[0, 0] = o.reshape(N_GRP, Q_BLOCK, HEAD_DIM)

        ps_far = jnp.sum(p_far.reshape(N_GRP, Q_BLOCK, wf), axis=0)
        ps_near = jnp.sum(p_near.reshape(N_GRP, Q_BLOCK, CMP_NEAR), axis=0)
        imp_ref[...] = (
            jnp.dot(ps_far, m_ref[CMP_PAD:CMP_PAD + wf, :], precision=HI, preferred_element_type=F32)
            + jnp.dot(ps_near, m_ref[pl.ds(start, CMP_NEAR), :], precision=HI, preferred_element_type=F32))

    step = min(CMP_FAR_STEP, nc)
    widths = list(range(step, nc + 1, step))
    for b, wf in enumerate(widths):
        lower = n_far > widths[b - 1] if b > 0 else True
        upper = n_far <= wf if b + 1 < len(widths) else True
        pl.when(jnp.logical_and(lower, upper))(functools.partial(attend, wf))

    imp = imp_ref[...].T
    t = i * Q_BLOCK + lax.broadcasted_iota(jnp.int32, (1, Q_BLOCK), 1)
    cur = t // SLC_BLOCK
    blk = lax.broadcasted_iota(jnp.int32, (nbp, 1), 0)
    forced = (blk == 0) | (blk == cur) | (blk == cur - 1)
    imp = jnp.where(forced, FORCE_SCORE, jnp.where(blk <= cur, imp, -FORCE_SCORE))
    n_blocks = (nc * CMP_STRIDE) // SLC_BLOCK
    imp = jnp.where(blk < n_blocks, imp, -jnp.inf)
    rank, _ = _topk_mark(imp, min(SLC_TOPN, n_blocks), axis=0)
    sel_ref[0, 0, 0] = jnp.where(rank < float(SLC_TOPN), 1.0, 0.0).astype(BF16)


def _overlap_matrix(nc, nbp):
    n_cmp = nc - 1
    n_slc = nc * CMP_STRIDE // SLC_BLOCK
    j = np.arange(n_slc)
    lo = np.clip((j * SLC_BLOCK - CMP_BLOCK) // CMP_STRIDE + 1, 0, n_cmp)
    hi = np.clip(-((-(j * SLC_BLOCK + SLC_BLOCK)) // CMP_STRIDE), 0, n_cmp)
    m = np.zeros((CMP_PAD + nc + SUBLANES, nbp), np.float32)
    n = np.arange(nc)[:, None]
    m[CMP_PAD:CMP_PAD + nc, :n_slc] = (n >= lo[None, :]) & (n < hi[None, :])
    return m


def _cmp_attn(q64, kv_c, tab_cmp, far_col, nbp):
    bsz, _, _, s, _ = q64.shape
    rows_c = kv_c.shape[3]
    nc = rows_c - CMP_PAD - SUBLANES
    nq = s // Q_BLOCK
    m_pad = jnp.asarray(_overlap_matrix(nc, nbp))
    return pl.pallas_call(
        _cmp_attn_kernel,
        grid=(bsz, N_KV, nq),
        in_specs=[pl.BlockSpec((1, 1, N_GRP, Q_BLOCK, HEAD_DIM), lambda b, k, i: (b, k, 0, i, 0)),
                  pl.BlockSpec((1, 1, 1, rows_c, HEAD_DIM), lambda b, k, i: (b, 0, k, 0, 0)),
                  pl.BlockSpec((1, 1, 1, rows_c, HEAD_DIM), lambda b, k, i: (b, 1, k, 0, 0)),
                  pl.BlockSpec((1, N_GRP, Q_BLOCK, CMP_NEAR), lambda b, k, i: (k, 0, 0, 0)),
                  pl.BlockSpec((1, N_GRP * Q_BLOCK, 1), lambda b, k, i: (k, 0, 0)),
                  pl.BlockSpec((rows_c, nbp), lambda b, k, i: (0, 0))],
        out_specs=[pl.BlockSpec((1, 1, N_GRP, Q_BLOCK, HEAD_DIM), lambda b, k, i: (b, k, 0, i, 0)),
                   pl.BlockSpec((1, 1, 1, nbp, Q_BLOCK), lambda b, k, i: (b, k, i, 0, 0))],
        out_shape=[jax.ShapeDtypeStruct((bsz, N_KV, N_GRP, s, HEAD_DIM), F32),
                   jax.ShapeDtypeStruct((bsz, N_KV, nq, nbp, Q_BLOCK), BF16)],
        scratch_shapes=[pltpu.VMEM((Q_BLOCK, nbp), F32)],
        compiler_params=_cparams(("parallel", "parallel", "arbitrary")),
        name="nsa_cmp_attn",
    )(q64, kv_c, kv_c, tab_cmp, far_col, m_pad)


def _slc_attn_kernel(qt_ref, far_ref, ka_ref, vt_ref, selt_ref, tab_ref, o_ref,
                     qa_ref, s0_ref, s1_ref, x0_ref, x1_ref, m_ref, acc_ref):
    i = pl.program_id(2)
    cols = N_GRP * Q_BLOCK
    ngroups = qa_ref.shape[0] // 2
    qt = qt_ref[0, 0, 0]
    qt_far = qt + far_ref[0]
    selneg = ((selt_ref[0, 0, 0].astype(F32) - 1.0) * MASK_BIG).astype(BF16)
    for g in range(ngroups):
        part = selneg[g * SLC_GROUP_BLOCKS:(g + 1) * SLC_GROUP_BLOCKS, :]
        part = jnp.concatenate([part] * N_GRP, axis=1)
        qa_ref[2 * g] = jnp.concatenate([qt_far, part], axis=0)
        qa_ref[2 * g + 1] = jnp.concatenate([qt, part], axis=0)

    tiles_per_group = SLC_GROUP_BLOCKS // SLC_TILE_BLOCKS
    sub = SLC_KEY_TILE // Q_BLOCK
    last_tile = vt_ref.shape[2] - 1
    n_pairs = (i // sub + 2) // 2
    n_far = jnp.maximum(0, (i - (SLC_NEAR_CHUNKS - 1)) // sub)
    far_pairs = jnp.maximum(0, (n_far - 1) // 2)
    first_table = jnp.where(n_far > 0, 2 * far_pairs + 1, 0)

    def produce(kt, ls, s_ref, mx_ref, with_table):
        near = (kt >= first_table).astype(jnp.int32)
        kc = jnp.minimum(kt, last_tile)
        ks = pl.multiple_of(kc * SLC_KEY_TILE, SLC_KEY_TILE)
        s = jnp.dot(ka_ref[0, 0, pl.ds(ks, SLC_KEY_TILE), :], qa_ref[2 * (kc // tiles_per_group) + near, :, ls],
                    preferred_element_type=F32)
        if with_table:
            chunks = []
            for a in range(sub):
                mm = i - sub * kt - a
                idx = jnp.where(mm < 0, SLC_TAB_MASKED, jnp.where(mm >= SLC_NEAR_CHUNKS, SLC_TAB_CONST, mm))
                chunks.append(tab_ref[0, idx, :, ls])
            s = s + jnp.concatenate(chunks, axis=0)
        s_ref[:, ls] = s
        mx_ref[:, ls] = jnp.max(s, axis=0, keepdims=True)

    def consume(kt, ls, s_ref, mx_ref):
        m_old = m_ref[:, ls]
        m_new = jnp.maximum(m_old, mx_ref[:, ls])
        m_ref[:, ls] = m_new
        p = jnp.exp(s_ref[:, ls] - m_new).astype(BF16)
        acc_ref[:, ls] = jnp.exp(m_old - m_new) * acc_ref[:, ls] + jnp.dot(
            vt_ref[0, 0, jnp.minimum(kt, last_tile)], p, preferred_element_type=F32)

    strips = [slice(c * SLC_LANE_STRIP, (c + 1) * SLC_LANE_STRIP) for c in range(cols // SLC_LANE_STRIP)]

    def pair_step(j, with_table):
        for ls in strips:
            produce(2 * j + 1, ls, s1_ref, x1_ref, with_table)
            consume(2 * j, ls, s0_ref, x0_ref)
        for ls in strips:
            produce(2 * j + 2, ls, s0_ref, x0_ref, with_table)
            consume(2 * j + 1, ls, s1_ref, x1_ref)

    m_ref[...] = jnp.full(m_ref.shape, -jnp.inf, F32)
    acc_ref[...] = jnp.zeros(acc_ref.shape, F32)

    @pl.when(n_far > 0)
    def _():
        for ls in strips:
            produce(0, ls, s0_ref, x0_ref, False)

    @pl.when(n_far == 0)
    def _():
        for ls in strips:
            produce(0, ls, s0_ref, x0_ref, True)

    def far_body(j, carry):
        pair_step(j, False)
        return carry

    def near_body(j, carry):
        pair_step(j, True)
        return carry

    lax.fori_loop(0, far_pairs, far_body, 0)
    lax.fori_loop(far_pairs, n_pairs, near_body, 0)
    o_ref[0, 0, 0] = acc_ref[:HEAD_DIM, :] / acc_ref[HEAD_DIM:HEAD_DIM + 1, :]


def _slc_attn(qt, far_t, k_aug, vt_aug, sel_t, tab_t):
    bsz, _, nq, _, cols = qt.shape
    s = k_aug.shape[2]
    nbp = sel_t.shape[3]
    ngroups = nbp // SLC_GROUP_BLOCKS
    nch = tab_t.shape[1]
    once = pl.Buffered(1)
    return pl.pallas_call(
        _slc_attn_kernel,
        grid=(bsz, N_KV, nq),
        in_specs=[pl.BlockSpec((1, 1, 1, LANES, cols), lambda b, k, i: (b, k, i, 0, 0)),
                  pl.BlockSpec((1, LANES, cols), lambda b, k, i: (k, 0, 0)),
                  pl.BlockSpec((1, 1, s, 2 * LANES), lambda b, k, i: (b, k, 0, 0), pipeline_mode=once),
                  pl.BlockSpec((1, 1, s // SLC_KEY_TILE, SLC_V_ROWS, SLC_KEY_TILE),
                               lambda b, k, i: (b, k, 0, 0, 0), pipeline_mode=once),
                  pl.BlockSpec((1, 1, 1, nbp, Q_BLOCK), lambda b, k, i: (b, k, i, 0, 0)),
                  pl.BlockSpec((1, nch, Q_BLOCK, cols), lambda b, k, i: (k, 0, 0, 0), pipeline_mode=once)],
        out_specs=pl.BlockSpec((1, 1, 1, HEAD_DIM, cols), lambda b, k, i: (b, k, i, 0, 0)),
        out_shape=jax.ShapeDtypeStruct((bsz, N_KV, nq, HEAD_DIM, cols), F32),
        scratch_shapes=[pltpu.VMEM((2 * ngroups, 2 * LANES, cols), BF16),
                        pltpu.VMEM((SLC_KEY_TILE, cols), F32),
                        pltpu.VMEM((SLC_KEY_TILE, cols), F32),
                        pltpu.VMEM((1, cols), F32),
                        pltpu.VMEM((1, cols), F32),
                        pltpu.VMEM((1, cols), F32),
                        pltpu.VMEM((SLC_V_ROWS, cols), F32)],
        compiler_params=_cparams(("parallel", "parallel", "arbitrary")),
        name="nsa_slc_attn",
    )(qt, far_t, k_aug, vt_aug, sel_t, tab_t)


def _win_attn_kernel(q_ref, k_ref, v_ref, tab_ref, o_ref):
    i = pl.program_id(2)
    rows = N_GRP * Q_BLOCK
    nw = WINDOW + Q_BLOCK
    q = q_ref[0, 0].reshape(rows, HEAD_DIM)
    qs = pl.multiple_of(i * Q_BLOCK, Q_BLOCK)
    k = k_ref[0, 0, pl.ds(qs, nw), :]
    v = v_ref[0, 0, pl.ds(qs, nw), :]
    s = _nt(q, k) + tab_ref[0].reshape(rows, nw)
    col = lax.broadcasted_iota(jnp.int32, (1, nw), 1)
    s = jnp.where(col >= WINDOW - i * Q_BLOCK, s, NEG_INF)
    m = jnp.max(s, axis=-1, keepdims=True)
    e = jnp.exp(s - m)
    p = e / jnp.sum(e, axis=-1, keepdims=True)
    o = jnp.dot(p.astype(BF16), v, preferred_element_type=F32)
    o_ref[0, 0] = o.reshape(N_GRP, Q_BLOCK, HEAD_DIM)


def _win_attn(q64, k_win, v_win, tab_win):
    bsz, _, _, s, _ = q64.shape
    nq = s // Q_BLOCK
    sp = k_win.shape[2]
    nw = WINDOW + Q_BLOCK
    return pl.pallas_call(
        _win_attn_kernel,
        grid=(bsz, N_KV, nq),
        in_specs=[pl.BlockSpec((1, 1, N_GRP, Q_BLOCK, HEAD_DIM), lambda b, k, i: (b, k, 0, i, 0)),
                  pl.BlockSpec((1, 1, sp, HEAD_DIM), lambda b, k, i: (b, k, 0, 0)),
                  pl.BlockSpec((1, 1, sp, HEAD_DIM), lambda b, k, i: (b, k, 0, 0)),
                  pl.BlockSpec((1, N_GRP, Q_BLOCK, nw), lambda b, k, i: (k, 0, 0, 0))],
        out_specs=pl.BlockSpec((1, 1, N_GRP, Q_BLOCK, HEAD_DIM), lambda b, k, i: (b, k, 0, i, 0)),
        out_shape=jax.ShapeDtypeStruct((bsz, N_KV, N_GRP, s, HEAD_DIM), F32),
        compiler_params=_cparams(("parallel", "parallel", "arbitrary")),
        name="nsa_win_attn",
    )(q64, k_win, v_win, tab_win)


def _ssd_kernel(xbc_ref, z_ref, dt_ref, cw_ref, cb_ref, dtb_ref, alog_ref, dskip_ref, nw_ref, o_ref,
                ext_ref, state_ref):
    ln = SSM_CHUNK
    d_ssm = z_ref.shape[2]
    nh = dtb_ref.shape[1]
    gw = d_ssm // SSM_GROUPS
    hpg = nh // SSM_GROUPS
    gn = SSM_GROUPS * SSM_STATE

    @pl.when(pl.program_id(1) == 0)
    def _():
        ext_ref[0:SUBLANES, :] = jnp.zeros((SUBLANES, ext_ref.shape[1]), F32)
        state_ref[...] = jnp.zeros(state_ref.shape, F32)

    ext_ref[SUBLANES:SUBLANES + ln, :] = xbc_ref[0]
    conv = cw_ref[0:1, :] * ext_ref[SUBLANES - CONV_WIDTH + 1:SUBLANES - CONV_WIDTH + 1 + ln, :]
    for k in range(1, CONV_WIDTH):
        lo = SUBLANES - CONV_WIDTH + 1 + k
        conv = conv + cw_ref[k:k + 1, :] * ext_ref[lo:lo + ln, :]
    conv = conv + cb_ref[...]
    ext_ref[0:SUBLANES, :] = xbc_ref[0, ln - SUBLANES:ln, :]
    xc = _silu(conv)
    xs = xc[:, :d_ssm]
    bm = xc[:, d_ssm:d_ssm + gn]
    cm = xc[:, d_ssm + gn:d_ssm + 2 * gn]

    xdt = dt_ref[0, :, :nh] + dtb_ref[...]
    dt = jnp.maximum(xdt, 0.0) + jnp.log1p(jnp.exp(-jnp.abs(xdt)))
    a = -jnp.exp(alog_ref[...])
    da = dt * a

    row = lax.broadcasted_iota(jnp.int32, (ln, ln), 0)
    colm = lax.broadcasted_iota(jnp.int32, (ln, ln), 1)
    causal = row >= colm
    acs = jnp.dot(causal.astype(F32), da, precision=HI, preferred_element_type=F32)
    eye = (lax.broadcasted_iota(jnp.int32, (2 * nh, 2 * nh), 0)
           == lax.broadcasted_iota(jnp.int32, (2 * nh, 2 * nh), 1)).astype(F32)
    rows_t = _nt(eye, jnp.concatenate([acs, dt], axis=1), precision=HI)
    expand = (lax.broadcasted_iota(jnp.int32, (GATE_PIECES * nh, d_ssm), 0) % nh
              == lax.broadcasted_iota(jnp.int32, (GATE_PIECES * nh, d_ssm), 1) // SSM_HEAD_DIM).astype(BF16)

    def spread(cols):
        pieces, rest = [], cols
        for _ in range(GATE_PIECES):
            piece = rest.astype(BF16)
            pieces.append(piece)
            rest = rest - piece.astype(F32)
        return jnp.dot(jnp.concatenate(pieces, axis=1), expand, preferred_element_type=F32)

    last = acs[ln - 1:ln, :]
    exp_acs_x = spread(jnp.exp(acs))
    w_x = spread(jnp.exp(last - acs) * dt)
    exp_last_x = exp_acs_x[ln - 1:ln, :]

    ys = []
    for g in range(SSM_GROUPS):
        cg = cm[:, g * SSM_STATE:(g + 1) * SSM_STATE].astype(BF16)
        bg32 = bm[:, g * SSM_STATE:(g + 1) * SSM_STATE]
        bg = bg32.astype(BF16)
        xg = xs[:, g * gw:(g + 1) * gw]
        cb = _nt(cg, bg)
        st = state_ref[g]
        y_state = jnp.dot(cg, st.astype(BF16), preferred_element_type=F32) * exp_acs_x[:, g * gw:(g + 1) * gw]
        y_heads = []
        for j in range(hpg):
            h = g * hpg + j
            seg = acs[:, h:h + 1] - rows_t[h:h + 1, :]
            decay = jnp.exp(jnp.where(causal, seg, -jnp.inf))
            mmat = cb * decay * rows_t[nh + h:nh + h + 1, :]
            xh = xg[:, j * SSM_HEAD_DIM:(j + 1) * SSM_HEAD_DIM].astype(BF16)
            y_heads.append(jnp.dot(mmat.astype(BF16), xh, preferred_element_type=F32))
        ys.append(jnp.concatenate(y_heads, axis=1) + y_state)
        xw = (xg * w_x[:, g * gw:(g + 1) * gw]).astype(BF16)
        state_ref[g] = st * exp_last_x[:, g * gw:(g + 1) * gw] + jnp.dot(
            bg32.T.astype(BF16), xw, preferred_element_type=F32)

    y = jnp.concatenate(ys, axis=1) + dskip_ref[...] * xs
    y = y * _silu(z_ref[0])
    outs = []
    for g in range(SSM_GROUPS):
        yg = y[:, g * gw:(g + 1) * gw]
        outs.append(yg * lax.rsqrt(jnp.mean(yg * yg, axis=-1, keepdims=True) + RMS_EPS))
    o_ref[0] = jnp.concatenate(outs, axis=1) * nw_ref[...]


def _ssd(proj, xbc_blk, z_blk, dt_blk, conv_w, conv_b, dt_bias, a_log, d_skip, norm_w):
    bsz, s, _ = proj.shape
    ch = xbc_blk[1]
    d_ssm = z_blk[1]
    nh = dt_bias.shape[0]
    gw = d_ssm // SSM_GROUPS
    nchunks = s // SSM_CHUNK
    full = lambda shape: pl.BlockSpec(shape, lambda b, c: (0,) * len(shape))
    cols = lambda blk: pl.BlockSpec((1, SSM_CHUNK, blk[1]), lambda b, c: (b, c, blk[0]))
    xbc = z = dt_raw = proj
    return pl.pallas_call(
        _ssd_kernel,
        grid=(bsz, nchunks),
        in_specs=[cols(xbc_blk), cols(z_blk), cols(dt_blk),
                  full((CONV_WIDTH, ch)), full((1, ch)), full((1, nh)), full((1, nh)),
                  full((1, d_ssm)), full((1, d_ssm))],
        out_specs=pl.BlockSpec((1, SSM_CHUNK, d_ssm), lambda b, c: (b, c, 0)),
        out_shape=jax.ShapeDtypeStruct((bsz, s, d_ssm), F32),
        scratch_shapes=[pltpu.VMEM((SUBLANES + SSM_CHUNK, ch), F32),
                        pltpu.VMEM((SSM_GROUPS, SSM_STATE, gw), F32)],
        compiler_params=_cparams(("parallel", "arbitrary")),
        name="ssd_scan",
    )(xbc, z, dt_raw, conv_w, conv_b.reshape(1, ch), dt_bias.reshape(1, nh), a_log.reshape(1, nh),
      jnp.repeat(d_skip, SSM_HEAD_DIM).reshape(1, d_ssm), norm_w.reshape(1, d_ssm))


def _mix_out_kernel(oc_ref, os_ref, ow_ref, gl_ref, ex_ref, an_ref, ssm_ref, w_ref, x_ref, g1_ref, o_ref):
    d_attn = oc_ref.shape[2]
    sig = jax.nn.sigmoid(gl_ref[0])
    pieces, rest = [], sig
    for _ in range(GATE_PIECES):
        piece = rest.astype(BF16)
        pieces.append(piece)
        rest = rest - piece.astype(F32)
    sig3 = jnp.concatenate(pieces, axis=1)
    gc = jnp.dot(sig3, ex_ref[0], preferred_element_type=F32)
    gs = jnp.dot(sig3, ex_ref[1], preferred_element_type=F32)
    gw = jnp.dot(sig3, ex_ref[2], preferred_element_type=F32)
    attn = gc * oc_ref[0] + gs * os_ref[0] + gw * ow_ref[0]
    attn = attn * lax.rsqrt(jnp.mean(attn * attn, axis=-1, keepdims=True) + RMS_EPS) * an_ref[...]
    mix = (jnp.dot(attn.astype(BF16), w_ref[:d_attn, :], preferred_element_type=F32)
           + jnp.dot(ssm_ref[0].astype(BF16), w_ref[d_attn:, :], preferred_element_type=F32))
    o_ref[0] = x_ref[0] + g1_ref[0] * mix


def _gate_expand():
    ex = np.zeros((3, LANES, N_HEADS * HEAD_DIM), np.float32)
    for r in range(3):
        for h in range(N_HEADS):
            ex[r, h * 3 + r, h * HEAD_DIM:(h + 1) * HEAD_DIM] = 1.0
    return np.concatenate([ex] * GATE_PIECES, axis=1)


def _mix_out(oc, os_, ow, gl, gl_blk, attn_norm, ssm, w_out_bf16, x, g1, tm=256):
    bsz, s, d = x.shape
    d_attn = oc.shape[2]
    d_ssm = ssm.shape[2]
    tok = lambda w: pl.BlockSpec((1, tm, w), lambda b, i: (b, i, 0))
    return pl.pallas_call(
        _mix_out_kernel,
        grid=(bsz, s // tm),
        in_specs=[tok(d_attn), tok(d_attn), tok(d_attn),
                  pl.BlockSpec((1, tm, gl_blk[1]), lambda b, i: (b, i, gl_blk[0])),
                  pl.BlockSpec((3, GATE_PIECES * LANES, d_attn), lambda b, i: (0, 0, 0)),
                  pl.BlockSpec((1, d_attn), lambda b, i: (0, 0)),
                  tok(d_ssm),
                  pl.BlockSpec((d_attn + d_ssm, d), lambda b, i: (0, 0)),
                  tok(d),
                  pl.BlockSpec((1, 1, d), lambda b, i: (b, 0, 0))],
        out_specs=tok(d),
        out_shape=jax.ShapeDtypeStruct((bsz, s, d), F32),
        compiler_params=_cparams(("parallel", "parallel")),
        name="mix_out",
    )(oc, os_, ow, gl, jnp.asarray(_gate_expand(), BF16), attn_norm.reshape(1, d_attn), ssm, w_out_bf16, x, g1)


def _peer_candidates():
    k, sub = PEER_TOPK, SUBLANES
    cells = [(0, b) for b in range(k)]
    cells += [(a, b) for a in range(1, sub) for b in range(sub)]
    cells += [(a, 0) for a in range(sub, k)]
    order = np.array([a * k + b if (a + 1) * (b + 1) <= k else -1 for a, b in cells], np.float32)
    group = np.zeros((k, LANES), np.float32)
    for row, (a, _) in enumerate(cells):
        group[a, row] = 1.0
    return cells, order, group


def _peer_route_kernel(q_ref, sk_ref, order_ref, group_ref, rank2_ref, e2_ref, n1_ref, c1_ref):
    k = PEER_TOPK

    def route(assume_distinct):
        q = q_ref[...]
        half = q.shape[1] // 2
        s1 = _nt(sk_ref[0], q[:, :half], precision=HI)
        s2 = _nt(sk_ref[1], q[:, half:], precision=HI)
        rank1, v1 = _topk_mark(s1, k, axis=0, assume_distinct=assume_distinct)
        rank2, v2 = _topk_mark(s2, k, axis=0, assume_distinct=assume_distinct)
        v1_all = jnp.concatenate(v1, axis=0)
        v2_all = jnp.concatenate(v2, axis=0)
        cand = jnp.concatenate([v1[0] + v2_all] + [v1[a] + v2_all[:SUBLANES] for a in range(1, SUBLANES)]
                               + [v1_all[SUBLANES:] + v2[0]], axis=0)
        order = order_ref[...]
        cand = jnp.where(order >= 0.0, cand, -jnp.inf)
        rank_c, best = _topk_mark(cand, k, axis=0, order=order, order_bound=float(k * k),
                                  assume_distinct=assume_distinct)
        chosen = jnp.where(rank_c < float(k), 1.0, 0.0).astype(BF16)
        chosen = jnp.concatenate([chosen, jnp.zeros((LANES - chosen.shape[0], chosen.shape[1]), BF16)], axis=0)
        count = jnp.dot(group_ref[...], chosen, preferred_element_type=F32)
        z = best[0] * 0.0
        for r in range(k):
            z = z + jnp.exp(best[r] - best[0])
        n1 = jnp.zeros(s1.shape, F32)
        for a in range(k):
            n1 = jnp.where(rank1 == float(a), count[a:a + 1, :], n1)
        rank2_ref[0] = rank2.astype(BF16)
        e2_ref[0] = jnp.exp(s2 - v2[0]).astype(BF16)
        n1_ref[0] = n1
        c1_ref[0] = jnp.exp(s1 - v1[0]) / z
        return jnp.maximum(jnp.maximum(_topk_clean(rank1, k, 0), _topk_clean(rank2, k, 0)),
                           _topk_clean(rank_c, k, 0))

    dirty = route(True)

    @pl.when(dirty > 0.0)
    def _():
        route(False)


def _peer_route(q, subkeys, tm=256):
    t, width = q.shape
    kd = width // PEER_HEADS
    nk = subkeys.shape[1]
    spec = pl.BlockSpec((1, nk, tm), lambda i, h: (h, 0, i))
    shape = jax.ShapeDtypeStruct((PEER_HEADS, nk, t), F32)
    cells, order, group = _peer_candidates()
    order = jnp.asarray(np.broadcast_to(order[:, None], (len(cells), tm)))
    return pl.pallas_call(
        _peer_route_kernel,
        grid=(t // tm, PEER_HEADS),
        in_specs=[pl.BlockSpec((tm, kd), lambda i, h: (i, h)),
                  pl.BlockSpec((2, nk, kd // 2), lambda i, h: (0, 0, 0)),
                  pl.BlockSpec((len(cells), tm), lambda i, h: (0, 0)),
                  pl.BlockSpec((PEER_TOPK, LANES), lambda i, h: (0, 0))],
        out_specs=[spec, spec, spec, spec],
        out_shape=[jax.ShapeDtypeStruct(shape.shape, BF16), jax.ShapeDtypeStruct(shape.shape, BF16), shape, shape],
        compiler_params=_cparams(("parallel", "parallel")),
        name="peer_route",
    )(q, subkeys, order, jnp.asarray(group, BF16))


def _peer_dense_kernel(ht_ref, u0_ref, un_ref, vt_ref, rank2_ref, e2_ref, n1_ref, c1_ref, o_ref, *scratch):
    j = pl.program_id(1)
    te = un_ref.shape[0]
    nk = rank2_ref.shape[1]
    tm = ht_ref.shape[1]
    strips = [slice(c * PEER_LANE_STRIP, (c + 1) * PEER_LANE_STRIP) for c in range(tm // PEER_LANE_STRIP)]

    units = [(e, c) for e in range(te // PEER_EXPERT_UNIT) for c in range(len(strips))]
    act_sets = (scratch[:len(units)], scratch[len(units):2 * len(units)])
    aw_refs = scratch[2 * len(units):]
    n_piece = PEER_EXPERT_UNIT // PEER_EXPERT_PIECE
    d_piece = o_ref.shape[0] // n_piece

    def produce(u_ref, act_refs, k, p):
        e, c = units[k]
        lo = e * PEER_EXPERT_UNIT + p * PEER_EXPERT_PIECE
        act_refs[k][p * PEER_EXPERT_PIECE:(p + 1) * PEER_EXPERT_PIECE, :] = jax.nn.gelu(
            jnp.dot(u_ref[lo:lo + PEER_EXPERT_PIECE, :], ht_ref[:, strips[c]], preferred_element_type=F32)
        ).astype(BF16)

    def gate(act_refs, k, r):
        e, c = units[k]
        ls = strips[c]
        i1 = (j * te + e * PEER_EXPERT_UNIT) // nk + r
        w = jnp.zeros((nk, PEER_LANE_STRIP), BF16)
        zero = jnp.zeros((nk, PEER_LANE_STRIP), BF16)
        for h in range(PEER_HEADS):
            n_row = n1_ref[h, pl.ds(i1, 1), ls].astype(BF16)
            c_row = c1_ref[h, pl.ds(i1, 1), ls].astype(BF16)
            w = w + jnp.where(rank2_ref[h, :, ls] < n_row, e2_ref[h, :, ls], zero) * c_row
        aw_refs[k][r * nk:(r + 1) * nk, :] = act_refs[k][r * nk:(r + 1) * nk, :] * w

    def combine(k, m):
        e, c = units[k]
        ds_ = slice(m * d_piece, (m + 1) * d_piece)
        es = slice(e * PEER_EXPERT_UNIT, (e + 1) * PEER_EXPERT_UNIT)
        o_ref[ds_, strips[c]] += jnp.dot(vt_ref[0, ds_, es], aw_refs[k][...], preferred_element_type=F32)

    @pl.when(j == 0)
    def _():
        o_ref[...] = jnp.zeros(o_ref.shape, F32)
        for k in range(len(units)):
            for p in range(n_piece):
                produce(u0_ref, act_sets[0], k, p)

    gates_per_piece = PEER_EXPERT_PIECE // nk

    def step(cur, nxt):
        for k in range(len(units)):
            for p in range(n_piece):
                produce(un_ref, nxt, k, p)
                for r in range(p * gates_per_piece, (p + 1) * gates_per_piece):
                    gate(cur, k, r)
                    if k > 0 and r == p * gates_per_piece:
                        combine(k - 1, p)
        for m in range(n_piece):
            combine(len(units) - 1, m)

    pl.when(j % 2 == 0)(functools.partial(step, act_sets[0], act_sets[1]))
    pl.when(j % 2 == 1)(functools.partial(step, act_sets[1], act_sets[0]))


def _peer_dense(h_t, u_bf16, v_bf16, rank2, e2, n1, c1, tm=512, te=512):
    d, t = h_t.shape
    n_exp = u_bf16.shape[0]
    nk = rank2.shape[1]
    v_t_bf16 = v_bf16.reshape(n_exp // te, te, d).transpose(0, 2, 1)
    n_units = (te // PEER_EXPERT_UNIT) * (tm // PEER_LANE_STRIP)
    n_chunks = n_exp // te
    assert n_chunks % 2 == 0
    route = pl.BlockSpec((PEER_HEADS, nk, tm), lambda i, j: (0, 0, i))
    return pl.pallas_call(
        _peer_dense_kernel,
        grid=(t // tm, n_chunks),
        in_specs=[pl.BlockSpec((d, tm), lambda i, j: (0, i)),
                  pl.BlockSpec((te, d), lambda i, j: (0, 0)),
                  pl.BlockSpec((te, d), lambda i, j: (jnp.minimum(j + 1, n_chunks - 1), 0)),
                  pl.BlockSpec((1, d, te), lambda i, j: (j, 0, 0)),
                  route, route, route, route],
        out_specs=pl.BlockSpec((d, tm), lambda i, j: (0, i)),
        out_shape=jax.ShapeDtypeStruct((d, t), F32),
        scratch_shapes=[pltpu.VMEM((PEER_EXPERT_UNIT, PEER_LANE_STRIP), BF16)] * (3 * n_units),
        compiler_params=_cparams(("parallel", "arbitrary")),
        name="peer_dense",
    )(h_t, u_bf16, u_bf16, v_t_bf16, rank2, e2, n1, c1)


def _residual_kernel(x_ref, yt_ref, g_ref, o_ref):
    o_ref[0] = x_ref[0] + g_ref[0] * yt_ref[...].T


def _residual_norm_kernel(x_ref, yt_ref, g_ref, w_ref, o_ref):
    x = x_ref[0] + g_ref[0] * yt_ref[...].T
    o_ref[0] = x * lax.rsqrt(jnp.mean(x * x, axis=-1, keepdims=True) + RMS_EPS) * w_ref[...]


def _residual(x, y_t, g, norm_w=None, tm=512):
    bsz, s, d = x.shape
    tok = pl.BlockSpec((1, tm, d), lambda b, i: (b, i, 0))
    in_specs = [tok, pl.BlockSpec((d, tm), lambda b, i: (0, b * (s // tm) + i)),
                pl.BlockSpec((1, 1, d), lambda b, i: (b, 0, 0))]
    args = [x, y_t, g]
    if norm_w is not None:
        in_specs.append(pl.BlockSpec((1, d), lambda b, i: (0, 0)))
        args.append(norm_w.reshape(1, d))
    return pl.pallas_call(
        _residual_kernel if norm_w is None else _residual_norm_kernel,
        grid=(bsz, s // tm),
        in_specs=in_specs,
        out_specs=tok,
        out_shape=jax.ShapeDtypeStruct((bsz, s, d), F32),
        compiler_params=_cparams(("parallel", "parallel")),
        name="residual",
    )(*args)


def _pad_cols(w, width):
    return jnp.pad(w, ((0, 0), (0, width - w.shape[1])))


def _nsa(q, kv, tables, cmp_pe, cmp_w1, cmp_w2):
    tab_win, tab_cmp, tab_slc, far_col, far_t = tables
    bsz, s, _ = q.shape
    nc = s // CMP_STRIDE
    nq = s // Q_BLOCK
    n_slc = s // SLC_BLOCK
    nbp = -(-n_slc // SLC_GROUP_BLOCKS) * SLC_GROUP_BLOCKS
    scale = HEAD_DIM ** -0.5
    q6 = (q * scale).astype(BF16).reshape(bsz, nq, Q_BLOCK, N_KV, N_GRP, HEAD_DIM)
    qh = q6.transpose(0, 3, 4, 1, 2, 5).reshape(bsz, N_KV, N_GRP, s, HEAD_DIM)
    qt = q6.transpose(0, 3, 1, 5, 4, 2).reshape(bsz, N_KV, nq, HEAD_DIM, N_GRP * Q_BLOCK)
    qt = jnp.pad(qt, ((0, 0),) * 3 + ((0, LANES - HEAD_DIM), (0, 0)))
    kv6 = kv.reshape(bsz, s, 6, N_KV, HEAD_DIM)

    kv_cmp = kv6[:, :, 0:2].reshape(bsz, nc, CMP_STRIDE, 2, N_KV, HEAD_DIM)
    kv_cmp = kv_cmp.transpose(0, 3, 4, 1, 2, 5).reshape(bsz, 2, N_KV, nc, CMP_STRIDE * HEAD_DIM)
    kv_c = _compress(kv_cmp, cmp_pe, cmp_w1, cmp_w2)
    o_c, sel_t = _cmp_attn(qh, kv_c, tab_cmp, far_col, nbp)

    k_slc = kv6[:, :, 2].transpose(0, 2, 1, 3).astype(BF16)
    nkt = s // SLC_KEY_TILE
    vt = kv6[:, :, 3].astype(BF16).reshape(bsz, nkt, SLC_KEY_TILE, N_KV, HEAD_DIM).transpose(0, 3, 1, 4, 2)
    vt_aug = jnp.concatenate([
        vt, jnp.ones((bsz, N_KV, nkt, 1, SLC_KEY_TILE), BF16),
        jnp.zeros((bsz, N_KV, nkt, SLC_V_ROWS - HEAD_DIM - 1, SLC_KEY_TILE), BF16)], axis=3)
    blk = np.arange(s) // SLC_BLOCK
    onehot = (blk[:, None] % SLC_GROUP_BLOCKS == np.arange(SLC_GROUP_BLOCKS)[None, :]).astype(np.float32)
    k_aug = jnp.concatenate([
        k_slc, jnp.ones((bsz, N_KV, s, SLC_BIAS_PIECES), BF16),
        jnp.zeros((bsz, N_KV, s, LANES - HEAD_DIM - SLC_BIAS_PIECES), BF16),
        jnp.broadcast_to(jnp.asarray(onehot, BF16), (bsz, N_KV, s, SLC_GROUP_BLOCKS))], axis=-1)
    o_st = _slc_attn(qt, far_t, k_aug, vt_aug, sel_t, tab_slc)
    o_s = o_st.reshape(bsz, N_KV, nq, HEAD_DIM, N_GRP, Q_BLOCK).transpose(0, 2, 5, 1, 4, 3)
    o_s = o_s.reshape(bsz, s, N_HEADS * HEAD_DIM)

    front = ((0, 0), (0, 0), (WINDOW, 0), (0, 0))
    k_win = jnp.pad(kv6[:, :, 4].transpose(0, 2, 1, 3).astype(BF16), front)
    v_win = jnp.pad(kv6[:, :, 5].transpose(0, 2, 1, 3).astype(BF16), front)
    o_w = _win_attn(qh, k_win, v_win, tab_win)

    back = lambda o: o.transpose(0, 3, 1, 2, 4).reshape(bsz, s, N_HEADS * HEAD_DIM)
    return back(o_c), o_s, back(o_w)


def kernel(x, c, ada_w, ada_b, norm_mix, norm_ffn, w_in, cmp_pe, cmp_w1, cmp_w2, rel_bias, attn_out_norm,
           conv_w, conv_b, dt_bias, a_log, d_skip, ssm_norm, w_out, peer_wq, peer_subkeys, peer_u, peer_v,
           norm_final):
    bsz, s, d = x.shape
    depth = ada_w.shape[0]
    d_attn = N_HEADS * HEAD_DIM
    n_kv = 6 * N_KV * HEAD_DIM
    n_gate = 3 * N_HEADS
    d_ssm = ssm_norm.shape[1]
    ch = conv_w.shape[2]
    nh = dt_bias.shape[1]

    mod = _ada_mod(c, ada_w, ada_b)
    tables = _bias_tables(rel_bias)

    for l in range(depth):
        sh1, sc1, g1, sh2, sc2, g2 = [mod[l, :, i * d:(i + 1) * d].reshape(bsz, 1, d) for i in range(6)]
        cuts = np.cumsum([0, d_attn, n_kv, n_gate, d_ssm, ch, nh])
        seg = dict(zip(("q", "kv", "gl", "z", "xbc", "dt"),
                       [w_in[l][:, cuts[i]:cuts[i + 1]] for i in range(6)]))
        seg["gl"] = _pad_cols(seg["gl"], LANES)
        seg["dt"] = _pad_cols(seg["dt"], LANES)
        order = ("q", "z", "kv", "gl", "dt", "xbc")
        w_cat = jnp.concatenate([seg[name] for name in order], axis=1).astype(BF16)
        start = dict(zip(order, np.cumsum([0] + [seg[name].shape[1] for name in order])[:-1]))
        blk = {name: (int(start[name]) // seg[name].shape[1], seg[name].shape[1]) for name in order}
        assert all(start[name] % seg[name].shape[1] == 0 for name in ("z", "gl", "dt", "xbc"))
        proj, _ = _norm_proj(x, norm_mix[l], sc1, sh1, w_cat)
        q = proj[:, :, start["q"]:start["q"] + d_attn]
        kv = proj[:, :, start["kv"]:start["kv"] + n_kv]

        o_c, o_s, o_w = _nsa(q, kv, tables, cmp_pe[l], cmp_w1[l], cmp_w2[l])
        ssm = _ssd(proj, blk["xbc"], blk["z"], blk["dt"], conv_w[l], conv_b[l], dt_bias[l], a_log[l], d_skip[l],
                   ssm_norm[l])
        x = _mix_out(o_c, o_s, o_w, proj, blk["gl"], attn_out_norm[l], ssm, w_out[l].astype(BF16), x, g1)

        pq, h2 = _norm_proj(x, norm_ffn[l], sc2, sh2, peer_wq[l].astype(BF16))
        rank2, e2, n1, c1 = _peer_route(pq.reshape(bsz * s, -1), peer_subkeys[l])
        h_t = h2.reshape(bsz * s, d).T
        ffn_t = _peer_dense(h_t, peer_u[l].astype(BF16), peer_v[l].astype(BF16), rank2, e2, n1, c1)
        x = _residual(x, ffn_t, g2, norm_final if l == depth - 1 else None)

    return x
```

```python
import functools
import math

import numpy as np
import jax
import jax.numpy as jnp
from jax import lax
from jax.experimental import pallas as pl
from jax.experimental.pallas import tpu as pltpu

F32 = jnp.float32
BF16 = jnp.bfloat16
HI = lax.Precision.HIGHEST

N_HEADS = 16
N_KV = 2
N_GRP = N_HEADS // N_KV
HEAD_DIM = 64
CMP_BLOCK = 32
CMP_STRIDE = 16
CMP_HIDDEN = 4 * HEAD_DIM
SLC_BLOCK = 64
SLC_TOPN = 16
WINDOW = 512
Q_BLOCK = 128
FORCE_SCORE = 1e4
NEG_INF = -1e30
REL_BUCKETS = 32
REL_MAX_DIST = 2048
SSM_HEAD_DIM = 64
SSM_GROUPS = 2
SSM_STATE = 128
CONV_WIDTH = 4
SSM_CHUNK = 256
PEER_HEADS = 8
PEER_TOPK = 16
RMS_EPS = 1e-6

LANES = 128
SUBLANES = 8
VMEM_LIMIT = 56 * 1024 * 1024

SLC_KEY_TILE = 512
SLC_TILE_BLOCKS = SLC_KEY_TILE // SLC_BLOCK
SLC_GROUP_BLOCKS = LANES
SLC_NEAR_CHUNKS = 14
SLC_TAB_MASKED = 14
SLC_TAB_CONST = 15
MASK_BIG = 2.0 ** 100
SLC_LANE_STRIP = 256
PEER_LANE_STRIP = 256
PEER_EXPERT_PIECE = 256
PEER_EXPERT_UNIT = 512
SLC_V_ROWS = HEAD_DIM + 16
SLC_BIAS_PIECES = 3
GATE_PIECES = 3

CMP_FAR_STEP = 256
CMP_NEAR = 128
CMP_PAD = CMP_NEAR - Q_BLOCK // CMP_STRIDE


def _cparams(sem, vmem=VMEM_LIMIT):
    return pltpu.CompilerParams(dimension_semantics=sem, vmem_limit_bytes=vmem)


def _nt(a, b, precision=None):
    return lax.dot_general(a, b, (((1,), (1,)), ((), ())), precision=precision,
                           preferred_element_type=F32)


def _silu(x):
    return x * jax.nn.sigmoid(x)


def _rel_bucket_np(d):
    d = np.maximum(np.asarray(d, np.int64), 0)
    max_exact = REL_BUCKETS // 2
    ratio = np.log(np.maximum(d, max_exact).astype(np.float64) / max_exact) / math.log(REL_MAX_DIST / max_exact)
    scaled = ratio * (REL_BUCKETS - max_exact)
    large = max_exact + np.floor(scaled).astype(np.int64)
    return np.where(d < max_exact, d, np.minimum(large, REL_BUCKETS - 1)).astype(np.int32)


def _ada_kernel(c_ref, w_ref, b_ref, o_ref):
    cond = _silu(c_ref[...])
    o_ref[0] = jnp.dot(cond, w_ref[0], precision=HI, preferred_element_type=F32) + b_ref[0]


def _ada_mod(c, ada_w, ada_b):
    depth, d, n = ada_w.shape
    bsz = c.shape[0]
    rows = SUBLANES
    c_pad = jnp.zeros((rows, d), F32).at[:bsz].set(c)
    tn = 1024
    out = pl.pallas_call(
        _ada_kernel,
        grid=(depth, n // tn),
        in_specs=[pl.BlockSpec((rows, d), lambda l, j: (0, 0)),
                  pl.BlockSpec((1, d, tn), lambda l, j: (l, 0, j)),
                  pl.BlockSpec((1, 1, tn), lambda l, j: (l, 0, j))],
        out_specs=pl.BlockSpec((1, rows, tn), lambda l, j: (l, 0, j)),
        out_shape=jax.ShapeDtypeStruct((depth, rows, n), F32),
        compiler_params=_cparams(("parallel", "parallel")),
        name="ada_mod",
    )(c_pad, ada_w, ada_b.reshape(depth, 1, n))
    return out[:, :bsz]


def _norm_proj_kernel(x_ref, nw_ref, sc_ref, sh_ref, w_ref, o_ref, h_ref, hs_ref):
    @pl.when(pl.program_id(2) == 0)
    def _():
        x = x_ref[0]
        y = x * lax.rsqrt(jnp.mean(x * x, axis=-1, keepdims=True) + RMS_EPS)
        h = (y * nw_ref[...]) * (1.0 + sc_ref[0]) + sh_ref[0]
        hs_ref[...] = h.astype(BF16)
        h_ref[0] = h.astype(BF16)

    o_ref[0] = jnp.dot(hs_ref[...], w_ref[0], preferred_element_type=F32)


def _norm_proj(x, nw, sc, sh, w_bf16, tm=1024, tn=512):
    bsz, s, d = x.shape
    n = w_bf16.shape[1]
    return pl.pallas_call(
        _norm_proj_kernel,
        grid=(bsz, s // tm, n // tn),
        in_specs=[pl.BlockSpec((1, tm, d), lambda b, i, j: (b, i, 0)),
                  pl.BlockSpec((1, d), lambda b, i, j: (0, 0)),
                  pl.BlockSpec((1, 1, d), lambda b, i, j: (b, 0, 0)),
                  pl.BlockSpec((1, 1, d), lambda b, i, j: (b, 0, 0)),
                  pl.BlockSpec((1, d, tn), lambda b, i, j: (j, 0, 0))],
        out_specs=[pl.BlockSpec((1, tm, tn), lambda b, i, j: (b, i, j)),
                   pl.BlockSpec((1, tm, d), lambda b, i, j: (b, i, 0))],
        out_shape=[jax.ShapeDtypeStruct((bsz, s, n), F32),
                   jax.ShapeDtypeStruct((bsz, s, d), BF16)],
        scratch_shapes=[pltpu.VMEM((tm, d), BF16)],
        compiler_params=_cparams(("parallel", "parallel", "arbitrary")),
        name="norm_proj",
    )(x, nw.reshape(1, d), sc, sh, w_bf16.reshape(d, n // tn, tn).transpose(1, 0, 2))


def _bias_kernel(rel_ref, bk_ref, o_ref):
    h = pl.program_id(0)
    bk = bk_ref[...]
    acc = jnp.full(bk.shape, NEG_INF, F32)
    for b in range(REL_BUCKETS):
        acc = jnp.where(bk == b, rel_ref[b, h], acc)
    o_ref[0] = acc


def _bias_tables(rel_bias):
    r = np.arange(Q_BLOCK)[:, None]
    dw = r - np.arange(WINDOW + Q_BLOCK)[None, :] + WINDOW
    win = np.where((dw >= 0) & (dw < WINDOW), _rel_bucket_np(dw), -1)
    off = CMP_STRIDE * CMP_PAD - (CMP_BLOCK - 1)
    dc = r + off - CMP_STRIDE * np.arange(CMP_NEAR)[None, :]
    cmp_near = np.where(dc >= 0, _rel_bucket_np(dc), -1)
    chunks = []
    for m in range(SLC_NEAR_CHUNKS):
        ds_ = Q_BLOCK * m + r - np.arange(Q_BLOCK)[None, :]
        chunks.append(np.where(ds_ >= 0, _rel_bucket_np(ds_), -1))
    assert _rel_bucket_np(Q_BLOCK * SLC_NEAR_CHUNKS - (Q_BLOCK - 1)) == REL_BUCKETS - 1
    assert _rel_bucket_np(off + CMP_STRIDE) == REL_BUCKETS - 1
    chunks.append(np.full((Q_BLOCK, Q_BLOCK), -1))
    chunks.append(np.full((Q_BLOCK, Q_BLOCK), REL_BUCKETS - 1))
    bk = np.concatenate([win, cmp_near] + chunks, axis=1).astype(np.int32)
    cols = bk.shape[1]
    out = pl.pallas_call(
        _bias_kernel,
        grid=(N_HEADS,),
        in_specs=[pl.BlockSpec(memory_space=pltpu.SMEM),
                  pl.BlockSpec((Q_BLOCK, cols), lambda h: (0, 0))],
        out_specs=pl.BlockSpec((1, Q_BLOCK, cols), lambda h: (h, 0, 0)),
        out_shape=jax.ShapeDtypeStruct((N_HEADS, Q_BLOCK, cols), F32),
        compiler_params=_cparams(("arbitrary",)),
        name="bias_tables",
    )(rel_bias, jnp.asarray(bk))
    nw = WINDOW + Q_BLOCK
    tab_win = out[:, :, :nw].reshape(N_KV, N_GRP, Q_BLOCK, nw)
    tab_cmp = out[:, :, nw:nw + CMP_NEAR].reshape(N_KV, N_GRP, Q_BLOCK, CMP_NEAR)
    nch = SLC_NEAR_CHUNKS + 2
    tab_slc = out[:, :, nw + CMP_NEAR:].reshape(N_KV, N_GRP, Q_BLOCK, nch, Q_BLOCK)
    tab_slc = tab_slc.transpose(0, 3, 4, 1, 2).reshape(N_KV, nch, Q_BLOCK, N_GRP * Q_BLOCK)
    far = rel_bias[REL_BUCKETS - 1].reshape(N_KV, N_GRP, 1)
    far_col = jnp.broadcast_to(far, (N_KV, N_GRP, Q_BLOCK)).reshape(N_KV, N_GRP * Q_BLOCK, 1)
    far_row = far_col.reshape(N_KV, 1, N_GRP * Q_BLOCK)
    pieces, rest = [], far_row
    for _ in range(SLC_BIAS_PIECES):
        piece = rest.astype(BF16)
        pieces.append(piece)
        rest = rest - piece.astype(F32)
    zeros = lambda w: jnp.zeros((N_KV, w, N_GRP * Q_BLOCK), BF16)
    far_t = jnp.concatenate([zeros(HEAD_DIM)] + pieces + [zeros(LANES - HEAD_DIM - SLC_BIAS_PIECES)], axis=1)
    return tab_win, tab_cmp, tab_slc, far_col, far_t


def _compress_kernel(a_ref, pe_ref, w1_ref, w2_ref, o_ref):
    a = a_ref[0, 0, 0]
    half = a.shape[1]
    lo = jnp.dot((a + pe_ref[0, 0:1, :]).astype(BF16), w1_ref[0, :half, :].astype(BF16),
                 preferred_element_type=F32)
    hi = jnp.dot((a + pe_ref[0, 1:2, :]).astype(BF16), w1_ref[0, half:, :].astype(BF16),
                 preferred_element_type=F32)
    nc = a.shape[0]
    hid = jax.nn.gelu(lo + pltpu.roll(hi, nc - 1, axis=0))
    out = jnp.dot(hid.astype(BF16), w2_ref[0].astype(BF16), preferred_element_type=F32)
    o_ref[0, 0, 0] = jnp.zeros(o_ref.shape[3:], F32)
    o_ref[0, 0, 0, CMP_PAD:CMP_PAD + nc, :] = out


def _compress(kv_cmp, cmp_pe, cmp_w1, cmp_w2):
    bsz, _, _, nc, half = kv_cmp.shape
    pe = cmp_pe.reshape(2, 2, half)
    rows = CMP_PAD + nc + SUBLANES
    return pl.pallas_call(
        _compress_kernel,
        grid=(bsz, 2, N_KV),
        in_specs=[pl.BlockSpec((1, 1, 1, nc, half), lambda b, w, k: (b, w, k, 0, 0)),
                  pl.BlockSpec((1, 2, half), lambda b, w, k: (w, 0, 0)),
                  pl.BlockSpec((1, 2 * half, CMP_HIDDEN), lambda b, w, k: (w, 0, 0)),
                  pl.BlockSpec((1, CMP_HIDDEN, HEAD_DIM), lambda b, w, k: (w, 0, 0))],
        out_specs=pl.BlockSpec((1, 1, 1, rows, HEAD_DIM), lambda b, w, k: (b, w, k, 0, 0)),
        out_shape=jax.ShapeDtypeStruct((bsz, 2, N_KV, rows, HEAD_DIM), F32),
        compiler_params=_cparams(("parallel", "parallel", "parallel")),
        name="nsa_compress",
    )(kv_cmp, pe, cmp_w1, cmp_w2)


def _topk_mark(vals, k, axis, order=None, order_bound=None, assume_distinct=False):
    n = vals.shape[axis] if order is None else order_bound
    if not assume_distinct:
        iota = lax.broadcasted_iota(jnp.int32, vals.shape, axis).astype(F32) if order is None else order
    rank = jnp.full(vals.shape, float(k), F32)
    work = vals
    picked = []
    for r in range(k):
        m = jnp.max(work, axis=axis, keepdims=True)
        if assume_distinct:
            hit = work == m
        else:
            ix = jnp.min(jnp.where(work == m, iota, float(n)), axis=axis, keepdims=True)
            hit = iota == ix
        rank = jnp.where(hit, float(r), rank)
        work = jnp.where(hit, -jnp.inf, work)
        picked.append(m)
    return rank, picked


def _topk_clean(rank, k, axis):
    marks = jnp.sum(jnp.where(rank < float(k), 1.0, 0.0), axis=axis, keepdims=True)
    return jnp.max(jnp.abs(marks - float(k)))


def _cmp_attn_kernel(q_ref, k_ref, v_ref, tab_ref, far_ref, m_ref, o_ref, sel_ref, imp_ref):
    i = pl.program_id(2)
    rows = N_GRP * Q_BLOCK
    nc = k_ref.shape[3] - CMP_PAD - SUBLANES
    nbp = sel_ref.shape[3]
    start = pl.multiple_of(i * (Q_BLOCK // CMP_STRIDE), SUBLANES)
    n_far = i * (Q_BLOCK // CMP_STRIDE) - CMP_PAD

    def attend(wf):
        q = q_ref[0, 0].reshape(rows, HEAD_DIM)
        k_far = k_ref[0, 0, 0, CMP_PAD:CMP_PAD + wf, :].astype(BF16)
        v_far = v_ref[0, 0, 0, CMP_PAD:CMP_PAD + wf, :].astype(BF16)
        k_near = k_ref[0, 0, 0, pl.ds(start, CMP_NEAR), :].astype(BF16)
        v_near = v_ref[0, 0, 0, pl.ds(start, CMP_NEAR), :].astype(BF16)

        n_idx = lax.broadcasted_iota(jnp.int32, (1, wf), 1)
        mask_far = n_idx < n_far
        s_far = jnp.where(mask_far, _nt(q, k_far) + far_ref[0], NEG_INF)
        tab = tab_ref[0].reshape(rows, CMP_NEAR)
        c_idx = lax.broadcasted_iota(jnp.int32, (1, CMP_NEAR), 1)
        mask_near = (c_idx >= -n_far) & (tab > 0.5 * NEG_INF)
        s_near = jnp.where(mask_near, _nt(q, k_near) + tab, NEG_INF)

        m = jnp.maximum(jnp.max(s_far, axis=-1, keepdims=True), jnp.max(s_near, axis=-1, keepdims=True))
        e_far = jnp.exp(s_far - m)
        e_near = jnp.exp(s_near - m)
        l = jnp.sum(e_far, axis=-1, keepdims=True) + jnp.sum(e_near, axis=-1, keepdims=True)
        p_far = jnp.where(mask_far, e_far / l, 0.0)
        p_near = jnp.where(mask_near, e_near / l, 0.0)
        o = (jnp.dot(p_far.astype(BF16), v_far, preferred_element_type=F32)
             + jnp.dot(p_near.astype(BF16), v_near, preferred_element_type=F32))
        o_ref[0, 0] = o.reshape(N_GRP, Q_BLOCK, HEAD_DIM)

        ps_far = jnp.sum(p_far.reshape(N_GRP, Q_BLOCK, wf), axis=0)
        ps_near = jnp.sum(p_near.reshape(N_GRP, Q_BLOCK, CMP_NEAR), axis=0)
        imp_ref[...] = (
            jnp.dot(ps_far, m_ref[CMP_PAD:CMP_PAD + wf, :], precision=HI, preferred_element_type=F32)
            + jnp.dot(ps_near, m_ref[pl.ds(start, CMP_NEAR), :], precision=HI, preferred_element_type=F32))

    step = min(CMP_FAR_STEP, nc)
    widths = list(range(step, nc + 1, step))
    for b, wf in enumerate(widths):
        lower = n_far > widths[b - 1] if b > 0 else True
        upper = n_far <= wf if b + 1 < len(widths) else True
        pl.when(jnp.logical_and(lower, upper))(functools.partial(attend, wf))

    imp = imp_ref[...].T
    t = i * Q_BLOCK + lax.broadcasted_iota(jnp.int32, (1, Q_BLOCK), 1)
    cur = t // SLC_BLOCK
    blk = lax.broadcasted_iota(jnp.int32, (nbp, 1), 0)
    forced = (blk == 0) | (blk == cur) | (blk == cur - 1)
    imp = jnp.where(forced, FORCE_SCORE, jnp.where(blk <= cur, imp, -FORCE_SCORE))
    n_blocks = (nc * CMP_STRIDE) // SLC_BLOCK
    imp = jnp.where(blk < n_blocks, imp, -jnp.inf)
    rank, _ = _topk_mark(imp, min(SLC_TOPN, n_blocks), axis=0)
    sel_ref[0, 0, 0] = jnp.where(rank < float(SLC_TOPN), 1.0, 0.0).astype(BF16)


def _overlap_matrix(nc, nbp):
    n_cmp = nc - 1
    n_slc = nc * CMP_STRIDE // SLC_BLOCK
    j = np.arange(n_slc)
    lo = np.clip((j * SLC_BLOCK - CMP_BLOCK) // CMP_STRIDE + 1, 0, n_cmp)
    hi = np.clip(-((-(j * SLC_BLOCK + SLC_BLOCK)) // CMP_STRIDE), 0, n_cmp)
    m = np.zeros((CMP_PAD + nc + SUBLANES, nbp), np.float32)
    n = np.arange(nc)[:, None]
    m[CMP_PAD:CMP_PAD + nc, :n_slc] = (n >= lo[None, :]) & (n < hi[None, :])
    return m


def _cmp_attn(q64, kv_c, tab_cmp, far_col, nbp):
    bsz, _, _, s, _ = q64.shape
    rows_c = kv_c.shape[3]
    nc = rows_c - CMP_PAD - SUBLANES
    nq = s // Q_BLOCK
    m_pad = jnp.asarray(_overlap_matrix(nc, nbp))
    return pl.pallas_call(
        _cmp_attn_kernel,
        grid=(bsz, N_KV, nq),
        in_specs=[pl.BlockSpec((1, 1, N_GRP, Q_BLOCK, HEAD_DIM), lambda b, k, i: (b, k, 0, i, 0)),
                  pl.BlockSpec((1, 1, 1, rows_c, HEAD_DIM), lambda b, k, i: (b, 0, k, 0, 0)),
                  pl.BlockSpec((1, 1, 1, rows_c, HEAD_DIM), lambda b, k, i: (b, 1, k, 0, 0)),
                  pl.BlockSpec((1, N_GRP, Q_BLOCK, CMP_NEAR), lambda b, k, i: (k, 0, 0, 0)),
                  pl.BlockSpec((1, N_GRP * Q_BLOCK, 1), lambda b, k, i: (k, 0, 0)),
                  pl.BlockSpec((rows_c, nbp), lambda b, k, i: (0, 0))],
        out_specs=[pl.BlockSpec((1, 1, N_GRP, Q_BLOCK, HEAD_DIM), lambda b, k, i: (b, k, 0, i, 0)),
                   pl.BlockSpec((1, 1, 1, nbp, Q_BLOCK), lambda b, k, i: (b, k, i, 0, 0))],
        out_shape=[jax.ShapeDtypeStruct((bsz, N_KV, N_GRP, s, HEAD_DIM), F32),
                   jax.ShapeDtypeStruct((bsz, N_KV, nq, nbp, Q_BLOCK), BF16)],
        scratch_shapes=[pltpu.VMEM((Q_BLOCK, nbp), F32)],
        compiler_params=_cparams(("parallel", "parallel", "arbitrary")),
        name="nsa_cmp_attn",
    )(q64, kv_c, kv_c, tab_cmp, far_col, m_pad)


def _slc_attn_kernel(qt_ref, far_ref, ka_ref, vt_ref, selt_ref, tab_ref, o_ref,
                     qa_ref, s0_ref, s1_ref, x0_ref, x1_ref, m_ref, acc_ref):
    i = pl.program_id(2)
    cols = N_GRP * Q_BLOCK
    ngroups = qa_ref.shape[0] // 2
    qt = qt_ref[0, 0, 0]
    qt_far = qt + far_ref[0]
    selneg = ((selt_ref[0, 0, 0].astype(F32) - 1.0) * MASK_BIG).astype(BF16)
    for g in range(ngroups):
        part = selneg[g * SLC_GROUP_BLOCKS:(g + 1) * SLC_GROUP_BLOCKS, :]
        part = jnp.concatenate([part] * N_GRP, axis=1)
        qa_ref[2 * g] = jnp.concatenate([qt_far, part], axis=0)
        qa_ref[2 * g + 1] = jnp.concatenate([qt, part], axis=0)

    tiles_per_group = SLC_GROUP_BLOCKS // SLC_TILE_BLOCKS
    sub = SLC_KEY_TILE // Q_BLOCK
    last_tile = vt_ref.shape[2] - 1
    n_pairs = (i // sub + 2) // 2
    n_far = jnp.maximum(0, (i - (SLC_NEAR_CHUNKS - 1)) // sub)
    far_pairs = jnp.maximum(0, (n_far - 1) // 2)
    first_table = jnp.where(n_far > 0, 2 * far_pairs + 1, 0)

    def produce(kt, ls, s_ref, mx_ref, with_table):
        near = (kt >= first_table).astype(jnp.int32)
        kc = jnp.minimum(kt, last_tile)
        ks = pl.multiple_of(kc * SLC_KEY_TILE, SLC_KEY_TILE)
        s = jnp.dot(ka_ref[0, 0, pl.ds(ks, SLC_KEY_TILE), :], qa_ref[2 * (kc // tiles_per_group) + near, :, ls],
                    preferred_element_type=F32)
        if with_table:
            chunks = []
            for a in range(sub):
                mm = i - sub * kt - a
                idx = jnp.where(mm < 0, SLC_TAB_MASKED, jnp.where(mm >= SLC_NEAR_CHUNKS, SLC_TAB_CONST, mm))
                chunks.append(tab_ref[0, idx, :, ls])
            s = s + jnp.concatenate(chunks, axis=0)
        s_ref[:, ls] = s
        mx_ref[:, ls] = jnp.max(s, axis=0, keepdims=True)

    def consume(kt, ls, s_ref, mx_ref):
        m_old = m_ref[:, ls]
        m_new = jnp.maximum(m_old, mx_ref[:, ls])
        m_ref[:, ls] = m_new
        p = jnp.exp(s_ref[:, ls] - m_new).astype(BF16)
        acc_ref[:, ls] = jnp.exp(m_old - m_new) * acc_ref[:, ls] + jnp.dot(
            vt_ref[0, 0, jnp.minimum(kt, last_tile)], p, preferred_element_type=F32)

    strips = [slice(c * SLC_LANE_STRIP, (c + 1) * SLC_LANE_STRIP) for c in range(cols // SLC_LANE_STRIP)]

    def pair_step(j, with_table):
        for ls in strips:
            produce(2 * j + 1, ls, s1_ref, x1_ref, with_table)
            consume(2 * j, ls, s0_ref, x0_ref)
        for ls in strips:
            produce(2 * j + 2, ls, s0_ref, x0_ref, with_table)
            consume(2 * j + 1, ls, s1_ref, x1_ref)

    m_ref[...] = jnp.full(m_ref.shape, -jnp.inf, F32)
    acc_ref[...] = jnp.zeros(acc_ref.shape, F32)

    @pl.when(n_far > 0)
    def _():
        for ls in strips:
            produce(0, ls, s0_ref, x0_ref, False)

    @pl.when(n_far == 0)
    def _():
        for ls in strips:
            produce(0, ls, s0_ref, x0_ref, True)

    def far_body(j, carry):
        pair_step(j, False)
        return carry

    def near_body(j, carry):
        pair_step(j, True)
        return carry

    lax.fori_loop(0, far_pairs, far_body, 0)
    lax.fori_loop(far_pairs, n_pairs, near_body, 0)
    o_ref[0, 0, 0] = acc_ref[:HEAD_DIM, :] / acc_ref[HEAD_DIM:HEAD_DIM + 1, :]


def _slc_attn(qt, far_t, k_aug, vt_aug, sel_t, tab_t):
    bsz, _, nq, _, cols = qt.shape
    s = k_aug.shape[2]
    nbp = sel_t.shape[3]
    ngroups = nbp // SLC_GROUP_BLOCKS
    nch = tab_t.shape[1]
    once = pl.Buffered(1)
    return pl.pallas_call(
        _slc_attn_kernel,
        grid=(bsz, N_KV, nq),
        in_specs=[pl.BlockSpec((1, 1, 1, LANES, cols), lambda b, k, i: (b, k, i, 0, 0)),
                  pl.BlockSpec((1, LANES, cols), lambda b, k, i: (k, 0, 0)),
                  pl.BlockSpec((1, 1, s, 2 * LANES), lambda b, k, i: (b, k, 0, 0), pipeline_mode=once),
                  pl.BlockSpec((1, 1, s // SLC_KEY_TILE, SLC_V_ROWS, SLC_KEY_TILE),
                               lambda b, k, i: (b, k, 0, 0, 0), pipeline_mode=once),
                  pl.BlockSpec((1, 1, 1, nbp, Q_BLOCK), lambda b, k, i: (b, k, i, 0, 0)),
                  pl.BlockSpec((1, nch, Q_BLOCK, cols), lambda b, k, i: (k, 0, 0, 0), pipeline_mode=once)],
        out_specs=pl.BlockSpec((1, 1, 1, HEAD_DIM, cols), lambda b, k, i: (b, k, i, 0, 0)),
        out_shape=jax.ShapeDtypeStruct((bsz, N_KV, nq, HEAD_DIM, cols), F32),
        scratch_shapes=[pltpu.VMEM((2 * ngroups, 2 * LANES, cols), BF16),
                        pltpu.VMEM((SLC_KEY_TILE, cols), F32),
                        pltpu.VMEM((SLC_KEY_TILE, cols), F32),
                        pltpu.VMEM((1, cols), F32),
                        pltpu.VMEM((1, cols), F32),
                        pltpu.VMEM((1, cols), F32),
                        pltpu.VMEM((SLC_V_ROWS, cols), F32)],
        compiler_params=_cparams(("parallel", "parallel", "arbitrary")),
        name="nsa_slc_attn",
    )(qt, far_t, k_aug, vt_aug, sel_t, tab_t)


def _win_attn_kernel(q_ref, k_ref, v_ref, tab_ref, o_ref):
    i = pl.program_id(2)
    rows = N_GRP * Q_BLOCK
    nw = WINDOW + Q_BLOCK
    q = q_ref[0, 0].reshape(rows, HEAD_DIM)
    qs = pl.multiple_of(i * Q_BLOCK, Q_BLOCK)
    k = k_ref[0, 0, pl.ds(qs, nw), :]
    v = v_ref[0, 0, pl.ds(qs, nw), :]
    s = _nt(q, k) + tab_ref[0].reshape(rows, nw)
    col = lax.broadcasted_iota(jnp.int32, (1, nw), 1)
    s = jnp.where(col >= WINDOW - i * Q_BLOCK, s, NEG_INF)
    m = jnp.max(s, axis=-1, keepdims=True)
    e = jnp.exp(s - m)
    p = e / jnp.sum(e, axis=-1, keepdims=True)
    o = jnp.dot(p.astype(BF16), v, preferred_element_type=F32)
    o_ref[0, 0] = o.reshape(N_GRP, Q_BLOCK, HEAD_DIM)


def _win_attn(q64, k_win, v_win, tab_win):
    bsz, _, _, s, _ = q64.shape
    nq = s // Q_BLOCK
    sp = k_win.shape[2]
    nw = WINDOW + Q_BLOCK
    return pl.pallas_call(
        _win_attn_kernel,
        grid=(bsz, N_KV, nq),
        in_specs=[pl.BlockSpec((1, 1, N_GRP, Q_BLOCK, HEAD_DIM), lambda b, k, i: (b, k, 0, i, 0)),
                  pl.BlockSpec((1, 1, sp, HEAD_DIM), lambda b, k, i: (b, k, 0, 0)),
                  pl.BlockSpec((1, 1, sp, HEAD_DIM), lambda b, k, i: (b, k, 0, 0)),
                  pl.BlockSpec((1, N_GRP, Q_BLOCK, nw), lambda b, k, i: (k, 0, 0, 0))],
        out_specs=pl.BlockSpec((1, 1, N_GRP, Q_BLOCK, HEAD_DIM), lambda b, k, i: (b, k, 0, i, 0)),
        out_shape=jax.ShapeDtypeStruct((bsz, N_KV, N_GRP, s, HEAD_DIM), F32),
        compiler_params=_cparams(("parallel", "parallel", "arbitrary")),
        name="nsa_win_attn",
    )(q64, k_win, v_win, tab_win)


def _ssd_kernel(xbc_ref, z_ref, dt_ref, cw_ref, cb_ref, dtb_ref, alog_ref, dskip_ref, nw_ref, o_ref,
                ext_ref, state_ref):
    ln = SSM_CHUNK
    d_ssm = z_ref.shape[2]
    nh = dtb_ref.shape[1]
    gw = d_ssm // SSM_GROUPS
    hpg = nh // SSM_GROUPS
    gn = SSM_GROUPS * SSM_STATE

    @pl.when(pl.program_id(1) == 0)
    def _():
        ext_ref[0:SUBLANES, :] = jnp.zeros((SUBLANES, ext_ref.shape[1]), F32)
        state_ref[...] = jnp.zeros(state_ref.shape, F32)

    ext_ref[SUBLANES:SUBLANES + ln, :] = xbc_ref[0]
    conv = cw_ref[0:1, :] * ext_ref[SUBLANES - CONV_WIDTH + 1:SUBLANES - CONV_WIDTH + 1 + ln, :]
    for k in range(1, CONV_WIDTH):
        lo = SUBLANES - CONV_WIDTH + 1 + k
        conv = conv + cw_ref[k:k + 1, :] * ext_ref[lo:lo + ln, :]
    conv = conv + cb_ref[...]
    ext_ref[0:SUBLANES, :] = xbc_ref[0, ln - SUBLANES:ln, :]
    xc = _silu(conv)
    xs = xc[:, :d_ssm]
    bm = xc[:, d_ssm:d_ssm + gn]
    cm = xc[:, d_ssm + gn:d_ssm + 2 * gn]

    xdt = dt_ref[0, :, :nh] + dtb_ref[...]
    dt = jnp.maximum(xdt, 0.0) + jnp.log1p(jnp.exp(-jnp.abs(xdt)))
    a = -jnp.exp(alog_ref[...])
    da = dt * a

    row = lax.broadcasted_iota(jnp.int32, (ln, ln), 0)
    colm = lax.broadcasted_iota(jnp.int32, (ln, ln), 1)
    causal = row >= colm
    acs = jnp.dot(causal.astype(F32), da, precision=HI, preferred_element_type=F32)
    eye = (lax.broadcasted_iota(jnp.int32, (2 * nh, 2 * nh), 0)
           == lax.broadcasted_iota(jnp.int32, (2 * nh, 2 * nh), 1)).astype(F32)
    rows_t = _nt(eye, jnp.concatenate([acs, dt], axis=1), precision=HI)
    expand = (lax.broadcasted_iota(jnp.int32, (GATE_PIECES * nh, d_ssm), 0) % nh
              == lax.broadcasted_iota(jnp.int32, (GATE_PIECES * nh, d_ssm), 1) // SSM_HEAD_DIM).astype(BF16)

    def spread(cols):
        pieces, rest = [], cols
        for _ in range(GATE_PIECES):
            piece = rest.astype(BF16)
            pieces.append(piece)
            rest = rest - piece.astype(F32)
        return jnp.dot(jnp.concatenate(pieces, axis=1), expand, preferred_element_type=F32)

    last = acs[ln - 1:ln, :]
    exp_acs_x = spread(jnp.exp(acs))
    w_x = spread(jnp.exp(last - acs) * dt)
    exp_last_x = exp_acs_x[ln - 1:ln, :]

    ys = []
    for g in range(SSM_GROUPS):
        cg = cm[:, g * SSM_STATE:(g + 1) * SSM_STATE].astype(BF16)
        bg32 = bm[:, g * SSM_STATE:(g + 1) * SSM_STATE]
        bg = bg32.astype(BF16)
        xg = xs[:, g * gw:(g + 1) * gw]
        cb = _nt(cg, bg)
        st = state_ref[g]
        y_state = jnp.dot(cg, st.astype(BF16), preferred_element_type=F32) * exp_acs_x[:, g * gw:(g + 1) * gw]
        y_heads = []
        for j in range(hpg):
            h = g * hpg + j
            seg = acs[:, h:h + 1] - rows_t[h:h + 1, :]
            decay = jnp.exp(jnp.where(causal, seg, -jnp.inf))
            mmat = cb * decay * rows_t[nh + h:nh + h + 1, :]
            xh = xg[:, j * SSM_HEAD_DIM:(j + 1) * SSM_HEAD_DIM].astype(BF16)
            y_heads.append(jnp.dot(mmat.astype(BF16), xh, preferred_element_type=F32))
        ys.append(jnp.concatenate(y_heads, axis=1) + y_state)
        xw = (xg * w_x[:, g * gw:(g + 1) * gw]).astype(BF16)
        state_ref[g] = st * exp_last_x[:, g * gw:(g + 1) * gw] + jnp.dot(
            bg32.T.astype(BF16), xw, preferred_element_type=F32)

    y = jnp.concatenate(ys, axis=1) + dskip_ref[...] * xs
    y = y * _silu(z_ref[0])
    outs = []
    for g in range(SSM_GROUPS):
        yg = y[:, g * gw:(g + 1) * gw]
        outs.append(yg * lax.rsqrt(jnp.mean(yg * yg, axis=-1, keepdims=True) + RMS_EPS))
    o_ref[0] = jnp.concatenate(outs, axis=1) * nw_ref[...]


def _ssd(proj, xbc_blk, z_blk, dt_blk, conv_w, conv_b, dt_bias, a_log, d_skip, norm_w):
    bsz, s, _ = proj.shape
    ch = xbc_blk[1]
    d_ssm = z_blk[1]
    nh = dt_bias.shape[0]
    gw = d_ssm // SSM_GROUPS
    nchunks = s // SSM_CHUNK
    full = lambda shape: pl.BlockSpec(shape, lambda b, c: (0,) * len(shape))
    cols = lambda blk: pl.BlockSpec((1, SSM_CHUNK, blk[1]), lambda b, c: (b, c, blk[0]))
    xbc = z = dt_raw = proj
    return pl.pallas_call(
        _ssd_kernel,
        grid=(bsz, nchunks),
        in_specs=[cols(xbc_blk), cols(z_blk), cols(dt_blk),
                  full((CONV_WIDTH, ch)), full((1, ch)), full((1, nh)), full((1, nh)),
                  full((1, d_ssm)), full((1, d_ssm))],
        out_specs=pl.BlockSpec((1, SSM_CHUNK, d_ssm), lambda b, c: (b, c, 0)),
        out_shape=jax.ShapeDtypeStruct((bsz, s, d_ssm), F32),
        scratch_shapes=[pltpu.VMEM((SUBLANES + SSM_CHUNK, ch), F32),
                        pltpu.VMEM((SSM_GROUPS, SSM_STATE, gw), F32)],
        compiler_params=_cparams(("parallel", "arbitrary")),
        name="ssd_scan",
    )(xbc, z, dt_raw, conv_w, conv_b.reshape(1, ch), dt_bias.reshape(1, nh), a_log.reshape(1, nh),
      jnp.repeat(d_skip, SSM_HEAD_DIM).reshape(1, d_ssm), norm_w.reshape(1, d_ssm))


def _mix_out_kernel(oc_ref, os_ref, ow_ref, gl_ref, ex_ref, an_ref, ssm_ref, w_ref, x_ref, g1_ref, o_ref):
    d_attn = oc_ref.shape[2]
    sig = jax.nn.sigmoid(gl_ref[0])
    pieces, rest = [], sig
    for _ in range(GATE_PIECES):
        piece = rest.astype(BF16)
        pieces.append(piece)
        rest = rest - piece.astype(F32)
    sig3 = jnp.concatenate(pieces, axis=1)
    gc = jnp.dot(sig3, ex_ref[0], preferred_element_type=F32)
    gs = jnp.dot(sig3, ex_ref[1], preferred_element_type=F32)
    gw = jnp.dot(sig3, ex_ref[2], preferred_element_type=F32)
    attn = gc * oc_ref[0] + gs * os_ref[0] + gw * ow_ref[0]
    attn = attn * lax.rsqrt(jnp.mean(attn * attn, axis=-1, keepdims=True) + RMS_EPS) * an_ref[...]
    mix = (jnp.dot(attn.astype(BF16), w_ref[:d_attn, :], preferred_element_type=F32)
           + jnp.dot(ssm_ref[0].astype(BF16), w_ref[d_attn:, :], preferred_element_type=F32))
    o_ref[0] = x_ref[0] + g1_ref[0] * mix


def _gate_expand():
    ex = np.zeros((3, LANES, N_HEADS * HEAD_DIM), np.float32)
    for r in range(3):
        for h in range(N_HEADS):
            ex[r, h * 3 + r, h * HEAD_DIM:(h + 1) * HEAD_DIM] = 1.0
    return np.concatenate([ex] * GATE_PIECES, axis=1)


def _mix_out(oc, os_, ow, gl, gl_blk, attn_norm, ssm, w_out_bf16, x, g1, tm=256):
    bsz, s, d = x.shape
    d_attn = oc.shape[2]
    d_ssm = ssm.shape[2]
    tok = lambda w: pl.BlockSpec((1, tm, w), lambda b, i: (b, i, 0))
    return pl.pallas_call(
        _mix_out_kernel,
        grid=(bsz, s // tm),
        in_specs=[tok(d_attn), tok(d_attn), tok(d_attn),
                  pl.BlockSpec((1, tm, gl_blk[1]), lambda b, i: (b, i, gl_blk[0])),
                  pl.BlockSpec((3, GATE_PIECES * LANES, d_attn), lambda b, i: (0, 0, 0)),
                  pl.BlockSpec((1, d_attn), lambda b, i: (0, 0)),
                  tok(d_ssm),
                  pl.BlockSpec((d_attn + d_ssm, d), lambda b, i: (0, 0)),
                  tok(d),
                  pl.BlockSpec((1, 1, d), lambda b, i: (b, 0, 0))],
        out_specs=tok(d),
        out_shape=jax.ShapeDtypeStruct((bsz, s, d), F32),
        compiler_params=_cparams(("parallel", "parallel")),
        name="mix_out",
    )(oc, os_, ow, gl, jnp.asarray(_gate_expand(), BF16), attn_norm.reshape(1, d_attn), ssm, w_out_bf16, x, g1)


def _peer_candidates():
    k, sub = PEER_TOPK, SUBLANES
    cells = [(0, b) for b in range(k)]
    cells += [(a, b) for a in range(1, sub) for b in range(sub)]
    cells += [(a, 0) for a in range(sub, k)]
    order = np.array([a * k + b if (a + 1) * (b + 1) <= k else -1 for a, b in cells], np.float32)
    group = np.zeros((k, LANES), np.float32)
    for row, (a, _) in enumerate(cells):
        group[a, row] = 1.0
    return cells, order, group


def _peer_route_kernel(q_ref, sk_ref, order_ref, group_ref, rank2_ref, e2_ref, n1_ref, c1_ref):
    k = PEER_TOPK

    def route(assume_distinct):
        q = q_ref[...]
        half = q.shape[1] // 2
        s1 = _nt(sk_ref[0], q[:, :half], precision=HI)
        s2 = _nt(sk_ref[1], q[:, half:], precision=HI)
        rank1, v1 = _topk_mark(s1, k, axis=0, assume_distinct=assume_distinct)
        rank2, v2 = _topk_mark(s2, k, axis=0, assume_distinct=assume_distinct)
        v1_all = jnp.concatenate(v1, axis=0)
        v2_all = jnp.concatenate(v2, axis=0)
        cand = jnp.concatenate([v1[0] + v2_all] + [v1[a] + v2_all[:SUBLANES] for a in range(1, SUBLANES)]
                               + [v1_all[SUBLANES:] + v2[0]], axis=0)
        order = order_ref[...]
        cand = jnp.where(order >= 0.0, cand, -jnp.inf)
        rank_c, best = _topk_mark(cand, k, axis=0, order=order, order_bound=float(k * k),
                                  assume_distinct=assume_distinct)
        chosen = jnp.where(rank_c < float(k), 1.0, 0.0).astype(BF16)
        chosen = jnp.concatenate([chosen, jnp.zeros((LANES - chosen.shape[0], chosen.shape[1]), BF16)], axis=0)
        count = jnp.dot(group_ref[...], chosen, preferred_element_type=F32)
        z = best[0] * 0.0
        for r in range(k):
            z = z + jnp.exp(best[r] - best[0])
        n1 = jnp.zeros(s1.shape, F32)
        for a in range(k):
            n1 = jnp.where(rank1 == float(a), count[a:a + 1, :], n1)
        rank2_ref[0] = rank2.astype(BF16)
        e2_ref[0] = jnp.exp(s2 - v2[0]).astype(BF16)
        n1_ref[0] = n1
        c1_ref[0] = jnp.exp(s1 - v1[0]) / z
        return jnp.maximum(jnp.maximum(_topk_clean(rank1, k, 0), _topk_clean(rank2, k, 0)),
                           _topk_clean(rank_c, k, 0))

    dirty = route(True)

    @pl.when(dirty > 0.0)
    def _():
        route(False)


def _peer_route(q, subkeys, tm=512):
    t, width = q.shape
    kd = width // PEER_HEADS
    nk = subkeys.shape[1]
    spec = pl.BlockSpec((1, nk, tm), lambda i, h: (h, 0, i))
    shape = jax.ShapeDtypeStruct((PEER_HEADS, nk, t), F32)
    cells, order, group = _peer_candidates()
    order = jnp.asarray(np.broadcast_to(order[:, None], (len(cells), tm)))
    return pl.pallas_call(
        _peer_route_kernel,
        grid=(t // tm, PEER_HEADS),
        in_specs=[pl.BlockSpec((tm, kd), lambda i, h: (i, h)),
                  pl.BlockSpec((2, nk, kd // 2), lambda i, h: (0, 0, 0)),
                  pl.BlockSpec((len(cells), tm), lambda i, h: (0, 0)),
                  pl.BlockSpec((PEER_TOPK, LANES), lambda i, h: (0, 0))],
        out_specs=[spec, spec, spec, spec],
        out_shape=[jax.ShapeDtypeStruct(shape.shape, BF16), jax.ShapeDtypeStruct(shape.shape, BF16), shape, shape],
        compiler_params=_cparams(("parallel", "parallel")),
        name="peer_route",
    )(q, subkeys, order, jnp.asarray(group, BF16))


def _peer_dense_kernel(ht_ref, u0_ref, un_ref, vt_ref, rank2_ref, e2_ref, n1_ref, c1_ref, o_ref, *scratch):
    j = pl.program_id(1)
    te = un_ref.shape[0]
    nk = rank2_ref.shape[1]
    tm = ht_ref.shape[1]
    strips = [slice(c * PEER_LANE_STRIP, (c + 1) * PEER_LANE_STRIP) for c in range(tm // PEER_LANE_STRIP)]

    units = [(e, c) for e in range(te // PEER_EXPERT_UNIT) for c in range(len(strips))]
    act_sets = (scratch[:len(units)], scratch[len(units):2 * len(units)])
    aw_refs = scratch[2 * len(units):]
    n_piece = PEER_EXPERT_UNIT // PEER_EXPERT_PIECE
    d_piece = o_ref.shape[0] // n_piece

    def produce(u_ref, act_refs, k, p):
        e, c = units[k]
        lo = e * PEER_EXPERT_UNIT + p * PEER_EXPERT_PIECE
        act_refs[k][p * PEER_EXPERT_PIECE:(p + 1) * PEER_EXPERT_PIECE, :] = jax.nn.gelu(
            jnp.dot(u_ref[lo:lo + PEER_EXPERT_PIECE, :], ht_ref[:, strips[c]], preferred_element_type=F32)
        ).astype(BF16)

    def gate(act_refs, k, r):
        e, c = units[k]
        ls = strips[c]
        i1 = (j * te + e * PEER_EXPERT_UNIT) // nk + r
        w = jnp.zeros((nk, PEER_LANE_STRIP), BF16)
        zero = jnp.zeros((nk, PEER_LANE_STRIP), BF16)
        for h in range(PEER_HEADS):
            n_row = n1_ref[h, pl.ds(i1, 1), ls].astype(BF16)
            c_row = c1_ref[h, pl.ds(i1, 1), ls].astype(BF16)
            w = w + jnp.where(rank2_ref[h, :, ls] < n_row, e2_ref[h, :, ls], zero) * c_row
        aw_refs[k][r * nk:(r + 1) * nk, :] = act_refs[k][r * nk:(r + 1) * nk, :] * w

    def combine(k, m):
        e, c = units[k]
        ds_ = slice(m * d_piece, (m + 1) * d_piece)
        es = slice(e * PEER_EXPERT_UNIT, (e + 1) * PEER_EXPERT_UNIT)
        o_ref[ds_, strips[c]] += jnp.dot(vt_ref[0, ds_, es], aw_refs[k][...], preferred_element_type=F32)

    @pl.when(j == 0)
    def _():
        o_ref[...] = jnp.zeros(o_ref.shape, F32)
        for k in range(len(units)):
            for p in range(n_piece):
                produce(u0_ref, act_sets[0], k, p)

    gates_per_piece = PEER_EXPERT_PIECE // nk

    def step(cur, nxt):
        for k in range(len(units)):
            for p in range(n_piece):
                produce(un_ref, nxt, k, p)
                for r in range(p * gates_per_piece, (p + 1) * gates_per_piece):
                    gate(cur, k, r)
                    if k > 0 and r == p * gates_per_piece:
                        combine(k - 1, p)
        for m in range(n_piece):
            combine(len(units) - 1, m)

    pl.when(j % 2 == 0)(functools.partial(step, act_sets[0], act_sets[1]))
    pl.when(j % 2 == 1)(functools.partial(step, act_sets[1], act_sets[0]))


def _peer_dense(h_t, u_bf16, v_bf16, rank2, e2, n1, c1, tm=512, te=512):
    d, t = h_t.shape
    n_exp = u_bf16.shape[0]
    nk = rank2.shape[1]
    v_t_bf16 = v_bf16.reshape(n_exp // te, te, d).transpose(0, 2, 1)
    n_units = (te // PEER_EXPERT_UNIT) * (tm // PEER_LANE_STRIP)
    n_chunks = n_exp // te
    assert n_chunks % 2 == 0
    route = pl.BlockSpec((PEER_HEADS, nk, tm), lambda i, j: (0, 0, i))
    return pl.pallas_call(
        _peer_dense_kernel,
        grid=(t // tm, n_chunks),
        in_specs=[pl.BlockSpec((d, tm), lambda i, j: (0, i)),
                  pl.BlockSpec((te, d), lambda i, j: (0, 0)),
                  pl.BlockSpec((te, d), lambda i, j: (jnp.minimum(j + 1, n_chunks - 1), 0)),
                  pl.BlockSpec((1, d, te), lambda i, j: (j, 0, 0)),
                  route, route, route, route],
        out_specs=pl.BlockSpec((d, tm), lambda i, j: (0, i)),
        out_shape=jax.ShapeDtypeStruct((d, t), F32),
        scratch_shapes=[pltpu.VMEM((PEER_EXPERT_UNIT, PEER_LANE_STRIP), BF16)] * (3 * n_units),
        compiler_params=_cparams(("parallel", "arbitrary")),
        name="peer_dense",
    )(h_t, u_bf16, u_bf16, v_t_bf16, rank2, e2, n1, c1)


def _residual_kernel(x_ref, yt_ref, g_ref, o_ref):
    o_ref[0] = x_ref[0] + g_ref[0] * yt_ref[...].T


def _residual_norm_kernel(x_ref, yt_ref, g_ref, w_ref, o_ref):
    x = x_ref[0] + g_ref[0] * yt_ref[...].T
    o_ref[0] = x * lax.rsqrt(jnp.mean(x * x, axis=-1, keepdims=True) + RMS_EPS) * w_ref[...]


def _residual(x, y_t, g, norm_w=None, tm=512):
    bsz, s, d = x.shape
    tok = pl.BlockSpec((1, tm, d), lambda b, i: (b, i, 0))
    in_specs = [tok, pl.BlockSpec((d, tm), lambda b, i: (0, b * (s // tm) + i)),
                pl.BlockSpec((1, 1, d), lambda b, i: (b, 0, 0))]
    args = [x, y_t, g]
    if norm_w is not None:
        in_specs.append(pl.BlockSpec((1, d), lambda b, i: (0, 0)))
        args.append(norm_w.reshape(1, d))
    return pl.pallas_call(
        _residual_kernel if norm_w is None else _residual_norm_kernel,
        grid=(bsz, s // tm),
        in_specs=in_specs,
        out_specs=tok,
        out_shape=jax.ShapeDtypeStruct((bsz, s, d), F32),
        compiler_params=_cparams(("parallel", "parallel")),
        name="residual",
    )(*args)


def _pad_cols(w, width):
    return jnp.pad(w, ((0, 0), (0, width - w.shape[1])))


def _nsa(q, kv, tables, cmp_pe, cmp_w1, cmp_w2):
    tab_win, tab_cmp, tab_slc, far_col, far_t = tables
    bsz, s, _ = q.shape
    nc = s // CMP_STRIDE
    nq = s // Q_BLOCK
    n_slc = s // SLC_BLOCK
    nbp = -(-n_slc // SLC_GROUP_BLOCKS) * SLC_GROUP_BLOCKS
    scale = HEAD_DIM ** -0.5
    q6 = (q * scale).astype(BF16).reshape(bsz, nq, Q_BLOCK, N_KV, N_GRP, HEAD_DIM)
    qh = q6.transpose(0, 3, 4, 1, 2, 5).reshape(bsz, N_KV, N_GRP, s, HEAD_DIM)
    qt = q6.transpose(0, 3, 1, 5, 4, 2).reshape(bsz, N_KV, nq, HEAD_DIM, N_GRP * Q_BLOCK)
    qt = jnp.pad(qt, ((0, 0),) * 3 + ((0, LANES - HEAD_DIM), (0, 0)))
    kv6 = kv.reshape(bsz, s, 6, N_KV, HEAD_DIM)

    kv_cmp = kv6[:, :, 0:2].reshape(bsz, nc, CMP_STRIDE, 2, N_KV, HEAD_DIM)
    kv_cmp = kv_cmp.transpose(0, 3, 4, 1, 2, 5).reshape(bsz, 2, N_KV, nc, CMP_STRIDE * HEAD_DIM)
    kv_c = _compress(kv_cmp, cmp_pe, cmp_w1, cmp_w2)
    o_c, sel_t = _cmp_attn(qh, kv_c, tab_cmp, far_col, nbp)

    k_slc = kv6[:, :, 2].transpose(0, 2, 1, 3).astype(BF16)
    nkt = s // SLC_KEY_TILE
    vt = kv6[:, :, 3].astype(BF16).reshape(bsz, nkt, SLC_KEY_TILE, N_KV, HEAD_DIM).transpose(0, 3, 1, 4, 2)
    vt_aug = jnp.concatenate([
        vt, jnp.ones((bsz, N_KV, nkt, 1, SLC_KEY_TILE), BF16),
        jnp.zeros((bsz, N_KV, nkt, SLC_V_ROWS - HEAD_DIM - 1, SLC_KEY_TILE), BF16)], axis=3)
    blk = np.arange(s) // SLC_BLOCK
    onehot = (blk[:, None] % SLC_GROUP_BLOCKS == np.arange(SLC_GROUP_BLOCKS)[None, :]).astype(np.float32)
    k_aug = jnp.concatenate([
        k_slc, jnp.ones((bsz, N_KV, s, SLC_BIAS_PIECES), BF16),
        jnp.zeros((bsz, N_KV, s, LANES - HEAD_DIM - SLC_BIAS_PIECES), BF16),
        jnp.broadcast_to(jnp.asarray(onehot, BF16), (bsz, N_KV, s, SLC_GROUP_BLOCKS))], axis=-1)
    o_st = _slc_attn(qt, far_t, k_aug, vt_aug, sel_t, tab_slc)
    o_s = o_st.reshape(bsz, N_KV, nq, HEAD_DIM, N_GRP, Q_BLOCK).transpose(0, 2, 5, 1, 4, 3)
    o_s = o_s.reshape(bsz, s, N_HEADS * HEAD_DIM)

    front = ((0, 0), (0, 0), (WINDOW, 0), (0, 0))
    k_win = jnp.pad(kv6[:, :, 4].transpose(0, 2, 1, 3).astype(BF16), front)
    v_win = jnp.pad(kv6[:, :, 5].transpose(0, 2, 1, 3).astype(BF16), front)
    o_w = _win_attn(qh, k_win, v_win, tab_win)

    back = lambda o: o.transpose(0, 3, 1, 2, 4).reshape(bsz, s, N_HEADS * HEAD_DIM)
    return back(o_c), o_s, back(o_w)


def kernel(x, c, ada_w, ada_b, norm_mix, norm_ffn, w_in, cmp_pe, cmp_w1, cmp_w2, rel_bias, attn_out_norm,
           conv_w, conv_b, dt_bias, a_log, d_skip, ssm_norm, w_out, peer_wq, peer_subkeys, peer_u, peer_v,
           norm_final):
    bsz, s, d = x.shape
    depth = ada_w.shape[0]
    d_attn = N_HEADS * HEAD_DIM
    n_kv = 6 * N_KV * HEAD_DIM
    n_gate = 3 * N_HEADS
    d_ssm = ssm_norm.shape[1]
    ch = conv_w.shape[2]
    nh = dt_bias.shape[1]

    mod = _ada_mod(c, ada_w, ada_b)
    tables = _bias_tables(rel_bias)

    for l in range(depth):
        sh1, sc1, g1, sh2, sc2, g2 = [mod[l, :, i * d:(i + 1) * d].reshape(bsz, 1, d) for i in range(6)]
        cuts = np.cumsum([0, d_attn, n_kv, n_gate, d_ssm, ch, nh])
        seg = dict(zip(("q", "kv", "gl", "z", "xbc", "dt"),
                       [w_in[l][:, cuts[i]:cuts[i + 1]] for i in range(6)]))
        seg["gl"] = _pad_cols(seg["gl"], LANES)
        seg["dt"] = _pad_cols(seg["dt"], LANES)
        order = ("q", "z", "kv", "gl", "dt", "xbc")
        w_cat = jnp.concatenate([seg[name] for name in order], axis=1).astype(BF16)
        start = dict(zip(order, np.cumsum([0] + [seg[name].shape[1] for name in order])[:-1]))
        blk = {name: (int(start[name]) // seg[name].shape[1], seg[name].shape[1]) for name in order}
        assert all(start[name] % seg[name].shape[1] == 0 for name in ("z", "gl", "dt", "xbc"))
        proj, _ = _norm_proj(x, norm_mix[l], sc1, sh1, w_cat)
        q = proj[:, :, start["q"]:start["q"] + d_attn]
        kv = proj[:, :, start["kv"]:start["kv"] + n_kv]

        o_c, o_s, o_w = _nsa(q, kv, tables, cmp_pe[l], cmp_w1[l], cmp_w2[l])
        ssm = _ssd(proj, blk["xbc"], blk["z"], blk["dt"], conv_w[l], conv_b[l], dt_bias[l], a_log[l], d_skip[l],
                   ssm_norm[l])
        x = _mix_out(o_c, o_s, o_w, proj, blk["gl"], attn_out_norm[l], ssm, w_out[l].astype(BF16), x, g1)

        pq, h2 = _norm_proj(x, norm_ffn[l], sc2, sh2, peer_wq[l].astype(BF16))
        rank2, e2, n1, c1 = _peer_route(pq.reshape(bsz * s, -1), peer_subkeys[l])
        h_t = h2.reshape(bsz * s, d).T
        ffn_t = _peer_dense(h_t, peer_u[l].astype(BF16), peer_v[l].astype(BF16), rank2, e2, n1, c1)
        x = _residual(x, ffn_t, g2, norm_final if l == depth - 1 else None)

    return x
```

```python
import functools
import math

import numpy as np
import jax
import jax.numpy as jnp
from jax import lax
from jax.experimental import pallas as pl
from jax.experimental.pallas import tpu as pltpu

F32 = jnp.float32
BF16 = jnp.bfloat16
HI = lax.Precision.HIGHEST

N_HEADS = 16
N_KV = 2
N_GRP = N_HEADS // N_KV
HEAD_DIM = 64
CMP_BLOCK = 32
CMP_STRIDE = 16
CMP_HIDDEN = 4 * HEAD_DIM
SLC_BLOCK = 64
SLC_TOPN = 16
WINDOW = 512
Q_BLOCK = 128
FORCE_SCORE = 1e4
NEG_INF = -1e30
REL_BUCKETS = 32
REL_MAX_DIST = 2048
SSM_HEAD_DIM = 64
SSM_GROUPS = 2
SSM_STATE = 128
CONV_WIDTH = 4
SSM_CHUNK = 256
PEER_HEADS = 8
PEER_TOPK = 16
RMS_EPS = 1e-6

LANES = 128
SUBLANES = 8
VMEM_LIMIT = 56 * 1024 * 1024

SLC_KEY_TILE = 512
SLC_TILE_BLOCKS = SLC_KEY_TILE // SLC_BLOCK
SLC_GROUP_BLOCKS = LANES
SLC_NEAR_CHUNKS = 14
SLC_TAB_MASKED = 14
SLC_TAB_CONST = 15
MASK_BIG = 2.0 ** 100
SLC_LANE_STRIP = 256
PEER_LANE_STRIP = 256
PEER_EXPERT_PIECE = 256
PEER_EXPERT_UNIT = 512
SLC_V_ROWS = HEAD_DIM + 16
SLC_BIAS_PIECES = 3
GATE_PIECES = 3

CMP_FAR_STEP = 256
CMP_NEAR = 128
CMP_PAD = CMP_NEAR - Q_BLOCK // CMP_STRIDE


def _cparams(sem, vmem=VMEM_LIMIT):
    return pltpu.CompilerParams(dimension_semantics=sem, vmem_limit_bytes=vmem)


def _nt(a, b, precision=None):
    return lax.dot_general(a, b, (((1,), (1,)), ((), ())), precision=precision,
                           preferred_element_type=F32)


def _silu(x):
    return x * jax.nn.sigmoid(x)


def _rel_bucket_np(d):
    d = np.maximum(np.asarray(d, np.int64), 0)
    max_exact = REL_BUCKETS // 2
    ratio = np.log(np.maximum(d, max_exact).astype(np.float64) / max_exact) / math.log(REL_MAX_DIST / max_exact)
    scaled = ratio * (REL_BUCKETS - max_exact)
    large = max_exact + np.floor(scaled).astype(np.int64)
    return np.where(d < max_exact, d, np.minimum(large, REL_BUCKETS - 1)).astype(np.int32)


def _ada_kernel(c_ref, w_ref, b_ref, o_ref):
    cond = _silu(c_ref[...])
    o_ref[0] = jnp.dot(cond, w_ref[0], precision=HI, preferred_element_type=F32) + b_ref[0]


def _ada_mod(c, ada_w, ada_b):
    depth, d, n = ada_w.shape
    bsz = c.shape[0]
    rows = SUBLANES
    c_pad = jnp.zeros((rows, d), F32).at[:bsz].set(c)
    tn = 1024
    out = pl.pallas_call(
        _ada_kernel,
        grid=(depth, n // tn),
        in_specs=[pl.BlockSpec((rows, d), lambda l, j: (0, 0)),
                  pl.BlockSpec((1, d, tn), lambda l, j: (l, 0, j)),
                  pl.BlockSpec((1, 1, tn), lambda l, j: (l, 0, j))],
        out_specs=pl.BlockSpec((1, rows, tn), lambda l, j: (l, 0, j)),
        out_shape=jax.ShapeDtypeStruct((depth, rows, n), F32),
        compiler_params=_cparams(("parallel", "parallel")),
        name="ada_mod",
    )(c_pad, ada_w, ada_b.reshape(depth, 1, n))
    return out[:, :bsz]


def _norm_proj_kernel(x_ref, nw_ref, sc_ref, sh_ref, w_ref, o_ref, h_ref, hs_ref):
    @pl.when(pl.program_id(2) == 0)
    def _():
        x = x_ref[0]
        y = x * lax.rsqrt(jnp.mean(x * x, axis=-1, keepdims=True) + RMS_EPS)
        h = (y * nw_ref[...]) * (1.0 + sc_ref[0]) + sh_ref[0]
        hs_ref[...] = h.astype(BF16)
        h_ref[0] = h.astype(BF16)

    o_ref[0] = jnp.dot(hs_ref[...], w_ref[...], preferred_element_type=F32)


def _norm_proj(x, nw, sc, sh, w_bf16, tm=1024, tn=512):
    bsz, s, d = x.shape
    n = w_bf16.shape[1]
    return pl.pallas_call(
        _norm_proj_kernel,
        grid=(bsz, s // tm, n // tn),
        in_specs=[pl.BlockSpec((1, tm, d), lambda b, i, j: (b, i, 0)),
                  pl.BlockSpec((1, d), lambda b, i, j: (0, 0)),
                  pl.BlockSpec((1, 1, d), lambda b, i, j: (b, 0, 0)),
                  pl.BlockSpec((1, 1, d), lambda b, i, j: (b, 0, 0)),
                  pl.BlockSpec((d, tn), lambda b, i, j: (0, j))],
        out_specs=[pl.BlockSpec((1, tm, tn), lambda b, i, j: (b, i, j)),
                   pl.BlockSpec((1, tm, d), lambda b, i, j: (b, i, 0))],
        out_shape=[jax.ShapeDtypeStruct((bsz, s, n), F32),
                   jax.ShapeDtypeStruct((bsz, s, d), BF16)],
        scratch_shapes=[pltpu.VMEM((tm, d), BF16)],
        compiler_params=_cparams(("parallel", "parallel", "arbitrary")),
        name="norm_proj",
    )(x, nw.reshape(1, d), sc, sh, w_bf16)


def _bias_kernel(rel_ref, bk_ref, o_ref):
    h = pl.program_id(0)
    bk = bk_ref[...]
    acc = jnp.full(bk.shape, NEG_INF, F32)
    for b in range(REL_BUCKETS):
        acc = jnp.where(bk == b, rel_ref[b, h], acc)
    o_ref[0] = acc


def _bias_tables(rel_bias):
    r = np.arange(Q_BLOCK)[:, None]
    dw = r - np.arange(WINDOW + Q_BLOCK)[None, :] + WINDOW
    win = np.where((dw >= 0) & (dw < WINDOW), _rel_bucket_np(dw), -1)
    off = CMP_STRIDE * CMP_PAD - (CMP_BLOCK - 1)
    dc = r + off - CMP_STRIDE * np.arange(CMP_NEAR)[None, :]
    cmp_near = np.where(dc >= 0, _rel_bucket_np(dc), -1)
    chunks = []
    for m in range(SLC_NEAR_CHUNKS):
        ds_ = Q_BLOCK * m + r - np.arange(Q_BLOCK)[None, :]
        chunks.append(np.where(ds_ >= 0, _rel_bucket_np(ds_), -1))
    assert _rel_bucket_np(Q_BLOCK * SLC_NEAR_CHUNKS - (Q_BLOCK - 1)) == REL_BUCKETS - 1
    assert _rel_bucket_np(off + CMP_STRIDE) == REL_BUCKETS - 1
    chunks.append(np.full((Q_BLOCK, Q_BLOCK), -1))
    chunks.append(np.full((Q_BLOCK, Q_BLOCK), REL_BUCKETS - 1))
    bk = np.concatenate([win, cmp_near] + chunks, axis=1).astype(np.int32)
    cols = bk.shape[1]
    out = pl.pallas_call(
        _bias_kernel,
        grid=(N_HEADS,),
        in_specs=[pl.BlockSpec(memory_space=pltpu.SMEM),
                  pl.BlockSpec((Q_BLOCK, cols), lambda h: (0, 0))],
        out_specs=pl.BlockSpec((1, Q_BLOCK, cols), lambda h: (h, 0, 0)),
        out_shape=jax.ShapeDtypeStruct((N_HEADS, Q_BLOCK, cols), F32),
        compiler_params=_cparams(("arbitrary",)),
        name="bias_tables",
    )(rel_bias, jnp.asarray(bk))
    nw = WINDOW + Q_BLOCK
    tab_win = out[:, :, :nw].reshape(N_KV, N_GRP, Q_BLOCK, nw)
    tab_cmp = out[:, :, nw:nw + CMP_NEAR].reshape(N_KV, N_GRP, Q_BLOCK, CMP_NEAR)
    nch = SLC_NEAR_CHUNKS + 2
    tab_slc = out[:, :, nw + CMP_NEAR:].reshape(N_KV, N_GRP, Q_BLOCK, nch, Q_BLOCK)
    tab_slc = tab_slc.transpose(0, 3, 4, 1, 2).reshape(N_KV, nch, Q_BLOCK, N_GRP * Q_BLOCK)
    far = rel_bias[REL_BUCKETS - 1].reshape(N_KV, N_GRP, 1)
    far_col = jnp.broadcast_to(far, (N_KV, N_GRP, Q_BLOCK)).reshape(N_KV, N_GRP * Q_BLOCK, 1)
    far_row = far_col.reshape(N_KV, 1, N_GRP * Q_BLOCK)
    pieces, rest = [], far_row
    for _ in range(SLC_BIAS_PIECES):
        piece = rest.astype(BF16)
        pieces.append(piece)
        rest = rest - piece.astype(F32)
    zeros = lambda w: jnp.zeros((N_KV, w, N_GRP * Q_BLOCK), BF16)
    far_t = jnp.concatenate([zeros(HEAD_DIM)] + pieces + [zeros(LANES - HEAD_DIM - SLC_BIAS_PIECES)], axis=1)
    return tab_win, tab_cmp, tab_slc, far_col, far_t


def _compress_kernel(a_ref, pe_ref, w1_ref, w2_ref, o_ref):
    a = a_ref[0, 0, 0]
    half = a.shape[1]
    lo = jnp.dot((a + pe_ref[0, 0:1, :]).astype(BF16), w1_ref[0, :half, :].astype(BF16),
                 preferred_element_type=F32)
    hi = jnp.dot((a + pe_ref[0, 1:2, :]).astype(BF16), w1_ref[0, half:, :].astype(BF16),
                 preferred_element_type=F32)
    nc = a.shape[0]
    hid = jax.nn.gelu(lo + pltpu.roll(hi, nc - 1, axis=0))
    out = jnp.dot(hid.astype(BF16), w2_ref[0].astype(BF16), preferred_element_type=F32)
    o_ref[0, 0, 0] = jnp.zeros(o_ref.shape[3:], F32)
    o_ref[0, 0, 0, CMP_PAD:CMP_PAD + nc, :] = out


def _compress(kv_cmp, cmp_pe, cmp_w1, cmp_w2):
    bsz, _, _, nc, half = kv_cmp.shape
    pe = cmp_pe.reshape(2, 2, half)
    rows = CMP_PAD + nc + SUBLANES
    return pl.pallas_call(
        _compress_kernel,
        grid=(bsz, 2, N_KV),
        in_specs=[pl.BlockSpec((1, 1, 1, nc, half), lambda b, w, k: (b, w, k, 0, 0)),
                  pl.BlockSpec((1, 2, half), lambda b, w, k: (w, 0, 0)),
                  pl.BlockSpec((1, 2 * half, CMP_HIDDEN), lambda b, w, k: (w, 0, 0)),
                  pl.BlockSpec((1, CMP_HIDDEN, HEAD_DIM), lambda b, w, k: (w, 0, 0))],
        out_specs=pl.BlockSpec((1, 1, 1, rows, HEAD_DIM), lambda b, w, k: (b, w, k, 0, 0)),
        out_shape=jax.ShapeDtypeStruct((bsz, 2, N_KV, rows, HEAD_DIM), F32),
        compiler_params=_cparams(("parallel", "parallel", "parallel")),
        name="nsa_compress",
    )(kv_cmp, pe, cmp_w1, cmp_w2)


def _topk_mark(vals, k, axis, order=None, order_bound=None, assume_distinct=False):
    n = vals.shape[axis] if order is None else order_bound
    if not assume_distinct:
        iota = lax.broadcasted_iota(jnp.int32, vals.shape, axis).astype(F32) if order is None else order
    rank = jnp.full(vals.shape, float(k), F32)
    work = vals
    picked = []
    for r in range(k):
        m = jnp.max(work, axis=axis, keepdims=True)
        if assume_distinct:
            hit = work == m
        else:
            ix = jnp.min(jnp.where(work == m, iota, float(n)), axis=axis, keepdims=True)
            hit = iota == ix
        rank = jnp.where(hit, float(r), rank)
        work = jnp.where(hit, -jnp.inf, work)
        picked.append(m)
    return rank, picked


def _topk_clean(rank, k, axis):
    marks = jnp.sum(jnp.where(rank < float(k), 1.0, 0.0), axis=axis, keepdims=True)
    return jnp.max(jnp.abs(marks - float(k)))


def _cmp_attn_kernel(q_ref, k_ref, v_ref, tab_ref, far_ref, m_ref, o_ref, sel_ref, imp_ref):
    i = pl.program_id(2)
    rows = N_GRP * Q_BLOCK
    nc = k_ref.shape[3] - CMP_PAD - SUBLANES
    nbp = sel_ref.shape[3]
    start = pl.multiple_of(i * (Q_BLOCK // CMP_STRIDE), SUBLANES)
    n_far = i * (Q_BLOCK // CMP_STRIDE) - CMP_PAD

    def attend(wf):
        q = q_ref[0, 0].reshape(rows, HEAD_DIM)
        k_far = k_ref[0, 0, 0, CMP_PAD:CMP_PAD + wf, :].astype(BF16)
        v_far = v_ref[0, 0, 0, CMP_PAD:CMP_PAD + wf, :].astype(BF16)
        k_near = k_ref[0, 0, 0, pl.ds(start, CMP_NEAR), :].astype(BF16)
        v_near = v_ref[0, 0, 0, pl.ds(start, CMP_NEAR), :].astype(BF16)

        n_idx = lax.broadcasted_iota(jnp.int32, (1, wf), 1)
        mask_far = n_idx < n_far
        s_far = jnp.where(mask_far, _nt(q, k_far) + far_ref[0], NEG_INF)
        tab = tab_ref[0].reshape(rows, CMP_NEAR)
        c_idx = lax.broadcasted_iota(jnp.int32, (1, CMP_NEAR), 1)
        mask_near = (c_idx >= -n_far) & (tab > 0.5 * NEG_INF)
        s_near = jnp.where(mask_near, _nt(q, k_near) + tab, NEG_INF)

        m = jnp.maximum(jnp.max(s_far, axis=-1, keepdims=True), jnp.max(s_near, axis=-1, keepdims=True))
        e_far = jnp.exp(s_far - m)
        e_near = jnp.exp(s_near - m)
        l = jnp.sum(e_far, axis=-1, keepdims=True) + jnp.sum(e_near, axis=-1, keepdims=True)
        p_far = jnp.where(mask_far, e_far / l, 0.0)
        p_near = jnp.where(mask_near, e_near / l, 0.0)
        o = (jnp.dot(p_far.astype(BF16), v_far, preferred_element_type=F32)
             + jnp.dot(p_near.astype(BF16), v_near, preferred_element_type=F32))
        o_ref[0, 0] = o.reshape(N_GRP, Q_BLOCK, HEAD_DIM)

        ps_far = jnp.sum(p_far.reshape(N_GRP, Q_BLOCK, wf), axis=0)
        ps_near = jnp.sum(p_near.reshape(N_GRP, Q_BLOCK, CMP_NEAR), axis=0)
        imp_ref[...] = (
            jnp.dot(ps_far, m_ref[CMP_PAD:CMP_PAD + wf, :], precision=HI, preferred_element_type=F32)
            + jnp.dot(ps_near, m_ref[pl.ds(start, CMP_NEAR), :], precision=HI, preferred_element_type=F32))

    step = min(CMP_FAR_STEP, nc)
    widths = list(range(step, nc + 1, step))
    for b, wf in enumerate(widths):
        lower = n_far > widths[b - 1] if b > 0 else True
        upper = n_far <= wf if b + 1 < len(widths) else True
        pl.when(jnp.logical_and(lower, upper))(functools.partial(attend, wf))

    imp = imp_ref[...].T
    t = i * Q_BLOCK + lax.broadcasted_iota(jnp.int32, (1, Q_BLOCK), 1)
    cur = t // SLC_BLOCK
    blk = lax.broadcasted_iota(jnp.int32, (nbp, 1), 0)
    forced = (blk == 0) | (blk == cur) | (blk == cur - 1)
    imp = jnp.where(forced, FORCE_SCORE, jnp.where(blk <= cur, imp, -FORCE_SCORE))
    n_blocks = (nc * CMP_STRIDE) // SLC_BLOCK
    imp = jnp.where(blk < n_blocks, imp, -jnp.inf)
    rank, _ = _topk_mark(imp, min(SLC_TOPN, n_blocks), axis=0)
    sel_ref[0, 0, 0] = jnp.where(rank < float(SLC_TOPN), 1.0, 0.0).astype(BF16)


def _overlap_matrix(nc, nbp):
    n_cmp = nc - 1
    n_slc = nc * CMP_STRIDE // SLC_BLOCK
    j = np.arange(n_slc)
    lo = np.clip((j * SLC_BLOCK - CMP_BLOCK) // CMP_STRIDE + 1, 0, n_cmp)
    hi = np.clip(-((-(j * SLC_BLOCK + SLC_BLOCK)) // CMP_STRIDE), 0, n_cmp)
    m = np.zeros((CMP_PAD + nc + SUBLANES, nbp), np.float32)
    n = np.arange(nc)[:, None]
    m[CMP_PAD:CMP_PAD + nc, :n_slc] = (n >= lo[None, :]) & (n < hi[None, :])
    return m


def _cmp_attn(q64, kv_c, tab_cmp, far_col, nbp):
    bsz, _, _, s, _ = q64.shape
    rows_c = kv_c.shape[3]
    nc = rows_c - CMP_PAD - SUBLANES
    nq = s // Q_BLOCK
    m_pad = jnp.asarray(_overlap_matrix(nc, nbp))
    return pl.pallas_call(
        _cmp_attn_kernel,
        grid=(bsz, N_KV, nq),
        in_specs=[pl.BlockSpec((1, 1, N_GRP, Q_BLOCK, HEAD_DIM), lambda b, k, i: (b, k, 0, i, 0)),
                  pl.BlockSpec((1, 1, 1, rows_c, HEAD_DIM), lambda b, k, i: (b, 0, k, 0, 0)),
                  pl.BlockSpec((1, 1, 1, rows_c, HEAD_DIM), lambda b, k, i: (b, 1, k, 0, 0)),
                  pl.BlockSpec((1, N_GRP, Q_BLOCK, CMP_NEAR), lambda b, k, i: (k, 0, 0, 0)),
                  pl.BlockSpec((1, N_GRP * Q_BLOCK, 1), lambda b, k, i: (k, 0, 0)),
                  pl.BlockSpec((rows_c, nbp), lambda b, k, i: (0, 0))],
        out_specs=[pl.BlockSpec((1, 1, N_GRP, Q_BLOCK, HEAD_DIM), lambda b, k, i: (b, k, 0, i, 0)),
                   pl.BlockSpec((1, 1, 1, nbp, Q_BLOCK), lambda b, k, i: (b, k, i, 0, 0))],
        out_shape=[jax.ShapeDtypeStruct((bsz, N_KV, N_GRP, s, HEAD_DIM), F32),
                   jax.ShapeDtypeStruct((bsz, N_KV, nq, nbp, Q_BLOCK), BF16)],
        scratch_shapes=[pltpu.VMEM((Q_BLOCK, nbp), F32)],
        compiler_params=_cparams(("parallel", "parallel", "arbitrary")),
        name="nsa_cmp_attn",
    )(q64, kv_c, kv_c, tab_cmp, far_col, m_pad)


def _slc_attn_kernel(qt_ref, far_ref, ka_ref, vt_ref, selt_ref, tab_ref, o_ref,
                     qa_ref, s0_ref, s1_ref, x0_ref, x1_ref, m_ref, acc_ref):
    i = pl.program_id(2)
    cols = N_GRP * Q_BLOCK
    ngroups = qa_ref.shape[0] // 2
    qt = qt_ref[0, 0, 0]
    qt_far = qt + far_ref[0]
    selneg = ((selt_ref[0, 0, 0].astype(F32) - 1.0) * MASK_BIG).astype(BF16)
    for g in range(ngroups):
        part = selneg[g * SLC_GROUP_BLOCKS:(g + 1) * SLC_GROUP_BLOCKS, :]
        part = jnp.concatenate([part] * N_GRP, axis=1)
        qa_ref[2 * g] = jnp.concatenate([qt_far, part], axis=0)
        qa_ref[2 * g + 1] = jnp.concatenate([qt, part], axis=0)

    tiles_per_group = SLC_GROUP_BLOCKS // SLC_TILE_BLOCKS
    sub = SLC_KEY_TILE // Q_BLOCK
    last_tile = vt_ref.shape[2] - 1
    n_pairs = (i // sub + 2) // 2
    n_far = jnp.maximum(0, (i - (SLC_NEAR_CHUNKS - 1)) // sub)
    far_pairs = jnp.maximum(0, (n_far - 1) // 2)
    first_table = jnp.where(n_far > 0, 2 * far_pairs + 1, 0)

    def produce(kt, ls, s_ref, mx_ref, with_table):
        near = (kt >= first_table).astype(jnp.int32)
        kc = jnp.minimum(kt, last_tile)
        ks = pl.multiple_of(kc * SLC_KEY_TILE, SLC_KEY_TILE)
        s = jnp.dot(ka_ref[0, 0, pl.ds(ks, SLC_KEY_TILE), :], qa_ref[2 * (kc // tiles_per_group) + near, :, ls],
                    preferred_element_type=F32)
        if with_table:
            chunks = []
            for a in range(sub):
                mm = i - sub * kt - a
                idx = jnp.where(mm < 0, SLC_TAB_MASKED, jnp.where(mm >= SLC_NEAR_CHUNKS, SLC_TAB_CONST, mm))
                chunks.append(tab_ref[0, idx, :, ls])
            s = s + jnp.concatenate(chunks, axis=0)
        s_ref[:, ls] = s
        mx_ref[:, ls] = jnp.max(s, axis=0, keepdims=True)

    def consume(kt, ls, s_ref, mx_ref):
        m_old = m_ref[:, ls]
        m_new = jnp.maximum(m_old, mx_ref[:, ls])
        m_ref[:, ls] = m_new
        p = jnp.exp(s_ref[:, ls] - m_new).astype(BF16)
        acc_ref[:, ls] = jnp.exp(m_old - m_new) * acc_ref[:, ls] + jnp.dot(
            vt_ref[0, 0, jnp.minimum(kt, last_tile)], p, preferred_element_type=F32)

    strips = [slice(c * SLC_LANE_STRIP, (c + 1) * SLC_LANE_STRIP) for c in range(cols // SLC_LANE_STRIP)]

    def pair_step(j, with_table):
        for ls in strips:
            produce(2 * j + 1, ls, s1_ref, x1_ref, with_table)
            consume(2 * j, ls, s0_ref, x0_ref)
        for ls in strips:
            produce(2 * j + 2, ls, s0_ref, x0_ref, with_table)
            consume(2 * j + 1, ls, s1_ref, x1_ref)

    m_ref[...] = jnp.full(m_ref.shape, -jnp.inf, F32)
    acc_ref[...] = jnp.zeros(acc_ref.shape, F32)

    @pl.when(n_far > 0)
    def _():
        for ls in strips:
            produce(0, ls, s0_ref, x0_ref, False)

    @pl.when(n_far == 0)
    def _():
        for ls in strips:
            produce(0, ls, s0_ref, x0_ref, True)

    def far_body(j, carry):
        pair_step(j, False)
        return carry

    def near_body(j, carry):
        pair_step(j, True)
        return carry

    lax.fori_loop(0, far_pairs, far_body, 0)
    lax.fori_loop(far_pairs, n_pairs, near_body, 0)
    o_ref[0, 0, 0] = acc_ref[:HEAD_DIM, :] / acc_ref[HEAD_DIM:HEAD_DIM + 1, :]


def _slc_attn(qt, far_t, k_aug, vt_aug, sel_t, tab_t):
    bsz, _, nq, _, cols = qt.shape
    s = k_aug.shape[2]
    nbp = sel_t.shape[3]
    ngroups = nbp // SLC_GROUP_BLOCKS
    nch = tab_t.shape[1]
    once = pl.Buffered(1)
    return pl.pallas_call(
        _slc_attn_kernel,
        grid=(bsz, N_KV, nq),
        in_specs=[pl.BlockSpec((1, 1, 1, LANES, cols), lambda b, k, i: (b, k, i, 0, 0)),
                  pl.BlockSpec((1, LANES, cols), lambda b, k, i: (k, 0, 0)),
                  pl.BlockSpec((1, 1, s, 2 * LANES), lambda b, k, i: (b, k, 0, 0), pipeline_mode=once),
                  pl.BlockSpec((1, 1, s // SLC_KEY_TILE, SLC_V_ROWS, SLC_KEY_TILE),
                               lambda b, k, i: (b, k, 0, 0, 0), pipeline_mode=once),
                  pl.BlockSpec((1, 1, 1, nbp, Q_BLOCK), lambda b, k, i: (b, k, i, 0, 0)),
                  pl.BlockSpec((1, nch, Q_BLOCK, cols), lambda b, k, i: (k, 0, 0, 0), pipeline_mode=once)],
        out_specs=pl.BlockSpec((1, 1, 1, HEAD_DIM, cols), lambda b, k, i: (b, k, i, 0, 0)),
        out_shape=jax.ShapeDtypeStruct((bsz, N_KV, nq, HEAD_DIM, cols), F32),
        scratch_shapes=[pltpu.VMEM((2 * ngroups, 2 * LANES, cols), BF16),
                        pltpu.VMEM((SLC_KEY_TILE, cols), F32),
                        pltpu.VMEM((SLC_KEY_TILE, cols), F32),
                        pltpu.VMEM((1, cols), F32),
                        pltpu.VMEM((1, cols), F32),
                        pltpu.VMEM((1, cols), F32),
                        pltpu.VMEM((SLC_V_ROWS, cols), F32)],
        compiler_params=_cparams(("parallel", "parallel", "arbitrary")),
        name="nsa_slc_attn",
    )(qt, far_t, k_aug, vt_aug, sel_t, tab_t)


def _win_attn_kernel(q_ref, k_ref, v_ref, tab_ref, o_ref):
    i = pl.program_id(2)
    rows = N_GRP * Q_BLOCK
    nw = WINDOW + Q_BLOCK
    q = q_ref[0, 0].reshape(rows, HEAD_DIM)
    qs = pl.multiple_of(i * Q_BLOCK, Q_BLOCK)
    k = k_ref[0, 0, pl.ds(qs, nw), :]
    v = v_ref[0, 0, pl.ds(qs, nw), :]
    s = _nt(q, k) + tab_ref[0].reshape(rows, nw)
    col = lax.broadcasted_iota(jnp.int32, (1, nw), 1)
    s = jnp.where(col >= WINDOW - i * Q_BLOCK, s, NEG_INF)
    m = jnp.max(s, axis=-1, keepdims=True)
    e = jnp.exp(s - m)
    p = e / jnp.sum(e, axis=-1, keepdims=True)
    o = jnp.dot(p.astype(BF16), v, preferred_element_type=F32)
    o_ref[0, 0] = o.reshape(N_GRP, Q_BLOCK, HEAD_DIM)


def _win_attn(q64, k_win, v_win, tab_win):
    bsz, _, _, s, _ = q64.shape
    nq = s // Q_BLOCK
    sp = k_win.shape[2]
    nw = WINDOW + Q_BLOCK
    return pl.pallas_call(
        _win_attn_kernel,
        grid=(bsz, N_KV, nq),
        in_specs=[pl.BlockSpec((1, 1, N_GRP, Q_BLOCK, HEAD_DIM), lambda b, k, i: (b, k, 0, i, 0)),
                  pl.BlockSpec((1, 1, sp, HEAD_DIM), lambda b, k, i: (b, k, 0, 0)),
                  pl.BlockSpec((1, 1, sp, HEAD_DIM), lambda b, k, i: (b, k, 0, 0)),
                  pl.BlockSpec((1, N_GRP, Q_BLOCK, nw), lambda b, k, i: (k, 0, 0, 0))],
        out_specs=pl.BlockSpec((1, 1, N_GRP, Q_BLOCK, HEAD_DIM), lambda b, k, i: (b, k, 0, i, 0)),
        out_shape=jax.ShapeDtypeStruct((bsz, N_KV, N_GRP, s, HEAD_DIM), F32),
        compiler_params=_cparams(("parallel", "parallel", "arbitrary")),
        name="nsa_win_attn",
    )(q64, k_win, v_win, tab_win)


def _ssd_kernel(xbc_ref, z_ref, dt_ref, cw_ref, cb_ref, dtb_ref, alog_ref, dskip_ref, nw_ref, o_ref,
                ext_ref, state_ref):
    ln = SSM_CHUNK
    d_ssm = z_ref.shape[2]
    nh = dtb_ref.shape[1]
    gw = d_ssm // SSM_GROUPS
    hpg = nh // SSM_GROUPS
    gn = SSM_GROUPS * SSM_STATE

    @pl.when(pl.program_id(1) == 0)
    def _():
        ext_ref[0:SUBLANES, :] = jnp.zeros((SUBLANES, ext_ref.shape[1]), F32)
        state_ref[...] = jnp.zeros(state_ref.shape, F32)

    ext_ref[SUBLANES:SUBLANES + ln, :] = xbc_ref[0]
    conv = cw_ref[0:1, :] * ext_ref[SUBLANES - CONV_WIDTH + 1:SUBLANES - CONV_WIDTH + 1 + ln, :]
    for k in range(1, CONV_WIDTH):
        lo = SUBLANES - CONV_WIDTH + 1 + k
        conv = conv + cw_ref[k:k + 1, :] * ext_ref[lo:lo + ln, :]
    conv = conv + cb_ref[...]
    ext_ref[0:SUBLANES, :] = xbc_ref[0, ln - SUBLANES:ln, :]
    xc = _silu(conv)
    xs = xc[:, :d_ssm]
    bm = xc[:, d_ssm:d_ssm + gn]
    cm = xc[:, d_ssm + gn:d_ssm + 2 * gn]

    xdt = dt_ref[0, :, :nh] + dtb_ref[...]
    dt = jnp.maximum(xdt, 0.0) + jnp.log1p(jnp.exp(-jnp.abs(xdt)))
    a = -jnp.exp(alog_ref[...])
    da = dt * a

    row = lax.broadcasted_iota(jnp.int32, (ln, ln), 0)
    colm = lax.broadcasted_iota(jnp.int32, (ln, ln), 1)
    causal = row >= colm
    acs = jnp.dot(causal.astype(F32), da, precision=HI, preferred_element_type=F32)
    eye = (lax.broadcasted_iota(jnp.int32, (2 * nh, 2 * nh), 0)
           == lax.broadcasted_iota(jnp.int32, (2 * nh, 2 * nh), 1)).astype(F32)
    rows_t = _nt(eye, jnp.concatenate([acs, dt], axis=1), precision=HI)
    expand = (lax.broadcasted_iota(jnp.int32, (GATE_PIECES * nh, d_ssm), 0) % nh
              == lax.broadcasted_iota(jnp.int32, (GATE_PIECES * nh, d_ssm), 1) // SSM_HEAD_DIM).astype(BF16)

    def spread(cols):
        pieces, rest = [], cols
        for _ in range(GATE_PIECES):
            piece = rest.astype(BF16)
            pieces.append(piece)
            rest = rest - piece.astype(F32)
        return jnp.dot(jnp.concatenate(pieces, axis=1), expand, preferred_element_type=F32)

    last = acs[ln - 1:ln, :]
    exp_acs_x = spread(jnp.exp(acs))
    w_x = spread(jnp.exp(last - acs) * dt)
    exp_last_x = exp_acs_x[ln - 1:ln, :]

    ys = []
    for g in range(SSM_GROUPS):
        cg = cm[:, g * SSM_STATE:(g + 1) * SSM_STATE].astype(BF16)
        bg32 = bm[:, g * SSM_STATE:(g + 1) * SSM_STATE]
        bg = bg32.astype(BF16)
        xg = xs[:, g * gw:(g + 1) * gw]
        cb = _nt(cg, bg)
        st = state_ref[g]
        y_state = jnp.dot(cg, st.astype(BF16), preferred_element_type=F32) * exp_acs_x[:, g * gw:(g + 1) * gw]
        y_heads = []
        for j in range(hpg):
            h = g * hpg + j
            seg = acs[:, h:h + 1] - rows_t[h:h + 1, :]
            decay = jnp.exp(jnp.where(causal, seg, -jnp.inf))
            mmat = cb * decay * rows_t[nh + h:nh + h + 1, :]
            xh = xg[:, j * SSM_HEAD_DIM:(j + 1) * SSM_HEAD_DIM].astype(BF16)
            y_heads.append(jnp.dot(mmat.astype(BF16), xh, preferred_element_type=F32))
        ys.append(jnp.concatenate(y_heads, axis=1) + y_state)
        xw = (xg * w_x[:, g * gw:(g + 1) * gw]).astype(BF16)
        state_ref[g] = st * exp_last_x[:, g * gw:(g + 1) * gw] + jnp.dot(
            bg32.T.astype(BF16), xw, preferred_element_type=F32)

    y = jnp.concatenate(ys, axis=1) + dskip_ref[...] * xs
    y = y * _silu(z_ref[0])
    outs = []
    for g in range(SSM_GROUPS):
        yg = y[:, g * gw:(g + 1) * gw]
        outs.append(yg * lax.rsqrt(jnp.mean(yg * yg, axis=-1, keepdims=True) + RMS_EPS))
    o_ref[0] = jnp.concatenate(outs, axis=1) * nw_ref[...]


def _ssd(proj, xbc_blk, z_blk, dt_blk, conv_w, conv_b, dt_bias, a_log, d_skip, norm_w):
    bsz, s, _ = proj.shape
    ch = xbc_blk[1]
    d_ssm = z_blk[1]
    nh = dt_bias.shape[0]
    gw = d_ssm // SSM_GROUPS
    nchunks = s // SSM_CHUNK
    full = lambda shape: pl.BlockSpec(shape, lambda b, c: (0,) * len(shape))
    cols = lambda blk: pl.BlockSpec((1, SSM_CHUNK, blk[1]), lambda b, c: (b, c, blk[0]))
    xbc = z = dt_raw = proj
    return pl.pallas_call(
        _ssd_kernel,
        grid=(bsz, nchunks),
        in_specs=[cols(xbc_blk), cols(z_blk), cols(dt_blk),
                  full((CONV_WIDTH, ch)), full((1, ch)), full((1, nh)), full((1, nh)),
                  full((1, d_ssm)), full((1, d_ssm))],
        out_specs=pl.BlockSpec((1, SSM_CHUNK, d_ssm), lambda b, c: (b, c, 0)),
        out_shape=jax.ShapeDtypeStruct((bsz, s, d_ssm), F32),
        scratch_shapes=[pltpu.VMEM((SUBLANES + SSM_CHUNK, ch), F32),
                        pltpu.VMEM((SSM_GROUPS, SSM_STATE, gw), F32)],
        compiler_params=_cparams(("parallel", "arbitrary")),
        name="ssd_scan",
    )(xbc, z, dt_raw, conv_w, conv_b.reshape(1, ch), dt_bias.reshape(1, nh), a_log.reshape(1, nh),
      jnp.repeat(d_skip, SSM_HEAD_DIM).reshape(1, d_ssm), norm_w.reshape(1, d_ssm))


def _mix_out_kernel(oc_ref, os_ref, ow_ref, gl_ref, ex_ref, an_ref, ssm_ref, w_ref, x_ref, g1_ref, o_ref):
    d_attn = oc_ref.shape[2]
    sig = jax.nn.sigmoid(gl_ref[0])
    pieces, rest = [], sig
    for _ in range(GATE_PIECES):
        piece = rest.astype(BF16)
        pieces.append(piece)
        rest = rest - piece.astype(F32)
    sig3 = jnp.concatenate(pieces, axis=1)
    gc = jnp.dot(sig3, ex_ref[0], preferred_element_type=F32)
    gs = jnp.dot(sig3, ex_ref[1], preferred_element_type=F32)
    gw = jnp.dot(sig3, ex_ref[2], preferred_element_type=F32)
    attn = gc * oc_ref[0] + gs * os_ref[0] + gw * ow_ref[0]
    attn = attn * lax.rsqrt(jnp.mean(attn * attn, axis=-1, keepdims=True) + RMS_EPS) * an_ref[...]
    mix = (jnp.dot(attn.astype(BF16), w_ref[:d_attn, :], preferred_element_type=F32)
           + jnp.dot(ssm_ref[0].astype(BF16), w_ref[d_attn:, :], preferred_element_type=F32))
    o_ref[0] = x_ref[0] + g1_ref[0] * mix


def _gate_expand():
    ex = np.zeros((3, LANES, N_HEADS * HEAD_DIM), np.float32)
    for r in range(3):
        for h in range(N_HEADS):
            ex[r, h * 3 + r, h * HEAD_DIM:(h + 1) * HEAD_DIM] = 1.0
    return np.concatenate([ex] * GATE_PIECES, axis=1)


def _mix_out(oc, os_, ow, gl, gl_blk, attn_norm, ssm, w_out_bf16, x, g1, tm=256):
    bsz, s, d = x.shape
    d_attn = oc.shape[2]
    d_ssm = ssm.shape[2]
    tok = lambda w: pl.BlockSpec((1, tm, w), lambda b, i: (b, i, 0))
    return pl.pallas_call(
        _mix_out_kernel,
        grid=(bsz, s // tm),
        in_specs=[tok(d_attn), tok(d_attn), tok(d_attn),
                  pl.BlockSpec((1, tm, gl_blk[1]), lambda b, i: (b, i, gl_blk[0])),
                  pl.BlockSpec((3, GATE_PIECES * LANES, d_attn), lambda b, i: (0, 0, 0)),
                  pl.BlockSpec((1, d_attn), lambda b, i: (0, 0)),
                  tok(d_ssm),
                  pl.BlockSpec((d_attn + d_ssm, d), lambda b, i: (0, 0)),
                  tok(d),
                  pl.BlockSpec((1, 1, d), lambda b, i: (b, 0, 0))],
        out_specs=tok(d),
        out_shape=jax.ShapeDtypeStruct((bsz, s, d), F32),
        compiler_params=_cparams(("parallel", "parallel")),
        name="mix_out",
    )(oc, os_, ow, gl, jnp.asarray(_gate_expand(), BF16), attn_norm.reshape(1, d_attn), ssm, w_out_bf16, x, g1)


def _peer_candidates():
    k, sub = PEER_TOPK, SUBLANES
    cells = [(0, b) for b in range(k)]
    cells += [(a, b) for a in range(1, sub) for b in range(sub)]
    cells += [(a, 0) for a in range(sub, k)]
    order = np.array([a * k + b if (a + 1) * (b + 1) <= k else -1 for a, b in cells], np.float32)
    group = np.zeros((k, LANES), np.float32)
    for row, (a, _) in enumerate(cells):
        group[a, row] = 1.0
    return cells, order, group


def _peer_route_kernel(q_ref, sk_ref, order_ref, group_ref, rank2_ref, e2_ref, n1_ref, c1_ref):
    k = PEER_TOPK

    def route(assume_distinct):
        q = q_ref[...]
        half = q.shape[1] // 2
        s1 = _nt(sk_ref[0], q[:, :half], precision=HI)
        s2 = _nt(sk_ref[1], q[:, half:], precision=HI)
        rank1, v1 = _topk_mark(s1, k, axis=0, assume_distinct=assume_distinct)
        rank2, v2 = _topk_mark(s2, k, axis=0, assume_distinct=assume_distinct)
        v1_all = jnp.concatenate(v1, axis=0)
        v2_all = jnp.concatenate(v2, axis=0)
        cand = jnp.concatenate([v1[0] + v2_all] + [v1[a] + v2_all[:SUBLANES] for a in range(1, SUBLANES)]
                               + [v1_all[SUBLANES:] + v2[0]], axis=0)
        order = order_ref[...]
        cand = jnp.where(order >= 0.0, cand, -jnp.inf)
        rank_c, best = _topk_mark(cand, k, axis=0, order=order, order_bound=float(k * k),
                                  assume_distinct=assume_distinct)
        chosen = jnp.where(rank_c < float(k), 1.0, 0.0).astype(BF16)
        chosen = jnp.concatenate([chosen, jnp.zeros((LANES - chosen.shape[0], chosen.shape[1]), BF16)], axis=0)
        count = jnp.dot(group_ref[...], chosen, preferred_element_type=F32)
        z = best[0] * 0.0
        for r in range(k):
            z = z + jnp.exp(best[r] - best[0])
        n1 = jnp.zeros(s1.shape, F32)
        for a in range(k):
            n1 = jnp.where(rank1 == float(a), count[a:a + 1, :], n1)
        rank2_ref[0] = rank2.astype(BF16)
        e2_ref[0] = jnp.exp(s2 - v2[0]).astype(BF16)
        n1_ref[0] = n1
        c1_ref[0] = jnp.exp(s1 - v1[0]) / z
        return jnp.maximum(jnp.maximum(_topk_clean(rank1, k, 0), _topk_clean(rank2, k, 0)),
                           _topk_clean(rank_c, k, 0))

    dirty = route(True)

    @pl.when(dirty > 0.0)
    def _():
        route(False)


def _peer_route(q, subkeys, tm=512):
    t, width = q.shape
    kd = width // PEER_HEADS
    nk = subkeys.shape[1]
    spec = pl.BlockSpec((1, nk, tm), lambda i, h: (h, 0, i))
    shape = jax.ShapeDtypeStruct((PEER_HEADS, nk, t), F32)
    cells, order, group = _peer_candidates()
    order = jnp.asarray(np.broadcast_to(order[:, None], (len(cells), tm)))
    return pl.pallas_call(
        _peer_route_kernel,
        grid=(t // tm, PEER_HEADS),
        in_specs=[pl.BlockSpec((tm, kd), lambda i, h: (i, h)),
                  pl.BlockSpec((2, nk, kd // 2), lambda i, h: (0, 0, 0)),
                  pl.BlockSpec((len(cells), tm), lambda i, h: (0, 0)),
                  pl.BlockSpec((PEER_TOPK, LANES), lambda i, h: (0, 0))],
        out_specs=[spec, spec, spec, spec],
        out_shape=[jax.ShapeDtypeStruct(shape.shape, BF16), jax.ShapeDtypeStruct(shape.shape, BF16), shape, shape],
        compiler_params=_cparams(("parallel", "parallel")),
        name="peer_route",
    )(q, subkeys, order, jnp.asarray(group, BF16))


def _peer_dense_kernel(ht_ref, u_ref, vt_ref, rank2_ref, e2_ref, n1_ref, c1_ref, o_ref, *scratch):
    j = pl.program_id(1)
    te = u_ref.shape[0]
    nk = rank2_ref.shape[1]

    @pl.when(j == 0)
    def _():
        o_ref[...] = jnp.zeros(o_ref.shape, F32)

    tm = ht_ref.shape[1]
    strips = [slice(c * PEER_LANE_STRIP, (c + 1) * PEER_LANE_STRIP) for c in range(tm // PEER_LANE_STRIP)]

    units = [(e, c) for e in range(te // PEER_EXPERT_UNIT) for c in range(len(strips))]
    act_refs = scratch[:len(units)]
    aw_refs = scratch[len(units):]
    n_piece = PEER_EXPERT_UNIT // PEER_EXPERT_PIECE
    d_piece = o_ref.shape[0] // n_piece

    def produce(k, p):
        e, c = units[k]
        lo = e * PEER_EXPERT_UNIT + p * PEER_EXPERT_PIECE
        act_refs[k][p * PEER_EXPERT_PIECE:(p + 1) * PEER_EXPERT_PIECE, :] = jax.nn.gelu(
            jnp.dot(u_ref[lo:lo + PEER_EXPERT_PIECE, :], ht_ref[:, strips[c]], preferred_element_type=F32)
        ).astype(BF16)

    def gate(k, r):
        e, c = units[k]
        ls = strips[c]
        i1 = (j * te + e * PEER_EXPERT_UNIT) // nk + r
        w = jnp.zeros((nk, PEER_LANE_STRIP), BF16)
        zero = jnp.zeros((nk, PEER_LANE_STRIP), BF16)
        for h in range(PEER_HEADS):
            n_row = n1_ref[h, pl.ds(i1, 1), ls].astype(BF16)
            c_row = c1_ref[h, pl.ds(i1, 1), ls].astype(BF16)
            w = w + jnp.where(rank2_ref[h, :, ls] < n_row, e2_ref[h, :, ls], zero) * c_row
        aw_refs[k][r * nk:(r + 1) * nk, :] = act_refs[k][r * nk:(r + 1) * nk, :] * w

    def combine(k, m):
        e, c = units[k]
        ds_ = slice(m * d_piece, (m + 1) * d_piece)
        es = slice(e * PEER_EXPERT_UNIT, (e + 1) * PEER_EXPERT_UNIT)
        o_ref[ds_, strips[c]] += jnp.dot(vt_ref[ds_, es], aw_refs[k][...], preferred_element_type=F32)

    gates_per_piece = PEER_EXPERT_PIECE // nk
    for p in range(n_piece):
        produce(0, p)
    for k in range(len(units)):
        for p in range(n_piece):
            if k + 1 < len(units):
                produce(k + 1, p)
            for r in range(p * gates_per_piece, (p + 1) * gates_per_piece):
                gate(k, r)
                if k > 0 and r == p * gates_per_piece:
                    combine(k - 1, p)
    for m in range(n_piece):
        combine(len(units) - 1, m)


def _peer_dense(h_t, u_bf16, v_t_bf16, rank2, e2, n1, c1, tm=512, te=512):
    d, t = h_t.shape
    n_exp = u_bf16.shape[0]
    nk = rank2.shape[1]
    n_units = (te // PEER_EXPERT_UNIT) * (tm // PEER_LANE_STRIP)
    route = pl.BlockSpec((PEER_HEADS, nk, tm), lambda i, j: (0, 0, i))
    return pl.pallas_call(
        _peer_dense_kernel,
        grid=(t // tm, n_exp // te),
        in_specs=[pl.BlockSpec((d, tm), lambda i, j: (0, i)),
                  pl.BlockSpec((te, d), lambda i, j: (j, 0)),
                  pl.BlockSpec((d, te), lambda i, j: (0, j)),
                  route, route, route, route],
        out_specs=pl.BlockSpec((d, tm), lambda i, j: (0, i)),
        out_shape=jax.ShapeDtypeStruct((d, t), F32),
        scratch_shapes=[pltpu.VMEM((PEER_EXPERT_UNIT, PEER_LANE_STRIP), BF16)] * (2 * n_units),
        compiler_params=_cparams(("parallel", "arbitrary")),
        name="peer_dense",
    )(h_t, u_bf16, v_t_bf16, rank2, e2, n1, c1)


def _residual_kernel(x_ref, yt_ref, g_ref, o_ref):
    o_ref[0] = x_ref[0] + g_ref[0] * yt_ref[...].T


def _residual_norm_kernel(x_ref, yt_ref, g_ref, w_ref, o_ref):
    x = x_ref[0] + g_ref[0] * yt_ref[...].T
    o_ref[0] = x * lax.rsqrt(jnp.mean(x * x, axis=-1, keepdims=True) + RMS_EPS) * w_ref[...]


def _residual(x, y_t, g, norm_w=None, tm=512):
    bsz, s, d = x.shape
    tok = pl.BlockSpec((1, tm, d), lambda b, i: (b, i, 0))
    in_specs = [tok, pl.BlockSpec((d, tm), lambda b, i: (0, b * (s // tm) + i)),
                pl.BlockSpec((1, 1, d), lambda b, i: (b, 0, 0))]
    args = [x, y_t, g]
    if norm_w is not None:
        in_specs.append(pl.BlockSpec((1, d), lambda b, i: (0, 0)))
        args.append(norm_w.reshape(1, d))
    return pl.pallas_call(
        _residual_kernel if norm_w is None else _residual_norm_kernel,
        grid=(bsz, s // tm),
        in_specs=in_specs,
        out_specs=tok,
        out_shape=jax.ShapeDtypeStruct((bsz, s, d), F32),
        compiler_params=_cparams(("parallel", "parallel")),
        name="residual",
    )(*args)


def _pad_cols(w, width):
    return jnp.pad(w, ((0, 0), (0, width - w.shape[1])))


def _nsa(q, kv, tables, cmp_pe, cmp_w1, cmp_w2):
    tab_win, tab_cmp, tab_slc, far_col, far_t = tables
    bsz, s, _ = q.shape
    nc = s // CMP_STRIDE
    nq = s // Q_BLOCK
    n_slc = s // SLC_BLOCK
    nbp = -(-n_slc // SLC_GROUP_BLOCKS) * SLC_GROUP_BLOCKS
    scale = HEAD_DIM ** -0.5
    q6 = (q * scale).astype(BF16).reshape(bsz, nq, Q_BLOCK, N_KV, N_GRP, HEAD_DIM)
    qh = q6.transpose(0, 3, 4, 1, 2, 5).reshape(bsz, N_KV, N_GRP, s, HEAD_DIM)
    qt = q6.transpose(0, 3, 1, 5, 4, 2).reshape(bsz, N_KV, nq, HEAD_DIM, N_GRP * Q_BLOCK)
    qt = jnp.pad(qt, ((0, 0),) * 3 + ((0, LANES - HEAD_DIM), (0, 0)))
    kv6 = kv.reshape(bsz, s, 6, N_KV, HEAD_DIM)

    kv_cmp = kv6[:, :, 0:2].reshape(bsz, nc, CMP_STRIDE, 2, N_KV, HEAD_DIM)
    kv_cmp = kv_cmp.transpose(0, 3, 4, 1, 2, 5).reshape(bsz, 2, N_KV, nc, CMP_STRIDE * HEAD_DIM)
    kv_c = _compress(kv_cmp, cmp_pe, cmp_w1, cmp_w2)
    o_c, sel_t = _cmp_attn(qh, kv_c, tab_cmp, far_col, nbp)

    k_slc = kv6[:, :, 2].transpose(0, 2, 1, 3).astype(BF16)
    nkt = s // SLC_KEY_TILE
    vt = kv6[:, :, 3].astype(BF16).reshape(bsz, nkt, SLC_KEY_TILE, N_KV, HEAD_DIM).transpose(0, 3, 1, 4, 2)
    vt_aug = jnp.concatenate([
        vt, jnp.ones((bsz, N_KV, nkt, 1, SLC_KEY_TILE), BF16),
        jnp.zeros((bsz, N_KV, nkt, SLC_V_ROWS - HEAD_DIM - 1, SLC_KEY_TILE), BF16)], axis=3)
    blk = np.arange(s) // SLC_BLOCK
    onehot = (blk[:, None] % SLC_GROUP_BLOCKS == np.arange(SLC_GROUP_BLOCKS)[None, :]).astype(np.float32)
    k_aug = jnp.concatenate([
        k_slc, jnp.ones((bsz, N_KV, s, SLC_BIAS_PIECES), BF16),
        jnp.zeros((bsz, N_KV, s, LANES - HEAD_DIM - SLC_BIAS_PIECES), BF16),
        jnp.broadcast_to(jnp.asarray(onehot, BF16), (bsz, N_KV, s, SLC_GROUP_BLOCKS))], axis=-1)
    o_st = _slc_attn(qt, far_t, k_aug, vt_aug, sel_t, tab_slc)
    o_s = o_st.reshape(bsz, N_KV, nq, HEAD_DIM, N_GRP, Q_BLOCK).transpose(0, 2, 5, 1, 4, 3)
    o_s = o_s.reshape(bsz, s, N_HEADS * HEAD_DIM)

    front = ((0, 0), (0, 0), (WINDOW, 0), (0, 0))
    k_win = jnp.pad(kv6[:, :, 4].transpose(0, 2, 1, 3).astype(BF16), front)
    v_win = jnp.pad(kv6[:, :, 5].transpose(0, 2, 1, 3).astype(BF16), front)
    o_w = _win_attn(qh, k_win, v_win, tab_win)

    back = lambda o: o.transpose(0, 3, 1, 2, 4).reshape(bsz, s, N_HEADS * HEAD_DIM)
    return back(o_c), o_s, back(o_w)


def kernel(x, c, ada_w, ada_b, norm_mix, norm_ffn, w_in, cmp_pe, cmp_w1, cmp_w2, rel_bias, attn_out_norm,
           conv_w, conv_b, dt_bias, a_log, d_skip, ssm_norm, w_out, peer_wq, peer_subkeys, peer_u, peer_v,
           norm_final):
    bsz, s, d = x.shape
    depth = ada_w.shape[0]
    d_attn = N_HEADS * HEAD_DIM
    n_kv = 6 * N_KV * HEAD_DIM
    n_gate = 3 * N_HEADS
    d_ssm = ssm_norm.shape[1]
    ch = conv_w.shape[2]
    nh = dt_bias.shape[1]

    mod = _ada_mod(c, ada_w, ada_b)
    tables = _bias_tables(rel_bias)

    for l in range(depth):
        sh1, sc1, g1, sh2, sc2, g2 = [mod[l, :, i * d:(i + 1) * d].reshape(bsz, 1, d) for i in range(6)]
        cuts = np.cumsum([0, d_attn, n_kv, n_gate, d_ssm, ch, nh])
        seg = dict(zip(("q", "kv", "gl", "z", "xbc", "dt"),
                       [w_in[l][:, cuts[i]:cuts[i + 1]] for i in range(6)]))
        seg["gl"] = _pad_cols(seg["gl"], LANES)
        seg["dt"] = _pad_cols(seg["dt"], LANES)
        order = ("q", "z", "kv", "gl", "dt", "xbc")
        w_cat = jnp.concatenate([seg[name] for name in order], axis=1).astype(BF16)
        start = dict(zip(order, np.cumsum([0] + [seg[name].shape[1] for name in order])[:-1]))
        blk = {name: (int(start[name]) // seg[name].shape[1], seg[name].shape[1]) for name in order}
        assert all(start[name] % seg[name].shape[1] == 0 for name in ("z", "gl", "dt", "xbc"))
        proj, _ = _norm_proj(x, norm_mix[l], sc1, sh1, w_cat)
        q = proj[:, :, start["q"]:start["q"] + d_attn]
        kv = proj[:, :, start["kv"]:start["kv"] + n_kv]

        o_c, o_s, o_w = _nsa(q, kv, tables, cmp_pe[l], cmp_w1[l], cmp_w2[l])
        ssm = _ssd(proj, blk["xbc"], blk["z"], blk["dt"], conv_w[l], conv_b[l], dt_bias[l], a_log[l], d_skip[l],
                   ssm_norm[l])
        x = _mix_out(o_c, o_s, o_w, proj, blk["gl"], attn_out_norm[l], ssm, w_out[l].astype(BF16), x, g1)

        pq, h2 = _norm_proj(x, norm_ffn[l], sc2, sh2, peer_wq[l].astype(BF16))
        rank2, e2, n1, c1 = _peer_route(pq.reshape(bsz * s, -1), peer_subkeys[l])
        h_t = h2.reshape(bsz * s, d).T
        ffn_t = _peer_dense(h_t, peer_u[l].astype(BF16), peer_v[l].astype(BF16).T, rank2, e2, n1, c1)
        x = _residual(x, ffn_t, g2, norm_final if l == depth - 1 else None)

    return x
```

```python
import functools
import math

import numpy as np
import jax
import jax.numpy as jnp
from jax import lax
from jax.experimental import pallas as pl
from jax.experimental.pallas import tpu as pltpu

F32 = jnp.float32
BF16 = jnp.bfloat16
HI = lax.Precision.HIGHEST

N_HEADS = 16
N_KV = 2
N_GRP = N_HEADS // N_KV
HEAD_DIM = 64
CMP_BLOCK = 32
CMP_STRIDE = 16
CMP_HIDDEN = 4 * HEAD_DIM
SLC_BLOCK = 64
SLC_TOPN = 16
WINDOW = 512
Q_BLOCK = 128
FORCE_SCORE = 1e4
NEG_INF = -1e30
REL_BUCKETS = 32
REL_MAX_DIST = 2048
SSM_HEAD_DIM = 64
SSM_GROUPS = 2
SSM_STATE = 128
CONV_WIDTH = 4
SSM_CHUNK = 256
PEER_HEADS = 8
PEER_TOPK = 16
RMS_EPS = 1e-6

LANES = 128
SUBLANES = 8
VMEM_LIMIT = 56 * 1024 * 1024

SLC_KEY_TILE = 512
SLC_TILE_BLOCKS = SLC_KEY_TILE // SLC_BLOCK
SLC_GROUP_BLOCKS = LANES
SLC_NEAR_CHUNKS = 14
SLC_TAB_MASKED = 14
SLC_TAB_CONST = 15
MASK_BIG = 2.0 ** 100
SLC_LANE_STRIP = 256
PEER_LANE_STRIP = 256
PEER_EXPERT_PIECE = 256
PEER_EXPERT_UNIT = 512
SLC_V_ROWS = HEAD_DIM + 16
SLC_BIAS_PIECES = 3
GATE_PIECES = 3

CMP_FAR_STEP = 256
CMP_NEAR = 128
CMP_PAD = CMP_NEAR - Q_BLOCK // CMP_STRIDE


def _cparams(sem, vmem=VMEM_LIMIT):
    return pltpu.CompilerParams(dimension_semantics=sem, vmem_limit_bytes=vmem)


def _nt(a, b, precision=None):
    return lax.dot_general(a, b, (((1,), (1,)), ((), ())), precision=precision,
                           preferred_element_type=F32)


def _silu(x):
    return x * jax.nn.sigmoid(x)


def _rel_bucket_np(d):
    d = np.maximum(np.asarray(d, np.int64), 0)
    max_exact = REL_BUCKETS // 2
    ratio = np.log(np.maximum(d, max_exact).astype(np.float64) / max_exact) / math.log(REL_MAX_DIST / max_exact)
    scaled = ratio * (REL_BUCKETS - max_exact)
    large = max_exact + np.floor(scaled).astype(np.int64)
    return np.where(d < max_exact, d, np.minimum(large, REL_BUCKETS - 1)).astype(np.int32)


def _ada_kernel(c_ref, w_ref, b_ref, o_ref):
    cond = _silu(c_ref[...])
    o_ref[0] = jnp.dot(cond, w_ref[0], precision=HI, preferred_element_type=F32) + b_ref[0]


def _ada_mod(c, ada_w, ada_b):
    depth, d, n = ada_w.shape
    bsz = c.shape[0]
    rows = SUBLANES
    c_pad = jnp.zeros((rows, d), F32).at[:bsz].set(c)
    tn = 1024
    out = pl.pallas_call(
        _ada_kernel,
        grid=(depth, n // tn),
        in_specs=[pl.BlockSpec((rows, d), lambda l, j: (0, 0)),
                  pl.BlockSpec((1, d, tn), lambda l, j: (l, 0, j)),
                  pl.BlockSpec((1, 1, tn), lambda l, j: (l, 0, j))],
        out_specs=pl.BlockSpec((1, rows, tn), lambda l, j: (l, 0, j)),
        out_shape=jax.ShapeDtypeStruct((depth, rows, n), F32),
        compiler_params=_cparams(("parallel", "parallel")),
        name="ada_mod",
    )(c_pad, ada_w, ada_b.reshape(depth, 1, n))
    return out[:, :bsz]


def _norm_proj_kernel(x_ref, nw_ref, sc_ref, sh_ref, w_ref, o_ref, h_ref, hs_ref):
    @pl.when(pl.program_id(2) == 0)
    def _():
        x = x_ref[0]
        y = x * lax.rsqrt(jnp.mean(x * x, axis=-1, keepdims=True) + RMS_EPS)
        h = (y * nw_ref[...]) * (1.0 + sc_ref[0]) + sh_ref[0]
        hs_ref[...] = h.astype(BF16)
        h_ref[0] = h.astype(BF16)

    o_ref[0] = jnp.dot(hs_ref[...], w_ref[...], preferred_element_type=F32)


def _norm_proj(x, nw, sc, sh, w_bf16, tm=1024, tn=512):
    bsz, s, d = x.shape
    n = w_bf16.shape[1]
    return pl.pallas_call(
        _norm_proj_kernel,
        grid=(bsz, s // tm, n // tn),
        in_specs=[pl.BlockSpec((1, tm, d), lambda b, i, j: (b, i, 0)),
                  pl.BlockSpec((1, d), lambda b, i, j: (0, 0)),
                  pl.BlockSpec((1, 1, d), lambda b, i, j: (b, 0, 0)),
                  pl.BlockSpec((1, 1, d), lambda b, i, j: (b, 0, 0)),
                  pl.BlockSpec((d, tn), lambda b, i, j: (0, j))],
        out_specs=[pl.BlockSpec((1, tm, tn), lambda b, i, j: (b, i, j)),
                   pl.BlockSpec((1, tm, d), lambda b, i, j: (b, i, 0))],
        out_shape=[jax.ShapeDtypeStruct((bsz, s, n), F32),
                   jax.ShapeDtypeStruct((bsz, s, d), BF16)],
        scratch_shapes=[pltpu.VMEM((tm, d), BF16)],
        compiler_params=_cparams(("parallel", "parallel", "arbitrary")),
        name="norm_proj",
    )(x, nw.reshape(1, d), sc, sh, w_bf16)


def _bias_kernel(rel_ref, bk_ref, o_ref):
    h = pl.program_id(0)
    bk = bk_ref[...]
    acc = jnp.full(bk.shape, NEG_INF, F32)
    for b in range(REL_BUCKETS):
        acc = jnp.where(bk == b, rel_ref[b, h], acc)
    o_ref[0] = acc


def _bias_tables(rel_bias):
    r = np.arange(Q_BLOCK)[:, None]
    dw = r - np.arange(WINDOW + Q_BLOCK)[None, :] + WINDOW
    win = np.where((dw >= 0) & (dw < WINDOW), _rel_bucket_np(dw), -1)
    off = CMP_STRIDE * CMP_PAD - (CMP_BLOCK - 1)
    dc = r + off - CMP_STRIDE * np.arange(CMP_NEAR)[None, :]
    cmp_near = np.where(dc >= 0, _rel_bucket_np(dc), -1)
    chunks = []
    for m in range(SLC_NEAR_CHUNKS):
        ds_ = Q_BLOCK * m + r - np.arange(Q_BLOCK)[None, :]
        chunks.append(np.where(ds_ >= 0, _rel_bucket_np(ds_), -1))
    assert _rel_bucket_np(Q_BLOCK * SLC_NEAR_CHUNKS - (Q_BLOCK - 1)) == REL_BUCKETS - 1
    assert _rel_bucket_np(off + CMP_STRIDE) == REL_BUCKETS - 1
    chunks.append(np.full((Q_BLOCK, Q_BLOCK), -1))
    chunks.append(np.full((Q_BLOCK, Q_BLOCK), REL_BUCKETS - 1))
    bk = np.concatenate([win, cmp_near] + chunks, axis=1).astype(np.int32)
    cols = bk.shape[1]
    out = pl.pallas_call(
        _bias_kernel,
        grid=(N_HEADS,),
        in_specs=[pl.BlockSpec(memory_space=pltpu.SMEM),
                  pl.BlockSpec((Q_BLOCK, cols), lambda h: (0, 0))],
        out_specs=pl.BlockSpec((1, Q_BLOCK, cols), lambda h: (h, 0, 0)),
        out_shape=jax.ShapeDtypeStruct((N_HEADS, Q_BLOCK, cols), F32),
        compiler_params=_cparams(("arbitrary",)),
        name="bias_tables",
    )(rel_bias, jnp.asarray(bk))
    nw = WINDOW + Q_BLOCK
    tab_win = out[:, :, :nw].reshape(N_KV, N_GRP, Q_BLOCK, nw)
    tab_cmp = out[:, :, nw:nw + CMP_NEAR].reshape(N_KV, N_GRP, Q_BLOCK, CMP_NEAR)
    nch = SLC_NEAR_CHUNKS + 2
    tab_slc = out[:, :, nw + CMP_NEAR:].reshape(N_KV, N_GRP, Q_BLOCK, nch, Q_BLOCK)
    tab_slc = tab_slc.transpose(0, 3, 4, 1, 2).reshape(N_KV, nch, Q_BLOCK, N_GRP * Q_BLOCK)
    far = rel_bias[REL_BUCKETS - 1].reshape(N_KV, N_GRP, 1)
    far_col = jnp.broadcast_to(far, (N_KV, N_GRP, Q_BLOCK)).reshape(N_KV, N_GRP * Q_BLOCK, 1)
    far_row = far_col.reshape(N_KV, 1, N_GRP * Q_BLOCK)
    pieces, rest = [], far_row
    for _ in range(SLC_BIAS_PIECES):
        piece = rest.astype(BF16)
        pieces.append(piece)
        rest = rest - piece.astype(F32)
    zeros = lambda w: jnp.zeros((N_KV, w, N_GRP * Q_BLOCK), BF16)
    far_t = jnp.concatenate([zeros(HEAD_DIM)] + pieces + [zeros(LANES - HEAD_DIM - SLC_BIAS_PIECES)], axis=1)
    return tab_win, tab_cmp, tab_slc, far_col, far_t


def _compress_kernel(a_ref, pe_ref, w1_ref, w2_ref, o_ref):
    a = a_ref[0, 0, 0]
    half = a.shape[1]
    lo = jnp.dot((a + pe_ref[0, 0:1, :]).astype(BF16), w1_ref[0, :half, :].astype(BF16),
                 preferred_element_type=F32)
    hi = jnp.dot((a + pe_ref[0, 1:2, :]).astype(BF16), w1_ref[0, half:, :].astype(BF16),
                 preferred_element_type=F32)
    nc = a.shape[0]
    hid = jax.nn.gelu(lo + pltpu.roll(hi, nc - 1, axis=0))
    out = jnp.dot(hid.astype(BF16), w2_ref[0].astype(BF16), preferred_element_type=F32)
    o_ref[0, 0, 0] = jnp.zeros(o_ref.shape[3:], F32)
    o_ref[0, 0, 0, CMP_PAD:CMP_PAD + nc, :] = out


def _compress(kv_cmp, cmp_pe, cmp_w1, cmp_w2):
    bsz, _, _, nc, half = kv_cmp.shape
    pe = cmp_pe.reshape(2, 2, half)
    rows = CMP_PAD + nc + SUBLANES
    return pl.pallas_call(
        _compress_kernel,
        grid=(bsz, 2, N_KV),
        in_specs=[pl.BlockSpec((1, 1, 1, nc, half), lambda b, w, k: (b, w, k, 0, 0)),
                  pl.BlockSpec((1, 2, half), lambda b, w, k: (w, 0, 0)),
                  pl.BlockSpec((1, 2 * half, CMP_HIDDEN), lambda b, w, k: (w, 0, 0)),
                  pl.BlockSpec((1, CMP_HIDDEN, HEAD_DIM), lambda b, w, k: (w, 0, 0))],
        out_specs=pl.BlockSpec((1, 1, 1, rows, HEAD_DIM), lambda b, w, k: (b, w, k, 0, 0)),
        out_shape=jax.ShapeDtypeStruct((bsz, 2, N_KV, rows, HEAD_DIM), F32),
        compiler_params=_cparams(("parallel", "parallel", "parallel")),
        name="nsa_compress",
    )(kv_cmp, pe, cmp_w1, cmp_w2)


def _topk_mark(vals, k, axis, order=None, order_bound=None, assume_distinct=False):
    n = vals.shape[axis] if order is None else order_bound
    if not assume_distinct:
        iota = lax.broadcasted_iota(jnp.int32, vals.shape, axis).astype(F32) if order is None else order
    rank = jnp.full(vals.shape, float(k), F32)
    work = vals
    picked = []
    for r in range(k):
        m = jnp.max(work, axis=axis, keepdims=True)
        if assume_distinct:
            hit = work == m
        else:
            ix = jnp.min(jnp.where(work == m, iota, float(n)), axis=axis, keepdims=True)
            hit = iota == ix
        rank = jnp.where(hit, float(r), rank)
        work = jnp.where(hit, -jnp.inf, work)
        picked.append(m)
    return rank, picked


def _topk_clean(rank, k, axis):
    marks = jnp.sum(jnp.where(rank < float(k), 1.0, 0.0), axis=axis, keepdims=True)
    return jnp.max(jnp.abs(marks - float(k)))


def _cmp_attn_kernel(q_ref, k_ref, v_ref, tab_ref, far_ref, m_ref, o_ref, sel_ref, imp_ref):
    i = pl.program_id(2)
    rows = N_GRP * Q_BLOCK
    nc = k_ref.shape[3] - CMP_PAD - SUBLANES
    nbp = sel_ref.shape[3]
    start = pl.multiple_of(i * (Q_BLOCK // CMP_STRIDE), SUBLANES)
    n_far = i * (Q_BLOCK // CMP_STRIDE) - CMP_PAD

    def attend(wf):
        q = q_ref[0, 0].reshape(rows, HEAD_DIM)
        k_far = k_ref[0, 0, 0, CMP_PAD:CMP_PAD + wf, :].astype(BF16)
        v_far = v_ref[0, 0, 0, CMP_PAD:CMP_PAD + wf, :].astype(BF16)
        k_near = k_ref[0, 0, 0, pl.ds(start, CMP_NEAR), :].astype(BF16)
        v_near = v_ref[0, 0, 0, pl.ds(start, CMP_NEAR), :].astype(BF16)

        n_idx = lax.broadcasted_iota(jnp.int32, (1, wf), 1)
        mask_far = n_idx < n_far
        s_far = jnp.where(mask_far, _nt(q, k_far) + far_ref[0], NEG_INF)
        tab = tab_ref[0].reshape(rows, CMP_NEAR)
        c_idx = lax.broadcasted_iota(jnp.int32, (1, CMP_NEAR), 1)
        mask_near = (c_idx >= -n_far) & (tab > 0.5 * NEG_INF)
        s_near = jnp.where(mask_near, _nt(q, k_near) + tab, NEG_INF)

        m = jnp.maximum(jnp.max(s_far, axis=-1, keepdims=True), jnp.max(s_near, axis=-1, keepdims=True))
        e_far = jnp.exp(s_far - m)
        e_near = jnp.exp(s_near - m)
        l = jnp.sum(e_far, axis=-1, keepdims=True) + jnp.sum(e_near, axis=-1, keepdims=True)
        p_far = jnp.where(mask_far, e_far / l, 0.0)
        p_near = jnp.where(mask_near, e_near / l, 0.0)
        o = (jnp.dot(p_far.astype(BF16), v_far, preferred_element_type=F32)
             + jnp.dot(p_near.astype(BF16), v_near, preferred_element_type=F32))
        o_ref[0, 0] = o.reshape(N_GRP, Q_BLOCK, HEAD_DIM)

        ps_far = jnp.sum(p_far.reshape(N_GRP, Q_BLOCK, wf), axis=0)
        ps_near = jnp.sum(p_near.reshape(N_GRP, Q_BLOCK, CMP_NEAR), axis=0)
        imp_ref[...] = (
            jnp.dot(ps_far, m_ref[CMP_PAD:CMP_PAD + wf, :], precision=HI, preferred_element_type=F32)
            + jnp.dot(ps_near, m_ref[pl.ds(start, CMP_NEAR), :], precision=HI, preferred_element_type=F32))

    step = min(CMP_FAR_STEP, nc)
    widths = list(range(step, nc + 1, step))
    for b, wf in enumerate(widths):
        lower = n_far > widths[b - 1] if b > 0 else True
        upper = n_far <= wf if b + 1 < len(widths) else True
        pl.when(jnp.logical_and(lower, upper))(functools.partial(attend, wf))

    imp = imp_ref[...].T
    t = i * Q_BLOCK + lax.broadcasted_iota(jnp.int32, (1, Q_BLOCK), 1)
    cur = t // SLC_BLOCK
    blk = lax.broadcasted_iota(jnp.int32, (nbp, 1), 0)
    forced = (blk == 0) | (blk == cur) | (blk == cur - 1)
    imp = jnp.where(forced, FORCE_SCORE, jnp.where(blk <= cur, imp, -FORCE_SCORE))
    n_blocks = (nc * CMP_STRIDE) // SLC_BLOCK
    imp = jnp.where(blk < n_blocks, imp, -jnp.inf)
    rank, _ = _topk_mark(imp, min(SLC_TOPN, n_blocks), axis=0)
    sel_ref[0, 0, 0] = jnp.where(rank < float(SLC_TOPN), 1.0, 0.0).astype(BF16)


def _overlap_matrix(nc, nbp):
    n_cmp = nc - 1
    n_slc = nc * CMP_STRIDE // SLC_BLOCK
    j = np.arange(n_slc)
    lo = np.clip((j * SLC_BLOCK - CMP_BLOCK) // CMP_STRIDE + 1, 0, n_cmp)
    hi = np.clip(-((-(j * SLC_BLOCK + SLC_BLOCK)) // CMP_STRIDE), 0, n_cmp)
    m = np.zeros((CMP_PAD + nc + SUBLANES, nbp), np.float32)
    n = np.arange(nc)[:, None]
    m[CMP_PAD:CMP_PAD + nc, :n_slc] = (n >= lo[None, :]) & (n < hi[None, :])
    return m


def _cmp_attn(q64, kv_c, tab_cmp, far_col, nbp):
    bsz, _, _, s, _ = q64.shape
    rows_c = kv_c.shape[3]
    nc = rows_c - CMP_PAD - SUBLANES
    nq = s // Q_BLOCK
    m_pad = jnp.asarray(_overlap_matrix(nc, nbp))
    return pl.pallas_call(
        _cmp_attn_kernel,
        grid=(bsz, N_KV, nq),
        in_specs=[pl.BlockSpec((1, 1, N_GRP, Q_BLOCK, HEAD_DIM), lambda b, k, i: (b, k, 0, i, 0)),
                  pl.BlockSpec((1, 1, 1, rows_c, HEAD_DIM), lambda b, k, i: (b, 0, k, 0, 0)),
                  pl.BlockSpec((1, 1, 1, rows_c, HEAD_DIM), lambda b, k, i: (b, 1, k, 0, 0)),
                  pl.BlockSpec((1, N_GRP, Q_BLOCK, CMP_NEAR), lambda b, k, i: (k, 0, 0, 0)),
                  pl.BlockSpec((1, N_GRP * Q_BLOCK, 1), lambda b, k, i: (k, 0, 0)),
                  pl.BlockSpec((rows_c, nbp), lambda b, k, i: (0, 0))],
        out_specs=[pl.BlockSpec((1, 1, N_GRP, Q_BLOCK, HEAD_DIM), lambda b, k, i: (b, k, 0, i, 0)),
                   pl.BlockSpec((1, 1, 1, nbp, Q_BLOCK), lambda b, k, i: (b, k, i, 0, 0))],
        out_shape=[jax.ShapeDtypeStruct((bsz, N_KV, N_GRP, s, HEAD_DIM), F32),
                   jax.ShapeDtypeStruct((bsz, N_KV, nq, nbp, Q_BLOCK), BF16)],
        scratch_shapes=[pltpu.VMEM((Q_BLOCK, nbp), F32)],
        compiler_params=_cparams(("parallel", "parallel", "arbitrary")),
        name="nsa_cmp_attn",
    )(q64, kv_c, kv_c, tab_cmp, far_col, m_pad)


def _slc_attn_kernel(qt_ref, far_ref, ka_ref, vt_ref, selt_ref, tab_ref, o_ref,
                     qa_ref, s0_ref, s1_ref, x0_ref, x1_ref, m_ref, acc_ref):
    i = pl.program_id(2)
    cols = N_GRP * Q_BLOCK
    ngroups = qa_ref.shape[0] // 2
    qt = qt_ref[0, 0, 0]
    qt_far = qt + far_ref[0]
    selneg = ((selt_ref[0, 0, 0].astype(F32) - 1.0) * MASK_BIG).astype(BF16)
    for g in range(ngroups):
        part = selneg[g * SLC_GROUP_BLOCKS:(g + 1) * SLC_GROUP_BLOCKS, :]
        part = jnp.concatenate([part] * N_GRP, axis=1)
        qa_ref[2 * g] = jnp.concatenate([qt_far, part], axis=0)
        qa_ref[2 * g + 1] = jnp.concatenate([qt, part], axis=0)

    tiles_per_group = SLC_GROUP_BLOCKS // SLC_TILE_BLOCKS
    sub = SLC_KEY_TILE // Q_BLOCK
    last_tile = vt_ref.shape[2] - 1
    n_pairs = (i // sub + 2) // 2
    n_far = jnp.maximum(0, (i - (SLC_NEAR_CHUNKS - 1)) // sub)
    far_pairs = jnp.maximum(0, (n_far - 1) // 2)
    first_table = jnp.where(n_far > 0, 2 * far_pairs + 1, 0)

    def produce(kt, ls, s_ref, mx_ref, with_table):
        near = (kt >= first_table).astype(jnp.int32)
        kc = jnp.minimum(kt, last_tile)
        ks = pl.multiple_of(kc * SLC_KEY_TILE, SLC_KEY_TILE)
        s = jnp.dot(ka_ref[0, 0, pl.ds(ks, SLC_KEY_TILE), :], qa_ref[2 * (kc // tiles_per_group) + near, :, ls],
                    preferred_element_type=F32)
        if with_table:
            chunks = []
            for a in range(sub):
                mm = i - sub * kt - a
                idx = jnp.where(mm < 0, SLC_TAB_MASKED, jnp.where(mm >= SLC_NEAR_CHUNKS, SLC_TAB_CONST, mm))
                chunks.append(tab_ref[0, idx, :, ls])
            s = s + jnp.concatenate(chunks, axis=0)
        s_ref[:, ls] = s
        mx_ref[:, ls] = jnp.max(s, axis=0, keepdims=True)

    def consume(kt, ls, s_ref, mx_ref):
        m_old = m_ref[:, ls]
        m_new = jnp.maximum(m_old, mx_ref[:, ls])
        m_ref[:, ls] = m_new
        p = jnp.exp(s_ref[:, ls] - m_new).astype(BF16)
        acc_ref[:, ls] = jnp.exp(m_old - m_new) * acc_ref[:, ls] + jnp.dot(
            vt_ref[0, 0, jnp.minimum(kt, last_tile)], p, preferred_element_type=F32)

    strips = [slice(c * SLC_LANE_STRIP, (c + 1) * SLC_LANE_STRIP) for c in range(cols // SLC_LANE_STRIP)]

    def pair_step(j, with_table):
        for ls in strips:
            produce(2 * j + 1, ls, s1_ref, x1_ref, with_table)
            consume(2 * j, ls, s0_ref, x0_ref)
        for ls in strips:
            produce(2 * j + 2, ls, s0_ref, x0_ref, with_table)
            consume(2 * j + 1, ls, s1_ref, x1_ref)

    m_ref[...] = jnp.full(m_ref.shape, -jnp.inf, F32)
    acc_ref[...] = jnp.zeros(acc_ref.shape, F32)

    @pl.when(n_far > 0)
    def _():
        for ls in strips:
            produce(0, ls, s0_ref, x0_ref, False)

    @pl.when(n_far == 0)
    def _():
        for ls in strips:
            produce(0, ls, s0_ref, x0_ref, True)

    def far_body(j, carry):
        pair_step(j, False)
        return carry

    def near_body(j, carry):
        pair_step(j, True)
        return carry

    lax.fori_loop(0, far_pairs, far_body, 0)
    lax.fori_loop(far_pairs, n_pairs, near_body, 0)
    o_ref[0, 0, 0] = acc_ref[:HEAD_DIM, :] / acc_ref[HEAD_DIM:HEAD_DIM + 1, :]


def _slc_attn(qt, far_t, k_aug, vt_aug, sel_t, tab_t):
    bsz, _, nq, _, cols = qt.shape
    s = k_aug.shape[2]
    nbp = sel_t.shape[3]
    ngroups = nbp // SLC_GROUP_BLOCKS
    nch = tab_t.shape[1]
    once = pl.Buffered(1)
    return pl.pallas_call(
        _slc_attn_kernel,
        grid=(bsz, N_KV, nq),
        in_specs=[pl.BlockSpec((1, 1, 1, LANES, cols), lambda b, k, i: (b, k, i, 0, 0)),
                  pl.BlockSpec((1, LANES, cols), lambda b, k, i: (k, 0, 0)),
                  pl.BlockSpec((1, 1, s, 2 * LANES), lambda b, k, i: (b, k, 0, 0), pipeline_mode=once),
                  pl.BlockSpec((1, 1, s // SLC_KEY_TILE, SLC_V_ROWS, SLC_KEY_TILE),
                               lambda b, k, i: (b, k, 0, 0, 0), pipeline_mode=once),
                  pl.BlockSpec((1, 1, 1, nbp, Q_BLOCK), lambda b, k, i: (b, k, i, 0, 0)),
                  pl.BlockSpec((1, nch, Q_BLOCK, cols), lambda b, k, i: (k, 0, 0, 0), pipeline_mode=once)],
        out_specs=pl.BlockSpec((1, 1, 1, HEAD_DIM, cols), lambda b, k, i: (b, k, i, 0, 0)),
        out_shape=jax.ShapeDtypeStruct((bsz, N_KV, nq, HEAD_DIM, cols), F32),
        scratch_shapes=[pltpu.VMEM((2 * ngroups, 2 * LANES, cols), BF16),
                        pltpu.VMEM((SLC_KEY_TILE, cols), F32),
                        pltpu.VMEM((SLC_KEY_TILE, cols), F32),
                        pltpu.VMEM((1, cols), F32),
                        pltpu.VMEM((1, cols), F32),
                        pltpu.VMEM((1, cols), F32),
                        pltpu.VMEM((SLC_V_ROWS, cols), F32)],
        compiler_params=_cparams(("parallel", "parallel", "arbitrary")),
        name="nsa_slc_attn",
    )(qt, far_t, k_aug, vt_aug, sel_t, tab_t)


def _win_attn_kernel(q_ref, k_ref, v_ref, tab_ref, o_ref):
    i = pl.program_id(2)
    rows = N_GRP * Q_BLOCK
    nw = WINDOW + Q_BLOCK
    q = q_ref[0, 0].reshape(rows, HEAD_DIM)
    qs = pl.multiple_of(i * Q_BLOCK, Q_BLOCK)
    k = k_ref[0, 0, pl.ds(qs, nw), :]
    v = v_ref[0, 0, pl.ds(qs, nw), :]
    s = _nt(q, k) + tab_ref[0].reshape(rows, nw)
    col = lax.broadcasted_iota(jnp.int32, (1, nw), 1)
    s = jnp.where(col >= WINDOW - i * Q_BLOCK, s, NEG_INF)
    m = jnp.max(s, axis=-1, keepdims=True)
    e = jnp.exp(s - m)
    o = jnp.dot(e.astype(BF16), v, preferred_element_type=F32) / jnp.sum(e, axis=-1, keepdims=True)
    o_ref[0, 0] = o.reshape(N_GRP, Q_BLOCK, HEAD_DIM)


def _win_attn(q64, k_win, v_win, tab_win):
    bsz, _, _, s, _ = q64.shape
    nq = s // Q_BLOCK
    sp = k_win.shape[2]
    nw = WINDOW + Q_BLOCK
    return pl.pallas_call(
        _win_attn_kernel,
        grid=(bsz, N_KV, nq),
        in_specs=[pl.BlockSpec((1, 1, N_GRP, Q_BLOCK, HEAD_DIM), lambda b, k, i: (b, k, 0, i, 0)),
                  pl.BlockSpec((1, 1, sp, HEAD_DIM), lambda b, k, i: (b, k, 0, 0)),
                  pl.BlockSpec((1, 1, sp, HEAD_DIM), lambda b, k, i: (b, k, 0, 0)),
                  pl.BlockSpec((1, N_GRP, Q_BLOCK, nw), lambda b, k, i: (k, 0, 0, 0))],
        out_specs=pl.BlockSpec((1, 1, N_GRP, Q_BLOCK, HEAD_DIM), lambda b, k, i: (b, k, 0, i, 0)),
        out_shape=jax.ShapeDtypeStruct((bsz, N_KV, N_GRP, s, HEAD_DIM), F32),
        compiler_params=_cparams(("parallel", "parallel", "arbitrary")),
        name="nsa_win_attn",
    )(q64, k_win, v_win, tab_win)


def _ssd_kernel(xbc_ref, z_ref, dt_ref, cw_ref, cb_ref, dtb_ref, alog_ref, dskip_ref, nw_ref, o_ref,
                ext_ref, state_ref):
    ln = SSM_CHUNK
    d_ssm = z_ref.shape[2]
    nh = dtb_ref.shape[1]
    gw = d_ssm // SSM_GROUPS
    hpg = nh // SSM_GROUPS
    gn = SSM_GROUPS * SSM_STATE

    @pl.when(pl.program_id(1) == 0)
    def _():
        ext_ref[0:SUBLANES, :] = jnp.zeros((SUBLANES, ext_ref.shape[1]), F32)
        state_ref[...] = jnp.zeros(state_ref.shape, F32)

    ext_ref[SUBLANES:SUBLANES + ln, :] = xbc_ref[0]
    conv = cw_ref[0:1, :] * ext_ref[SUBLANES - CONV_WIDTH + 1:SUBLANES - CONV_WIDTH + 1 + ln, :]
    for k in range(1, CONV_WIDTH):
        lo = SUBLANES - CONV_WIDTH + 1 + k
        conv = conv + cw_ref[k:k + 1, :] * ext_ref[lo:lo + ln, :]
    conv = conv + cb_ref[...]
    ext_ref[0:SUBLANES, :] = xbc_ref[0, ln - SUBLANES:ln, :]
    xc = _silu(conv)
    xs = xc[:, :d_ssm]
    bm = xc[:, d_ssm:d_ssm + gn]
    cm = xc[:, d_ssm + gn:d_ssm + 2 * gn]

    xdt = dt_ref[0, :, :nh] + dtb_ref[...]
    dt = jnp.maximum(xdt, 0.0) + jnp.log1p(jnp.exp(-jnp.abs(xdt)))
    a = -jnp.exp(alog_ref[...])
    da = dt * a

    row = lax.broadcasted_iota(jnp.int32, (ln, ln), 0)
    colm = lax.broadcasted_iota(jnp.int32, (ln, ln), 1)
    causal = row >= colm
    acs = jnp.dot(causal.astype(F32), da, precision=HI, preferred_element_type=F32)
    eye = (lax.broadcasted_iota(jnp.int32, (2 * nh, 2 * nh), 0)
           == lax.broadcasted_iota(jnp.int32, (2 * nh, 2 * nh), 1)).astype(F32)
    rows_t = _nt(eye, jnp.concatenate([acs, dt], axis=1), precision=HI)
    expand = (lax.broadcasted_iota(jnp.int32, (GATE_PIECES * nh, d_ssm), 0) % nh
              == lax.broadcasted_iota(jnp.int32, (GATE_PIECES * nh, d_ssm), 1) // SSM_HEAD_DIM).astype(BF16)

    def spread(cols):
        pieces, rest = [], cols
        for _ in range(GATE_PIECES):
            piece = rest.astype(BF16)
            pieces.append(piece)
            rest = rest - piece.astype(F32)
        return jnp.dot(jnp.concatenate(pieces, axis=1), expand, preferred_element_type=F32)

    last = acs[ln - 1:ln, :]
    exp_acs_x = spread(jnp.exp(acs))
    w_x = spread(jnp.exp(last - acs) * dt)
    exp_last_x = exp_acs_x[ln - 1:ln, :]

    ys = []
    for g in range(SSM_GROUPS):
        cg = cm[:, g * SSM_STATE:(g + 1) * SSM_STATE].astype(BF16)
        bg32 = bm[:, g * SSM_STATE:(g + 1) * SSM_STATE]
        bg = bg32.astype(BF16)
        xg = xs[:, g * gw:(g + 1) * gw]
        cb = _nt(cg, bg)
        st = state_ref[g]
        y_state = jnp.dot(cg, st.astype(BF16), preferred_element_type=F32) * exp_acs_x[:, g * gw:(g + 1) * gw]
        y_heads = []
        for j in range(hpg):
            h = g * hpg + j
            seg = acs[:, h:h + 1] - rows_t[h:h + 1, :]
            decay = jnp.exp(jnp.where(causal, seg, -jnp.inf))
            mmat = cb * decay * rows_t[nh + h:nh + h + 1, :]
            xh = xg[:, j * SSM_HEAD_DIM:(j + 1) * SSM_HEAD_DIM].astype(BF16)
            y_heads.append(jnp.dot(mmat.astype(BF16), xh, preferred_element_type=F32))
        ys.append(jnp.concatenate(y_heads, axis=1) + y_state)
        xw = (xg * w_x[:, g * gw:(g + 1) * gw]).astype(BF16)
        state_ref[g] = st * exp_last_x[:, g * gw:(g + 1) * gw] + jnp.dot(
            bg32.T.astype(BF16), xw, preferred_element_type=F32)

    y = jnp.concatenate(ys, axis=1) + dskip_ref[...] * xs
    y = y * _silu(z_ref[0])
    outs = []
    for g in range(SSM_GROUPS):
        yg = y[:, g * gw:(g + 1) * gw]
        outs.append(yg * lax.rsqrt(jnp.mean(yg * yg, axis=-1, keepdims=True) + RMS_EPS))
    o_ref[0] = jnp.concatenate(outs, axis=1) * nw_ref[...]


def _ssd(proj, xbc_blk, z_blk, dt_blk, conv_w, conv_b, dt_bias, a_log, d_skip, norm_w):
    bsz, s, _ = proj.shape
    ch = xbc_blk[1]
    d_ssm = z_blk[1]
    nh = dt_bias.shape[0]
    gw = d_ssm // SSM_GROUPS
    nchunks = s // SSM_CHUNK
    full = lambda shape: pl.BlockSpec(shape, lambda b, c: (0,) * len(shape))
    cols = lambda blk: pl.BlockSpec((1, SSM_CHUNK, blk[1]), lambda b, c: (b, c, blk[0]))
    xbc = z = dt_raw = proj
    return pl.pallas_call(
        _ssd_kernel,
        grid=(bsz, nchunks),
        in_specs=[cols(xbc_blk), cols(z_blk), cols(dt_blk),
                  full((CONV_WIDTH, ch)), full((1, ch)), full((1, nh)), full((1, nh)),
                  full((1, d_ssm)), full((1, d_ssm))],
        out_specs=pl.BlockSpec((1, SSM_CHUNK, d_ssm), lambda b, c: (b, c, 0)),
        out_shape=jax.ShapeDtypeStruct((bsz, s, d_ssm), F32),
        scratch_shapes=[pltpu.VMEM((SUBLANES + SSM_CHUNK, ch), F32),
                        pltpu.VMEM((SSM_GROUPS, SSM_STATE, gw), F32)],
        compiler_params=_cparams(("parallel", "arbitrary")),
        name="ssd_scan",
    )(xbc, z, dt_raw, conv_w, conv_b.reshape(1, ch), dt_bias.reshape(1, nh), a_log.reshape(1, nh),
      jnp.repeat(d_skip, SSM_HEAD_DIM).reshape(1, d_ssm), norm_w.reshape(1, d_ssm))


def _mix_out_kernel(oc_ref, os_ref, ow_ref, gl_ref, ex_ref, an_ref, ssm_ref, w_ref, x_ref, g1_ref, o_ref):
    d_attn = oc_ref.shape[2]
    sig = jax.nn.sigmoid(gl_ref[0])
    pieces, rest = [], sig
    for _ in range(GATE_PIECES):
        piece = rest.astype(BF16)
        pieces.append(piece)
        rest = rest - piece.astype(F32)
    sig3 = jnp.concatenate(pieces, axis=1)
    gc = jnp.dot(sig3, ex_ref[0], preferred_element_type=F32)
    gs = jnp.dot(sig3, ex_ref[1], preferred_element_type=F32)
    gw = jnp.dot(sig3, ex_ref[2], preferred_element_type=F32)
    attn = gc * oc_ref[0] + gs * os_ref[0] + gw * ow_ref[0]
    attn = attn * lax.rsqrt(jnp.mean(attn * attn, axis=-1, keepdims=True) + RMS_EPS) * an_ref[...]
    mix = (jnp.dot(attn.astype(BF16), w_ref[:d_attn, :], preferred_element_type=F32)
           + jnp.dot(ssm_ref[0].astype(BF16), w_ref[d_attn:, :], preferred_element_type=F32))
    o_ref[0] = x_ref[0] + g1_ref[0] * mix


def _gate_expand():
    ex = np.zeros((3, LANES, N_HEADS * HEAD_DIM), np.float32)
    for r in range(3):
        for h in range(N_HEADS):
            ex[r, h * 3 + r, h * HEAD_DIM:(h + 1) * HEAD_DIM] = 1.0
    return np.concatenate([ex] * GATE_PIECES, axis=1)


def _mix_out(oc, os_, ow, gl, gl_blk, attn_norm, ssm, w_out_bf16, x, g1, tm=512):
    bsz, s, d = x.shape
    d_attn = oc.shape[2]
    d_ssm = ssm.shape[2]
    tok = lambda w: pl.BlockSpec((1, tm, w), lambda b, i: (b, i, 0))
    return pl.pallas_call(
        _mix_out_kernel,
        grid=(bsz, s // tm),
        in_specs=[tok(d_attn), tok(d_attn), tok(d_attn),
                  pl.BlockSpec((1, tm, gl_blk[1]), lambda b, i: (b, i, gl_blk[0])),
                  pl.BlockSpec((3, GATE_PIECES * LANES, d_attn), lambda b, i: (0, 0, 0)),
                  pl.BlockSpec((1, d_attn), lambda b, i: (0, 0)),
                  tok(d_ssm),
                  pl.BlockSpec((d_attn + d_ssm, d), lambda b, i: (0, 0)),
                  tok(d),
                  pl.BlockSpec((1, 1, d), lambda b, i: (b, 0, 0))],
        out_specs=tok(d),
        out_shape=jax.ShapeDtypeStruct((bsz, s, d), F32),
        compiler_params=_cparams(("parallel", "parallel")),
        name="mix_out",
    )(oc, os_, ow, gl, jnp.asarray(_gate_expand(), BF16), attn_norm.reshape(1, d_attn), ssm, w_out_bf16, x, g1)


def _peer_candidates():
    k, sub = PEER_TOPK, SUBLANES
    cells = [(0, b) for b in range(k)]
    cells += [(a, b) for a in range(1, sub) for b in range(sub)]
    cells += [(a, 0) for a in range(sub, k)]
    order = np.array([a * k + b if (a + 1) * (b + 1) <= k else -1 for a, b in cells], np.float32)
    group = np.zeros((k, LANES), np.float32)
    for row, (a, _) in enumerate(cells):
        group[a, row] = 1.0
    return cells, order, group


def _peer_route_kernel(q_ref, sk_ref, order_ref, group_ref, rank2_ref, e2_ref, n1_ref, c1_ref):
    k = PEER_TOPK

    def route(assume_distinct):
        q = q_ref[...]
        half = q.shape[1] // 2
        s1 = _nt(sk_ref[0], q[:, :half], precision=HI)
        s2 = _nt(sk_ref[1], q[:, half:], precision=HI)
        rank1, v1 = _topk_mark(s1, k, axis=0, assume_distinct=assume_distinct)
        rank2, v2 = _topk_mark(s2, k, axis=0, assume_distinct=assume_distinct)
        v1_all = jnp.concatenate(v1, axis=0)
        v2_all = jnp.concatenate(v2, axis=0)
        cand = jnp.concatenate([v1[0] + v2_all] + [v1[a] + v2_all[:SUBLANES] for a in range(1, SUBLANES)]
                               + [v1_all[SUBLANES:] + v2[0]], axis=0)
        order = order_ref[...]
        cand = jnp.where(order >= 0.0, cand, -jnp.inf)
        rank_c, best = _topk_mark(cand, k, axis=0, order=order, order_bound=float(k * k),
                                  assume_distinct=assume_distinct)
        chosen = jnp.where(rank_c < float(k), 1.0, 0.0).astype(BF16)
        chosen = jnp.concatenate([chosen, jnp.zeros((LANES - chosen.shape[0], chosen.shape[1]), BF16)], axis=0)
        count = jnp.dot(group_ref[...], chosen, preferred_element_type=F32)
        z = best[0] * 0.0
        for r in range(k):
            z = z + jnp.exp(best[r] - best[0])
        n1 = jnp.zeros(s1.shape, F32)
        for a in range(k):
            n1 = jnp.where(rank1 == float(a), count[a:a + 1, :], n1)
        rank2_ref[0] = rank2.astype(BF16)
        e2_ref[0] = jnp.exp(s2 - v2[0]).astype(BF16)
        n1_ref[0] = n1
        c1_ref[0] = jnp.exp(s1 - v1[0]) / z
        return jnp.maximum(jnp.maximum(_topk_clean(rank1, k, 0), _topk_clean(rank2, k, 0)),
                           _topk_clean(rank_c, k, 0))

    dirty = route(True)

    @pl.when(dirty > 0.0)
    def _():
        route(False)


def _peer_route(q, subkeys, tm=512):
    t, width = q.shape
    kd = width // PEER_HEADS
    nk = subkeys.shape[1]
    spec = pl.BlockSpec((1, nk, tm), lambda i, h: (h, 0, i))
    shape = jax.ShapeDtypeStruct((PEER_HEADS, nk, t), F32)
    cells, order, group = _peer_candidates()
    order = jnp.asarray(np.broadcast_to(order[:, None], (len(cells), tm)))
    return pl.pallas_call(
        _peer_route_kernel,
        grid=(t // tm, PEER_HEADS),
        in_specs=[pl.BlockSpec((tm, kd), lambda i, h: (i, h)),
                  pl.BlockSpec((2, nk, kd // 2), lambda i, h: (0, 0, 0)),
                  pl.BlockSpec((len(cells), tm), lambda i, h: (0, 0)),
                  pl.BlockSpec((PEER_TOPK, LANES), lambda i, h: (0, 0))],
        out_specs=[spec, spec, spec, spec],
        out_shape=[jax.ShapeDtypeStruct(shape.shape, BF16), jax.ShapeDtypeStruct(shape.shape, BF16), shape, shape],
        compiler_params=_cparams(("parallel", "parallel")),
        name="peer_route",
    )(q, subkeys, order, jnp.asarray(group, BF16))


def _peer_dense_kernel(ht_ref, u_ref, vt_ref, rank2_ref, e2_ref, n1_ref, c1_ref, o_ref, *scratch):
    j = pl.program_id(1)
    te = u_ref.shape[0]
    nk = rank2_ref.shape[1]

    @pl.when(j == 0)
    def _():
        o_ref[...] = jnp.zeros(o_ref.shape, F32)

    tm = ht_ref.shape[1]
    strips = [slice(c * PEER_LANE_STRIP, (c + 1) * PEER_LANE_STRIP) for c in range(tm // PEER_LANE_STRIP)]

    units = [(e, c) for e in range(te // PEER_EXPERT_UNIT) for c in range(len(strips))]
    act_refs = scratch[:len(units)]
    aw_refs = scratch[len(units):]
    n_piece = PEER_EXPERT_UNIT // PEER_EXPERT_PIECE
    d_piece = o_ref.shape[0] // n_piece

    def produce(k, p):
        e, c = units[k]
        lo = e * PEER_EXPERT_UNIT + p * PEER_EXPERT_PIECE
        act_refs[k][p * PEER_EXPERT_PIECE:(p + 1) * PEER_EXPERT_PIECE, :] = jax.nn.gelu(
            jnp.dot(u_ref[lo:lo + PEER_EXPERT_PIECE, :], ht_ref[:, strips[c]], preferred_element_type=F32)
        ).astype(BF16)

    def gate(k, r):
        e, c = units[k]
        ls = strips[c]
        i1 = (j * te + e * PEER_EXPERT_UNIT) // nk + r
        w = jnp.zeros((nk, PEER_LANE_STRIP), BF16)
        zero = jnp.zeros((nk, PEER_LANE_STRIP), BF16)
        for h in range(PEER_HEADS):
            n_row = n1_ref[h, pl.ds(i1, 1), ls].astype(BF16)
            c_row = c1_ref[h, pl.ds(i1, 1), ls].astype(BF16)
            w = w + jnp.where(rank2_ref[h, :, ls] < n_row, e2_ref[h, :, ls], zero) * c_row
        aw_refs[k][r * nk:(r + 1) * nk, :] = act_refs[k][r * nk:(r + 1) * nk, :] * w

    def combine(k, m):
        e, c = units[k]
        ds_ = slice(m * d_piece, (m + 1) * d_piece)
        es = slice(e * PEER_EXPERT_UNIT, (e + 1) * PEER_EXPERT_UNIT)
        o_ref[ds_, strips[c]] += jnp.dot(vt_ref[ds_, es], aw_refs[k][...], preferred_element_type=F32)

    gates_per_piece = PEER_EXPERT_PIECE // nk
    for p in range(n_piece):
        produce(0, p)
    for k in range(len(units)):
        for p in range(n_piece):
            if k + 1 < len(units):
                produce(k + 1, p)
            for r in range(p * gates_per_piece, (p + 1) * gates_per_piece):
                gate(k, r)
                if k > 0 and r == p * gates_per_piece:
                    combine(k - 1, p)
    for m in range(n_piece):
        combine(len(units) - 1, m)


def _peer_dense(h_t, u_bf16, v_t_bf16, rank2, e2, n1, c1, tm=512, te=512):
    d, t = h_t.shape
    n_exp = u_bf16.shape[0]
    nk = rank2.shape[1]
    n_units = (te // PEER_EXPERT_UNIT) * (tm // PEER_LANE_STRIP)
    route = pl.BlockSpec((PEER_HEADS, nk, tm), lambda i, j: (0, 0, i))
    return pl.pallas_call(
        _peer_dense_kernel,
        grid=(t // tm, n_exp // te),
        in_specs=[pl.BlockSpec((d, tm), lambda i, j: (0, i)),
                  pl.BlockSpec((te, d), lambda i, j: (j, 0)),
                  pl.BlockSpec((d, te), lambda i, j: (0, j)),
                  route, route, route, route],
        out_specs=pl.BlockSpec((d, tm), lambda i, j: (0, i)),
        out_shape=jax.ShapeDtypeStruct((d, t), F32),
        scratch_shapes=[pltpu.VMEM((PEER_EXPERT_UNIT, PEER_LANE_STRIP), BF16)] * (2 * n_units),
        compiler_params=_cparams(("parallel", "arbitrary")),
        name="peer_dense",
    )(h_t, u_bf16, v_t_bf16, rank2, e2, n1, c1)


def _residual_kernel(x_ref, yt_ref, g_ref, o_ref):
    o_ref[0] = x_ref[0] + g_ref[0] * yt_ref[...].T


def _residual_norm_kernel(x_ref, yt_ref, g_ref, w_ref, o_ref):
    x = x_ref[0] + g_ref[0] * yt_ref[...].T
    o_ref[0] = x * lax.rsqrt(jnp.mean(x * x, axis=-1, keepdims=True) + RMS_EPS) * w_ref[...]


def _residual(x, y_t, g, norm_w=None, tm=512):
    bsz, s, d = x.shape
    tok = pl.BlockSpec((1, tm, d), lambda b, i: (b, i, 0))
    in_specs = [tok, pl.BlockSpec((d, tm), lambda b, i: (0, b * (s // tm) + i)),
                pl.BlockSpec((1, 1, d), lambda b, i: (b, 0, 0))]
    args = [x, y_t, g]
    if norm_w is not None:
        in_specs.append(pl.BlockSpec((1, d), lambda b, i: (0, 0)))
        args.append(norm_w.reshape(1, d))
    return pl.pallas_call(
        _residual_kernel if norm_w is None else _residual_norm_kernel,
        grid=(bsz, s // tm),
        in_specs=in_specs,
        out_specs=tok,
        out_shape=jax.ShapeDtypeStruct((bsz, s, d), F32),
        compiler_params=_cparams(("parallel", "parallel")),
        name="residual",
    )(*args)


def _pad_cols(w, width):
    return jnp.pad(w, ((0, 0), (0, width - w.shape[1])))


def _nsa(q, kv, tables, cmp_pe, cmp_w1, cmp_w2):
    tab_win, tab_cmp, tab_slc, far_col, far_t = tables
    bsz, s, _ = q.shape
    nc = s // CMP_STRIDE
    nq = s // Q_BLOCK
    n_slc = s // SLC_BLOCK
    nbp = -(-n_slc // SLC_GROUP_BLOCKS) * SLC_GROUP_BLOCKS
    scale = HEAD_DIM ** -0.5
    q6 = (q * scale).astype(BF16).reshape(bsz, nq, Q_BLOCK, N_KV, N_GRP, HEAD_DIM)
    qh = q6.transpose(0, 3, 4, 1, 2, 5).reshape(bsz, N_KV, N_GRP, s, HEAD_DIM)
    qt = q6.transpose(0, 3, 1, 5, 4, 2).reshape(bsz, N_KV, nq, HEAD_DIM, N_GRP * Q_BLOCK)
    qt = jnp.pad(qt, ((0, 0),) * 3 + ((0, LANES - HEAD_DIM), (0, 0)))
    kv6 = kv.reshape(bsz, s, 6, N_KV, HEAD_DIM)

    kv_cmp = kv6[:, :, 0:2].reshape(bsz, nc, CMP_STRIDE, 2, N_KV, HEAD_DIM)
    kv_cmp = kv_cmp.transpose(0, 3, 4, 1, 2, 5).reshape(bsz, 2, N_KV, nc, CMP_STRIDE * HEAD_DIM)
    kv_c = _compress(kv_cmp, cmp_pe, cmp_w1, cmp_w2)
    o_c, sel_t = _cmp_attn(qh, kv_c, tab_cmp, far_col, nbp)

    k_slc = kv6[:, :, 2].transpose(0, 2, 1, 3).astype(BF16)
    nkt = s // SLC_KEY_TILE
    vt = kv6[:, :, 3].astype(BF16).reshape(bsz, nkt, SLC_KEY_TILE, N_KV, HEAD_DIM).transpose(0, 3, 1, 4, 2)
    vt_aug = jnp.concatenate([
        vt, jnp.ones((bsz, N_KV, nkt, 1, SLC_KEY_TILE), BF16),
        jnp.zeros((bsz, N_KV, nkt, SLC_V_ROWS - HEAD_DIM - 1, SLC_KEY_TILE), BF16)], axis=3)
    blk = np.arange(s) // SLC_BLOCK
    onehot = (blk[:, None] % SLC_GROUP_BLOCKS == np.arange(SLC_GROUP_BLOCKS)[None, :]).astype(np.float32)
    k_aug = jnp.concatenate([
        k_slc, jnp.ones((bsz, N_KV, s, SLC_BIAS_PIECES), BF16),
        jnp.zeros((bsz, N_KV, s, LANES - HEAD_DIM - SLC_BIAS_PIECES), BF16),
        jnp.broadcast_to(jnp.asarray(onehot, BF16), (bsz, N_KV, s, SLC_GROUP_BLOCKS))], axis=-1)
    o_st = _slc_attn(qt, far_t, k_aug, vt_aug, sel_t, tab_slc)
    o_s = o_st.reshape(bsz, N_KV, nq, HEAD_DIM, N_GRP, Q_BLOCK).transpose(0, 2, 5, 1, 4, 3)
    o_s = o_s.reshape(bsz, s, N_HEADS * HEAD_DIM)

    front = ((0, 0), (0, 0), (WINDOW, 0), (0, 0))
    k_win = jnp.pad(kv6[:, :, 4].transpose(0, 2, 1, 3).astype(BF16), front)
    v_win = jnp.pad(kv6[:, :, 5].transpose(0, 2, 1, 3).astype(BF16), front)
    o_w = _win_attn(qh, k_win, v_win, tab_win)

    back = lambda o: o.transpose(0, 3, 1, 2, 4).reshape(bsz, s, N_HEADS * HEAD_DIM)
    return back(o_c), o_s, back(o_w)


def kernel(x, c, ada_w, ada_b, norm_mix, norm_ffn, w_in, cmp_pe, cmp_w1, cmp_w2, rel_bias, attn_out_norm,
           conv_w, conv_b, dt_bias, a_log, d_skip, ssm_norm, w_out, peer_wq, peer_subkeys, peer_u, peer_v,
           norm_final):
    bsz, s, d = x.shape
    depth = ada_w.shape[0]
    d_attn = N_HEADS * HEAD_DIM
    n_kv = 6 * N_KV * HEAD_DIM
    n_gate = 3 * N_HEADS
    d_ssm = ssm_norm.shape[1]
    ch = conv_w.shape[2]
    nh = dt_bias.shape[1]

    mod = _ada_mod(c, ada_w, ada_b)
    tables = _bias_tables(rel_bias)

    for l in range(depth):
        sh1, sc1, g1, sh2, sc2, g2 = [mod[l, :, i * d:(i + 1) * d].reshape(bsz, 1, d) for i in range(6)]
        cuts = np.cumsum([0, d_attn, n_kv, n_gate, d_ssm, ch, nh])
        seg = dict(zip(("q", "kv", "gl", "z", "xbc", "dt"),
                       [w_in[l][:, cuts[i]:cuts[i + 1]] for i in range(6)]))
        seg["gl"] = _pad_cols(seg["gl"], LANES)
        seg["dt"] = _pad_cols(seg["dt"], LANES)
        order = ("q", "z", "kv", "gl", "dt", "xbc")
        w_cat = jnp.concatenate([seg[name] for name in order], axis=1).astype(BF16)
        start = dict(zip(order, np.cumsum([0] + [seg[name].shape[1] for name in order])[:-1]))
        blk = {name: (int(start[name]) // seg[name].shape[1], seg[name].shape[1]) for name in order}
        assert all(start[name] % seg[name].shape[1] == 0 for name in ("z", "gl", "dt", "xbc"))
        proj, _ = _norm_proj(x, norm_mix[l], sc1, sh1, w_cat)
        q = proj[:, :, start["q"]:start["q"] + d_attn]
        kv = proj[:, :, start["kv"]:start["kv"] + n_kv]

        o_c, o_s, o_w = _nsa(q, kv, tables, cmp_pe[l], cmp_w1[l], cmp_w2[l])
        ssm = _ssd(proj, blk["xbc"], blk["z"], blk["dt"], conv_w[l], conv_b[l], dt_bias[l], a_log[l], d_skip[l],
                   ssm_norm[l])
        x = _mix_out(o_c, o_s, o_w, proj, blk["gl"], attn_out_norm[l], ssm, w_out[l].astype(BF16), x, g1)

        pq, h2 = _norm_proj(x, norm_ffn[l], sc2, sh2, peer_wq[l].astype(BF16))
        rank2, e2, n1, c1 = _peer_route(pq.reshape(bsz * s, -1), peer_subkeys[l])
        h_t = h2.reshape(bsz * s, d).T
        ffn_t = _peer_dense(h_t, peer_u[l].astype(BF16), peer_v[l].astype(BF16).T, rank2, e2, n1, c1)
        x = _residual(x, ffn_t, g2, norm_final if l == depth - 1 else None)

    return x
```
